```python
import jax, jax.numpy as jnp
from jax import lax
import numpy as np

D_MODEL = 1024
BATCH = 8
SEQ = 4096
DEPTH = 2

FOX_HEADS = 8
FOX_HEAD_DIM = 64
FOX_WIDTH = FOX_HEADS * FOX_HEAD_DIM
MLA_HEADS = 8
MLA_NOPE_DIM = 64
MLA_ROPE_DIM = 32
MLA_V_DIM = 64
MLA_WIDTH = MLA_HEADS * MLA_V_DIM
D_MIX = FOX_WIDTH + MLA_WIDTH
Q_LORA_RANK = 256
KV_LORA_RANK = 128
ROPE_THETA = 10000.0
IN_WIDTH = 3 * FOX_WIDTH + FOX_HEADS + Q_LORA_RANK + KV_LORA_RANK + MLA_ROPE_DIM
Q_BLOCK = 128
N_GROUPS = 8
EXPERTS_PER_GROUP = 4
N_EXPERTS = N_GROUPS * EXPERTS_PER_GROUP
TOP_K_IN_GROUP = 2
D_EXPERT = 256
EXPERT_BLOCK = 128
NORM_EPS = 1e-6

kernel_name = 'hybrid_fox_mla_hmoe'


def rms_norm(x, gain):
    xf = x.astype(jnp.float32)
    y = xf * lax.rsqrt(jnp.mean(xf * xf, axis=-1, keepdims=True) + NORM_EPS)
    return (y * gain.astype(jnp.float32)).astype(x.dtype)


def rotary(x, positions):
    half = MLA_ROPE_DIM // 2
    inv_freq = ROPE_THETA ** (-jnp.arange(half, dtype=jnp.float32) / half)
    ang = positions.astype(jnp.float32)[..., None] * inv_freq
    ang = ang.reshape(ang.shape[:2] + (1,) * (x.ndim - 3) + (half,))
    cos, sin = jnp.cos(ang), jnp.sin(ang)
    xf = x.astype(jnp.float32)
    x1, x2 = xf[..., :half], xf[..., half:]
    return jnp.concatenate([x1 * cos - x2 * sin, x2 * cos + x1 * sin], axis=-1).astype(x.dtype)


def causal_block_attention(q, k, v, log_forget_cum=None):
    B, S, H, dk = q.shape
    dv = v.shape[-1]
    nb = S // Q_BLOCK
    scale = dk ** -0.5
    kf = k.astype(jnp.float32)
    vf = v.astype(jnp.float32)
    key_pos = jnp.arange(S)
    q_blocks = q.reshape(B, nb, Q_BLOCK, H, dk).transpose(1, 0, 2, 3, 4)
    has_decay = log_forget_cum is not None
    if has_decay:
        d_keys = log_forget_cum.transpose(0, 2, 1)
        d_blocks = log_forget_cum.reshape(B, nb, Q_BLOCK, H).transpose(1, 0, 3, 2)
        xs = (jnp.arange(nb), q_blocks, d_blocks)
    else:
        xs = (jnp.arange(nb), q_blocks)

    def one_block(args):
        i, q_i = args[0], args[1]
        s = jnp.einsum('bqhd,bkhd->bhqk', q_i.astype(jnp.float32), kf) * scale
        if has_decay:
            s = s + args[2][..., :, None] - d_keys[:, :, None, :]
        q_pos = i * Q_BLOCK + jnp.arange(Q_BLOCK)
        s = jnp.where(key_pos[None, :] <= q_pos[:, None], s, -jnp.inf)
        p = jax.nn.softmax(s, axis=-1)
        return jnp.einsum('bhqk,bkhd->bqhd', p, vf)

    out = lax.map(one_block, xs)
    return out.transpose(1, 0, 2, 3, 4).reshape(B, S, H, dv).astype(v.dtype)


def hybrid_mixer(xn, positions, w_in, b_f, q_norm, w_uq, kv_norm, w_ukv, fox_out_norm, mla_out_norm, w_o):
    B, S, _ = xn.shape
    proj = xn @ w_in
    offs = [FOX_WIDTH, 2 * FOX_WIDTH, 3 * FOX_WIDTH, 3 * FOX_WIDTH + FOX_HEADS,
            3 * FOX_WIDTH + FOX_HEADS + Q_LORA_RANK,
            3 * FOX_WIDTH + FOX_HEADS + Q_LORA_RANK + KV_LORA_RANK]
    fq, fk, fv, f_logit, c_q, c_kv, k_rope = jnp.split(proj, offs, axis=-1)

    log_f = jax.nn.log_sigmoid(f_logit.astype(jnp.float32) + b_f.astype(jnp.float32))
    log_f_cum = jnp.cumsum(log_f, axis=1)
    fox = causal_block_attention(fq.reshape(B, S, FOX_HEADS, FOX_HEAD_DIM),
                                 fk.reshape(B, S, FOX_HEADS, FOX_HEAD_DIM),
                                 fv.reshape(B, S, FOX_HEADS, FOX_HEAD_DIM), log_f_cum)
    fox = rms_norm(fox, fox_out_norm.reshape(FOX_HEADS, FOX_HEAD_DIM)).reshape(B, S, FOX_WIDTH)

    q = (rms_norm(c_q, q_norm) @ w_uq).reshape(B, S, MLA_HEADS, MLA_NOPE_DIM + MLA_ROPE_DIM)
    q = jnp.concatenate([q[..., :MLA_NOPE_DIM], rotary(q[..., MLA_NOPE_DIM:], positions)], axis=-1)
    kv = (rms_norm(c_kv, kv_norm) @ w_ukv).reshape(B, S, MLA_HEADS, MLA_NOPE_DIM + MLA_V_DIM)
    k_nope, v = kv[..., :MLA_NOPE_DIM], kv[..., MLA_NOPE_DIM:]
    k_pe = rotary(k_rope, positions)
    k = jnp.concatenate([k_nope, jnp.broadcast_to(k_pe[:, :, None, :], (B, S, MLA_HEADS, MLA_ROPE_DIM))], axis=-1)
    mla = causal_block_attention(q, k, v)
    mla = rms_norm(mla, mla_out_norm.reshape(MLA_HEADS, MLA_V_DIM)).reshape(B, S, MLA_WIDTH)

    return jnp.concatenate([fox, mla], axis=-1) @ w_o


def hierarchical_moe(xn, w_group, w_router, w_gate, w_up, w_down):
    B, S, D = xn.shape
    T = B * S
    xt = xn.reshape(T, D)
    group_p = jax.nn.softmax((xt @ w_group).astype(jnp.float32), axis=-1)
    group_w, group_id = lax.top_k(group_p, 1)
    expert_logits = (xt @ w_router).astype(jnp.float32).reshape(T, N_GROUPS, EXPERTS_PER_GROUP)
    in_group = jnp.take_along_axis(expert_logits, group_id[:, :, None], axis=1)[:, 0]
    top_logits, local_id = lax.top_k(in_group, TOP_K_IN_GROUP)
    gates = group_w * jax.nn.softmax(top_logits, axis=-1)
    expert_id = group_id * EXPERTS_PER_GROUP + local_id

    A = T * TOP_K_IN_GROUP
    e_flat = expert_id.reshape(A)
    tok_flat = jnp.repeat(jnp.arange(T), TOP_K_IN_GROUP)
    g_flat = gates.reshape(A)
    order = jnp.argsort(e_flat)
    e_sorted, tok_sorted, g_sorted = e_flat[order], tok_flat[order], g_flat[order]
    counts = jnp.bincount(e_flat, length=N_EXPERTS)
    padded = (counts + EXPERT_BLOCK - 1) // EXPERT_BLOCK * EXPERT_BLOCK
    start = jnp.cumsum(counts) - counts
    pend = jnp.cumsum(padded)
    pstart = pend - padded
    dest = pstart[e_sorted] + (jnp.arange(A) - start[e_sorted])
    n_blocks = (A + N_EXPERTS * (EXPERT_BLOCK - 1) + EXPERT_BLOCK - 1) // EXPERT_BLOCK
    buf = jnp.zeros((n_blocks * EXPERT_BLOCK, D), xn.dtype).at[dest].set(xt[tok_sorted])
    block_expert = jnp.clip(jnp.searchsorted(pend, jnp.arange(n_blocks) * EXPERT_BLOCK, side='right'), 0, N_EXPERTS - 1)

    def expert_block(args):
        xb, e = args
        h = jax.nn.silu(xb @ w_gate[e]) * (xb @ w_up[e])
        return h @ w_down[e]

    ybuf = lax.map(expert_block, (buf.reshape(n_blocks, EXPERT_BLOCK, D), block_expert)).reshape(-1, D)
    contrib = ybuf[dest].astype(jnp.float32) * g_sorted[:, None]
    y = jax.ops.segment_sum(contrib, tok_sorted, num_segments=T)
    return y.reshape(B, S, D).astype(xn.dtype)


def setup_inputs(seed: int = 0) -> dict:
    key = jax.random.key(seed)
    ks = jax.random.split(key, 20)
    f32 = jnp.float32

    def dense(k, shape, fan_in):
        return jax.random.normal(k, shape, f32) * (fan_in ** -0.5)

    def gain(k, shape):
        return 1.0 + 0.02 * jax.random.normal(k, shape, f32)

    return {
        'x': jax.random.normal(ks[0], (BATCH, SEQ, D_MODEL), f32),
        'positions': jnp.broadcast_to(jnp.arange(SEQ, dtype=jnp.int32), (BATCH, SEQ)),
        'attn_norm': gain(ks[1], (DEPTH, D_MODEL)),
        'w_in': dense(ks[2], (DEPTH, D_MODEL, IN_WIDTH), D_MODEL),
        'b_f': jax.random.uniform(ks[3], (DEPTH, FOX_HEADS), f32, 1.0, 4.0),
        'q_norm': gain(ks[4], (DEPTH, Q_LORA_RANK)),
        'w_uq': dense(ks[5], (DEPTH, Q_LORA_RANK, MLA_HEADS * (MLA_NOPE_DIM + MLA_ROPE_DIM)), Q_LORA_RANK),
        'kv_norm': gain(ks[6], (DEPTH, KV_LORA_RANK)),
        'w_ukv': dense(ks[7], (DEPTH, KV_LORA_RANK, MLA_HEADS * (MLA_NOPE_DIM + MLA_V_DIM)), KV_LORA_RANK),
        'fox_out_norm': gain(ks[8], (DEPTH, FOX_WIDTH)),
        'mla_out_norm': gain(ks[9], (DEPTH, MLA_WIDTH)),
        'w_o': dense(ks[10], (DEPTH, D_MIX, D_MODEL), D_MIX),
        'ffn_norm': gain(ks[11], (DEPTH, D_MODEL)),
        'w_group': dense(ks[12], (DEPTH, D_MODEL, N_GROUPS), D_MODEL),
        'w_router': dense(ks[13], (DEPTH, D_MODEL, N_EXPERTS), D_MODEL),
        'w_gate': dense(ks[14], (DEPTH, N_EXPERTS, D_MODEL, D_EXPERT), D_MODEL),
        'w_up': dense(ks[15], (DEPTH, N_EXPERTS, D_MODEL, D_EXPERT), D_MODEL),
        'w_down': dense(ks[16], (DEPTH, N_EXPERTS, D_EXPERT, D_MODEL), D_EXPERT),
        'final_norm': gain(ks[17], (D_MODEL,)),
    }


def reference(x, positions, attn_norm, w_in, b_f, q_norm, w_uq, kv_norm, w_ukv, fox_out_norm, mla_out_norm, w_o,
              ffn_norm, w_group, w_router, w_gate, w_up, w_down, final_norm):
    h = x
    for l in range(DEPTH):
        h = h + hybrid_mixer(rms_norm(h, attn_norm[l]), positions, w_in[l], b_f[l], q_norm[l], w_uq[l],
                             kv_norm[l], w_ukv[l], fox_out_norm[l], mla_out_norm[l], w_o[l])
        h = h + hierarchical_moe(rms_norm(h, ffn_norm[l]), w_group[l], w_router[l], w_gate[l], w_up[l], w_down[l])
    return rms_norm(h, final_norm)
```

```python
import functools

import jax
import jax.numpy as jnp
from jax import lax
from jax.experimental import pallas as pl
from jax.experimental.pallas import tpu as pltpu

FOX_HEADS = 8
FOX_HEAD_DIM = 64
FOX_WIDTH = FOX_HEADS * FOX_HEAD_DIM
MLA_HEADS = 8
MLA_NOPE_DIM = 64
MLA_ROPE_DIM = 32
MLA_V_DIM = 64
MLA_WIDTH = MLA_HEADS * MLA_V_DIM
Q_LORA_RANK = 256
KV_LORA_RANK = 128
ROPE_THETA = 10000.0
N_GROUPS = 8
EXPERTS_PER_GROUP = 4
N_EXPERTS = N_GROUPS * EXPERTS_PER_GROUP
D_EXPERT = 256
NORM_EPS = 1e-6

LANES = 128
SUBLANES = 8
VMEM_LIMIT_BYTES = 56 * 1024 * 1024

ROW_TILE = 512
ATTN_TILE = 512
MOE_BLOCK = 512
HEAD_PAD = 128
ROPE_HALF = MLA_ROPE_DIM // 2
NEG_BIG = -1e30

GID_LANE = 0
RANK_LANE = 1
GATE_LANE0 = N_GROUPS


def _cparams(*semantics):
    return pltpu.CompilerParams(dimension_semantics=semantics, vmem_limit_bytes=VMEM_LIMIT_BYTES)


def _rms(x, gain):
    ms = jnp.mean(x * x, axis=-1, keepdims=True)
    return x * lax.rsqrt(ms + NORM_EPS) * gain


def _dot(a, b):
    return jnp.dot(a, b, preferred_element_type=jnp.float32)


def _dot_nt(a, b):
    return lax.dot_general(a, b, (((1,), (1,)), ((), ())), preferred_element_type=jnp.float32)


def _rope_table_kernel(pos_ref, invf_ref, cos_ref, sin_ref):
    ang = pos_ref[...].astype(jnp.float32) * invf_ref[...]
    cos_ref[...] = jnp.cos(ang)
    sin_ref[...] = jnp.sin(ang)


def _rope_tables(positions):
    T = positions.size
    per_row = LANES // ROPE_HALF
    rows = T // per_row
    inv_freq = ROPE_THETA ** (-jnp.arange(ROPE_HALF, dtype=jnp.float32) / ROPE_HALF)
    pos_rep = jnp.repeat(positions.reshape(T), ROPE_HALF).reshape(rows, LANES)
    invf_rep = jnp.tile(inv_freq, per_row).reshape(1, LANES)
    rt = min(rows, 1024)
    cos, sin = pl.pallas_call(
        _rope_table_kernel,
        grid=(rows // rt,),
        in_specs=[pl.BlockSpec((rt, LANES), lambda i: (i, 0)),
                  pl.BlockSpec((1, LANES), lambda i: (0, 0))],
        out_specs=[pl.BlockSpec((rt, LANES), lambda i: (i, 0))] * 2,
        out_shape=[jax.ShapeDtypeStruct((rows, LANES), jnp.float32)] * 2,
        compiler_params=_cparams("parallel"),
        name="rope_tables",
    )(pos_rep, invf_rep)
    return cos.reshape(T, ROPE_HALF), sin.reshape(T, ROPE_HALF)


def _gather_rows(src_hbm, idx_ref, dst_ref, sem, n_rows):
    def issue(r, carry):
        pltpu.make_async_copy(src_hbm.at[pl.ds(idx_ref[0, 0, r], 1)],
                              dst_ref.at[pl.ds(r, 1)], sem).start()
        return carry
    lax.fori_loop(0, n_rows, issue, 0)
    pltpu.make_async_copy(src_hbm.at[pl.ds(0, n_rows)], dst_ref, sem).wait()


def _pre_attn_kernel(*refs, has_moe_in, tt):
    if has_moe_in:
        dest_ref, h_ref, ys_hbm = refs[:3]
        refs = refs[3:]
    else:
        h_ref = refs[0]
        refs = refs[1:]
    (anorm_ref, wmain_ref, wf_ref, bf_ref, qnorm_ref, wqm_ref, wqs_ref, kvnorm_ref,
     wkn_ref, wv_ref, cq_ref, sq_ref, csk_ref) = refs[:13]
    refs = refs[13:]
    if has_moe_in:
        hout_ref = refs[0]
        refs = refs[1:]
    fq_ref, fk_ref, fv_ref, d_ref, qm_ref, km_ref, vm_ref = refs[:7]
    refs = refs[7:]
    carry_ref = refs[0]
    if has_moe_in:
        ybuf_ref, sem = refs[1:3]

    si = pl.program_id(1)

    h = h_ref[...]
    if has_moe_in:
        _gather_rows(ys_hbm, dest_ref, ybuf_ref, sem, tt)
        h = h + ybuf_ref[...]
        hout_ref[...] = h

    xn = _rms(h, anorm_ref[...]).astype(jnp.bfloat16)

    fw = FOX_WIDTH
    fq_ref[...] = (_dot(xn, wmain_ref[:, 0:fw]) * (FOX_HEAD_DIM ** -0.5)).astype(jnp.bfloat16)
    fk_ref[...] = _dot(xn, wmain_ref[:, fw:2 * fw]).astype(jnp.bfloat16)
    fv_ref[...] = _dot(xn, wmain_ref[:, 2 * fw:3 * fw]).astype(jnp.bfloat16)

    z = _dot_nt(wf_ref[...], xn) + bf_ref[...]
    logf = jnp.minimum(z, 0.0) - jnp.log1p(jnp.exp(-jnp.abs(z)))
    lane_t = lax.broadcasted_iota(jnp.int32, logf.shape, 1)
    shift = 1
    while shift < tt:
        logf = logf + jnp.where(lane_t >= shift, pltpu.roll(logf, shift, axis=1), 0.0)
        shift *= 2

    @pl.when(si == 0)
    def _():
        carry_ref[...] = jnp.zeros_like(carry_ref)

    dcum = logf + carry_ref[:, 0:1]
    d_ref[0] = dcum
    carry_ref[...] = jnp.broadcast_to(dcum[:, tt - 1:tt], carry_ref.shape)

    o = 3 * fw
    cq = _rms(_dot(xn, wmain_ref[:, o:o + Q_LORA_RANK]), qnorm_ref[...]).astype(jnp.bfloat16)
    o += Q_LORA_RANK
    q_main = _dot(cq, wqm_ref[...])
    q_swap = _dot(cq, wqs_ref[...])
    cq_t = cq_ref[...]
    sq_t = sq_ref[...]
    scale = (MLA_NOPE_DIM + MLA_ROPE_DIM) ** -0.5
    for hd in range(MLA_HEADS):
        sl = slice(hd * HEAD_PAD, (hd + 1) * HEAD_PAD)
        qm_ref[:, sl] = ((q_main[:, sl] * cq_t + q_swap[:, sl] * sq_t) * scale).astype(jnp.bfloat16)

    ckv = _rms(_dot(xn, wmain_ref[:, o:o + KV_LORA_RANK]), kvnorm_ref[...]).astype(jnp.bfloat16)
    o += KV_LORA_RANK
    vm_ref[...] = _dot(ckv, wv_ref[...]).astype(jnp.bfloat16)
    k_nope = _dot(ckv, wkn_ref[...])
    kr = _dot(xn, wmain_ref[:, o:o + LANES]) * csk_ref[...]
    kr = kr + pltpu.roll(kr, LANES // 2, axis=1)
    lane = lax.broadcasted_iota(jnp.int32, kr.shape, 1)
    k_pe = jnp.where(lane >= MLA_NOPE_DIM, kr, 0.0)
    for hd in range(MLA_HEADS):
        sl = slice(hd * HEAD_PAD, (hd + 1) * HEAD_PAD)
        km_ref[:, sl] = (k_nope[:, sl] + k_pe).astype(jnp.bfloat16)


def _pre_attn(h, moe_in, B, S, lw, tabs):
    T, D = h.shape
    tt = min(ROW_TILE, S)
    ns = S // tt
    has_moe_in = moe_in is not None
    row = lambda b, s: (b * ns + s, 0)
    const = lambda b, s: (0, 0)

    def rows(width):
        return pl.BlockSpec((tt, width), row)

    def full(a):
        return pl.BlockSpec(a.shape, const)

    weights = [lw["attn_norm"], lw["w_main"], lw["w_f"], lw["b_f"], lw["q_norm"], lw["wq_main"],
               lw["wq_swap"], lw["kv_norm"], lw["wk_nope"], lw["wv"]]
    in_specs = [rows(D)]
    args = [h]
    if has_moe_in:
        dest3, ys = moe_in
        in_specs = [pl.BlockSpec((1, 1, tt), lambda b, s: (b * ns + s, 0, 0), memory_space=pltpu.SMEM),
                    rows(D), pl.BlockSpec(memory_space=pl.ANY)]
        args = [dest3, h, ys]
    in_specs += [full(w) for w in weights] + [rows(LANES)] * 3
    args += weights + list(tabs)

    bf16 = jnp.bfloat16
    out_shape = [jax.ShapeDtypeStruct((T, FOX_WIDTH), bf16)] * 3
    out_specs = [rows(FOX_WIDTH)] * 3
    out_shape += [jax.ShapeDtypeStruct((B, FOX_HEADS, S), jnp.float32)]
    out_specs += [pl.BlockSpec((1, FOX_HEADS, tt), lambda b, s: (b, 0, s))]
    out_shape += [jax.ShapeDtypeStruct((T, MLA_HEADS * HEAD_PAD), bf16)] * 2
    out_specs += [rows(MLA_HEADS * HEAD_PAD)] * 2
    out_shape += [jax.ShapeDtypeStruct((T, MLA_WIDTH), bf16)]
    out_specs += [rows(MLA_WIDTH)]
    scratch = [pltpu.VMEM((FOX_HEADS, LANES), jnp.float32)]
    if has_moe_in:
        out_shape = [jax.ShapeDtypeStruct((T, D), jnp.float32)] + out_shape
        out_specs = [rows(D)] + out_specs
        scratch += [pltpu.VMEM((tt, D), jnp.float32), pltpu.SemaphoreType.DMA(())]

    outs = pl.pallas_call(
        functools.partial(_pre_attn_kernel, has_moe_in=has_moe_in, tt=tt),
        grid=(B, ns),
        in_specs=in_specs,
        out_specs=out_specs,
        out_shape=out_shape,
        scratch_shapes=scratch,
        compiler_params=_cparams("arbitrary", "arbitrary"),
        name="pre_attn",
    )(*args)
    if has_moe_in:
        return outs[0], outs[1:]
    return h, outs


def _attn_kernel(*refs, has_decay, width, ta):
    if has_decay:
        q_ref, k_ref, v_ref, d_ref, g_ref, o_ref, m_scr, l_scr, acc_scr = refs
    else:
        q_ref, k_ref, v_ref, g_ref, o_ref, m_scr, l_scr, acc_scr = refs
        d_ref = None
    qi = pl.program_id(2)
    half = LANES // 2
    lane = lax.broadcasted_iota(jnp.int32, (1, LANES), 1)

    q = q_ref[...]
    if width == LANES:
        q_heads = [jnp.where(lane < half, q, jnp.zeros_like(q)), jnp.where(lane >= half, q, jnp.zeros_like(q))]
        k_cols = [slice(0, LANES), slice(0, LANES)]
    else:
        q_heads = [q[:, 0:HEAD_PAD], q[:, HEAD_PAD:2 * HEAD_PAD]]
        k_cols = [slice(0, HEAD_PAD), slice(HEAD_PAD, 2 * HEAD_PAD)]

    m_scr[...] = jnp.full_like(m_scr, NEG_BIG)
    l_scr[...] = jnp.zeros_like(l_scr)
    acc_scr[...] = jnp.zeros_like(acc_scr)

    def tile_update(kt, masked):
        k0 = pl.multiple_of(kt * ta, ta)
        v = v_ref[pl.ds(k0, ta), :]
        for hh in range(2):
            kk = k_ref[pl.ds(k0, ta), k_cols[hh]]
            s = _dot_nt(q_heads[hh], kk)
            if has_decay:
                s = s - d_ref[0, 0, kt, hh:hh + 1, :]
            if masked:
                r = lax.broadcasted_iota(jnp.int32, s.shape, 0)
                c = lax.broadcasted_iota(jnp.int32, s.shape, 1)
                s = jnp.where(c <= r, s, NEG_BIG)
            m_prev = m_scr[hh]
            m_new = jnp.maximum(m_prev, jnp.max(s, axis=1, keepdims=True))
            p = jnp.exp(s - m_new)
            alpha = jnp.exp(m_prev - m_new)
            l_scr[hh] = alpha * l_scr[hh] + jnp.sum(p, axis=1, keepdims=True)
            acc_scr[hh] = alpha * acc_scr[hh] + _dot(p.astype(jnp.bfloat16), v)
            m_scr[hh] = m_new

    def body(kt, carry):
        tile_update(kt, False)
        return carry

    lax.fori_loop(0, qi, body, 0)
    tile_update(qi, True)

    o0 = acc_scr[0] / l_scr[0]
    o1 = acc_scr[1] / l_scr[1]
    first = lane < half
    o = jnp.where(first, o0, o1)
    sq = o * o
    ms0 = jnp.sum(jnp.where(first, sq, 0.0), axis=1, keepdims=True) / half
    ms1 = jnp.sum(jnp.where(first, 0.0, sq), axis=1, keepdims=True) / half
    inv = jnp.where(first, lax.rsqrt(ms0 + NORM_EPS), lax.rsqrt(ms1 + NORM_EPS))
    o_ref[...] = (o * inv * g_ref[...]).astype(o_ref.dtype)


def _attention(q, k, v, dkeys, gain, B, S, width):
    T = q.shape[0]
    ta = min(ATTN_TILE, S)
    nq = S // ta
    n_pairs = q.shape[1] // width
    has_decay = dkeys is not None
    in_specs = [pl.BlockSpec((ta, width), lambda b, j, i: (b * nq + i, j)),
                pl.BlockSpec((S, width), lambda b, j, i: (b, j)),
                pl.BlockSpec((S, LANES), lambda b, j, i: (b, j))]
    args = [q, k, v]
    if has_decay:
        in_specs.append(pl.BlockSpec((1, 1, nq, 2, ta), lambda b, j, i: (b, j, 0, 0, 0)))
        args.append(dkeys)
    in_specs.append(pl.BlockSpec((1, LANES), lambda b, j, i: (0, j)))
    args.append(gain)
    return pl.pallas_call(
        functools.partial(_attn_kernel, has_decay=has_decay, width=width, ta=ta),
        grid=(B, n_pairs, nq),
        in_specs=in_specs,
        out_specs=pl.BlockSpec((ta, LANES), lambda b, j, i: (b * nq + i, j)),
        out_shape=jax.ShapeDtypeStruct((T, n_pairs * LANES), jnp.bfloat16),
        scratch_shapes=[pltpu.VMEM((2, ta, 1), jnp.float32),
                        pltpu.VMEM((2, ta, 1), jnp.float32),
                        pltpu.VMEM((2, ta, LANES), jnp.float32)],
        compiler_params=_cparams("parallel", "parallel", "arbitrary"),
        name="attn_fox" if has_decay else "attn_mla",
    )(*args)


def _post_attn_kernel(fox_ref, mla_ref, h_ref, wo_ref, fnorm_ref, wrh_ref, wrl_ref, tri_ref,
                      h2_ref, xr_ref, route_ref, cnt_ref, carry_ref):
    i = pl.program_id(0)
    D = h_ref.shape[1]
    h2 = h_ref[...] + _dot(fox_ref[...], wo_ref[0:FOX_WIDTH, :]) + _dot(mla_ref[...], wo_ref[FOX_WIDTH:, :])
    h2_ref[...] = h2
    xn = _rms(h2, fnorm_ref[...])
    xr_ref[:, 0:D] = xn

    x_hi = xn.astype(jnp.bfloat16)
    x_lo = (xn - x_hi.astype(jnp.float32)).astype(jnp.bfloat16)
    logits = _dot(x_hi, wrh_ref[...]) + _dot(x_lo, wrh_ref[...]) + _dot(x_hi, wrl_ref[...])

    lane = lax.broadcasted_iota(jnp.int32, logits.shape, 1)
    lane_f = lane.astype(jnp.float32)
    far = float(LANES)

    gl = jnp.where(lane < N_GROUPS, logits, NEG_BIG)
    gmax = jnp.max(gl, axis=1, keepdims=True)
    group_w = 1.0 / jnp.sum(jnp.exp(gl - gmax), axis=1, keepdims=True)
    gid = jnp.min(jnp.where(gl == gmax, lane_f, far), axis=1, keepdims=True)

    lo = GATE_LANE0 + EXPERTS_PER_GROUP * gid
    el = jnp.where((lane_f >= lo) & (lane_f < lo + EXPERTS_PER_GROUP), logits, NEG_BIG)
    e1 = jnp.max(el, axis=1, keepdims=True)
    i1 = jnp.min(jnp.where(el == e1, lane_f, far), axis=1, keepdims=True)
    el2 = jnp.where(lane_f == i1, NEG_BIG, el)
    e2 = jnp.max(el2, axis=1, keepdims=True)
    i2 = jnp.min(jnp.where(el2 == e2, lane_f, far), axis=1, keepdims=True)
    t = jnp.exp(e2 - e1)
    g1 = group_w / (1.0 + t)
    g2 = group_w * t / (1.0 + t)
    gates = jnp.where(lane_f == i1, g1, jnp.where(lane_f == i2, g2, 0.0))

    @pl.when(i == 0)
    def _():
        carry_ref[...] = jnp.zeros_like(carry_ref)

    onehot = lane_f == gid
    prefix = _dot(tri_ref[...], jnp.where(onehot, 1.0, 0.0).astype(jnp.bfloat16)) + carry_ref[...]
    rank = jnp.sum(jnp.where(onehot, prefix, 0.0), axis=1, keepdims=True)
    carry = carry_ref[...] + jnp.sum(jnp.where(onehot, 1.0, 0.0), axis=0, keepdims=True)
    carry_ref[...] = carry
    cnt_ref[...] = carry

    info = gates + jnp.where(lane == GID_LANE, gid, 0.0) + jnp.where(lane == RANK_LANE, rank, 0.0)
    xr_ref[:, D:D + LANES] = info
    route_ref[...] = info


def _post_attn(fox, mla, h, lw):
    T, D = h.shape
    tt = min(ROW_TILE, T)
    row = lambda i: (i, 0)
    const = lambda i: (0, 0)
    tri = (jnp.arange(tt)[:, None] > jnp.arange(tt)[None, :]).astype(jnp.bfloat16)
    weights = [lw["w_o"], lw["ffn_norm"], lw["wr_hi"], lw["wr_lo"], tri]
    return pl.pallas_call(
        _post_attn_kernel,
        grid=(T // tt,),
        in_specs=[pl.BlockSpec((tt, FOX_WIDTH), row), pl.BlockSpec((tt, MLA_WIDTH), row),
                  pl.BlockSpec((tt, D), row)] + [pl.BlockSpec(w.shape, const) for w in weights],
        out_specs=[pl.BlockSpec((tt, D), row), pl.BlockSpec((tt, D + LANES), row),
                   pl.BlockSpec((tt, LANES), row), pl.BlockSpec((1, LANES), const)],
        out_shape=[jax.ShapeDtypeStruct((T, D), jnp.float32),
                   jax.ShapeDtypeStruct((T, D + LANES), jnp.float32),
                   jax.ShapeDtypeStruct((T, LANES), jnp.float32),
                   jax.ShapeDtypeStruct((1, LANES), jnp.float32)],
        scratch_shapes=[pltpu.VMEM((1, LANES), jnp.float32)],
        compiler_params=_cparams("arbitrary"),
        name="post_attn",
    )(fox, mla, h, *weights)


def _dispatch_kernel(dest_ref, xr_ref, init_hbm, xs_hbm, sem, *, tt):
    del init_hbm

    def issue(r, carry):
        pltpu.make_async_copy(xr_ref.at[pl.ds(r, 1)], xs_hbm.at[pl.ds(dest_ref[0, 0, r], 1)], sem).start()
        return carry
    lax.fori_loop(0, tt, issue, 0)
    pltpu.make_async_copy(xr_ref, xs_hbm.at[pl.ds(0, tt)], sem).wait()


def _dispatch(xr, dest3, n_sorted):
    T, W = xr.shape
    tt = dest3.shape[2]
    zeros = jnp.zeros((n_sorted, W), xr.dtype)
    return pl.pallas_call(
        functools.partial(_dispatch_kernel, tt=tt),
        grid=(T // tt,),
        in_specs=[pl.BlockSpec((1, 1, tt), lambda i: (i, 0, 0), memory_space=pltpu.SMEM),
                  pl.BlockSpec((tt, W), lambda i: (i, 0)),
                  pl.BlockSpec(memory_space=pl.ANY)],
        out_specs=pl.BlockSpec(memory_space=pl.ANY),
        out_shape=jax.ShapeDtypeStruct((n_sorted, W), xr.dtype),
        scratch_shapes=[pltpu.SemaphoreType.DMA(())],
        input_output_aliases={2: 0},
        compiler_params=_cparams("arbitrary"),
        name="moe_dispatch",
    )(dest3, xr, zeros)


def _moe_kernel(bg_ref, xs_ref, wg_ref, wu_ref, wd_ref, ys_ref):
    D = ys_ref.shape[1]
    g = bg_ref[pl.program_id(0)]
    x = xs_ref[:, 0:D].astype(jnp.bfloat16)
    info = xs_ref[:, D:D + LANES]
    lane = lax.broadcasted_iota(jnp.int32, info.shape, 1)
    y = jnp.zeros(ys_ref.shape, jnp.float32)
    for j in range(EXPERTS_PER_GROUP):
        gate = jnp.sum(jnp.where(lane == GATE_LANE0 + EXPERTS_PER_GROUP * g + j, info, 0.0),
                       axis=1, keepdims=True)
        a = _dot(x, wg_ref[j])
        u = _dot(x, wu_ref[j])
        hj = (a * jax.nn.sigmoid(a) * u).astype(jnp.bfloat16)
        y = y + gate * _dot(hj, wd_ref[j])
    ys_ref[...] = y


def _moe(xs, block_group, lw):
    n_sorted, W = xs.shape
    D = W - LANES
    nb = n_sorted // MOE_BLOCK
    epg = EXPERTS_PER_GROUP
    grid_spec = pltpu.PrefetchScalarGridSpec(
        num_scalar_prefetch=1,
        grid=(nb,),
        in_specs=[pl.BlockSpec((MOE_BLOCK, W), lambda b, bg: (b, 0)),
                  pl.BlockSpec((epg, D, D_EXPERT), lambda b, bg: (bg[b], 0, 0)),
                  pl.BlockSpec((epg, D, D_EXPERT), lambda b, bg: (bg[b], 0, 0)),
                  pl.BlockSpec((epg, D_EXPERT, D), lambda b, bg: (bg[b], 0, 0))],
        out_specs=pl.BlockSpec((MOE_BLOCK, D), lambda b, bg: (b, 0)),
    )
    return pl.pallas_call(
        _moe_kernel,
        grid_spec=grid_spec,
        out_shape=jax.ShapeDtypeStruct((n_sorted, D), jnp.float32),
        compiler_params=_cparams("arbitrary"),
        name="moe_experts",
    )(block_group, xs, lw["w_gate"], lw["w_up"], lw["w_down"])


def _final_kernel(dest_ref, h_ref, ys_hbm, gain_ref, o_ref, ybuf_ref, sem, *, tt):
    _gather_rows(ys_hbm, dest_ref, ybuf_ref, sem, tt)
    o_ref[...] = _rms(h_ref[...] + ybuf_ref[...], gain_ref[...])


def _final(h, dest3, ys, gain):
    T, D = h.shape
    tt = dest3.shape[2]
    return pl.pallas_call(
        functools.partial(_final_kernel, tt=tt),
        grid=(T // tt,),
        in_specs=[pl.BlockSpec((1, 1, tt), lambda i: (i, 0, 0), memory_space=pltpu.SMEM),
                  pl.BlockSpec((tt, D), lambda i: (i, 0)),
                  pl.BlockSpec(memory_space=pl.ANY),
                  pl.BlockSpec((1, D), lambda i: (0, 0))],
        out_specs=pl.BlockSpec((tt, D), lambda i: (i, 0)),
        out_shape=jax.ShapeDtypeStruct((T, D), jnp.float32),
        scratch_shapes=[pltpu.VMEM((tt, D), jnp.float32), pltpu.SemaphoreType.DMA(())],
        compiler_params=_cparams("arbitrary"),
        name="final_norm",
    )(dest3, h, ys, gain)


def _prep_layer(l, w_in, b_f, w_uq, w_ukv, w_o, w_group, w_router, w_gate, w_up, w_down,
                attn_norm, q_norm, kv_norm, ffn_norm):
    bf16 = jnp.bfloat16
    D = w_in.shape[1]
    fw = FOX_WIDTH
    o_f = 3 * fw
    o_cq = o_f + FOX_HEADS
    o_ckv = o_cq + Q_LORA_RANK
    o_kr = o_ckv + KV_LORA_RANK
    wi = w_in[l]
    kr = wi[:, o_kr:o_kr + MLA_ROPE_DIM]
    kr_swap = jnp.concatenate([-kr[:, ROPE_HALF:], kr[:, :ROPE_HALF]], axis=1)
    zpad = jnp.zeros((D, MLA_ROPE_DIM), wi.dtype)
    w_main = jnp.concatenate([wi[:, 0:o_f], wi[:, o_cq:o_kr], kr_swap, zpad, kr, zpad], axis=1)

    qd = MLA_NOPE_DIM + MLA_ROPE_DIM
    wq = w_uq[l].reshape(Q_LORA_RANK, MLA_HEADS, qd)
    zq = jnp.zeros((Q_LORA_RANK, MLA_HEADS, HEAD_PAD - qd), wq.dtype)
    wq_main = jnp.concatenate([wq, zq], axis=2)
    x1 = wq[:, :, MLA_NOPE_DIM:MLA_NOPE_DIM + ROPE_HALF]
    x2 = wq[:, :, MLA_NOPE_DIM + ROPE_HALF:]
    wq_swap = jnp.concatenate([jnp.zeros_like(wq[:, :, :MLA_NOPE_DIM]), -x2, x1, zq], axis=2)

    wkv = w_ukv[l].reshape(KV_LORA_RANK, MLA_HEADS, MLA_NOPE_DIM + MLA_V_DIM)
    wk_nope = jnp.concatenate([wkv[:, :, :MLA_NOPE_DIM],
                               jnp.zeros((KV_LORA_RANK, MLA_HEADS, HEAD_PAD - MLA_NOPE_DIM), wkv.dtype)], axis=2)
    wv = wkv[:, :, MLA_NOPE_DIM:]

    wr = jnp.concatenate([w_group[l], w_router[l],
                          jnp.zeros((D, LANES - N_GROUPS - N_EXPERTS), w_group.dtype)], axis=1)
    wr_hi = wr.astype(bf16)
    wr_lo = (wr - wr_hi.astype(jnp.float32)).astype(bf16)

    return dict(
        attn_norm=attn_norm[l].reshape(1, D),
        w_main=w_main.astype(bf16),
        w_f=wi[:, o_f:o_cq].T.astype(bf16),
        b_f=b_f[l].reshape(FOX_HEADS, 1),
        q_norm=q_norm[l].reshape(1, Q_LORA_RANK),
        wq_main=wq_main.reshape(Q_LORA_RANK, MLA_HEADS * HEAD_PAD).astype(bf16),
        wq_swap=wq_swap.reshape(Q_LORA_RANK, MLA_HEADS * HEAD_PAD).astype(bf16),
        kv_norm=kv_norm[l].reshape(1, KV_LORA_RANK),
        wk_nope=wk_nope.reshape(KV_LORA_RANK, MLA_HEADS * HEAD_PAD).astype(bf16),
        wv=wv.reshape(KV_LORA_RANK, MLA_WIDTH).astype(bf16),
        w_o=w_o[l].astype(bf16),
        ffn_norm=ffn_norm[l].reshape(1, D),
        wr_hi=wr_hi,
        wr_lo=wr_lo,
        w_gate=w_gate[l].astype(bf16),
        w_up=w_up[l].astype(bf16),
        w_down=w_down[l].astype(bf16),
    )


def _routing_plan(route, counts, n_blocks):
    gid = route[:, GID_LANE].astype(jnp.int32)
    rank = route[:, RANK_LANE].astype(jnp.int32)
    cnt = counts[0, :N_GROUPS].astype(jnp.int32)
    padded = (cnt + MOE_BLOCK - 1) // MOE_BLOCK * MOE_BLOCK
    pend = jnp.cumsum(padded)
    pstart = pend - padded
    dest = pstart[gid] + rank
    block_group = jnp.clip(
        jnp.searchsorted(pend, jnp.arange(n_blocks, dtype=jnp.int32) * MOE_BLOCK, side="right"),
        0, N_GROUPS - 1).astype(jnp.int32)
    return dest, block_group


def kernel(x, positions, attn_norm, w_in, b_f, q_norm, w_uq, kv_norm, w_ukv, fox_out_norm, mla_out_norm, w_o,
           ffn_norm, w_group, w_router, w_gate, w_up, w_down, final_norm):
    B, S, D = x.shape
    T = B * S
    depth = w_in.shape[0]
    ta = min(ATTN_TILE, S)
    tt = min(ROW_TILE, S)
    n_blocks = -(-(T + N_GROUPS * (MOE_BLOCK - 1)) // MOE_BLOCK)

    cos, sin = _rope_tables(positions)
    one = jnp.ones((T, MLA_NOPE_DIM), jnp.float32)
    zero32 = jnp.zeros((T, MLA_ROPE_DIM), jnp.float32)
    cq_tab = jnp.concatenate([one, cos, cos, zero32], axis=1)
    sq_tab = jnp.concatenate([jnp.zeros_like(one), sin, sin, zero32], axis=1)
    csk_tab = jnp.concatenate([sin, sin, zero32, cos, cos, zero32], axis=1)
    tabs = (cq_tab, sq_tab, csk_tab)

    h = x.reshape(T, D)
    moe_in = None
    for l in range(depth):
        lw = _prep_layer(l, w_in, b_f, w_uq, w_ukv, w_o, w_group, w_router, w_gate, w_up, w_down,
                         attn_norm, q_norm, kv_norm, ffn_norm)
        h, (fq, fk, fv, dcum, qm, km, vm) = _pre_attn(h, moe_in, B, S, lw, tabs)
        dkeys = dcum.reshape(B, FOX_HEADS // 2, 2, S // ta, ta).transpose(0, 1, 3, 2, 4)
        fox = _attention(fq, fk, fv, dkeys, fox_out_norm[l].reshape(1, FOX_WIDTH), B, S, LANES)
        mla = _attention(qm, km, vm, None, mla_out_norm[l].reshape(1, MLA_WIDTH), B, S, 2 * HEAD_PAD)
        h, xr, route, counts = _post_attn(fox, mla, h, lw)
        dest, block_group = _routing_plan(route, counts, n_blocks)
        dest3 = dest.reshape(T // tt, 1, tt)
        xs = _dispatch(xr, dest3, n_blocks * MOE_BLOCK)
        ys = _moe(xs, block_group, lw)
        moe_in = (dest3, ys)
    out = _final(h, moe_in[0], moe_in[1], final_norm.reshape(1, D))
    return out.reshape(B, S, D)
```

```python
import functools
import math

import jax
import jax.numpy as jnp
from jax import lax
from jax.experimental import pallas as pl
from jax.experimental.pallas import tpu as pltpu

FOX_HEADS = 8
FOX_HEAD_DIM = 64
FOX_WIDTH = FOX_HEADS * FOX_HEAD_DIM
MLA_HEADS = 8
MLA_NOPE_DIM = 64
MLA_ROPE_DIM = 32
MLA_V_DIM = 64
MLA_WIDTH = MLA_HEADS * MLA_V_DIM
Q_LORA_RANK = 256
KV_LORA_RANK = 128
ROPE_THETA = 10000.0
N_GROUPS = 8
EXPERTS_PER_GROUP = 4
N_EXPERTS = N_GROUPS * EXPERTS_PER_GROUP
D_EXPERT = 256
NORM_EPS = 1e-6

LANES = 128
SUBLANES = 8
BF16_SUBLANES = 16
VMEM_LIMIT_BYTES = 56 * 1024 * 1024

ROW_TILE = 512
ATTN_TILE = 512
MOE_BLOCK = 512
HEAD_PAD = 128
V_ROWS = 2 * MLA_V_DIM + BF16_SUBLANES
ROPE_HALF = MLA_ROPE_DIM // 2
NEG_BIG = -1e30
LOG2E = math.log2(math.e)
DECAY_PARTS = 3

GID_LANE = 0
RANK_LANE = 1
GATE_LANE0 = N_GROUPS


def _cparams(*semantics):
    return pltpu.CompilerParams(dimension_semantics=semantics, vmem_limit_bytes=VMEM_LIMIT_BYTES)


def _rms(x, gain):
    ms = jnp.mean(x * x, axis=-1, keepdims=True)
    return x * lax.rsqrt(ms + NORM_EPS) * gain


def _dot(a, b):
    return jnp.dot(a, b, preferred_element_type=jnp.float32)


def _dot_nt(a, b):
    return lax.dot_general(a, b, (((1,), (1,)), ((), ())), preferred_element_type=jnp.float32)


def _split_bf16(x, parts):
    out = []
    for _ in range(parts - 1):
        piece = x.astype(jnp.bfloat16).astype(jnp.float32)
        out.append(piece)
        x = x - piece
    out.append(x.astype(jnp.bfloat16).astype(jnp.float32))
    return out


def _rope_table_kernel(pos_ref, invf_ref, cos_ref, sin_ref):
    ang = pos_ref[...].astype(jnp.float32) * invf_ref[...]
    cos_ref[...] = jnp.cos(ang)
    sin_ref[...] = jnp.sin(ang)


def _rope_tables(positions):
    T = positions.size
    inv_freq = ROPE_THETA ** (-jnp.arange(ROPE_HALF, dtype=jnp.float32) / ROPE_HALF)
    ct = min(T, 4096)
    return pl.pallas_call(
        _rope_table_kernel,
        grid=(T // ct,),
        in_specs=[pl.BlockSpec((1, ct), lambda i: (0, i)),
                  pl.BlockSpec((ROPE_HALF, 1), lambda i: (0, 0))],
        out_specs=[pl.BlockSpec((ROPE_HALF, ct), lambda i: (0, i))] * 2,
        out_shape=[jax.ShapeDtypeStruct((ROPE_HALF, T), jnp.float32)] * 2,
        compiler_params=_cparams("parallel"),
        name="rope_tables",
    )(positions.reshape(1, T), inv_freq.reshape(ROPE_HALF, 1))


def _gather_rows(src_hbm, idx_ref, dst_ref, sem, n_rows):
    def issue(r, carry):
        pltpu.make_async_copy(src_hbm.at[pl.ds(idx_ref[0, 0, r], 1)],
                              dst_ref.at[pl.ds(r, 1)], sem).start()
        return carry
    lax.fori_loop(0, n_rows, issue, 0)
    pltpu.make_async_copy(src_hbm.at[pl.ds(0, n_rows)], dst_ref, sem).wait()


def _pre_attn_kernel(*refs, has_moe_in, tt):
    if has_moe_in:
        dest_ref, h_ref, ys_hbm = refs[:3]
        refs = refs[3:]
    else:
        h_ref = refs[0]
        refs = refs[1:]
    (anorm_ref, wrow_ref, wfq_ref, wfv_ref, bf_ref, tri_ref, place_ref, qnorm_ref, wqm_ref, wqs_ref,
     kvnorm_ref, wkn_ref, wv_ref, cq_ref, sq_ref, csk_ref) = refs[:16]
    refs = refs[16:]
    if has_moe_in:
        hout_ref = refs[0]
        refs = refs[1:]
    fq_ref, fk_ref, fv_ref, qm_ref, km_ref, vm_ref = refs[:6]
    refs = refs[6:]
    carry_ref = refs[0]
    if has_moe_in:
        ybuf_ref, sem = refs[1:3]

    si = pl.program_id(1)
    bf16 = jnp.bfloat16
    half = LANES // 2

    h = h_ref[...]
    if has_moe_in:
        _gather_rows(ys_hbm, dest_ref, ybuf_ref, sem, tt)
        h = h + ybuf_ref[...]
        hout_ref[...] = h

    xn = _rms(h, anorm_ref[...]).astype(bf16)
    lane = lax.broadcasted_iota(jnp.int32, (tt, LANES), 1)
    sub = lax.broadcasted_iota(jnp.int32, (BF16_SUBLANES, tt), 0)
    ones_rows = jnp.ones((BF16_SUBLANES, tt), bf16)

    fq_t = _dot_nt(wfq_ref[...], xn) * (FOX_HEAD_DIM ** -0.5 * LOG2E)
    q_aug = jnp.where(sub < DECAY_PARTS, -1.0, 0.0).astype(bf16)
    q_zero = jnp.zeros((HEAD_PAD - FOX_HEAD_DIM - BF16_SUBLANES, tt), bf16)
    for hd in range(FOX_HEADS):
        fq_ref[0, hd, 0:FOX_HEAD_DIM, :] = fq_t[hd * FOX_HEAD_DIM:(hd + 1) * FOX_HEAD_DIM].astype(bf16)
        fq_ref[0, hd, FOX_HEAD_DIM:FOX_HEAD_DIM + BF16_SUBLANES, :] = q_aug
        fq_ref[0, hd, FOX_HEAD_DIM + BF16_SUBLANES:, :] = q_zero
    fv_t = _dot_nt(wfv_ref[...], xn)
    for p in range(FOX_HEADS // 2):
        fv_ref[0, p, 0, 0:LANES, :] = fv_t[p * LANES:(p + 1) * LANES].astype(bf16)
        fv_ref[0, p, 0, LANES:, :] = ones_rows

    fw = FOX_WIDTH
    o = fw
    z = _dot(xn, wrow_ref[:, o:o + LANES]) + bf_ref[...]
    o += LANES
    logf = jnp.minimum(z, 0.0) - jnp.log1p(jnp.exp(-jnp.abs(z)))
    pieces = jnp.concatenate([p_.astype(bf16) for p_ in _split_bf16(logf, DECAY_PARTS)], axis=1)
    csum = _dot(tri_ref[...], pieces)

    @pl.when(si == 0)
    def _():
        carry_ref[...] = jnp.zeros_like(carry_ref)

    dcum = csum[:, 0:LANES] + csum[:, LANES:2 * LANES] + csum[:, 2 * LANES:] + carry_ref[...]
    carry_ref[...] = dcum[tt - 1:tt, :]
    d_parts = _split_bf16(dcum * LOG2E, DECAY_PARTS)
    d_sel = jnp.where(lane < FOX_HEADS, d_parts[0],
                      jnp.where(lane < 2 * FOX_HEADS, d_parts[1],
                                jnp.where(lane < 3 * FOX_HEADS, d_parts[2], 0.0)))
    k_aug = _dot(d_sel.astype(bf16), place_ref[...])

    fk = _dot(xn, wrow_ref[:, 0:fw])
    for p in range(FOX_HEADS // 2):
        blk = fk[:, p * LANES:(p + 1) * LANES]
        for hh, b_ in ((0, blk), (1, pltpu.roll(blk, half, axis=1))):
            sl = slice((2 * p + hh) * HEAD_PAD, (2 * p + hh + 1) * HEAD_PAD)
            fk_ref[:, sl] = jnp.where(lane < FOX_HEAD_DIM, b_, k_aug[:, sl]).astype(bf16)

    cq = _rms(_dot(xn, wrow_ref[:, o:o + Q_LORA_RANK]), qnorm_ref[...]).astype(bf16)
    o += Q_LORA_RANK
    qm_t = _dot_nt(wqm_ref[...], cq)
    qs_t = _dot_nt(wqs_ref[...], cq)
    cq_t = cq_ref[...]
    sq_t = sq_ref[...]
    scale = (MLA_NOPE_DIM + MLA_ROPE_DIM) ** -0.5 * LOG2E
    for hd in range(MLA_HEADS):
        sl = slice(hd * HEAD_PAD, (hd + 1) * HEAD_PAD)
        qm_ref[0, hd] = ((qm_t[sl] * cq_t + qs_t[sl] * sq_t) * scale).astype(bf16)

    ckv = _rms(_dot(xn, wrow_ref[:, o:o + KV_LORA_RANK]), kvnorm_ref[...]).astype(bf16)
    o += KV_LORA_RANK
    mv_t = _dot_nt(wv_ref[...], ckv)
    for p in range(MLA_HEADS // 2):
        vm_ref[0, p, 0, 0:LANES, :] = mv_t[p * LANES:(p + 1) * LANES].astype(bf16)
        vm_ref[0, p, 0, LANES:, :] = ones_rows
    k_nope = _dot(ckv, wkn_ref[...])
    kr = _dot(xn, wrow_ref[:, o:o + LANES]) * csk_ref[...]
    kr = kr + pltpu.roll(kr, half, axis=1)
    k_pe = jnp.where(lane >= MLA_NOPE_DIM, kr, 0.0)
    for hd in range(MLA_HEADS):
        sl = slice(hd * HEAD_PAD, (hd + 1) * HEAD_PAD)
        km_ref[:, sl] = (k_nope[:, sl] + k_pe).astype(bf16)


def _pre_attn(h, moe_in, B, S, lw, tabs):
    T, D = h.shape
    tt = min(ROW_TILE, S)
    ns = S // tt
    has_moe_in = moe_in is not None
    row = lambda b, s: (b * ns + s, 0)
    const = lambda b, s: (0, 0)
    cq_tab, sq_tab, csk_tab = tabs

    def rows(width):
        return pl.BlockSpec((tt, width), row)

    def full(a):
        return pl.BlockSpec(a.shape, const)

    weights = [lw["attn_norm"], lw["w_rows"], lw["w_fq_t"], lw["w_fv_t"], lw["b_f"], lw["tri_incl"],
               lw["place"], lw["q_norm"], lw["wq_main_t"], lw["wq_swap_t"], lw["kv_norm"],
               lw["wk_nope"], lw["wv_t"]]
    in_specs = [rows(D)]
    args = [h]
    if has_moe_in:
        dest3, ys = moe_in
        in_specs = [pl.BlockSpec((1, 1, tt), lambda b, s: (b * ns + s, 0, 0), memory_space=pltpu.SMEM),
                    rows(D), pl.BlockSpec(memory_space=pl.ANY)]
        args = [dest3, h, ys]
    col = lambda b, s: (0, b * ns + s)
    in_specs += [full(w) for w in weights]
    in_specs += [pl.BlockSpec((HEAD_PAD, tt), col), pl.BlockSpec((HEAD_PAD, tt), col), rows(LANES)]
    args += weights + [cq_tab, sq_tab, csk_tab]

    bf16 = jnp.bfloat16
    qt_shape = jax.ShapeDtypeStruct((B, FOX_HEADS, HEAD_PAD, S), bf16)
    qt_spec = pl.BlockSpec((1, FOX_HEADS, HEAD_PAD, tt), lambda b, s: (b, 0, 0, s))
    k_shape = jax.ShapeDtypeStruct((T, FOX_HEADS * HEAD_PAD), bf16)
    k_spec = rows(FOX_HEADS * HEAD_PAD)
    vt_shape = jax.ShapeDtypeStruct((B, FOX_HEADS // 2, ns, V_ROWS, tt), bf16)
    vt_spec = pl.BlockSpec((1, FOX_HEADS // 2, 1, V_ROWS, tt), lambda b, s: (b, 0, s, 0, 0))
    out_shape = [qt_shape, k_shape, vt_shape] * 2
    out_specs = [qt_spec, k_spec, vt_spec] * 2
    scratch = [pltpu.VMEM((1, LANES), jnp.float32)]
    if has_moe_in:
        out_shape = [jax.ShapeDtypeStruct((T, D), jnp.float32)] + out_shape
        out_specs = [rows(D)] + out_specs
        scratch += [pltpu.VMEM((tt, D), jnp.float32), pltpu.SemaphoreType.DMA(())]

    outs = pl.pallas_call(
        functools.partial(_pre_attn_kernel, has_moe_in=has_moe_in, tt=tt),
        grid=(B, ns),
        in_specs=in_specs,
        out_specs=out_specs,
        out_shape=out_shape,
        scratch_shapes=scratch,
        compiler_params=_cparams("arbitrary", "arbitrary"),
        name="pre_attn",
    )(*args)
    if has_moe_in:
        return outs[0], outs[1:]
    return h, outs


def _attn_kernel(qt_ref, k_ref, vt_ref, g_ref, o_ref, m_scr, acc_scr, *, ta):
    qi = pl.program_id(2)
    vd = MLA_V_DIM
    qts = [qt_ref[0, 0], qt_ref[0, 1]]

    m_scr[...] = jnp.full_like(m_scr, NEG_BIG)
    acc_scr[...] = jnp.zeros_like(acc_scr)

    def tile_update(kt, masked):
        k0 = pl.multiple_of(kt * ta, ta)
        vt = vt_ref[0, 0, kt]
        for hh in range(2):
            kk = k_ref[pl.ds(k0, ta), hh * HEAD_PAD:(hh + 1) * HEAD_PAD]
            s = _dot(kk, qts[hh])
            if masked:
                r = lax.broadcasted_iota(jnp.int32, s.shape, 0)
                c = lax.broadcasted_iota(jnp.int32, s.shape, 1)
                s = jnp.where(r <= c, s, NEG_BIG)
            m_prev = m_scr[hh]
            m_new = jnp.maximum(m_prev, jnp.max(s, axis=0, keepdims=True))
            alpha = jnp.exp2(m_prev - m_new)
            p = jnp.exp2(s - m_new).astype(jnp.bfloat16)
            acc_scr[hh] = alpha * acc_scr[hh] + _dot(vt, p)
            m_scr[hh] = m_new

    def body(kt, carry):
        tile_update(kt, False)
        return carry

    lax.fori_loop(0, qi, body, 0)
    tile_update(qi, True)

    outs = []
    for hh in range(2):
        acc = acc_scr[hh]
        oh = acc[hh * vd:(hh + 1) * vd] / acc[2 * vd:2 * vd + 1]
        ms = jnp.mean(oh * oh, axis=0, keepdims=True)
        outs.append(oh * lax.rsqrt(ms + NORM_EPS))
    o_t = jnp.concatenate(outs, axis=0) * g_ref[...]
    o_ref[...] = o_t.T.astype(o_ref.dtype)


def _attention(qt, k, vt, gain_rep, B, S, name):
    T = k.shape[0]
    ta = min(ATTN_TILE, S)
    nq = S // ta
    n_pairs = qt.shape[1] // 2
    return pl.pallas_call(
        functools.partial(_attn_kernel, ta=ta),
        grid=(B, n_pairs, nq),
        in_specs=[pl.BlockSpec((1, 2, HEAD_PAD, ta), lambda b, j, i: (b, j, 0, i)),
                  pl.BlockSpec((S, 2 * HEAD_PAD), lambda b, j, i: (b, j)),
                  pl.BlockSpec((1, 1, nq, V_ROWS, ta), lambda b, j, i: (b, j, 0, 0, 0)),
                  pl.BlockSpec((LANES, ta), lambda b, j, i: (j, 0))],
        out_specs=pl.BlockSpec((ta, LANES), lambda b, j, i: (b * nq + i, j)),
        out_shape=jax.ShapeDtypeStruct((T, n_pairs * LANES), jnp.bfloat16),
        scratch_shapes=[pltpu.VMEM((2, 1, ta), jnp.float32),
                        pltpu.VMEM((2, V_ROWS, ta), jnp.float32)],
        compiler_params=_cparams("parallel", "parallel", "arbitrary"),
        name=name,
    )(qt, k, vt, gain_rep)


def _post_attn_kernel(fox_ref, mla_ref, h_ref, wo_ref, fnorm_ref, wrh_ref, wrl_ref, tri_ref,
                      h2_ref, xr_ref, route_ref, cnt_ref, carry_ref):
    i = pl.program_id(0)
    D = h_ref.shape[1]
    h2 = h_ref[...] + _dot(fox_ref[...], wo_ref[0:FOX_WIDTH, :]) + _dot(mla_ref[...], wo_ref[FOX_WIDTH:, :])
    h2_ref[...] = h2
    xn = _rms(h2, fnorm_ref[...])
    xr_ref[:, 0:D] = xn

    x_hi = xn.astype(jnp.bfloat16)
    x_lo = (xn - x_hi.astype(jnp.float32)).astype(jnp.bfloat16)
    logits = _dot(x_hi, wrh_ref[...]) + _dot(x_lo, wrh_ref[...]) + _dot(x_hi, wrl_ref[...])

    lane = lax.broadcasted_iota(jnp.int32, logits.shape, 1)
    lane_f = lane.astype(jnp.float32)
    far = float(LANES)

    gl = jnp.where(lane < N_GROUPS, logits, NEG_BIG)
    gmax = jnp.max(gl, axis=1, keepdims=True)
    group_w = 1.0 / jnp.sum(jnp.exp(gl - gmax), axis=1, keepdims=True)
    gid = jnp.min(jnp.where(gl == gmax, lane_f, far), axis=1, keepdims=True)

    lo = GATE_LANE0 + EXPERTS_PER_GROUP * gid
    el = jnp.where((lane_f >= lo) & (lane_f < lo + EXPERTS_PER_GROUP), logits, NEG_BIG)
    e1 = jnp.max(el, axis=1, keepdims=True)
    i1 = jnp.min(jnp.where(el == e1, lane_f, far), axis=1, keepdims=True)
    el2 = jnp.where(lane_f == i1, NEG_BIG, el)
    e2 = jnp.max(el2, axis=1, keepdims=True)
    i2 = jnp.min(jnp.where(el2 == e2, lane_f, far), axis=1, keepdims=True)
    t = jnp.exp(e2 - e1)
    g1 = group_w / (1.0 + t)
    g2 = group_w * t / (1.0 + t)
    gates = jnp.where(lane_f == i1, g1, jnp.where(lane_f == i2, g2, 0.0))

    @pl.when(i == 0)
    def _():
        carry_ref[...] = jnp.zeros_like(carry_ref)

    onehot = lane_f == gid
    prefix = _dot(tri_ref[...], jnp.where(onehot, 1.0, 0.0).astype(jnp.bfloat16)) + carry_ref[...]
    rank = jnp.sum(jnp.where(onehot, prefix, 0.0), axis=1, keepdims=True)
    carry = carry_ref[...] + jnp.sum(jnp.where(onehot, 1.0, 0.0), axis=0, keepdims=True)
    carry_ref[...] = carry
    cnt_ref[...] = carry

    info = gates + jnp.where(lane == GID_LANE, gid, 0.0) + jnp.where(lane == RANK_LANE, rank, 0.0)
    xr_ref[:, D:D + LANES] = info
    route_ref[...] = info


def _post_attn(fox, mla, h, lw):
    T, D = h.shape
    tt = min(ROW_TILE, T)
    row = lambda i: (i, 0)
    const = lambda i: (0, 0)
    weights = [lw["w_o"], lw["ffn_norm"], lw["wr_hi"], lw["wr_lo"], lw["tri_excl"]]
    return pl.pallas_call(
        _post_attn_kernel,
        grid=(T // tt,),
        in_specs=[pl.BlockSpec((tt, FOX_WIDTH), row), pl.BlockSpec((tt, MLA_WIDTH), row),
                  pl.BlockSpec((tt, D), row)] + [pl.BlockSpec(w.shape, const) for w in weights],
        out_specs=[pl.BlockSpec((tt, D), row), pl.BlockSpec((tt, D + LANES), row),
                   pl.BlockSpec((tt, LANES), row), pl.BlockSpec((1, LANES), const)],
        out_shape=[jax.ShapeDtypeStruct((T, D), jnp.float32),
                   jax.ShapeDtypeStruct((T, D + LANES), jnp.float32),
                   jax.ShapeDtypeStruct((T, LANES), jnp.float32),
                   jax.ShapeDtypeStruct((1, LANES), jnp.float32)],
        scratch_shapes=[pltpu.VMEM((1, LANES), jnp.float32)],
        compiler_params=_cparams("arbitrary"),
        name="post_attn",
    )(fox, mla, h, *weights)


def _dispatch_kernel(dest_ref, xr_ref, init_hbm, xs_hbm, sem, *, tt):
    del init_hbm

    def issue(r, carry):
        pltpu.make_async_copy(xr_ref.at[pl.ds(r, 1)], xs_hbm.at[pl.ds(dest_ref[0, 0, r], 1)], sem).start()
        return carry
    lax.fori_loop(0, tt, issue, 0)
    pltpu.make_async_copy(xr_ref, xs_hbm.at[pl.ds(0, tt)], sem).wait()


def _dispatch(xr, dest3, n_sorted):
    T, W = xr.shape
    tt = dest3.shape[2]
    zeros = jnp.zeros((n_sorted, W), xr.dtype)
    return pl.pallas_call(
        functools.partial(_dispatch_kernel, tt=tt),
        grid=(T // tt,),
        in_specs=[pl.BlockSpec((1, 1, tt), lambda i: (i, 0, 0), memory_space=pltpu.SMEM),
                  pl.BlockSpec((tt, W), lambda i: (i, 0)),
                  pl.BlockSpec(memory_space=pl.ANY)],
        out_specs=pl.BlockSpec(memory_space=pl.ANY),
        out_shape=jax.ShapeDtypeStruct((n_sorted, W), xr.dtype),
        scratch_shapes=[pltpu.SemaphoreType.DMA(())],
        input_output_aliases={2: 0},
        compiler_params=_cparams("arbitrary"),
        name="moe_dispatch",
    )(dest3, xr, zeros)


def _moe_kernel(bg_ref, xs_ref, wg_ref, wu_ref, wd_ref, ys_ref):
    D = ys_ref.shape[1]
    g = bg_ref[pl.program_id(0)]
    x = xs_ref[:, 0:D].astype(jnp.bfloat16)
    info = xs_ref[:, D:D + LANES]
    lane = lax.broadcasted_iota(jnp.int32, info.shape, 1)
    y = jnp.zeros(ys_ref.shape, jnp.float32)
    for j in range(EXPERTS_PER_GROUP):
        gate = jnp.sum(jnp.where(lane == GATE_LANE0 + EXPERTS_PER_GROUP * g + j, info, 0.0),
                       axis=1, keepdims=True)
        a = _dot(x, wg_ref[j])
        u = _dot(x, wu_ref[j])
        hj = (a * jax.nn.sigmoid(a) * u).astype(jnp.bfloat16)
        y = y + gate * _dot(hj, wd_ref[j])
    ys_ref[...] = y


def _moe(xs, block_group, lw):
    n_sorted, W = xs.shape
    D = W - LANES
    nb = n_sorted // MOE_BLOCK
    epg = EXPERTS_PER_GROUP
    grid_spec = pltpu.PrefetchScalarGridSpec(
        num_scalar_prefetch=1,
        grid=(nb,),
        in_specs=[pl.BlockSpec((MOE_BLOCK, W), lambda b, bg: (b, 0)),
                  pl.BlockSpec((epg, D, D_EXPERT), lambda b, bg: (bg[b], 0, 0)),
                  pl.BlockSpec((epg, D, D_EXPERT), lambda b, bg: (bg[b], 0, 0)),
                  pl.BlockSpec((epg, D_EXPERT, D), lambda b, bg: (bg[b], 0, 0))],
        out_specs=pl.BlockSpec((MOE_BLOCK, D), lambda b, bg: (b, 0)),
    )
    return pl.pallas_call(
        _moe_kernel,
        grid_spec=grid_spec,
        out_shape=jax.ShapeDtypeStruct((n_sorted, D), jnp.float32),
        compiler_params=_cparams("arbitrary"),
        name="moe_experts",
    )(block_group, xs, lw["w_gate"], lw["w_up"], lw["w_down"])


def _final_kernel(dest_ref, h_ref, ys_hbm, gain_ref, o_ref, ybuf_ref, sem, *, tt):
    _gather_rows(ys_hbm, dest_ref, ybuf_ref, sem, tt)
    o_ref[...] = _rms(h_ref[...] + ybuf_ref[...], gain_ref[...])


def _final(h, dest3, ys, gain):
    T, D = h.shape
    tt = dest3.shape[2]
    return pl.pallas_call(
        functools.partial(_final_kernel, tt=tt),
        grid=(T // tt,),
        in_specs=[pl.BlockSpec((1, 1, tt), lambda i: (i, 0, 0), memory_space=pltpu.SMEM),
                  pl.BlockSpec((tt, D), lambda i: (i, 0)),
                  pl.BlockSpec(memory_space=pl.ANY),
                  pl.BlockSpec((1, D), lambda i: (0, 0))],
        out_specs=pl.BlockSpec((tt, D), lambda i: (i, 0)),
        out_shape=jax.ShapeDtypeStruct((T, D), jnp.float32),
        scratch_shapes=[pltpu.VMEM((tt, D), jnp.float32), pltpu.SemaphoreType.DMA(())],
        compiler_params=_cparams("arbitrary"),
        name="final_norm",
    )(dest3, h, ys, gain)


def _prep_layer(l, tt, w_in, b_f, w_uq, w_ukv, w_o, w_group, w_router, w_gate, w_up, w_down,
                attn_norm, q_norm, kv_norm, ffn_norm):
    bf16 = jnp.bfloat16
    D = w_in.shape[1]
    fw = FOX_WIDTH
    o_f = 3 * fw
    o_cq = o_f + FOX_HEADS
    o_ckv = o_cq + Q_LORA_RANK
    o_kr = o_ckv + KV_LORA_RANK
    wi = w_in[l]
    w_f = wi[:, o_f:o_cq]
    w_f3 = jnp.concatenate([w_f] * DECAY_PARTS + [jnp.zeros((D, LANES - DECAY_PARTS * FOX_HEADS), wi.dtype)], axis=1)
    b_f3 = jnp.concatenate([b_f[l]] * DECAY_PARTS + [jnp.zeros((LANES - DECAY_PARTS * FOX_HEADS,), b_f.dtype)])
    kr = wi[:, o_kr:o_kr + MLA_ROPE_DIM]
    kr_swap = jnp.concatenate([-kr[:, ROPE_HALF:], kr[:, :ROPE_HALF]], axis=1)
    zpad = jnp.zeros((D, MLA_ROPE_DIM), wi.dtype)
    w_rows = jnp.concatenate([wi[:, fw:2 * fw], w_f3, wi[:, o_cq:o_kr], kr_swap, zpad, kr, zpad], axis=1)

    src = jnp.arange(LANES)
    dst = (src % FOX_HEADS) * HEAD_PAD + FOX_HEAD_DIM + src // FOX_HEADS
    place = ((jnp.arange(FOX_HEADS * HEAD_PAD)[None, :] == dst[:, None])
             & (src[:, None] < DECAY_PARTS * FOX_HEADS)).astype(bf16)

    qd = MLA_NOPE_DIM + MLA_ROPE_DIM
    wq = w_uq[l].reshape(Q_LORA_RANK, MLA_HEADS, qd)
    zq = jnp.zeros((Q_LORA_RANK, MLA_HEADS, HEAD_PAD - qd), wq.dtype)
    wq_main = jnp.concatenate([wq, zq], axis=2).reshape(Q_LORA_RANK, MLA_HEADS * HEAD_PAD)
    x1 = wq[:, :, MLA_NOPE_DIM:MLA_NOPE_DIM + ROPE_HALF]
    x2 = wq[:, :, MLA_NOPE_DIM + ROPE_HALF:]
    wq_swap = jnp.concatenate([jnp.zeros_like(wq[:, :, :MLA_NOPE_DIM]), -x2, x1, zq], axis=2)
    wq_swap = wq_swap.reshape(Q_LORA_RANK, MLA_HEADS * HEAD_PAD)

    wkv = w_ukv[l].reshape(KV_LORA_RANK, MLA_HEADS, MLA_NOPE_DIM + MLA_V_DIM)
    wk_nope = jnp.concatenate([wkv[:, :, :MLA_NOPE_DIM],
                               jnp.zeros((KV_LORA_RANK, MLA_HEADS, HEAD_PAD - MLA_NOPE_DIM), wkv.dtype)], axis=2)
    wv = wkv[:, :, MLA_NOPE_DIM:].reshape(KV_LORA_RANK, MLA_WIDTH)

    wr = jnp.concatenate([w_group[l], w_router[l],
                          jnp.zeros((D, LANES - N_GROUPS - N_EXPERTS), w_group.dtype)], axis=1)
    wr_hi = wr.astype(bf16)
    wr_lo = (wr - wr_hi.astype(jnp.float32)).astype(bf16)

    ar = jnp.arange(tt)
    return dict(
        attn_norm=attn_norm[l].reshape(1, D),
        w_rows=w_rows.astype(bf16),
        w_fq_t=wi[:, 0:fw].T.astype(bf16),
        w_fv_t=wi[:, 2 * fw:3 * fw].T.astype(bf16),
        b_f=b_f3.reshape(1, LANES),
        tri_incl=(ar[:, None] >= ar[None, :]).astype(bf16),
        tri_excl=(ar[:, None] > ar[None, :]).astype(bf16),
        place=place,
        q_norm=q_norm[l].reshape(1, Q_LORA_RANK),
        wq_main_t=wq_main.T.astype(bf16),
        wq_swap_t=wq_swap.T.astype(bf16),
        kv_norm=kv_norm[l].reshape(1, KV_LORA_RANK),
        wk_nope=wk_nope.reshape(KV_LORA_RANK, MLA_HEADS * HEAD_PAD).astype(bf16),
        wv_t=wv.T.astype(bf16),
        w_o=w_o[l].astype(bf16),
        ffn_norm=ffn_norm[l].reshape(1, D),
        wr_hi=wr_hi,
        wr_lo=wr_lo,
        w_gate=w_gate[l].astype(bf16),
        w_up=w_up[l].astype(bf16),
        w_down=w_down[l].astype(bf16),
    )


def _routing_plan(route, counts, n_blocks):
    gid = route[:, GID_LANE].astype(jnp.int32)
    rank = route[:, RANK_LANE].astype(jnp.int32)
    cnt = counts[0, :N_GROUPS].astype(jnp.int32)
    padded = (cnt + MOE_BLOCK - 1) // MOE_BLOCK * MOE_BLOCK
    pend = jnp.cumsum(padded)
    pstart = pend - padded
    dest = pstart[gid] + rank
    block_start = jnp.arange(n_blocks, dtype=jnp.int32) * MOE_BLOCK
    block_group = jnp.sum((pend[None, :] <= block_start[:, None]).astype(jnp.int32), axis=1)
    return dest, jnp.minimum(block_group, N_GROUPS - 1)


def kernel(x, positions, attn_norm, w_in, b_f, q_norm, w_uq, kv_norm, w_ukv, fox_out_norm, mla_out_norm, w_o,
           ffn_norm, w_group, w_router, w_gate, w_up, w_down, final_norm):
    B, S, D = x.shape
    T = B * S
    depth = w_in.shape[0]
    ta = min(ATTN_TILE, S)
    tt = min(ROW_TILE, S)
    n_blocks = -(-(T + N_GROUPS * (MOE_BLOCK - 1)) // MOE_BLOCK)

    cos_t, sin_t = _rope_tables(positions)
    ones_t = jnp.ones((MLA_NOPE_DIM, T), jnp.float32)
    zero_t = jnp.zeros((HEAD_PAD - MLA_NOPE_DIM - MLA_ROPE_DIM, T), jnp.float32)
    cq_tab = jnp.concatenate([ones_t, cos_t, cos_t, zero_t], axis=0)
    sq_tab = jnp.concatenate([jnp.zeros_like(ones_t), sin_t, sin_t, zero_t], axis=0)
    cos, sin = cos_t.T, sin_t.T
    zero32 = jnp.zeros((T, MLA_ROPE_DIM), jnp.float32)
    csk_tab = jnp.concatenate([sin, sin, zero32, cos, cos, zero32], axis=1)
    tabs = (cq_tab, sq_tab, csk_tab)

    h = x.reshape(T, D)
    moe_in = None
    for l in range(depth):
        lw = _prep_layer(l, tt, w_in, b_f, w_uq, w_ukv, w_o, w_group, w_router, w_gate, w_up, w_down,
                         attn_norm, q_norm, kv_norm, ffn_norm)
        h, (fq_t, fk, fv_t, mq_t, mk, mv_t) = _pre_attn(h, moe_in, B, S, lw, tabs)
        fox_gain = jnp.broadcast_to(fox_out_norm[l].reshape(FOX_WIDTH, 1), (FOX_WIDTH, ta))
        mla_gain = jnp.broadcast_to(mla_out_norm[l].reshape(MLA_WIDTH, 1), (MLA_WIDTH, ta))
        fox = _attention(fq_t, fk, fv_t, fox_gain, B, S, "attn_fox")
        mla = _attention(mq_t, mk, mv_t, mla_gain, B, S, "attn_mla")
        h, xr, route, counts = _post_attn(fox, mla, h, lw)
        dest, block_group = _routing_plan(route, counts, n_blocks)
        dest3 = dest.reshape(T // tt, 1, tt)
        xs = _dispatch(xr, dest3, n_blocks * MOE_BLOCK)
        ys = _moe(xs, block_group, lw)
        moe_in = (dest3, ys)
    out = _final(h, moe_in[0], moe_in[1], final_norm.reshape(1, D))
    return out.reshape(B, S, D)
```

```python
import functools
import math

import jax
import jax.numpy as jnp
from jax import lax
from jax.experimental import pallas as pl
from jax.experimental.pallas import tpu as pltpu

FOX_HEADS = 8
FOX_HEAD_DIM = 64
FOX_WIDTH = FOX_HEADS * FOX_HEAD_DIM
MLA_HEADS = 8
MLA_NOPE_DIM = 64
MLA_ROPE_DIM = 32
MLA_V_DIM = 64
MLA_WIDTH = MLA_HEADS * MLA_V_DIM
Q_LORA_RANK = 256
KV_LORA_RANK = 128
ROPE_THETA = 10000.0
N_GROUPS = 8
EXPERTS_PER_GROUP = 4
N_EXPERTS = N_GROUPS * EXPERTS_PER_GROUP
D_EXPERT = 256
NORM_EPS = 1e-6

LANES = 128
SUBLANES = 8
BF16_SUBLANES = 16
VMEM_LIMIT_BYTES = 56 * 1024 * 1024

ROW_TILE = 512
ATTN_TILE = 512
MOE_BLOCK = 512
ROW_DMA_UNROLL = 8
HEAD_PAD = 128
V_ROWS = 2 * MLA_V_DIM + BF16_SUBLANES
ROPE_HALF = MLA_ROPE_DIM // 2
NEG_BIG = -1e30
LOG2E = math.log2(math.e)
DECAY_PARTS = 3

GID_LANE = 0
RANK_LANE = 1
GATE_LANE0 = N_GROUPS


def _cparams(*semantics):
    return pltpu.CompilerParams(dimension_semantics=semantics, vmem_limit_bytes=VMEM_LIMIT_BYTES)


def _rms(x, gain):
    ms = jnp.mean(x * x, axis=-1, keepdims=True)
    return x * lax.rsqrt(ms + NORM_EPS) * gain


def _dot(a, b):
    return jnp.dot(a, b, preferred_element_type=jnp.float32)


def _dot_nt(a, b):
    return lax.dot_general(a, b, (((1,), (1,)), ((), ())), preferred_element_type=jnp.float32)


def _split_bf16(x, parts):
    out = []
    for _ in range(parts - 1):
        piece = x.astype(jnp.bfloat16).astype(jnp.float32)
        out.append(piece)
        x = x - piece
    out.append(x.astype(jnp.bfloat16).astype(jnp.float32))
    return out


def _rope_table_kernel(pos_ref, invf_ref, cos_ref, sin_ref):
    ang = pos_ref[...].astype(jnp.float32) * invf_ref[...]
    cos_ref[...] = jnp.cos(ang)
    sin_ref[...] = jnp.sin(ang)


def _rope_tables(positions):
    T = positions.size
    inv_freq = ROPE_THETA ** (-jnp.arange(ROPE_HALF, dtype=jnp.float32) / ROPE_HALF)
    ct = min(T, 4096)
    return pl.pallas_call(
        _rope_table_kernel,
        grid=(T // ct,),
        in_specs=[pl.BlockSpec((1, ct), lambda i: (0, i)),
                  pl.BlockSpec((ROPE_HALF, 1), lambda i: (0, 0))],
        out_specs=[pl.BlockSpec((ROPE_HALF, ct), lambda i: (0, i))] * 2,
        out_shape=[jax.ShapeDtypeStruct((ROPE_HALF, T), jnp.float32)] * 2,
        compiler_params=_cparams("parallel"),
        name="rope_tables",
    )(positions.reshape(1, T), inv_freq.reshape(ROPE_HALF, 1))


def _gather_rows(src_hbm, idx_ref, dst_ref, sem, n_rows):
    def issue(r, carry):
        pltpu.make_async_copy(src_hbm.at[pl.ds(idx_ref[0, 0, r], 1)],
                              dst_ref.at[pl.ds(r, 1)], sem).start()
        return carry
    lax.fori_loop(0, n_rows, issue, 0, unroll=ROW_DMA_UNROLL)
    pltpu.make_async_copy(src_hbm.at[pl.ds(0, n_rows)], dst_ref, sem).wait()


def _pre_attn_kernel(*refs, has_moe_in, tt):
    if has_moe_in:
        dest_ref, h_ref, ys_hbm = refs[:3]
        refs = refs[3:]
    else:
        h_ref = refs[0]
        refs = refs[1:]
    (anorm_ref, wrow_ref, wfq_ref, wfv_ref, bf_ref, tri_ref, place_ref, qnorm_ref, wqm_ref, wqs_ref,
     kvnorm_ref, wkn_ref, wv_ref, cq_ref, sq_ref, csk_ref) = refs[:16]
    refs = refs[16:]
    if has_moe_in:
        hout_ref = refs[0]
        refs = refs[1:]
    fq_ref, fk_ref, fv_ref, qm_ref, km_ref, vm_ref = refs[:6]
    refs = refs[6:]
    carry_ref = refs[0]
    if has_moe_in:
        ybuf_ref, sem = refs[1:3]

    si = pl.program_id(1)
    bf16 = jnp.bfloat16
    half = LANES // 2

    h = h_ref[...]
    if has_moe_in:
        _gather_rows(ys_hbm, dest_ref, ybuf_ref, sem, tt)
        h = h + ybuf_ref[...]
        hout_ref[...] = h

    xn = _rms(h, anorm_ref[...]).astype(bf16)
    lane = lax.broadcasted_iota(jnp.int32, (tt, LANES), 1)
    sub = lax.broadcasted_iota(jnp.int32, (BF16_SUBLANES, tt), 0)
    ones_rows = jnp.ones((BF16_SUBLANES, tt), bf16)

    fq_t = _dot_nt(wfq_ref[...], xn) * (FOX_HEAD_DIM ** -0.5 * LOG2E)
    q_aug = jnp.where(sub < DECAY_PARTS, -1.0, 0.0).astype(bf16)
    q_zero = jnp.zeros((HEAD_PAD - FOX_HEAD_DIM - BF16_SUBLANES, tt), bf16)
    for hd in range(FOX_HEADS):
        fq_ref[0, hd, 0:FOX_HEAD_DIM, :] = fq_t[hd * FOX_HEAD_DIM:(hd + 1) * FOX_HEAD_DIM].astype(bf16)
        fq_ref[0, hd, FOX_HEAD_DIM:FOX_HEAD_DIM + BF16_SUBLANES, :] = q_aug
        fq_ref[0, hd, FOX_HEAD_DIM + BF16_SUBLANES:, :] = q_zero
    fv_t = _dot_nt(wfv_ref[...], xn)
    for p in range(FOX_HEADS // 2):
        fv_ref[0, p, 0, 0:LANES, :] = fv_t[p * LANES:(p + 1) * LANES].astype(bf16)
        fv_ref[0, p, 0, LANES:, :] = ones_rows

    fw = FOX_WIDTH
    o = fw
    z = _dot(xn, wrow_ref[:, o:o + LANES]) + bf_ref[...]
    o += LANES
    logf = jnp.minimum(z, 0.0) - jnp.log1p(jnp.exp(-jnp.abs(z)))
    pieces = jnp.concatenate([p_.astype(bf16) for p_ in _split_bf16(logf, DECAY_PARTS)], axis=1)
    csum = _dot(tri_ref[...], pieces)

    @pl.when(si == 0)
    def _():
        carry_ref[...] = jnp.zeros_like(carry_ref)

    dcum = csum[:, 0:LANES] + csum[:, LANES:2 * LANES] + csum[:, 2 * LANES:] + carry_ref[...]
    carry_ref[...] = dcum[tt - 1:tt, :]
    d_parts = _split_bf16(dcum * LOG2E, DECAY_PARTS)
    d_sel = jnp.where(lane < FOX_HEADS, d_parts[0],
                      jnp.where(lane < 2 * FOX_HEADS, d_parts[1],
                                jnp.where(lane < 3 * FOX_HEADS, d_parts[2], 0.0)))
    k_aug = _dot(d_sel.astype(bf16), place_ref[...])

    fk = _dot(xn, wrow_ref[:, 0:fw])
    for p in range(FOX_HEADS // 2):
        blk = fk[:, p * LANES:(p + 1) * LANES]
        for hh, b_ in ((0, blk), (1, pltpu.roll(blk, half, axis=1))):
            sl = slice((2 * p + hh) * HEAD_PAD, (2 * p + hh + 1) * HEAD_PAD)
            fk_ref[:, sl] = jnp.where(lane < FOX_HEAD_DIM, b_, k_aug[:, sl]).astype(bf16)

    cq = _rms(_dot(xn, wrow_ref[:, o:o + Q_LORA_RANK]), qnorm_ref[...]).astype(bf16)
    o += Q_LORA_RANK
    qm_t = _dot_nt(wqm_ref[...], cq)
    qs_t = _dot_nt(wqs_ref[...], cq)
    cq_t = cq_ref[...]
    sq_t = sq_ref[...]
    scale = (MLA_NOPE_DIM + MLA_ROPE_DIM) ** -0.5 * LOG2E
    for hd in range(MLA_HEADS):
        sl = slice(hd * HEAD_PAD, (hd + 1) * HEAD_PAD)
        qm_ref[0, hd] = ((qm_t[sl] * cq_t + qs_t[sl] * sq_t) * scale).astype(bf16)

    ckv = _rms(_dot(xn, wrow_ref[:, o:o + KV_LORA_RANK]), kvnorm_ref[...]).astype(bf16)
    o += KV_LORA_RANK
    mv_t = _dot_nt(wv_ref[...], ckv)
    for p in range(MLA_HEADS // 2):
        vm_ref[0, p, 0, 0:LANES, :] = mv_t[p * LANES:(p + 1) * LANES].astype(bf16)
        vm_ref[0, p, 0, LANES:, :] = ones_rows
    k_nope = _dot(ckv, wkn_ref[...])
    kr = _dot(xn, wrow_ref[:, o:o + LANES]) * csk_ref[...]
    kr = kr + pltpu.roll(kr, half, axis=1)
    k_pe = jnp.where(lane >= MLA_NOPE_DIM, kr, 0.0)
    for hd in range(MLA_HEADS):
        sl = slice(hd * HEAD_PAD, (hd + 1) * HEAD_PAD)
        km_ref[:, sl] = (k_nope[:, sl] + k_pe).astype(bf16)


def _pre_attn(h, moe_in, B, S, lw, tabs):
    T, D = h.shape
    tt = min(ROW_TILE, S)
    ns = S // tt
    has_moe_in = moe_in is not None
    row = lambda b, s: (b * ns + s, 0)
    const = lambda b, s: (0, 0)
    cq_tab, sq_tab, csk_tab = tabs

    def rows(width):
        return pl.BlockSpec((tt, width), row)

    def full(a):
        return pl.BlockSpec(a.shape, const)

    weights = [lw["attn_norm"], lw["w_rows"], lw["w_fq_t"], lw["w_fv_t"], lw["b_f"], lw["tri_incl"],
               lw["place"], lw["q_norm"], lw["wq_main_t"], lw["wq_swap_t"], lw["kv_norm"],
               lw["wk_nope"], lw["wv_t"]]
    in_specs = [rows(D)]
    args = [h]
    if has_moe_in:
        dest3, ys = moe_in
        in_specs = [pl.BlockSpec((1, 1, tt), lambda b, s: (b * ns + s, 0, 0), memory_space=pltpu.SMEM),
                    rows(D), pl.BlockSpec(memory_space=pl.ANY)]
        args = [dest3, h, ys]
    col = lambda b, s: (0, b * ns + s)
    in_specs += [full(w) for w in weights]
    in_specs += [pl.BlockSpec((HEAD_PAD, tt), col), pl.BlockSpec((HEAD_PAD, tt), col), rows(LANES)]
    args += weights + [cq_tab, sq_tab, csk_tab]

    bf16 = jnp.bfloat16
    qt_shape = jax.ShapeDtypeStruct((B, FOX_HEADS, HEAD_PAD, S), bf16)
    qt_spec = pl.BlockSpec((1, FOX_HEADS, HEAD_PAD, tt), lambda b, s: (b, 0, 0, s))
    k_shape = jax.ShapeDtypeStruct((T, FOX_HEADS * HEAD_PAD), bf16)
    k_spec = rows(FOX_HEADS * HEAD_PAD)
    vt_shape = jax.ShapeDtypeStruct((B, FOX_HEADS // 2, ns, V_ROWS, tt), bf16)
    vt_spec = pl.BlockSpec((1, FOX_HEADS // 2, 1, V_ROWS, tt), lambda b, s: (b, 0, s, 0, 0))
    out_shape = [qt_shape, k_shape, vt_shape] * 2
    out_specs = [qt_spec, k_spec, vt_spec] * 2
    scratch = [pltpu.VMEM((1, LANES), jnp.float32)]
    if has_moe_in:
        out_shape = [jax.ShapeDtypeStruct((T, D), jnp.float32)] + out_shape
        out_specs = [rows(D)] + out_specs
        scratch += [pltpu.VMEM((tt, D), jnp.float32), pltpu.SemaphoreType.DMA(())]

    outs = pl.pallas_call(
        functools.partial(_pre_attn_kernel, has_moe_in=has_moe_in, tt=tt),
        grid=(B, ns),
        in_specs=in_specs,
        out_specs=out_specs,
        out_shape=out_shape,
        scratch_shapes=scratch,
        compiler_params=_cparams("arbitrary", "arbitrary"),
        name="pre_attn",
    )(*args)
    if has_moe_in:
        return outs[0], outs[1:]
    return h, outs


def _attn_kernel(qt_ref, k_ref, vt_ref, g_ref, o_ref, s_scr, mc_scr, m_scr, acc_scr, *, ta):
    n = pl.program_id(2)
    vd = MLA_V_DIM

    m_scr[...] = jnp.full_like(m_scr, NEG_BIG)
    acc_scr[...] = jnp.zeros_like(acc_scr)

    def produce(kt, slot, masked):
        k0 = pl.multiple_of(kt * ta, ta)
        for hh in range(2):
            kk = k_ref[pl.ds(k0, ta), hh * HEAD_PAD:(hh + 1) * HEAD_PAD]
            s = _dot(kk, qt_ref[0, hh])
            if masked:
                r = lax.broadcasted_iota(jnp.int32, s.shape, 0)
                c = lax.broadcasted_iota(jnp.int32, s.shape, 1)
                s = jnp.where(r <= c, s, NEG_BIG)
            s_scr[slot, hh] = s
            mc_scr[slot, hh] = jnp.max(s, axis=0, keepdims=True)

    def consume(kt, slot):
        vt = vt_ref[0, 0, kt]
        for hh in range(2):
            m_prev = m_scr[hh]
            m_new = jnp.maximum(m_prev, mc_scr[slot, hh])
            alpha = jnp.exp2(m_prev - m_new)
            p = jnp.exp2(s_scr[slot, hh] - m_new).astype(jnp.bfloat16)
            acc_scr[hh] = alpha * acc_scr[hh] + _dot(vt, p)
            m_scr[hh] = m_new

    @pl.when(n == 0)
    def _():
        produce(0, 0, True)

    @pl.when(n > 0)
    def _():
        produce(0, 0, False)

    def body(t, carry):
        for par in range(2):
            @pl.when(t % 2 == par)
            def _():
                produce(t + 1, 1 - par, False)
                consume(t, par)
        return carry

    lax.fori_loop(0, n - 1, body, 0)

    for par in range(2):
        @pl.when((n > 0) & (n % 2 == par))
        def _():
            produce(n, par, True)
            consume(n - 1, 1 - par)

    for par in range(2):
        @pl.when(n % 2 == par)
        def _():
            consume(n, par)

    outs = []
    for hh in range(2):
        acc = acc_scr[hh]
        oh = acc[hh * vd:(hh + 1) * vd] / acc[2 * vd:2 * vd + 1]
        ms = jnp.mean(oh * oh, axis=0, keepdims=True)
        outs.append(oh * lax.rsqrt(ms + NORM_EPS))
    o_t = jnp.concatenate(outs, axis=0) * g_ref[...]
    o_ref[...] = o_t.T.astype(o_ref.dtype)


def _attention(qt, k, vt, gain_rep, B, S, name):
    T = k.shape[0]
    ta = min(ATTN_TILE, S)
    nq = S // ta
    n_pairs = qt.shape[1] // 2
    return pl.pallas_call(
        functools.partial(_attn_kernel, ta=ta),
        grid=(B, n_pairs, nq),
        in_specs=[pl.BlockSpec((1, 2, HEAD_PAD, ta), lambda b, j, i: (b, j, 0, i)),
                  pl.BlockSpec((S, 2 * HEAD_PAD), lambda b, j, i: (b, j)),
                  pl.BlockSpec((1, 1, nq, V_ROWS, ta), lambda b, j, i: (b, j, 0, 0, 0)),
                  pl.BlockSpec((LANES, ta), lambda b, j, i: (j, 0))],
        out_specs=pl.BlockSpec((ta, LANES), lambda b, j, i: (b * nq + i, j)),
        out_shape=jax.ShapeDtypeStruct((T, n_pairs * LANES), jnp.bfloat16),
        scratch_shapes=[pltpu.VMEM((2, 2, ta, ta), jnp.float32),
                        pltpu.VMEM((2, 2, 1, ta), jnp.float32),
                        pltpu.VMEM((2, 1, ta), jnp.float32),
                        pltpu.VMEM((2, V_ROWS, ta), jnp.float32)],
        compiler_params=_cparams("parallel", "parallel", "arbitrary"),
        name=name,
    )(qt, k, vt, gain_rep)


def _post_attn_kernel(fox_ref, mla_ref, h_ref, wo_ref, fnorm_ref, wrh_ref, wrl_ref, tri_ref,
                      h2_ref, xr_ref, route_ref, cnt_ref, carry_ref):
    i = pl.program_id(0)
    D = h_ref.shape[1]
    h2 = h_ref[...] + _dot(fox_ref[...], wo_ref[0:FOX_WIDTH, :]) + _dot(mla_ref[...], wo_ref[FOX_WIDTH:, :])
    h2_ref[...] = h2
    xn = _rms(h2, fnorm_ref[...])
    xr_ref[:, 0:D] = xn

    x_hi = xn.astype(jnp.bfloat16)
    x_lo = (xn - x_hi.astype(jnp.float32)).astype(jnp.bfloat16)
    logits = _dot(x_hi, wrh_ref[...]) + _dot(x_lo, wrh_ref[...]) + _dot(x_hi, wrl_ref[...])

    lane = lax.broadcasted_iota(jnp.int32, logits.shape, 1)
    lane_f = lane.astype(jnp.float32)
    far = float(LANES)

    gl = jnp.where(lane < N_GROUPS, logits, NEG_BIG)
    gmax = jnp.max(gl, axis=1, keepdims=True)
    group_w = 1.0 / jnp.sum(jnp.exp(gl - gmax), axis=1, keepdims=True)
    gid = jnp.min(jnp.where(gl == gmax, lane_f, far), axis=1, keepdims=True)

    lo = GATE_LANE0 + EXPERTS_PER_GROUP * gid
    el = jnp.where((lane_f >= lo) & (lane_f < lo + EXPERTS_PER_GROUP), logits, NEG_BIG)
    e1 = jnp.max(el, axis=1, keepdims=True)
    i1 = jnp.min(jnp.where(el == e1, lane_f, far), axis=1, keepdims=True)
    el2 = jnp.where(lane_f == i1, NEG_BIG, el)
    e2 = jnp.max(el2, axis=1, keepdims=True)
    i2 = jnp.min(jnp.where(el2 == e2, lane_f, far), axis=1, keepdims=True)
    t = jnp.exp(e2 - e1)
    g1 = group_w / (1.0 + t)
    g2 = group_w * t / (1.0 + t)
    gates = jnp.where(lane_f == i1, g1, jnp.where(lane_f == i2, g2, 0.0))

    @pl.when(i == 0)
    def _():
        carry_ref[...] = jnp.zeros_like(carry_ref)

    onehot = lane_f == gid
    prefix = _dot(tri_ref[...], jnp.where(onehot, 1.0, 0.0).astype(jnp.bfloat16)) + carry_ref[...]
    rank = jnp.sum(jnp.where(onehot, prefix, 0.0), axis=1, keepdims=True)
    carry = carry_ref[...] + jnp.sum(jnp.where(onehot, 1.0, 0.0), axis=0, keepdims=True)
    carry_ref[...] = carry
    cnt_ref[...] = carry

    info = gates + jnp.where(lane == GID_LANE, gid, 0.0) + jnp.where(lane == RANK_LANE, rank, 0.0)
    xr_ref[:, D:D + LANES] = info
    route_ref[...] = info


def _post_attn(fox, mla, h, lw):
    T, D = h.shape
    tt = min(ROW_TILE, T)
    row = lambda i: (i, 0)
    const = lambda i: (0, 0)
    weights = [lw["w_o"], lw["ffn_norm"], lw["wr_hi"], lw["wr_lo"], lw["tri_excl"]]
    return pl.pallas_call(
        _post_attn_kernel,
        grid=(T // tt,),
        in_specs=[pl.BlockSpec((tt, FOX_WIDTH), row), pl.BlockSpec((tt, MLA_WIDTH), row),
                  pl.BlockSpec((tt, D), row)] + [pl.BlockSpec(w.shape, const) for w in weights],
        out_specs=[pl.BlockSpec((tt, D), row), pl.BlockSpec((tt, D + LANES), row),
                   pl.BlockSpec((tt, LANES), row), pl.BlockSpec((1, LANES), const)],
        out_shape=[jax.ShapeDtypeStruct((T, D), jnp.float32),
                   jax.ShapeDtypeStruct((T, D + LANES), jnp.float32),
                   jax.ShapeDtypeStruct((T, LANES), jnp.float32),
                   jax.ShapeDtypeStruct((1, LANES), jnp.float32)],
        scratch_shapes=[pltpu.VMEM((1, LANES), jnp.float32)],
        compiler_params=_cparams("arbitrary"),
        name="post_attn",
    )(fox, mla, h, *weights)


def _dispatch_kernel(dest_ref, xr_ref, init_hbm, xs_hbm, sem, *, tt):
    del init_hbm

    def issue(r, carry):
        pltpu.make_async_copy(xr_ref.at[pl.ds(r, 1)], xs_hbm.at[pl.ds(dest_ref[0, 0, r], 1)], sem).start()
        return carry
    lax.fori_loop(0, tt, issue, 0, unroll=ROW_DMA_UNROLL)
    pltpu.make_async_copy(xr_ref, xs_hbm.at[pl.ds(0, tt)], sem).wait()


def _dispatch(xr, dest3, n_sorted):
    T, W = xr.shape
    tt = dest3.shape[2]
    zeros = jnp.zeros((n_sorted, W), xr.dtype)
    return pl.pallas_call(
        functools.partial(_dispatch_kernel, tt=tt),
        grid=(T // tt,),
        in_specs=[pl.BlockSpec((1, 1, tt), lambda i: (i, 0, 0), memory_space=pltpu.SMEM),
                  pl.BlockSpec((tt, W), lambda i: (i, 0)),
                  pl.BlockSpec(memory_space=pl.ANY)],
        out_specs=pl.BlockSpec(memory_space=pl.ANY),
        out_shape=jax.ShapeDtypeStruct((n_sorted, W), xr.dtype),
        scratch_shapes=[pltpu.SemaphoreType.DMA(())],
        input_output_aliases={2: 0},
        compiler_params=_cparams("arbitrary"),
        name="moe_dispatch",
    )(dest3, xr, zeros)


def _moe_kernel(bg_ref, xs_ref, wg32_ref, wu32_ref, wd32_ref, ys_ref, wg_ref, wu_ref, wd_ref):
    D = ys_ref.shape[1]
    b = pl.program_id(0)
    g = bg_ref[b]

    @pl.when((b == 0) | (g != bg_ref[jnp.maximum(b - 1, 0)]))
    def _():
        wg_ref[...] = wg32_ref[...].astype(jnp.bfloat16)
        wu_ref[...] = wu32_ref[...].astype(jnp.bfloat16)
        wd_ref[...] = wd32_ref[...].astype(jnp.bfloat16)

    x = xs_ref[:, 0:D].astype(jnp.bfloat16)
    info = xs_ref[:, D:D + LANES]
    lane = lax.broadcasted_iota(jnp.int32, info.shape, 1)
    y = jnp.zeros(ys_ref.shape, jnp.float32)
    for j in range(EXPERTS_PER_GROUP):
        gate = jnp.sum(jnp.where(lane == GATE_LANE0 + EXPERTS_PER_GROUP * g + j, info, 0.0),
                       axis=1, keepdims=True)
        a = _dot(x, wg_ref[j])
        u = _dot(x, wu_ref[j])
        hj = (a * jax.nn.sigmoid(a) * u).astype(jnp.bfloat16)
        y = y + gate * _dot(hj, wd_ref[j])
    ys_ref[...] = y


def _moe(xs, block_group, layer, w_gate, w_up, w_down):
    n_sorted, W = xs.shape
    D = W - LANES
    nb = n_sorted // MOE_BLOCK
    epg = EXPERTS_PER_GROUP
    grid_spec = pltpu.PrefetchScalarGridSpec(
        num_scalar_prefetch=1,
        grid=(nb,),
        in_specs=[pl.BlockSpec((MOE_BLOCK, W), lambda b, bg: (b, 0)),
                  pl.BlockSpec((None, epg, D, D_EXPERT), lambda b, bg: (layer, bg[b], 0, 0)),
                  pl.BlockSpec((None, epg, D, D_EXPERT), lambda b, bg: (layer, bg[b], 0, 0)),
                  pl.BlockSpec((None, epg, D_EXPERT, D), lambda b, bg: (layer, bg[b], 0, 0))],
        out_specs=pl.BlockSpec((MOE_BLOCK, D), lambda b, bg: (b, 0)),
        scratch_shapes=[pltpu.VMEM((epg, D, D_EXPERT), jnp.bfloat16),
                        pltpu.VMEM((epg, D, D_EXPERT), jnp.bfloat16),
                        pltpu.VMEM((epg, D_EXPERT, D), jnp.bfloat16)],
    )
    return pl.pallas_call(
        _moe_kernel,
        grid_spec=grid_spec,
        out_shape=jax.ShapeDtypeStruct((n_sorted, D), jnp.float32),
        compiler_params=_cparams("arbitrary"),
        name="moe_experts",
    )(block_group, xs, w_gate, w_up, w_down)


def _final_kernel(dest_ref, h_ref, ys_hbm, gain_ref, o_ref, ybuf_ref, sem, *, tt):
    _gather_rows(ys_hbm, dest_ref, ybuf_ref, sem, tt)
    o_ref[...] = _rms(h_ref[...] + ybuf_ref[...], gain_ref[...])


def _final(h, dest3, ys, gain):
    T, D = h.shape
    tt = dest3.shape[2]
    return pl.pallas_call(
        functools.partial(_final_kernel, tt=tt),
        grid=(T // tt,),
        in_specs=[pl.BlockSpec((1, 1, tt), lambda i: (i, 0, 0), memory_space=pltpu.SMEM),
                  pl.BlockSpec((tt, D), lambda i: (i, 0)),
                  pl.BlockSpec(memory_space=pl.ANY),
                  pl.BlockSpec((1, D), lambda i: (0, 0))],
        out_specs=pl.BlockSpec((tt, D), lambda i: (i, 0)),
        out_shape=jax.ShapeDtypeStruct((T, D), jnp.float32),
        scratch_shapes=[pltpu.VMEM((tt, D), jnp.float32), pltpu.SemaphoreType.DMA(())],
        compiler_params=_cparams("arbitrary"),
        name="final_norm",
    )(dest3, h, ys, gain)


def _prep_layer(l, tt, w_in, b_f, w_uq, w_ukv, w_o, w_group, w_router,
                attn_norm, q_norm, kv_norm, ffn_norm):
    bf16 = jnp.bfloat16
    D = w_in.shape[1]
    fw = FOX_WIDTH
    o_f = 3 * fw
    o_cq = o_f + FOX_HEADS
    o_ckv = o_cq + Q_LORA_RANK
    o_kr = o_ckv + KV_LORA_RANK
    wi = w_in[l]
    w_f = wi[:, o_f:o_cq]
    w_f3 = jnp.concatenate([w_f] * DECAY_PARTS + [jnp.zeros((D, LANES - DECAY_PARTS * FOX_HEADS), wi.dtype)], axis=1)
    b_f3 = jnp.concatenate([b_f[l]] * DECAY_PARTS + [jnp.zeros((LANES - DECAY_PARTS * FOX_HEADS,), b_f.dtype)])
    kr = wi[:, o_kr:o_kr + MLA_ROPE_DIM]
    kr_swap = jnp.concatenate([-kr[:, ROPE_HALF:], kr[:, :ROPE_HALF]], axis=1)
    zpad = jnp.zeros((D, MLA_ROPE_DIM), wi.dtype)
    w_rows = jnp.concatenate([wi[:, fw:2 * fw], w_f3, wi[:, o_cq:o_kr], kr_swap, zpad, kr, zpad], axis=1)

    src = jnp.arange(LANES)
    dst = (src % FOX_HEADS) * HEAD_PAD + FOX_HEAD_DIM + src // FOX_HEADS
    place = ((jnp.arange(FOX_HEADS * HEAD_PAD)[None, :] == dst[:, None])
             & (src[:, None] < DECAY_PARTS * FOX_HEADS)).astype(bf16)

    qd = MLA_NOPE_DIM + MLA_ROPE_DIM
    wq = w_uq[l].reshape(Q_LORA_RANK, MLA_HEADS, qd)
    zq = jnp.zeros((Q_LORA_RANK, MLA_HEADS, HEAD_PAD - qd), wq.dtype)
    wq_main = jnp.concatenate([wq, zq], axis=2).reshape(Q_LORA_RANK, MLA_HEADS * HEAD_PAD)
    x1 = wq[:, :, MLA_NOPE_DIM:MLA_NOPE_DIM + ROPE_HALF]
    x2 = wq[:, :, MLA_NOPE_DIM + ROPE_HALF:]
    wq_swap = jnp.concatenate([jnp.zeros_like(wq[:, :, :MLA_NOPE_DIM]), -x2, x1, zq], axis=2)
    wq_swap = wq_swap.reshape(Q_LORA_RANK, MLA_HEADS * HEAD_PAD)

    wkv = w_ukv[l].reshape(KV_LORA_RANK, MLA_HEADS, MLA_NOPE_DIM + MLA_V_DIM)
    wk_nope = jnp.concatenate([wkv[:, :, :MLA_NOPE_DIM],
                               jnp.zeros((KV_LORA_RANK, MLA_HEADS, HEAD_PAD - MLA_NOPE_DIM), wkv.dtype)], axis=2)
    wv = wkv[:, :, MLA_NOPE_DIM:].reshape(KV_LORA_RANK, MLA_WIDTH)

    wr = jnp.concatenate([w_group[l], w_router[l],
                          jnp.zeros((D, LANES - N_GROUPS - N_EXPERTS), w_group.dtype)], axis=1)
    wr_hi = wr.astype(bf16)
    wr_lo = (wr - wr_hi.astype(jnp.float32)).astype(bf16)

    ar = jnp.arange(tt)
    return dict(
        attn_norm=attn_norm[l].reshape(1, D),
        w_rows=w_rows.astype(bf16),
        w_fq_t=wi[:, 0:fw].T.astype(bf16),
        w_fv_t=wi[:, 2 * fw:3 * fw].T.astype(bf16),
        b_f=b_f3.reshape(1, LANES),
        tri_incl=(ar[:, None] >= ar[None, :]).astype(bf16),
        tri_excl=(ar[:, None] > ar[None, :]).astype(bf16),
        place=place,
        q_norm=q_norm[l].reshape(1, Q_LORA_RANK),
        wq_main_t=wq_main.T.astype(bf16),
        wq_swap_t=wq_swap.T.astype(bf16),
        kv_norm=kv_norm[l].reshape(1, KV_LORA_RANK),
        wk_nope=wk_nope.reshape(KV_LORA_RANK, MLA_HEADS * HEAD_PAD).astype(bf16),
        wv_t=wv.T.astype(bf16),
        w_o=w_o[l].astype(bf16),
        ffn_norm=ffn_norm[l].reshape(1, D),
        wr_hi=wr_hi,
        wr_lo=wr_lo,
    )


def _routing_plan(route, counts, n_blocks):
    gid = route[:, GID_LANE].astype(jnp.int32)
    rank = route[:, RANK_LANE].astype(jnp.int32)
    cnt = counts[0, :N_GROUPS].astype(jnp.int32)
    padded = (cnt + MOE_BLOCK - 1) // MOE_BLOCK * MOE_BLOCK
    pend = jnp.cumsum(padded)
    pstart = pend - padded
    dest = pstart[gid] + rank
    block_start = jnp.arange(n_blocks, dtype=jnp.int32) * MOE_BLOCK
    block_group = jnp.sum((pend[None, :] <= block_start[:, None]).astype(jnp.int32), axis=1)
    return dest, jnp.minimum(block_group, N_GROUPS - 1)


def kernel(x, positions, attn_norm, w_in, b_f, q_norm, w_uq, kv_norm, w_ukv, fox_out_norm, mla_out_norm, w_o,
           ffn_norm, w_group, w_router, w_gate, w_up, w_down, final_norm):
    B, S, D = x.shape
    T = B * S
    depth = w_in.shape[0]
    ta = min(ATTN_TILE, S)
    tt = min(ROW_TILE, S)
    n_blocks = -(-(T + N_GROUPS * (MOE_BLOCK - 1)) // MOE_BLOCK)

    cos_t, sin_t = _rope_tables(positions)
    ones_t = jnp.ones((MLA_NOPE_DIM, T), jnp.float32)
    zero_t = jnp.zeros((HEAD_PAD - MLA_NOPE_DIM - MLA_ROPE_DIM, T), jnp.float32)
    cq_tab = jnp.concatenate([ones_t, cos_t, cos_t, zero_t], axis=0)
    sq_tab = jnp.concatenate([jnp.zeros_like(ones_t), sin_t, sin_t, zero_t], axis=0)
    cos, sin = cos_t.T, sin_t.T
    zero32 = jnp.zeros((T, MLA_ROPE_DIM), jnp.float32)
    csk_tab = jnp.concatenate([sin, sin, zero32, cos, cos, zero32], axis=1)
    tabs = (cq_tab, sq_tab, csk_tab)

    h = x.reshape(T, D)
    moe_in = None
    for l in range(depth):
        lw = _prep_layer(l, tt, w_in, b_f, w_uq, w_ukv, w_o, w_group, w_router,
                         attn_norm, q_norm, kv_norm, ffn_norm)
        h, (fq_t, fk, fv_t, mq_t, mk, mv_t) = _pre_attn(h, moe_in, B, S, lw, tabs)
        fox_gain = jnp.broadcast_to(fox_out_norm[l].reshape(FOX_WIDTH, 1), (FOX_WIDTH, ta))
        mla_gain = jnp.broadcast_to(mla_out_norm[l].reshape(MLA_WIDTH, 1), (MLA_WIDTH, ta))
        fox = _attention(fq_t, fk, fv_t, fox_gain, B, S, "attn_fox")
        mla = _attention(mq_t, mk, mv_t, mla_gain, B, S, "attn_mla")
        h, xr, route, counts = _post_attn(fox, mla, h, lw)
        dest, block_group = _routing_plan(route, counts, n_blocks)
        dest3 = dest.reshape(T // tt, 1, tt)
        xs = _dispatch(xr, dest3, n_blocks * MOE_BLOCK)
        ys = _moe(xs, block_group, l, w_gate, w_up, w_down)
        moe_in = (dest3, ys)
    out = _final(h, moe_in[0], moe_in[1], final_norm.reshape(1, D))
    return out.reshape(B, S, D)
```

```python
import functools
import math

import jax
import jax.numpy as jnp
from jax import lax
from jax.experimental import pallas as pl
from jax.experimental.pallas import tpu as pltpu

FOX_HEADS = 8
FOX_HEAD_DIM = 64
FOX_WIDTH = FOX_HEADS * FOX_HEAD_DIM
MLA_HEADS = 8
MLA_NOPE_DIM = 64
MLA_ROPE_DIM = 32
MLA_V_DIM = 64
MLA_WIDTH = MLA_HEADS * MLA_V_DIM
Q_LORA_RANK = 256
KV_LORA_RANK = 128
ROPE_THETA = 10000.0
N_GROUPS = 8
EXPERTS_PER_GROUP = 4
N_EXPERTS = N_GROUPS * EXPERTS_PER_GROUP
D_EXPERT = 256
NORM_EPS = 1e-6

LANES = 128
SUBLANES = 8
BF16_SUBLANES = 16
VMEM_LIMIT_BYTES = 56 * 1024 * 1024

ROW_TILE = 512
ATTN_TILE = 512
MOE_BLOCK = 512
ROW_DMA_UNROLL = 8
HEAD_PAD = 128
V_ROWS = MLA_V_DIM + BF16_SUBLANES
ROPE_HALF = MLA_ROPE_DIM // 2
NEG_BIG = -1e30
LOG2E = math.log2(math.e)
DECAY_PARTS = 3

GID_LANE = 0
RANK_LANE = 1
GATE_LANE0 = N_GROUPS


def _cparams(*semantics):
    return pltpu.CompilerParams(dimension_semantics=semantics, vmem_limit_bytes=VMEM_LIMIT_BYTES)


def _rms(x, gain):
    ms = jnp.mean(x * x, axis=-1, keepdims=True)
    return x * lax.rsqrt(ms + NORM_EPS) * gain


def _dot(a, b):
    return jnp.dot(a, b, preferred_element_type=jnp.float32)


def _dot_nt(a, b):
    return lax.dot_general(a, b, (((1,), (1,)), ((), ())), preferred_element_type=jnp.float32)


def _split_bf16(x, parts):
    out = []
    for _ in range(parts - 1):
        piece = x.astype(jnp.bfloat16).astype(jnp.float32)
        out.append(piece)
        x = x - piece
    out.append(x.astype(jnp.bfloat16).astype(jnp.float32))
    return out


def _rope_table_kernel(pos_ref, invf_ref, cos_ref, sin_ref):
    ang = pos_ref[...].astype(jnp.float32) * invf_ref[...]
    cos_ref[...] = jnp.cos(ang)
    sin_ref[...] = jnp.sin(ang)


def _rope_tables(positions):
    T = positions.size
    inv_freq = ROPE_THETA ** (-jnp.arange(ROPE_HALF, dtype=jnp.float32) / ROPE_HALF)
    ct = min(T, 4096)
    return pl.pallas_call(
        _rope_table_kernel,
        grid=(T // ct,),
        in_specs=[pl.BlockSpec((1, ct), lambda i: (0, i)),
                  pl.BlockSpec((ROPE_HALF, 1), lambda i: (0, 0))],
        out_specs=[pl.BlockSpec((ROPE_HALF, ct), lambda i: (0, i))] * 2,
        out_shape=[jax.ShapeDtypeStruct((ROPE_HALF, T), jnp.float32)] * 2,
        compiler_params=_cparams("parallel"),
        name="rope_tables",
    )(positions.reshape(1, T), inv_freq.reshape(ROPE_HALF, 1))


def _gather_rows(src_hbm, idx_ref, dst_ref, sem, n_rows):
    def issue(r, carry):
        pltpu.make_async_copy(src_hbm.at[pl.ds(idx_ref[0, 0, r], 1)],
                              dst_ref.at[pl.ds(r, 1)], sem).start()
        return carry
    lax.fori_loop(0, n_rows, issue, 0, unroll=ROW_DMA_UNROLL)
    pltpu.make_async_copy(src_hbm.at[pl.ds(0, n_rows)], dst_ref, sem).wait()


def _pre_attn_kernel(*refs, has_moe_in, tt):
    if has_moe_in:
        dest_ref, h_ref, ys_hbm = refs[:3]
        refs = refs[3:]
    else:
        h_ref = refs[0]
        refs = refs[1:]
    (anorm_ref, wrow_ref, wfq_ref, wfv_ref, bf_ref, tri_ref, place_ref, qnorm_ref, wqm_ref, wqs_ref,
     kvnorm_ref, wkn_ref, wv_ref, cos_ref, sin_ref, csk_ref) = refs[:16]
    refs = refs[16:]
    if has_moe_in:
        hout_ref = refs[0]
        refs = refs[1:]
    fq_ref, fk_ref, fv_ref, qm_ref, km_ref, vm_ref = refs[:6]
    refs = refs[6:]
    carry_ref = refs[0]
    if has_moe_in:
        ybuf_ref, sem = refs[1:3]

    si = pl.program_id(1)
    bf16 = jnp.bfloat16
    half = LANES // 2

    h = h_ref[...]
    if has_moe_in:
        _gather_rows(ys_hbm, dest_ref, ybuf_ref, sem, tt)
        h = h + ybuf_ref[...]
        hout_ref[...] = h

    xn = _rms(h, anorm_ref[...]).astype(bf16)
    lane = lax.broadcasted_iota(jnp.int32, (tt, LANES), 1)
    sub = lax.broadcasted_iota(jnp.int32, (BF16_SUBLANES, tt), 0)
    ones_rows = jnp.ones((BF16_SUBLANES, tt), bf16)

    fq_t = _dot_nt(wfq_ref[...], xn) * (FOX_HEAD_DIM ** -0.5 * LOG2E)
    q_aug = jnp.where(sub < DECAY_PARTS, -1.0, 0.0).astype(bf16)
    q_zero = jnp.zeros((HEAD_PAD - FOX_HEAD_DIM - BF16_SUBLANES, tt), bf16)
    for hd in range(FOX_HEADS):
        fq_ref[0, hd, 0, 0:FOX_HEAD_DIM, :] = fq_t[hd * FOX_HEAD_DIM:(hd + 1) * FOX_HEAD_DIM].astype(bf16)
        fq_ref[0, hd, 0, FOX_HEAD_DIM:FOX_HEAD_DIM + BF16_SUBLANES, :] = q_aug
        fq_ref[0, hd, 0, FOX_HEAD_DIM + BF16_SUBLANES:, :] = q_zero
    fv_t = _dot_nt(wfv_ref[...], xn)
    vd = MLA_V_DIM
    for hd in range(FOX_HEADS):
        fv_ref[0, hd, 0, 0:vd, :] = fv_t[hd * vd:(hd + 1) * vd].astype(bf16)
        fv_ref[0, hd, 0, vd:, :] = ones_rows

    fw = FOX_WIDTH
    o = fw
    z = _dot(xn, wrow_ref[:, o:o + LANES]) + bf_ref[...]
    o += LANES
    logf = jnp.minimum(z, 0.0) - jnp.log1p(jnp.exp(-jnp.abs(z)))
    pieces = jnp.concatenate([p_.astype(bf16) for p_ in _split_bf16(logf, DECAY_PARTS)], axis=1)
    csum = _dot(tri_ref[...], pieces)

    @pl.when(si == 0)
    def _():
        carry_ref[...] = jnp.zeros_like(carry_ref)

    dcum = csum[:, 0:LANES] + csum[:, LANES:2 * LANES] + csum[:, 2 * LANES:] + carry_ref[...]
    carry_ref[...] = dcum[tt - 1:tt, :]
    d_parts = _split_bf16(dcum * LOG2E, DECAY_PARTS)
    d_sel = jnp.where(lane < FOX_HEADS, d_parts[0],
                      jnp.where(lane < 2 * FOX_HEADS, d_parts[1],
                                jnp.where(lane < 3 * FOX_HEADS, d_parts[2], 0.0)))
    k_aug = _dot(d_sel.astype(bf16), place_ref[...])

    fk = _dot(xn, wrow_ref[:, 0:fw])
    for p in range(FOX_HEADS // 2):
        blk = fk[:, p * LANES:(p + 1) * LANES]
        for hh, b_ in ((0, blk), (1, pltpu.roll(blk, half, axis=1))):
            sl = slice((2 * p + hh) * HEAD_PAD, (2 * p + hh + 1) * HEAD_PAD)
            fk_ref[:, sl] = jnp.where(lane < FOX_HEAD_DIM, b_, k_aug[:, sl]).astype(bf16)

    cq = _rms(_dot(xn, wrow_ref[:, o:o + Q_LORA_RANK]), qnorm_ref[...]).astype(bf16)
    o += Q_LORA_RANK
    qm_t = _dot_nt(wqm_ref[...], cq)
    qs_t = _dot_nt(wqs_ref[...], cq)
    cos_t = cos_ref[...]
    sin_t = sin_ref[...]
    scale = (MLA_NOPE_DIM + MLA_ROPE_DIM) ** -0.5 * LOG2E
    q_zero = jnp.zeros((HEAD_PAD - MLA_NOPE_DIM - MLA_ROPE_DIM, tt), jnp.float32)
    for hd in range(MLA_HEADS):
        blk = qm_t[hd * HEAD_PAD:(hd + 1) * HEAD_PAD]
        rows = [blk[0:MLA_NOPE_DIM]]
        for j in range(2):
            main = blk[MLA_NOPE_DIM + j * ROPE_HALF:MLA_NOPE_DIM + (j + 1) * ROPE_HALF]
            swap = qs_t[hd * MLA_ROPE_DIM + j * ROPE_HALF:hd * MLA_ROPE_DIM + (j + 1) * ROPE_HALF]
            rows.append(main * cos_t + swap * sin_t)
        rows.append(q_zero)
        qm_ref[0, hd, 0] = (jnp.concatenate(rows, axis=0) * scale).astype(bf16)

    ckv = _rms(_dot(xn, wrow_ref[:, o:o + KV_LORA_RANK]), kvnorm_ref[...]).astype(bf16)
    o += KV_LORA_RANK
    mv_t = _dot_nt(wv_ref[...], ckv)
    for hd in range(MLA_HEADS):
        vm_ref[0, hd, 0, 0:vd, :] = mv_t[hd * vd:(hd + 1) * vd].astype(bf16)
        vm_ref[0, hd, 0, vd:, :] = ones_rows
    k_nope = _dot(ckv, wkn_ref[...])
    kr = _dot(xn, wrow_ref[:, o:o + LANES]) * csk_ref[...]
    kr = kr + pltpu.roll(kr, half, axis=1)
    k_pe = jnp.where(lane >= MLA_NOPE_DIM, kr, 0.0)
    for hd in range(MLA_HEADS):
        sl = slice(hd * HEAD_PAD, (hd + 1) * HEAD_PAD)
        km_ref[:, sl] = (k_nope[:, sl] + k_pe).astype(bf16)


def _pre_attn(h, moe_in, B, S, lw, tabs):
    T, D = h.shape
    tt = min(ROW_TILE, S)
    ns = S // tt
    has_moe_in = moe_in is not None
    row = lambda b, s: (b * ns + s, 0)
    const = lambda b, s: (0, 0)
    cos_t, sin_t, csk_tab = tabs

    def rows(width):
        return pl.BlockSpec((tt, width), row)

    def full(a):
        return pl.BlockSpec(a.shape, const)

    weights = [lw["attn_norm"], lw["w_rows"], lw["w_fq_t"], lw["w_fv_t"], lw["b_f"], lw["tri_incl"],
               lw["place"], lw["q_norm"], lw["wq_main_t"], lw["wq_swap_t"], lw["kv_norm"],
               lw["wk_nope"], lw["wv_t"]]
    in_specs = [rows(D)]
    args = [h]
    if has_moe_in:
        dest3, ys = moe_in
        in_specs = [pl.BlockSpec((1, 1, tt), lambda b, s: (b * ns + s, 0, 0), memory_space=pltpu.SMEM),
                    rows(D), pl.BlockSpec(memory_space=pl.ANY)]
        args = [dest3, h, ys]
    col = lambda b, s: (0, b * ns + s)
    in_specs += [full(w) for w in weights]
    in_specs += [pl.BlockSpec((ROPE_HALF, tt), col), pl.BlockSpec((ROPE_HALF, tt), col), rows(LANES)]
    args += weights + [cos_t, sin_t, csk_tab]

    bf16 = jnp.bfloat16
    qt_shape = jax.ShapeDtypeStruct((B, FOX_HEADS, ns, HEAD_PAD, tt), bf16)
    qt_spec = pl.BlockSpec((1, FOX_HEADS, 1, HEAD_PAD, tt), lambda b, s: (b, 0, s, 0, 0))
    k_shape = jax.ShapeDtypeStruct((T, FOX_HEADS * HEAD_PAD), bf16)
    k_spec = rows(FOX_HEADS * HEAD_PAD)
    vt_shape = jax.ShapeDtypeStruct((B, FOX_HEADS, ns, V_ROWS, tt), bf16)
    vt_spec = pl.BlockSpec((1, FOX_HEADS, 1, V_ROWS, tt), lambda b, s: (b, 0, s, 0, 0))
    out_shape = [qt_shape, k_shape, vt_shape] * 2
    out_specs = [qt_spec, k_spec, vt_spec] * 2
    scratch = [pltpu.VMEM((1, LANES), jnp.float32)]
    if has_moe_in:
        out_shape = [jax.ShapeDtypeStruct((T, D), jnp.float32)] + out_shape
        out_specs = [rows(D)] + out_specs
        scratch += [pltpu.VMEM((tt, D), jnp.float32), pltpu.SemaphoreType.DMA(())]

    outs = pl.pallas_call(
        functools.partial(_pre_attn_kernel, has_moe_in=has_moe_in, tt=tt),
        grid=(B, ns),
        in_specs=in_specs,
        out_specs=out_specs,
        out_shape=out_shape,
        scratch_shapes=scratch,
        compiler_params=_cparams("arbitrary", "arbitrary"),
        name="pre_attn",
    )(*args)
    if has_moe_in:
        return outs[0], outs[1:]
    return h, outs


def _attn_kernel(qt_ref, k_ref, vt_ref, g_ref, o_ref, s_scr, mc_scr, m_scr, acc_scr, *, ta, nq):
    vd = MLA_V_DIM

    def q_tile(n, carry):
        m_scr[...] = jnp.full_like(m_scr, NEG_BIG)
        acc_scr[...] = jnp.zeros_like(acc_scr)

        def produce(kt, slot, masked):
            k0 = pl.multiple_of(kt * ta, ta)
            for hh in range(2):
                kk = k_ref[pl.ds(k0, ta), hh * HEAD_PAD:(hh + 1) * HEAD_PAD]
                s = _dot(kk, qt_ref[0, hh, n])
                if masked:
                    r = lax.broadcasted_iota(jnp.int32, s.shape, 0)
                    c = lax.broadcasted_iota(jnp.int32, s.shape, 1)
                    s = jnp.where(r <= c, s, NEG_BIG)
                s_scr[slot, hh] = s
                mc_scr[slot, hh] = jnp.max(s, axis=0, keepdims=True)

        def consume(kt, slot):
            for hh in range(2):
                m_prev = m_scr[hh]
                m_new = jnp.maximum(m_prev, mc_scr[slot, hh])
                alpha = jnp.exp2(m_prev - m_new)
                p = jnp.exp2(s_scr[slot, hh] - m_new).astype(jnp.bfloat16)
                acc_scr[hh] = alpha * acc_scr[hh] + _dot(vt_ref[0, hh, kt], p)
                m_scr[hh] = m_new

        @pl.when(n == 0)
        def _():
            produce(0, 0, True)

        @pl.when(n > 0)
        def _():
            produce(0, 0, False)

        plain_steps = jnp.maximum(n - 1, 0)

        def two_steps(u, c):
            t = 2 * u
            produce(t + 1, 1, False)
            consume(t, 0)
            produce(t + 2, 0, False)
            consume(t + 1, 1)
            return c

        lax.fori_loop(0, plain_steps // 2, two_steps, 0)

        @pl.when(plain_steps % 2 == 1)
        def _():
            produce(n - 1, 1, False)
            consume(n - 2, 0)

        for par in range(2):
            @pl.when((n > 0) & (n % 2 == par))
            def _():
                produce(n, par, True)
                consume(n - 1, 1 - par)

        for par in range(2):
            @pl.when(n % 2 == par)
            def _():
                consume(n, par)

        outs = []
        for hh in range(2):
            acc = acc_scr[hh]
            oh = acc[0:vd] / acc[vd:vd + 1]
            ms = jnp.mean(oh * oh, axis=0, keepdims=True)
            outs.append(oh * lax.rsqrt(ms + NORM_EPS))
        o_t = jnp.concatenate(outs, axis=0) * g_ref[...]
        o_ref[pl.ds(pl.multiple_of(n * ta, ta), ta), :] = o_t.T.astype(o_ref.dtype)
        return carry

    lax.fori_loop(0, nq, q_tile, 0)


def _attention(qt, k, vt, gain_rep, B, S, name):
    T = k.shape[0]
    ta = min(ATTN_TILE, S)
    nq = S // ta
    n_pairs = qt.shape[1] // 2
    return pl.pallas_call(
        functools.partial(_attn_kernel, ta=ta, nq=nq),
        grid=(B, n_pairs),
        in_specs=[pl.BlockSpec((1, 2, nq, HEAD_PAD, ta), lambda b, j: (b, j, 0, 0, 0)),
                  pl.BlockSpec((S, 2 * HEAD_PAD), lambda b, j: (b, j)),
                  pl.BlockSpec((1, 2, nq, V_ROWS, ta), lambda b, j: (b, j, 0, 0, 0)),
                  pl.BlockSpec((LANES, ta), lambda b, j: (j, 0))],
        out_specs=pl.BlockSpec((S, LANES), lambda b, j: (b, j)),
        out_shape=jax.ShapeDtypeStruct((T, n_pairs * LANES), jnp.bfloat16),
        scratch_shapes=[pltpu.VMEM((2, 2, ta, ta), jnp.float32),
                        pltpu.VMEM((2, 2, 1, ta), jnp.float32),
                        pltpu.VMEM((2, 1, ta), jnp.float32),
                        pltpu.VMEM((2, V_ROWS, ta), jnp.float32)],
        compiler_params=_cparams("parallel", "parallel"),
        name=name,
    )(qt, k, vt, gain_rep)


def _post_attn_kernel(fox_ref, mla_ref, h_ref, wo_ref, fnorm_ref, wrh_ref, wrl_ref, tri_ref,
                      h2_ref, xr_ref, route_ref, cnt_ref, carry_ref):
    i = pl.program_id(0)
    D = h_ref.shape[1]
    h2 = h_ref[...] + _dot(fox_ref[...], wo_ref[0:FOX_WIDTH, :]) + _dot(mla_ref[...], wo_ref[FOX_WIDTH:, :])
    h2_ref[...] = h2
    xn = _rms(h2, fnorm_ref[...])
    xr_ref[:, 0:D] = xn

    x_hi = xn.astype(jnp.bfloat16)
    x_lo = (xn - x_hi.astype(jnp.float32)).astype(jnp.bfloat16)
    logits = _dot(x_hi, wrh_ref[...]) + _dot(x_lo, wrh_ref[...]) + _dot(x_hi, wrl_ref[...])

    lane = lax.broadcasted_iota(jnp.int32, logits.shape, 1)
    lane_f = lane.astype(jnp.float32)
    far = float(LANES)

    gl = jnp.where(lane < N_GROUPS, logits, NEG_BIG)
    gmax = jnp.max(gl, axis=1, keepdims=True)
    group_w = 1.0 / jnp.sum(jnp.exp(gl - gmax), axis=1, keepdims=True)
    gid = jnp.min(jnp.where(gl == gmax, lane_f, far), axis=1, keepdims=True)

    lo = GATE_LANE0 + EXPERTS_PER_GROUP * gid
    el = jnp.where((lane_f >= lo) & (lane_f < lo + EXPERTS_PER_GROUP), logits, NEG_BIG)
    e1 = jnp.max(el, axis=1, keepdims=True)
    i1 = jnp.min(jnp.where(el == e1, lane_f, far), axis=1, keepdims=True)
    el2 = jnp.where(lane_f == i1, NEG_BIG, el)
    e2 = jnp.max(el2, axis=1, keepdims=True)
    i2 = jnp.min(jnp.where(el2 == e2, lane_f, far), axis=1, keepdims=True)
    t = jnp.exp(e2 - e1)
    g1 = group_w / (1.0 + t)
    g2 = group_w * t / (1.0 + t)
    gates = jnp.where(lane_f == i1, g1, jnp.where(lane_f == i2, g2, 0.0))

    @pl.when(i == 0)
    def _():
        carry_ref[...] = jnp.zeros_like(carry_ref)

    onehot = lane_f == gid
    prefix = _dot(tri_ref[...], jnp.where(onehot, 1.0, 0.0).astype(jnp.bfloat16)) + carry_ref[...]
    rank = jnp.sum(jnp.where(onehot, prefix, 0.0), axis=1, keepdims=True)
    carry = carry_ref[...] + jnp.sum(jnp.where(onehot, 1.0, 0.0), axis=0, keepdims=True)
    carry_ref[...] = carry
    cnt_ref[...] = carry

    info = gates + jnp.where(lane == GID_LANE, gid, 0.0) + jnp.where(lane == RANK_LANE, rank, 0.0)
    xr_ref[:, D:D + LANES] = info
    route_ref[...] = info


def _post_attn(fox, mla, h, lw):
    T, D = h.shape
    tt = min(ROW_TILE, T)
    row = lambda i: (i, 0)
    const = lambda i: (0, 0)
    weights = [lw["w_o"], lw["ffn_norm"], lw["wr_hi"], lw["wr_lo"], lw["tri_excl"]]
    return pl.pallas_call(
        _post_attn_kernel,
        grid=(T // tt,),
        in_specs=[pl.BlockSpec((tt, FOX_WIDTH), row), pl.BlockSpec((tt, MLA_WIDTH), row),
                  pl.BlockSpec((tt, D), row)] + [pl.BlockSpec(w.shape, const) for w in weights],
        out_specs=[pl.BlockSpec((tt, D), row), pl.BlockSpec((tt, D + LANES), row),
                   pl.BlockSpec((tt, LANES), row), pl.BlockSpec((1, LANES), const)],
        out_shape=[jax.ShapeDtypeStruct((T, D), jnp.float32),
                   jax.ShapeDtypeStruct((T, D + LANES), jnp.float32),
                   jax.ShapeDtypeStruct((T, LANES), jnp.float32),
                   jax.ShapeDtypeStruct((1, LANES), jnp.float32)],
        scratch_shapes=[pltpu.VMEM((1, LANES), jnp.float32)],
        compiler_params=_cparams("arbitrary"),
        name="post_attn",
    )(fox, mla, h, *weights)


def _dispatch_kernel(dest_ref, xr_ref, init_hbm, xs_hbm, sem, *, tt):
    del init_hbm

    def issue(r, carry):
        pltpu.make_async_copy(xr_ref.at[pl.ds(r, 1)], xs_hbm.at[pl.ds(dest_ref[0, 0, r], 1)], sem).start()
        return carry
    lax.fori_loop(0, tt, issue, 0, unroll=ROW_DMA_UNROLL)
    pltpu.make_async_copy(xr_ref, xs_hbm.at[pl.ds(0, tt)], sem).wait()


def _dispatch(xr, dest3, n_sorted):
    T, W = xr.shape
    tt = dest3.shape[2]
    zeros = jnp.zeros((n_sorted, W), xr.dtype)
    return pl.pallas_call(
        functools.partial(_dispatch_kernel, tt=tt),
        grid=(T // tt,),
        in_specs=[pl.BlockSpec((1, 1, tt), lambda i: (i, 0, 0), memory_space=pltpu.SMEM),
                  pl.BlockSpec((tt, W), lambda i: (i, 0)),
                  pl.BlockSpec(memory_space=pl.ANY)],
        out_specs=pl.BlockSpec(memory_space=pl.ANY),
        out_shape=jax.ShapeDtypeStruct((n_sorted, W), xr.dtype),
        scratch_shapes=[pltpu.SemaphoreType.DMA(())],
        input_output_aliases={2: 0},
        compiler_params=_cparams("arbitrary"),
        name="moe_dispatch",
    )(dest3, xr, zeros)


def _moe_kernel(bg_ref, xs_ref, wg32_ref, wu32_ref, wd32_ref, ys_ref, wg_ref, wu_ref, wd_ref):
    D = ys_ref.shape[1]
    b = pl.program_id(0)
    g = bg_ref[b]

    @pl.when((b == 0) | (g != bg_ref[jnp.maximum(b - 1, 0)]))
    def _():
        wg_ref[...] = wg32_ref[...].astype(jnp.bfloat16)
        wu_ref[...] = wu32_ref[...].astype(jnp.bfloat16)
        wd_ref[...] = wd32_ref[...].astype(jnp.bfloat16)

    x = xs_ref[:, 0:D].astype(jnp.bfloat16)
    info = xs_ref[:, D:D + LANES]
    lane = lax.broadcasted_iota(jnp.int32, info.shape, 1)
    y = jnp.zeros(ys_ref.shape, jnp.float32)
    for j in range(EXPERTS_PER_GROUP):
        gate = jnp.sum(jnp.where(lane == GATE_LANE0 + EXPERTS_PER_GROUP * g + j, info, 0.0),
                       axis=1, keepdims=True)
        a = _dot(x, wg_ref[j])
        u = _dot(x, wu_ref[j])
        hj = (a * jax.nn.sigmoid(a) * u).astype(jnp.bfloat16)
        y = y + gate * _dot(hj, wd_ref[j])
    ys_ref[...] = y


def _moe(xs, block_group, layer, w_gate, w_up, w_down):
    n_sorted, W = xs.shape
    D = W - LANES
    nb = n_sorted // MOE_BLOCK
    epg = EXPERTS_PER_GROUP
    grid_spec = pltpu.PrefetchScalarGridSpec(
        num_scalar_prefetch=1,
        grid=(nb,),
        in_specs=[pl.BlockSpec((MOE_BLOCK, W), lambda b, bg: (b, 0)),
                  pl.BlockSpec((None, epg, D, D_EXPERT), lambda b, bg: (layer, bg[b], 0, 0)),
                  pl.BlockSpec((None, epg, D, D_EXPERT), lambda b, bg: (layer, bg[b], 0, 0)),
                  pl.BlockSpec((None, epg, D_EXPERT, D), lambda b, bg: (layer, bg[b], 0, 0))],
        out_specs=pl.BlockSpec((MOE_BLOCK, D), lambda b, bg: (b, 0)),
        scratch_shapes=[pltpu.VMEM((epg, D, D_EXPERT), jnp.bfloat16),
                        pltpu.VMEM((epg, D, D_EXPERT), jnp.bfloat16),
                        pltpu.VMEM((epg, D_EXPERT, D), jnp.bfloat16)],
    )
    return pl.pallas_call(
        _moe_kernel,
        grid_spec=grid_spec,
        out_shape=jax.ShapeDtypeStruct((n_sorted, D), jnp.float32),
        compiler_params=_cparams("arbitrary"),
        name="moe_experts",
    )(block_group, xs, w_gate, w_up, w_down)


def _final_kernel(dest_ref, h_ref, ys_hbm, gain_ref, o_ref, ybuf_ref, sem, *, tt):
    _gather_rows(ys_hbm, dest_ref, ybuf_ref, sem, tt)
    o_ref[...] = _rms(h_ref[...] + ybuf_ref[...], gain_ref[...])


def _final(h, dest3, ys, gain):
    T, D = h.shape
    tt = dest3.shape[2]
    return pl.pallas_call(
        functools.partial(_final_kernel, tt=tt),
        grid=(T // tt,),
        in_specs=[pl.BlockSpec((1, 1, tt), lambda i: (i, 0, 0), memory_space=pltpu.SMEM),
                  pl.BlockSpec((tt, D), lambda i: (i, 0)),
                  pl.BlockSpec(memory_space=pl.ANY),
                  pl.BlockSpec((1, D), lambda i: (0, 0))],
        out_specs=pl.BlockSpec((tt, D), lambda i: (i, 0)),
        out_shape=jax.ShapeDtypeStruct((T, D), jnp.float32),
        scratch_shapes=[pltpu.VMEM((tt, D), jnp.float32), pltpu.SemaphoreType.DMA(())],
        compiler_params=_cparams("arbitrary"),
        name="final_norm",
    )(dest3, h, ys, gain)


def _prep_layer(l, tt, w_in, b_f, w_uq, w_ukv, w_o, w_group, w_router,
                attn_norm, q_norm, kv_norm, ffn_norm):
    bf16 = jnp.bfloat16
    D = w_in.shape[1]
    fw = FOX_WIDTH
    o_f = 3 * fw
    o_cq = o_f + FOX_HEADS
    o_ckv = o_cq + Q_LORA_RANK
    o_kr = o_ckv + KV_LORA_RANK
    wi = w_in[l]
    w_f = wi[:, o_f:o_cq]
    w_f3 = jnp.concatenate([w_f] * DECAY_PARTS + [jnp.zeros((D, LANES - DECAY_PARTS * FOX_HEADS), wi.dtype)], axis=1)
    b_f3 = jnp.concatenate([b_f[l]] * DECAY_PARTS + [jnp.zeros((LANES - DECAY_PARTS * FOX_HEADS,), b_f.dtype)])
    kr = wi[:, o_kr:o_kr + MLA_ROPE_DIM]
    kr_swap = jnp.concatenate([-kr[:, ROPE_HALF:], kr[:, :ROPE_HALF]], axis=1)
    zpad = jnp.zeros((D, MLA_ROPE_DIM), wi.dtype)
    w_rows = jnp.concatenate([wi[:, fw:2 * fw], w_f3, wi[:, o_cq:o_kr], kr_swap, zpad, kr, zpad], axis=1)

    src = jnp.arange(LANES)
    dst = (src % FOX_HEADS) * HEAD_PAD + FOX_HEAD_DIM + src // FOX_HEADS
    place = ((jnp.arange(FOX_HEADS * HEAD_PAD)[None, :] == dst[:, None])
             & (src[:, None] < DECAY_PARTS * FOX_HEADS)).astype(bf16)

    qd = MLA_NOPE_DIM + MLA_ROPE_DIM
    wq = w_uq[l].reshape(Q_LORA_RANK, MLA_HEADS, qd)
    zq = jnp.zeros((Q_LORA_RANK, MLA_HEADS, HEAD_PAD - qd), wq.dtype)
    wq_main = jnp.concatenate([wq, zq], axis=2).reshape(Q_LORA_RANK, MLA_HEADS * HEAD_PAD)
    x1 = wq[:, :, MLA_NOPE_DIM:MLA_NOPE_DIM + ROPE_HALF]
    x2 = wq[:, :, MLA_NOPE_DIM + ROPE_HALF:]
    wq_swap = jnp.concatenate([-x2, x1], axis=2).reshape(Q_LORA_RANK, MLA_HEADS * MLA_ROPE_DIM)

    wkv = w_ukv[l].reshape(KV_LORA_RANK, MLA_HEADS, MLA_NOPE_DIM + MLA_V_DIM)
    wk_nope = jnp.concatenate([wkv[:, :, :MLA_NOPE_DIM],
                               jnp.zeros((KV_LORA_RANK, MLA_HEADS, HEAD_PAD - MLA_NOPE_DIM), wkv.dtype)], axis=2)
    wv = wkv[:, :, MLA_NOPE_DIM:].reshape(KV_LORA_RANK, MLA_WIDTH)

    wr = jnp.concatenate([w_group[l], w_router[l],
                          jnp.zeros((D, LANES - N_GROUPS - N_EXPERTS), w_group.dtype)], axis=1)
    wr_hi = wr.astype(bf16)
    wr_lo = (wr - wr_hi.astype(jnp.float32)).astype(bf16)

    ar = jnp.arange(tt)
    return dict(
        attn_norm=attn_norm[l].reshape(1, D),
        w_rows=w_rows.astype(bf16),
        w_fq_t=wi[:, 0:fw].T.astype(bf16),
        w_fv_t=wi[:, 2 * fw:3 * fw].T.astype(bf16),
        b_f=b_f3.reshape(1, LANES),
        tri_incl=(ar[:, None] >= ar[None, :]).astype(bf16),
        tri_excl=(ar[:, None] > ar[None, :]).astype(bf16),
        place=place,
        q_norm=q_norm[l].reshape(1, Q_LORA_RANK),
        wq_main_t=wq_main.T.astype(bf16),
        wq_swap_t=wq_swap.T.astype(bf16),
        kv_norm=kv_norm[l].reshape(1, KV_LORA_RANK),
        wk_nope=wk_nope.reshape(KV_LORA_RANK, MLA_HEADS * HEAD_PAD).astype(bf16),
        wv_t=wv.T.astype(bf16),
        w_o=w_o[l].astype(bf16),
        ffn_norm=ffn_norm[l].reshape(1, D),
        wr_hi=wr_hi,
        wr_lo=wr_lo,
    )


def _routing_plan(route, counts, n_blocks):
    gid = route[:, GID_LANE].astype(jnp.int32)
    rank = route[:, RANK_LANE].astype(jnp.int32)
    cnt = counts[0, :N_GROUPS].astype(jnp.int32)
    padded = (cnt + MOE_BLOCK - 1) // MOE_BLOCK * MOE_BLOCK
    pend = jnp.cumsum(padded)
    pstart = pend - padded
    dest = pstart[gid] + rank
    block_start = jnp.arange(n_blocks, dtype=jnp.int32) * MOE_BLOCK
    block_group = jnp.sum((pend[None, :] <= block_start[:, None]).astype(jnp.int32), axis=1)
    return dest, jnp.minimum(block_group, N_GROUPS - 1)


def kernel(x, positions, attn_norm, w_in, b_f, q_norm, w_uq, kv_norm, w_ukv, fox_out_norm, mla_out_norm, w_o,
           ffn_norm, w_group, w_router, w_gate, w_up, w_down, final_norm):
    B, S, D = x.shape
    T = B * S
    depth = w_in.shape[0]
    ta = min(ATTN_TILE, S)
    tt = min(ROW_TILE, S)
    n_blocks = -(-(T + N_GROUPS * (MOE_BLOCK - 1)) // MOE_BLOCK)

    cos_t, sin_t = _rope_tables(positions)
    cos, sin = cos_t.T, sin_t.T
    zero32 = jnp.zeros((T, MLA_ROPE_DIM), jnp.float32)
    csk_tab = jnp.concatenate([sin, sin, zero32, cos, cos, zero32], axis=1)
    tabs = (cos_t, sin_t, csk_tab)

    h = x.reshape(T, D)
    moe_in = None
    for l in range(depth):
        lw = _prep_layer(l, tt, w_in, b_f, w_uq, w_ukv, w_o, w_group, w_router,
                         attn_norm, q_norm, kv_norm, ffn_norm)
        h, (fq_t, fk, fv_t, mq_t, mk, mv_t) = _pre_attn(h, moe_in, B, S, lw, tabs)
        fox_gain = jnp.broadcast_to(fox_out_norm[l].reshape(FOX_WIDTH, 1), (FOX_WIDTH, ta))
        mla_gain = jnp.broadcast_to(mla_out_norm[l].reshape(MLA_WIDTH, 1), (MLA_WIDTH, ta))
        fox = _attention(fq_t, fk, fv_t, fox_gain, B, S, "attn_fox")
        mla = _attention(mq_t, mk, mv_t, mla_gain, B, S, "attn_mla")
        h, xr, route, counts = _post_attn(fox, mla, h, lw)
        dest, block_group = _routing_plan(route, counts, n_blocks)
        dest3 = dest.reshape(T // tt, 1, tt)
        xs = _dispatch(xr, dest3, n_blocks * MOE_BLOCK)
        ys = _moe(xs, block_group, l, w_gate, w_up, w_down)
        moe_in = (dest3, ys)
    out = _final(h, moe_in[0], moe_in[1], final_norm.reshape(1, D))
    return out.reshape(B, S, D)
```

```python
import functools
import math

import jax
import jax.numpy as jnp
from jax import lax
from jax.experimental import pallas as pl
from jax.experimental.pallas import tpu as pltpu

FOX_HEADS = 8
FOX_HEAD_DIM = 64
FOX_WIDTH = FOX_HEADS * FOX_HEAD_DIM
MLA_HEADS = 8
MLA_NOPE_DIM = 64
MLA_ROPE_DIM = 32
MLA_V_DIM = 64
MLA_WIDTH = MLA_HEADS * MLA_V_DIM
Q_LORA_RANK = 256
KV_LORA_RANK = 128
ROPE_THETA = 10000.0
N_GROUPS = 8
EXPERTS_PER_GROUP = 4
N_EXPERTS = N_GROUPS * EXPERTS_PER_GROUP
D_EXPERT = 256
NORM_EPS = 1e-6

LANES = 128
SUBLANES = 8
BF16_SUBLANES = 16
VMEM_LIMIT_BYTES = 56 * 1024 * 1024

ROW_TILE = 512
ATTN_TILE = 512
MOE_BLOCK = 512
ROW_DMA_UNROLL = 8
HEAD_PAD = 128
V_ROWS = MLA_V_DIM + BF16_SUBLANES
ROPE_HALF = MLA_ROPE_DIM // 2
NEG_BIG = -1e30
LOG2E = math.log2(math.e)
DECAY_PARTS = 3

GID_INDEX = EXPERTS_PER_GROUP
RANK_INDEX = EXPERTS_PER_GROUP + 1
ROUTE_ROWS = SUBLANES


def _cparams(*semantics):
    return pltpu.CompilerParams(dimension_semantics=semantics, vmem_limit_bytes=VMEM_LIMIT_BYTES)


def _rms(x, gain):
    ms = jnp.mean(x * x, axis=-1, keepdims=True)
    return x * lax.rsqrt(ms + NORM_EPS) * gain


def _dot(a, b):
    return jnp.dot(a, b, preferred_element_type=jnp.float32)


def _dot_nt(a, b):
    return lax.dot_general(a, b, (((1,), (1,)), ((), ())), preferred_element_type=jnp.float32)


def _split_bf16(x, parts):
    out = []
    for _ in range(parts - 1):
        piece = x.astype(jnp.bfloat16).astype(jnp.float32)
        out.append(piece)
        x = x - piece
    out.append(x.astype(jnp.bfloat16).astype(jnp.float32))
    return out


def _rope_table_kernel(pos_ref, invf_ref, cos_ref, sin_ref):
    ang = pos_ref[...].astype(jnp.float32) * invf_ref[...]
    cos_ref[...] = jnp.cos(ang)
    sin_ref[...] = jnp.sin(ang)


def _rope_tables(positions):
    T = positions.size
    inv_freq = ROPE_THETA ** (-jnp.arange(ROPE_HALF, dtype=jnp.float32) / ROPE_HALF)
    ct = min(T, 4096)
    return pl.pallas_call(
        _rope_table_kernel,
        grid=(T // ct,),
        in_specs=[pl.BlockSpec((1, ct), lambda i: (0, i)),
                  pl.BlockSpec((ROPE_HALF, 1), lambda i: (0, 0))],
        out_specs=[pl.BlockSpec((ROPE_HALF, ct), lambda i: (0, i))] * 2,
        out_shape=[jax.ShapeDtypeStruct((ROPE_HALF, T), jnp.float32)] * 2,
        compiler_params=_cparams("parallel"),
        name="rope_tables",
    )(positions.reshape(1, T), inv_freq.reshape(ROPE_HALF, 1))


def _gather_rows(src_hbm, idx_ref, dst_ref, sem, n_rows):
    def issue(r, carry):
        pltpu.make_async_copy(src_hbm.at[pl.ds(idx_ref[0, 0, r], 1)],
                              dst_ref.at[pl.ds(r, 1)], sem).start()
        return carry
    lax.fori_loop(0, n_rows, issue, 0, unroll=ROW_DMA_UNROLL)
    pltpu.make_async_copy(src_hbm.at[pl.ds(0, n_rows)], dst_ref, sem).wait()


def _pre_attn_kernel(*refs, has_moe_in, tt):
    if has_moe_in:
        dest_ref, h_ref, ys_hbm = refs[:3]
        refs = refs[3:]
    else:
        h_ref = refs[0]
        refs = refs[1:]
    (anorm_ref, wrow_ref, wfq_ref, wfv_ref, bf_ref, tri_ref, place_ref, qnorm_ref, wqm_ref, wqs_ref,
     kvnorm_ref, wkn_ref, wv_ref, cos_ref, sin_ref) = refs[:15]
    refs = refs[15:]
    if has_moe_in:
        hout_ref = refs[0]
        refs = refs[1:]
    fq_ref, fk_ref, fv_ref, qm_ref, km_ref, vm_ref = refs[:6]
    refs = refs[6:]
    carry_ref = refs[0]
    if has_moe_in:
        ybuf_ref, sem = refs[1:3]

    si = pl.program_id(1)
    bf16 = jnp.bfloat16
    half = LANES // 2

    h = h_ref[...]
    if has_moe_in:
        _gather_rows(ys_hbm, dest_ref, ybuf_ref, sem, tt)
        h = h + ybuf_ref[...]
        hout_ref[...] = h

    xn = _rms(h, anorm_ref[...]).astype(bf16)
    lane = lax.broadcasted_iota(jnp.int32, (tt, LANES), 1)
    sub = lax.broadcasted_iota(jnp.int32, (BF16_SUBLANES, tt), 0)
    ones_rows = jnp.ones((BF16_SUBLANES, tt), bf16)

    fq_t = _dot_nt(wfq_ref[...], xn) * (FOX_HEAD_DIM ** -0.5 * LOG2E)
    q_aug = jnp.where(sub < DECAY_PARTS, -1.0, 0.0).astype(bf16)
    q_zero = jnp.zeros((HEAD_PAD - FOX_HEAD_DIM - BF16_SUBLANES, tt), bf16)
    for hd in range(FOX_HEADS):
        fq_ref[0, hd, 0, 0:FOX_HEAD_DIM, :] = fq_t[hd * FOX_HEAD_DIM:(hd + 1) * FOX_HEAD_DIM].astype(bf16)
        fq_ref[0, hd, 0, FOX_HEAD_DIM:FOX_HEAD_DIM + BF16_SUBLANES, :] = q_aug
        fq_ref[0, hd, 0, FOX_HEAD_DIM + BF16_SUBLANES:, :] = q_zero
    fv_t = _dot_nt(wfv_ref[...], xn)
    vd = MLA_V_DIM
    for hd in range(FOX_HEADS):
        fv_ref[0, hd, 0, 0:vd, :] = fv_t[hd * vd:(hd + 1) * vd].astype(bf16)
        fv_ref[0, hd, 0, vd:, :] = ones_rows

    fw = FOX_WIDTH
    o = fw
    z = _dot(xn, wrow_ref[:, o:o + LANES]) + bf_ref[...]
    o += LANES
    logf = jnp.minimum(z, 0.0) - jnp.log1p(jnp.exp(-jnp.abs(z)))
    pieces = jnp.concatenate([p_.astype(bf16) for p_ in _split_bf16(logf, DECAY_PARTS)], axis=1)
    csum = _dot(tri_ref[...], pieces)

    @pl.when(si == 0)
    def _():
        carry_ref[...] = jnp.zeros_like(carry_ref)

    dcum = csum[:, 0:LANES] + csum[:, LANES:2 * LANES] + csum[:, 2 * LANES:] + carry_ref[...]
    carry_ref[...] = dcum[tt - 1:tt, :]
    d_parts = _split_bf16(dcum * LOG2E, DECAY_PARTS)
    d_sel = jnp.where(lane < FOX_HEADS, d_parts[0],
                      jnp.where(lane < 2 * FOX_HEADS, d_parts[1],
                                jnp.where(lane < 3 * FOX_HEADS, d_parts[2], 0.0)))
    k_aug = _dot(d_sel.astype(bf16), place_ref[...])

    fk = _dot(xn, wrow_ref[:, 0:fw])
    for p in range(FOX_HEADS // 2):
        blk = fk[:, p * LANES:(p + 1) * LANES]
        for hh, b_ in ((0, blk), (1, pltpu.roll(blk, half, axis=1))):
            sl = slice((2 * p + hh) * HEAD_PAD, (2 * p + hh + 1) * HEAD_PAD)
            fk_ref[:, sl] = jnp.where(lane < FOX_HEAD_DIM, b_, k_aug[:, sl]).astype(bf16)

    cq = _rms(_dot(xn, wrow_ref[:, o:o + Q_LORA_RANK]), qnorm_ref[...]).astype(bf16)
    o += Q_LORA_RANK
    qm_t = _dot_nt(wqm_ref[...], cq)
    qs_t = _dot_nt(wqs_ref[...], cq)
    cos_t = cos_ref[...]
    sin_t = sin_ref[...]
    scale = (MLA_NOPE_DIM + MLA_ROPE_DIM) ** -0.5 * LOG2E
    q_zero = jnp.zeros((HEAD_PAD - MLA_NOPE_DIM - MLA_ROPE_DIM, tt), jnp.float32)
    for hd in range(MLA_HEADS):
        blk = qm_t[hd * HEAD_PAD:(hd + 1) * HEAD_PAD]
        rows = [blk[0:MLA_NOPE_DIM]]
        for j in range(2):
            main = blk[MLA_NOPE_DIM + j * ROPE_HALF:MLA_NOPE_DIM + (j + 1) * ROPE_HALF]
            swap = qs_t[hd * MLA_ROPE_DIM + j * ROPE_HALF:hd * MLA_ROPE_DIM + (j + 1) * ROPE_HALF]
            rows.append(main * cos_t + swap * sin_t)
        rows.append(q_zero)
        qm_ref[0, hd, 0] = (jnp.concatenate(rows, axis=0) * scale).astype(bf16)

    ckv = _rms(_dot(xn, wrow_ref[:, o:o + KV_LORA_RANK]), kvnorm_ref[...]).astype(bf16)
    o += KV_LORA_RANK
    mv_t = _dot_nt(wv_ref[...], ckv)
    for hd in range(MLA_HEADS):
        vm_ref[0, hd, 0, 0:vd, :] = mv_t[hd * vd:(hd + 1) * vd].astype(bf16)
        vm_ref[0, hd, 0, vd:, :] = ones_rows
    k_nope = _dot(ckv, wkn_ref[...])
    z_rows = jnp.zeros((MLA_ROPE_DIM, tt), jnp.float32)
    csk = jnp.concatenate([sin_t, sin_t, z_rows, cos_t, cos_t, z_rows], axis=0).T
    kr = _dot(xn, wrow_ref[:, o:o + LANES]) * csk
    kr = kr + pltpu.roll(kr, half, axis=1)
    k_pe = jnp.where(lane >= MLA_NOPE_DIM, kr, 0.0)
    for hd in range(MLA_HEADS):
        sl = slice(hd * HEAD_PAD, (hd + 1) * HEAD_PAD)
        km_ref[:, sl] = (k_nope[:, sl] + k_pe).astype(bf16)


def _pre_attn(h, moe_in, B, S, lw, tabs):
    T, D = h.shape
    tt = min(ROW_TILE, S)
    ns = S // tt
    has_moe_in = moe_in is not None
    row = lambda b, s: (b * ns + s, 0)
    const = lambda b, s: (0, 0)
    cos_t, sin_t = tabs

    def rows(width):
        return pl.BlockSpec((tt, width), row)

    def full(a):
        return pl.BlockSpec(a.shape, const)

    weights = [lw["attn_norm"], lw["w_rows"], lw["w_fq_t"], lw["w_fv_t"], lw["b_f"], lw["tri_incl"],
               lw["place"], lw["q_norm"], lw["wq_main_t"], lw["wq_swap_t"], lw["kv_norm"],
               lw["wk_nope"], lw["wv_t"]]
    in_specs = [rows(D)]
    args = [h]
    if has_moe_in:
        dest3, ys = moe_in
        in_specs = [pl.BlockSpec((1, 1, tt), lambda b, s: (b * ns + s, 0, 0), memory_space=pltpu.SMEM),
                    rows(D), pl.BlockSpec(memory_space=pl.ANY)]
        args = [dest3, h, ys]
    col = lambda b, s: (0, b * ns + s)
    in_specs += [full(w) for w in weights]
    in_specs += [pl.BlockSpec((ROPE_HALF, tt), col), pl.BlockSpec((ROPE_HALF, tt), col)]
    args += weights + [cos_t, sin_t]

    bf16 = jnp.bfloat16
    qt_shape = jax.ShapeDtypeStruct((B, FOX_HEADS, ns, HEAD_PAD, tt), bf16)
    qt_spec = pl.BlockSpec((1, FOX_HEADS, 1, HEAD_PAD, tt), lambda b, s: (b, 0, s, 0, 0))
    k_shape = jax.ShapeDtypeStruct((T, FOX_HEADS * HEAD_PAD), bf16)
    k_spec = rows(FOX_HEADS * HEAD_PAD)
    vt_shape = jax.ShapeDtypeStruct((B, FOX_HEADS, ns, V_ROWS, tt), bf16)
    vt_spec = pl.BlockSpec((1, FOX_HEADS, 1, V_ROWS, tt), lambda b, s: (b, 0, s, 0, 0))
    out_shape = [qt_shape, k_shape, vt_shape] * 2
    out_specs = [qt_spec, k_spec, vt_spec] * 2
    scratch = [pltpu.VMEM((1, LANES), jnp.float32)]
    if has_moe_in:
        out_shape = [jax.ShapeDtypeStruct((T, D), jnp.float32)] + out_shape
        out_specs = [rows(D)] + out_specs
        scratch += [pltpu.VMEM((tt, D), jnp.float32), pltpu.SemaphoreType.DMA(())]

    outs = pl.pallas_call(
        functools.partial(_pre_attn_kernel, has_moe_in=has_moe_in, tt=tt),
        grid=(B, ns),
        in_specs=in_specs,
        out_specs=out_specs,
        out_shape=out_shape,
        scratch_shapes=scratch,
        compiler_params=_cparams("arbitrary", "arbitrary"),
        name="pre_attn",
    )(*args)
    if has_moe_in:
        return outs[0], outs[1:]
    return h, outs


def _attn_kernel(qt_ref, k_ref, vt_ref, g_ref, o_ref, s_scr, mc_scr, m_scr, acc_scr, *, ta, nq):
    vd = MLA_V_DIM

    def produce(q, kt, slot, masked):
        k0 = pl.multiple_of(kt * ta, ta)
        for hh in range(2):
            kk = k_ref[pl.ds(k0, ta), hh * HEAD_PAD:(hh + 1) * HEAD_PAD]
            s = _dot(kk, qt_ref[0, hh, q])
            if masked:
                r = lax.broadcasted_iota(jnp.int32, s.shape, 0)
                c = lax.broadcasted_iota(jnp.int32, s.shape, 1)
                s = jnp.where(r <= c, s, NEG_BIG)
            s_scr[slot, hh] = s
            mc_scr[slot, hh] = jnp.max(s, axis=0, keepdims=True)

    def consume(kt, slot):
        for hh in range(2):
            m_prev = m_scr[hh]
            m_new = jnp.maximum(m_prev, mc_scr[slot, hh])
            alpha = jnp.exp2(m_prev - m_new)
            p = jnp.exp2(s_scr[slot, hh] - m_new).astype(jnp.bfloat16)
            acc_scr[hh] = alpha * acc_scr[hh] + _dot(vt_ref[0, hh, kt], p)
            m_scr[hh] = m_new

    def reset():
        m_scr[...] = jnp.full_like(m_scr, NEG_BIG)
        acc_scr[...] = jnp.zeros_like(acc_scr)

    def finish(q):
        outs = []
        for hh in range(2):
            acc = acc_scr[hh]
            oh = acc[0:vd] / acc[vd:vd + 1]
            ms = jnp.mean(oh * oh, axis=0, keepdims=True)
            outs.append(oh * lax.rsqrt(ms + NORM_EPS))
        o_t = jnp.concatenate(outs, axis=0) * g_ref[...]
        o_ref[pl.ds(pl.multiple_of(q * ta, ta), ta), :] = o_t.T.astype(o_ref.dtype)

    reset()
    produce(0, 0, 0, True)

    def q_tile(n, carry):
        first = (n * (n + 1) // 2) % 2

        def step(j, slot):
            produce(n, j, 1 - slot, False)
            consume(jnp.where(j == 0, n, j - 1), slot)

        for par in range(2):
            @pl.when(first == par)
            def _():
                def two_steps(u, c):
                    step(2 * u, par)
                    step(2 * u + 1, 1 - par)
                    return c

                lax.fori_loop(0, n // 2, two_steps, 0)

                @pl.when(n % 2 == 1)
                def _():
                    step(n - 1, par)

        last_slot = (first + n) % 2
        last_kt = jnp.maximum(n - 1, 0)
        for par in range(2):
            @pl.when((last_slot == par) & (n < nq - 1))
            def _():
                produce(n + 1, n + 1, 1 - par, True)
                consume(last_kt, par)
                finish(n)
                reset()

        @pl.when(n == nq - 1)
        def _():
            consume(last_kt, ((nq - 1) * nq // 2 + nq - 1) % 2)
            finish(n)

        return carry

    lax.fori_loop(0, nq, q_tile, 0)


def _attention(qt, k, vt, gain_rep, B, S, name):
    T = k.shape[0]
    ta = min(ATTN_TILE, S)
    nq = S // ta
    n_pairs = qt.shape[1] // 2
    return pl.pallas_call(
        functools.partial(_attn_kernel, ta=ta, nq=nq),
        grid=(B, n_pairs),
        in_specs=[pl.BlockSpec((1, 2, nq, HEAD_PAD, ta), lambda b, j: (b, j, 0, 0, 0)),
                  pl.BlockSpec((S, 2 * HEAD_PAD), lambda b, j: (b, j)),
                  pl.BlockSpec((1, 2, nq, V_ROWS, ta), lambda b, j: (b, j, 0, 0, 0)),
                  pl.BlockSpec((LANES, ta), lambda b, j: (j, 0))],
        out_specs=pl.BlockSpec((S, LANES), lambda b, j: (b, j)),
        out_shape=jax.ShapeDtypeStruct((T, n_pairs * LANES), jnp.bfloat16),
        scratch_shapes=[pltpu.VMEM((2, 2, ta, ta), jnp.float32),
                        pltpu.VMEM((2, 2, 1, ta), jnp.float32),
                        pltpu.VMEM((2, 1, ta), jnp.float32),
                        pltpu.VMEM((2, V_ROWS, ta), jnp.float32)],
        compiler_params=_cparams("parallel", "parallel"),
        name=name,
    )(qt, k, vt, gain_rep)


def _post_attn_kernel(fox_ref, mla_ref, h_ref, wo_ref, fnorm_ref, wrh_ref, wrl_ref, tri_ref,
                      h2_ref, xr_ref, route_ref, cnt_ref, carry_ref):
    i = pl.program_id(0)
    D = h_ref.shape[1]
    h2 = h_ref[...] + _dot(fox_ref[...], wo_ref[0:FOX_WIDTH, :]) + _dot(mla_ref[...], wo_ref[FOX_WIDTH:, :])
    h2_ref[...] = h2
    xn = _rms(h2, fnorm_ref[...])
    xr_ref[:, 0:D] = xn

    x_hi = xn.astype(jnp.bfloat16)
    x_lo = (xn - x_hi.astype(jnp.float32)).astype(jnp.bfloat16)
    logits = _dot(x_hi, wrh_ref[...]) + _dot(x_lo, wrh_ref[...]) + _dot(x_hi, wrl_ref[...])

    lt = logits.T
    tt = lt.shape[1]
    ng = N_GROUPS
    gl = lt[0:ng]
    sub = lax.broadcasted_iota(jnp.int32, (ng, tt), 0).astype(jnp.float32)

    gmax = jnp.max(gl, axis=0, keepdims=True)
    group_w = 1.0 / jnp.sum(jnp.exp(gl - gmax), axis=0, keepdims=True)
    gid = jnp.min(jnp.where(gl == gmax, sub, float(ng)), axis=0, keepdims=True)
    onehot = sub == gid

    ins = [jnp.sum(jnp.where(onehot, lt[ng * (j + 1):ng * (j + 2)], 0.0), axis=0, keepdims=True)
           for j in range(EXPERTS_PER_GROUP)]

    def first_argmax(vals):
        top = functools.reduce(jnp.maximum, vals)
        idx = jnp.full_like(top, float(len(vals) - 1))
        for j in range(len(vals) - 2, -1, -1):
            idx = jnp.where(vals[j] == top, float(j), idx)
        return top, idx

    e1, i1 = first_argmax(ins)
    e2, i2 = first_argmax([jnp.where(i1 == j, NEG_BIG, v) for j, v in enumerate(ins)])
    t = jnp.exp(e2 - e1)
    g1 = group_w / (1.0 + t)
    g2 = group_w * t / (1.0 + t)
    gates = [jnp.where(i1 == j, g1, jnp.where(i2 == j, g2, 0.0)) for j in range(EXPERTS_PER_GROUP)]

    @pl.when(i == 0)
    def _():
        carry_ref[...] = jnp.zeros_like(carry_ref)

    ones = jnp.where(onehot, 1.0, 0.0)
    prefix = _dot(ones.astype(jnp.bfloat16), tri_ref[...]) + carry_ref[:, 0:1]
    rank = jnp.sum(jnp.where(onehot, prefix, 0.0), axis=0, keepdims=True)
    carry = carry_ref[...] + jnp.sum(ones, axis=1, keepdims=True)
    carry_ref[...] = carry
    cnt_ref[...] = carry

    pad_rows = jnp.zeros((ROUTE_ROWS - EXPERTS_PER_GROUP - 2, tt), jnp.float32)
    info_t = jnp.concatenate(gates + [gid, rank, pad_rows], axis=0)
    route_ref[...] = info_t
    info_pad = jnp.zeros((LANES - ROUTE_ROWS, tt), jnp.float32)
    xr_ref[:, D:D + LANES] = jnp.concatenate([info_t, info_pad], axis=0).T


def _post_attn(fox, mla, h, lw):
    T, D = h.shape
    tt = min(ROW_TILE, T)
    row = lambda i: (i, 0)
    const = lambda i: (0, 0)
    weights = [lw["w_o"], lw["ffn_norm"], lw["wr_hi"], lw["wr_lo"], lw["tri_before"]]
    return pl.pallas_call(
        _post_attn_kernel,
        grid=(T // tt,),
        in_specs=[pl.BlockSpec((tt, FOX_WIDTH), row), pl.BlockSpec((tt, MLA_WIDTH), row),
                  pl.BlockSpec((tt, D), row)] + [pl.BlockSpec(w.shape, const) for w in weights],
        out_specs=[pl.BlockSpec((tt, D), row), pl.BlockSpec((tt, D + LANES), row),
                   pl.BlockSpec((ROUTE_ROWS, tt), lambda i: (0, i)), pl.BlockSpec((N_GROUPS, LANES), const)],
        out_shape=[jax.ShapeDtypeStruct((T, D), jnp.float32),
                   jax.ShapeDtypeStruct((T, D + LANES), jnp.float32),
                   jax.ShapeDtypeStruct((ROUTE_ROWS, T), jnp.float32),
                   jax.ShapeDtypeStruct((N_GROUPS, LANES), jnp.float32)],
        scratch_shapes=[pltpu.VMEM((N_GROUPS, LANES), jnp.float32)],
        compiler_params=_cparams("arbitrary"),
        name="post_attn",
    )(fox, mla, h, *weights)


def _dispatch_kernel(fill_ref, dest_ref, xr_ref, xs_hbm, zero_ref, sem, fill_sem, *, tt, n_sorted):
    first = pl.program_id(0) == 0

    def fill_copies(act):
        for g in range(N_GROUPS):
            start, length = fill_ref[g], fill_ref[N_GROUPS + g]
            head = (-start) & (SUBLANES - 1)
            for r in range(SUBLANES - 1):
                @pl.when(r < head)
                def _(start=start, r=r):
                    act(pltpu.make_async_copy(zero_ref.at[pl.ds(0, 1)], xs_hbm.at[pl.ds(start + r, 1)], fill_sem))
            base, rest = start + head, length - head
            size = MOE_BLOCK // 2
            while size >= SUBLANES:
                @pl.when((rest & size) != 0)
                def _(base=base, rest=rest, size=size):
                    at = pl.multiple_of(base + (rest & ~(2 * size - 1)), SUBLANES)
                    act(pltpu.make_async_copy(zero_ref.at[pl.ds(0, size)], xs_hbm.at[pl.ds(at, size)], fill_sem))
                size //= 2
        used = fill_ref[2 * N_GROUPS]
        for blk in range(N_GROUPS):
            @pl.when(used + (blk + 1) * MOE_BLOCK <= n_sorted)
            def _(blk=blk):
                at = pl.multiple_of(used + blk * MOE_BLOCK, MOE_BLOCK)
                act(pltpu.make_async_copy(zero_ref, xs_hbm.at[pl.ds(at, MOE_BLOCK)], fill_sem))

    @pl.when(first)
    def _():
        zero_ref[...] = jnp.zeros_like(zero_ref)
        fill_copies(lambda cp: cp.start())

    def issue(r, carry):
        pltpu.make_async_copy(xr_ref.at[pl.ds(r, 1)], xs_hbm.at[pl.ds(dest_ref[0, 0, r], 1)], sem).start()
        return carry
    lax.fori_loop(0, tt, issue, 0, unroll=ROW_DMA_UNROLL)
    pltpu.make_async_copy(xr_ref, xs_hbm.at[pl.ds(0, tt)], sem).wait()

    @pl.when(first)
    def _():
        fill_copies(lambda cp: cp.wait())


def _dispatch(xr, dest3, fill, n_sorted):
    T, W = xr.shape
    tt = dest3.shape[2]
    grid_spec = pltpu.PrefetchScalarGridSpec(
        num_scalar_prefetch=1,
        grid=(T // tt,),
        in_specs=[pl.BlockSpec((1, 1, tt), lambda i, f: (i, 0, 0), memory_space=pltpu.SMEM),
                  pl.BlockSpec((tt, W), lambda i, f: (i, 0))],
        out_specs=pl.BlockSpec(memory_space=pl.ANY),
        scratch_shapes=[pltpu.VMEM((MOE_BLOCK, W), xr.dtype),
                        pltpu.SemaphoreType.DMA(()), pltpu.SemaphoreType.DMA(())],
    )
    return pl.pallas_call(
        functools.partial(_dispatch_kernel, tt=tt, n_sorted=n_sorted),
        grid_spec=grid_spec,
        out_shape=jax.ShapeDtypeStruct((n_sorted, W), xr.dtype),
        compiler_params=_cparams("arbitrary"),
        name="moe_dispatch",
    )(fill, dest3, xr)


def _moe_kernel(bg_ref, nv_ref, xs_ref, wg32_ref, wu32_ref, wd32_ref, ys_ref, wg_ref, wu_ref, wd_ref):
    D = ys_ref.shape[1]
    b = pl.program_id(0)
    g = bg_ref[b]
    n_valid = nv_ref[b]

    @pl.when((b == 0) | (g != bg_ref[jnp.maximum(b - 1, 0)]))
    def _():
        wg_ref[...] = wg32_ref[...].astype(jnp.bfloat16)
        wu_ref[...] = wu32_ref[...].astype(jnp.bfloat16)
        wd_ref[...] = wd32_ref[...].astype(jnp.bfloat16)

    @pl.when(n_valid == 0)
    def _():
        ys_ref[...] = jnp.zeros_like(ys_ref)

    @pl.when(n_valid > 0)
    def _():
        x = xs_ref[:, 0:D].astype(jnp.bfloat16)
        info = xs_ref[:, D:D + LANES]
        y = jnp.zeros(ys_ref.shape, jnp.float32)
        for j in range(EXPERTS_PER_GROUP):
            a = _dot(x, wg_ref[j])
            u = _dot(x, wu_ref[j])
            hj = (a * jax.nn.sigmoid(a) * u).astype(jnp.bfloat16)
            y = y + info[:, j:j + 1] * _dot(hj, wd_ref[j])
        ys_ref[...] = y


def _moe(xs, block_group, block_valid, layer, w_gate, w_up, w_down):
    n_sorted, W = xs.shape
    D = W - LANES
    nb = n_sorted // MOE_BLOCK
    epg = EXPERTS_PER_GROUP
    grid_spec = pltpu.PrefetchScalarGridSpec(
        num_scalar_prefetch=2,
        grid=(nb,),
        in_specs=[pl.BlockSpec((MOE_BLOCK, W), lambda b, bg, nv: (b, 0)),
                  pl.BlockSpec((None, epg, D, D_EXPERT), lambda b, bg, nv: (layer, bg[b], 0, 0)),
                  pl.BlockSpec((None, epg, D, D_EXPERT), lambda b, bg, nv: (layer, bg[b], 0, 0)),
                  pl.BlockSpec((None, epg, D_EXPERT, D), lambda b, bg, nv: (layer, bg[b], 0, 0))],
        out_specs=pl.BlockSpec((MOE_BLOCK, D), lambda b, bg, nv: (b, 0)),
        scratch_shapes=[pltpu.VMEM((epg, D, D_EXPERT), jnp.bfloat16),
                        pltpu.VMEM((epg, D, D_EXPERT), jnp.bfloat16),
                        pltpu.VMEM((epg, D_EXPERT, D), jnp.bfloat16)],
    )
    return pl.pallas_call(
        _moe_kernel,
        grid_spec=grid_spec,
        out_shape=jax.ShapeDtypeStruct((n_sorted, D), jnp.float32),
        compiler_params=_cparams("arbitrary"),
        name="moe_experts",
    )(block_group, block_valid, xs, w_gate, w_up, w_down)


def _final_kernel(dest_ref, h_ref, ys_hbm, gain_ref, o_ref, ybuf_ref, sem, *, tt):
    _gather_rows(ys_hbm, dest_ref, ybuf_ref, sem, tt)
    o_ref[...] = _rms(h_ref[...] + ybuf_ref[...], gain_ref[...])


def _final(h, dest3, ys, gain):
    T, D = h.shape
    tt = dest3.shape[2]
    return pl.pallas_call(
        functools.partial(_final_kernel, tt=tt),
        grid=(T // tt,),
        in_specs=[pl.BlockSpec((1, 1, tt), lambda i: (i, 0, 0), memory_space=pltpu.SMEM),
                  pl.BlockSpec((tt, D), lambda i: (i, 0)),
                  pl.BlockSpec(memory_space=pl.ANY),
                  pl.BlockSpec((1, D), lambda i: (0, 0))],
        out_specs=pl.BlockSpec((tt, D), lambda i: (i, 0)),
        out_shape=jax.ShapeDtypeStruct((T, D), jnp.float32),
        scratch_shapes=[pltpu.VMEM((tt, D), jnp.float32), pltpu.SemaphoreType.DMA(())],
        compiler_params=_cparams("arbitrary"),
        name="final_norm",
    )(dest3, h, ys, gain)


def _prep_layer(l, tt, w_in, b_f, w_uq, w_ukv, w_o, w_group, w_router,
                attn_norm, q_norm, kv_norm, ffn_norm):
    bf16 = jnp.bfloat16
    D = w_in.shape[1]
    fw = FOX_WIDTH
    o_f = 3 * fw
    o_cq = o_f + FOX_HEADS
    o_ckv = o_cq + Q_LORA_RANK
    o_kr = o_ckv + KV_LORA_RANK
    wi = w_in[l]
    w_f = wi[:, o_f:o_cq]
    w_f3 = jnp.concatenate([w_f] * DECAY_PARTS + [jnp.zeros((D, LANES - DECAY_PARTS * FOX_HEADS), wi.dtype)], axis=1)
    b_f3 = jnp.concatenate([b_f[l]] * DECAY_PARTS + [jnp.zeros((LANES - DECAY_PARTS * FOX_HEADS,), b_f.dtype)])
    kr = wi[:, o_kr:o_kr + MLA_ROPE_DIM]
    kr_swap = jnp.concatenate([-kr[:, ROPE_HALF:], kr[:, :ROPE_HALF]], axis=1)
    zpad = jnp.zeros((D, MLA_ROPE_DIM), wi.dtype)
    w_rows = jnp.concatenate([wi[:, fw:2 * fw], w_f3, wi[:, o_cq:o_kr], kr_swap, zpad, kr, zpad], axis=1)

    src = jnp.arange(LANES)
    dst = (src % FOX_HEADS) * HEAD_PAD + FOX_HEAD_DIM + src // FOX_HEADS
    place = ((jnp.arange(FOX_HEADS * HEAD_PAD)[None, :] == dst[:, None])
             & (src[:, None] < DECAY_PARTS * FOX_HEADS)).astype(bf16)

    qd = MLA_NOPE_DIM + MLA_ROPE_DIM
    wq = w_uq[l].reshape(Q_LORA_RANK, MLA_HEADS, qd)
    zq = jnp.zeros((Q_LORA_RANK, MLA_HEADS, HEAD_PAD - qd), wq.dtype)
    wq_main = jnp.concatenate([wq, zq], axis=2).reshape(Q_LORA_RANK, MLA_HEADS * HEAD_PAD)
    x1 = wq[:, :, MLA_NOPE_DIM:MLA_NOPE_DIM + ROPE_HALF]
    x2 = wq[:, :, MLA_NOPE_DIM + ROPE_HALF:]
    wq_swap = jnp.concatenate([-x2, x1], axis=2).reshape(Q_LORA_RANK, MLA_HEADS * MLA_ROPE_DIM)

    wkv = w_ukv[l].reshape(KV_LORA_RANK, MLA_HEADS, MLA_NOPE_DIM + MLA_V_DIM)
    wk_nope = jnp.concatenate([wkv[:, :, :MLA_NOPE_DIM],
                               jnp.zeros((KV_LORA_RANK, MLA_HEADS, HEAD_PAD - MLA_NOPE_DIM), wkv.dtype)], axis=2)
    wv = wkv[:, :, MLA_NOPE_DIM:].reshape(KV_LORA_RANK, MLA_WIDTH)

    w_exp = w_router[l].reshape(D, N_GROUPS, EXPERTS_PER_GROUP).transpose(0, 2, 1).reshape(D, N_EXPERTS)
    wr = jnp.concatenate([w_group[l], w_exp,
                          jnp.zeros((D, LANES - N_GROUPS - N_EXPERTS), w_group.dtype)], axis=1)
    wr_hi = wr.astype(bf16)
    wr_lo = (wr - wr_hi.astype(jnp.float32)).astype(bf16)

    ar = jnp.arange(tt)
    return dict(
        attn_norm=attn_norm[l].reshape(1, D),
        w_rows=w_rows.astype(bf16),
        w_fq_t=wi[:, 0:fw].T.astype(bf16),
        w_fv_t=wi[:, 2 * fw:3 * fw].T.astype(bf16),
        b_f=b_f3.reshape(1, LANES),
        tri_incl=(ar[:, None] >= ar[None, :]).astype(bf16),
        tri_before=(ar[:, None] < ar[None, :]).astype(bf16),
        place=place,
        q_norm=q_norm[l].reshape(1, Q_LORA_RANK),
        wq_main_t=wq_main.T.astype(bf16),
        wq_swap_t=wq_swap.T.astype(bf16),
        kv_norm=kv_norm[l].reshape(1, KV_LORA_RANK),
        wk_nope=wk_nope.reshape(KV_LORA_RANK, MLA_HEADS * HEAD_PAD).astype(bf16),
        wv_t=wv.T.astype(bf16),
        w_o=w_o[l].astype(bf16),
        ffn_norm=ffn_norm[l].reshape(1, D),
        wr_hi=wr_hi,
        wr_lo=wr_lo,
    )


def _routing_plan(route, counts, n_blocks):
    gid = route[GID_INDEX].astype(jnp.int32)
    rank = route[RANK_INDEX].astype(jnp.int32)
    cnt = counts[:, 0].astype(jnp.int32)
    padded = (cnt + MOE_BLOCK - 1) // MOE_BLOCK * MOE_BLOCK
    pend = jnp.cumsum(padded)
    pstart = pend - padded
    dest = pstart[gid] + rank
    block_start = jnp.arange(n_blocks, dtype=jnp.int32) * MOE_BLOCK
    block_group = jnp.sum((pend[None, :] <= block_start[:, None]).astype(jnp.int32), axis=1)
    block_group = jnp.minimum(block_group, N_GROUPS - 1)
    block_valid = jnp.clip((pstart + cnt)[block_group] - block_start, 0, MOE_BLOCK)
    fill = jnp.concatenate([pstart + cnt, padded - cnt, pend[-1:]])
    return dest, block_group, block_valid, fill


def kernel(x, positions, attn_norm, w_in, b_f, q_norm, w_uq, kv_norm, w_ukv, fox_out_norm, mla_out_norm, w_o,
           ffn_norm, w_group, w_router, w_gate, w_up, w_down, final_norm):
    B, S, D = x.shape
    T = B * S
    depth = w_in.shape[0]
    ta = min(ATTN_TILE, S)
    tt = min(ROW_TILE, S)
    n_blocks = -(-(T + N_GROUPS * (MOE_BLOCK - 1)) // MOE_BLOCK)

    cos_t, sin_t = _rope_tables(positions)
    tabs = (cos_t, sin_t)

    h = x.reshape(T, D)
    moe_in = None
    for l in range(depth):
        lw = _prep_layer(l, tt, w_in, b_f, w_uq, w_ukv, w_o, w_group, w_router,
                         attn_norm, q_norm, kv_norm, ffn_norm)
        h, (fq_t, fk, fv_t, mq_t, mk, mv_t) = _pre_attn(h, moe_in, B, S, lw, tabs)
        fox_gain = jnp.broadcast_to(fox_out_norm[l].reshape(FOX_WIDTH, 1), (FOX_WIDTH, ta))
        mla_gain = jnp.broadcast_to(mla_out_norm[l].reshape(MLA_WIDTH, 1), (MLA_WIDTH, ta))
        fox = _attention(fq_t, fk, fv_t, fox_gain, B, S, "attn_fox")
        mla = _attention(mq_t, mk, mv_t, mla_gain, B, S, "attn_mla")
        h, xr, route, counts = _post_attn(fox, mla, h, lw)
        dest, block_group, block_valid, fill = _routing_plan(route, counts, n_blocks)
        dest3 = dest.reshape(T // tt, 1, tt)
        xs = _dispatch(xr, dest3, fill, n_blocks * MOE_BLOCK)
        ys = _moe(xs, block_group, block_valid, l, w_gate, w_up, w_down)
        moe_in = (dest3, ys)
    out = _final(h, moe_in[0], moe_in[1], final_norm.reshape(1, D))
    return out.reshape(B, S, D)
```

```python
import functools
import math

import jax
import jax.numpy as jnp
from jax import lax
from jax.experimental import pallas as pl
from jax.experimental.pallas import tpu as pltpu

FOX_HEADS = 8
FOX_HEAD_DIM = 64
FOX_WIDTH = FOX_HEADS * FOX_HEAD_DIM
MLA_HEADS = 8
MLA_NOPE_DIM = 64
MLA_ROPE_DIM = 32
MLA_V_DIM = 64
MLA_WIDTH = MLA_HEADS * MLA_V_DIM
Q_LORA_RANK = 256
KV_LORA_RANK = 128
ROPE_THETA = 10000.0
N_GROUPS = 8
EXPERTS_PER_GROUP = 4
N_EXPERTS = N_GROUPS * EXPERTS_PER_GROUP
D_EXPERT = 256
NORM_EPS = 1e-6

LANES = 128
SUBLANES = 8
BF16_SUBLANES = 16
VMEM_LIMIT_BYTES = 56 * 1024 * 1024

ROW_TILE = 512
ATTN_TILE = 512
MOE_BLOCK = 512
ROW_DMA_UNROLL = 8
HEAD_PAD = 128
V_ROWS = MLA_V_DIM + BF16_SUBLANES
ROPE_HALF = MLA_ROPE_DIM // 2
NEG_BIG = -1e30
LOG2E = math.log2(math.e)
DECAY_PARTS = 3

GID_INDEX = EXPERTS_PER_GROUP
RANK_INDEX = EXPERTS_PER_GROUP + 1
ROUTE_ROWS = SUBLANES


def _cparams(*semantics):
    return pltpu.CompilerParams(dimension_semantics=semantics, vmem_limit_bytes=VMEM_LIMIT_BYTES)


def _rms(x, gain):
    ms = jnp.mean(x * x, axis=-1, keepdims=True)
    return x * lax.rsqrt(ms + NORM_EPS) * gain


def _dot(a, b):
    return jnp.dot(a, b, preferred_element_type=jnp.float32)


def _dot_nt(a, b):
    return lax.dot_general(a, b, (((1,), (1,)), ((), ())), preferred_element_type=jnp.float32)


def _split_bf16(x, parts):
    out = []
    for _ in range(parts - 1):
        piece = x.astype(jnp.bfloat16).astype(jnp.float32)
        out.append(piece)
        x = x - piece
    out.append(x.astype(jnp.bfloat16).astype(jnp.float32))
    return out


def _rope_table_kernel(pos_ref, invf_ref, cos_ref, sin_ref):
    ang = pos_ref[...].astype(jnp.float32) * invf_ref[...]
    cos_ref[...] = jnp.cos(ang)
    sin_ref[...] = jnp.sin(ang)


def _rope_tables(positions):
    T = positions.size
    inv_freq = ROPE_THETA ** (-jnp.arange(ROPE_HALF, dtype=jnp.float32) / ROPE_HALF)
    ct = min(T, 4096)
    return pl.pallas_call(
        _rope_table_kernel,
        grid=(T // ct,),
        in_specs=[pl.BlockSpec((1, ct), lambda i: (0, i)),
                  pl.BlockSpec((ROPE_HALF, 1), lambda i: (0, 0))],
        out_specs=[pl.BlockSpec((ROPE_HALF, ct), lambda i: (0, i))] * 2,
        out_shape=[jax.ShapeDtypeStruct((ROPE_HALF, T), jnp.float32)] * 2,
        compiler_params=_cparams("parallel"),
        name="rope_tables",
    )(positions.reshape(1, T), inv_freq.reshape(ROPE_HALF, 1))


def _gather_rows(src_hbm, idx_ref, idx_next_ref, buf_ref, sems, step, n_steps, n_rows):
    slot = step % 2

    def request(ref, to_slot):
        def issue(r, carry):
            pltpu.make_async_copy(src_hbm.at[pl.ds(ref[0, 0, r], 1)],
                                  buf_ref.at[to_slot, pl.ds(r, 1)], sems.at[to_slot]).start()
            return carry
        lax.fori_loop(0, n_rows, issue, 0, unroll=ROW_DMA_UNROLL)

    @pl.when(step == 0)
    def _():
        request(idx_ref, slot)

    @pl.when(step + 1 < n_steps)
    def _():
        request(idx_next_ref, 1 - slot)

    pltpu.make_async_copy(src_hbm.at[pl.ds(0, n_rows)], buf_ref.at[slot], sems.at[slot]).wait()
    return buf_ref[slot]


def _pre_attn_kernel(*refs, has_moe_in, tt):
    if has_moe_in:
        dest_ref, dest_next_ref, h_ref, ys_hbm = refs[:4]
        refs = refs[4:]
    else:
        h_ref = refs[0]
        refs = refs[1:]
    (anorm_ref, wrow_ref, wfq_ref, wfv_ref, bf_ref, tri_ref, place_ref, qnorm_ref, wqm_ref, wqs_ref,
     kvnorm_ref, wkn_ref, wv_ref, cos_ref, sin_ref) = refs[:15]
    refs = refs[15:]
    if has_moe_in:
        hout_ref = refs[0]
        refs = refs[1:]
    fq_ref, fk_ref, fv_ref, qm_ref, km_ref, vm_ref = refs[:6]
    refs = refs[6:]
    carry_ref = refs[0]
    if has_moe_in:
        ybuf_ref, sems = refs[1:3]

    si = pl.program_id(1)
    bf16 = jnp.bfloat16
    half = LANES // 2

    h = h_ref[...]
    if has_moe_in:
        step = pl.program_id(0) * pl.num_programs(1) + si
        n_steps = pl.num_programs(0) * pl.num_programs(1)
        h = h + _gather_rows(ys_hbm, dest_ref, dest_next_ref, ybuf_ref, sems, step, n_steps, tt)
        hout_ref[...] = h

    xn = _rms(h, anorm_ref[...]).astype(bf16)
    lane = lax.broadcasted_iota(jnp.int32, (tt, LANES), 1)
    sub = lax.broadcasted_iota(jnp.int32, (BF16_SUBLANES, tt), 0)
    ones_rows = jnp.ones((BF16_SUBLANES, tt), bf16)

    fq_t = _dot_nt(wfq_ref[...], xn) * (FOX_HEAD_DIM ** -0.5 * LOG2E)
    q_aug = jnp.where(sub < DECAY_PARTS, -1.0, 0.0).astype(bf16)
    q_zero = jnp.zeros((HEAD_PAD - FOX_HEAD_DIM - BF16_SUBLANES, tt), bf16)
    for hd in range(FOX_HEADS):
        fq_ref[0, hd, 0, 0:FOX_HEAD_DIM, :] = fq_t[hd * FOX_HEAD_DIM:(hd + 1) * FOX_HEAD_DIM].astype(bf16)
        fq_ref[0, hd, 0, FOX_HEAD_DIM:FOX_HEAD_DIM + BF16_SUBLANES, :] = q_aug
        fq_ref[0, hd, 0, FOX_HEAD_DIM + BF16_SUBLANES:, :] = q_zero
    fv_t = _dot_nt(wfv_ref[...], xn)
    vd = MLA_V_DIM
    for hd in range(FOX_HEADS):
        fv_ref[0, hd, 0, 0:vd, :] = fv_t[hd * vd:(hd + 1) * vd].astype(bf16)
        fv_ref[0, hd, 0, vd:, :] = ones_rows

    fw = FOX_WIDTH
    o = fw
    z = _dot(xn, wrow_ref[:, o:o + LANES]) + bf_ref[...]
    o += LANES
    logf = jnp.minimum(z, 0.0) - jnp.log1p(jnp.exp(-jnp.abs(z)))
    pieces = jnp.concatenate([p_.astype(bf16) for p_ in _split_bf16(logf, DECAY_PARTS)], axis=1)
    csum = _dot(tri_ref[...], pieces)

    @pl.when(si == 0)
    def _():
        carry_ref[...] = jnp.zeros_like(carry_ref)

    dcum = csum[:, 0:LANES] + csum[:, LANES:2 * LANES] + csum[:, 2 * LANES:] + carry_ref[...]
    carry_ref[...] = dcum[tt - 1:tt, :]
    d_parts = _split_bf16(dcum * LOG2E, DECAY_PARTS)
    d_sel = jnp.where(lane < FOX_HEADS, d_parts[0],
                      jnp.where(lane < 2 * FOX_HEADS, d_parts[1],
                                jnp.where(lane < 3 * FOX_HEADS, d_parts[2], 0.0)))
    k_aug = _dot(d_sel.astype(bf16), place_ref[...])

    fk = _dot(xn, wrow_ref[:, 0:fw])
    for p in range(FOX_HEADS // 2):
        blk = fk[:, p * LANES:(p + 1) * LANES]
        for hh, b_ in ((0, blk), (1, pltpu.roll(blk, half, axis=1))):
            sl = slice((2 * p + hh) * HEAD_PAD, (2 * p + hh + 1) * HEAD_PAD)
            fk_ref[:, sl] = jnp.where(lane < FOX_HEAD_DIM, b_, k_aug[:, sl]).astype(bf16)

    cq = _rms(_dot(xn, wrow_ref[:, o:o + Q_LORA_RANK]), qnorm_ref[...]).astype(bf16)
    o += Q_LORA_RANK
    qm_t = _dot_nt(wqm_ref[...], cq)
    qs_t = _dot_nt(wqs_ref[...], cq)
    cos_t = cos_ref[...]
    sin_t = sin_ref[...]
    scale = (MLA_NOPE_DIM + MLA_ROPE_DIM) ** -0.5 * LOG2E
    q_zero = jnp.zeros((HEAD_PAD - MLA_NOPE_DIM - MLA_ROPE_DIM, tt), jnp.float32)
    for hd in range(MLA_HEADS):
        blk = qm_t[hd * HEAD_PAD:(hd + 1) * HEAD_PAD]
        rows = [blk[0:MLA_NOPE_DIM]]
        for j in range(2):
            main = blk[MLA_NOPE_DIM + j * ROPE_HALF:MLA_NOPE_DIM + (j + 1) * ROPE_HALF]
            swap = qs_t[hd * MLA_ROPE_DIM + j * ROPE_HALF:hd * MLA_ROPE_DIM + (j + 1) * ROPE_HALF]
            rows.append(main * cos_t + swap * sin_t)
        rows.append(q_zero)
        qm_ref[0, hd, 0] = (jnp.concatenate(rows, axis=0) * scale).astype(bf16)

    ckv = _rms(_dot(xn, wrow_ref[:, o:o + KV_LORA_RANK]), kvnorm_ref[...]).astype(bf16)
    o += KV_LORA_RANK
    mv_t = _dot_nt(wv_ref[...], ckv)
    for hd in range(MLA_HEADS):
        vm_ref[0, hd, 0, 0:vd, :] = mv_t[hd * vd:(hd + 1) * vd].astype(bf16)
        vm_ref[0, hd, 0, vd:, :] = ones_rows
    k_nope = _dot(ckv, wkn_ref[...])
    z_rows = jnp.zeros((MLA_ROPE_DIM, tt), jnp.float32)
    csk = jnp.concatenate([sin_t, sin_t, z_rows, cos_t, cos_t, z_rows], axis=0).T
    kr = _dot(xn, wrow_ref[:, o:o + LANES]) * csk
    kr = kr + pltpu.roll(kr, half, axis=1)
    k_pe = jnp.where(lane >= MLA_NOPE_DIM, kr, 0.0)
    for hd in range(MLA_HEADS):
        sl = slice(hd * HEAD_PAD, (hd + 1) * HEAD_PAD)
        km_ref[:, sl] = (k_nope[:, sl] + k_pe).astype(bf16)


def _pre_attn(h, moe_in, B, S, lw, tabs):
    T, D = h.shape
    tt = min(ROW_TILE, S)
    ns = S // tt
    has_moe_in = moe_in is not None
    row = lambda b, s: (b * ns + s, 0)
    const = lambda b, s: (0, 0)
    cos_t, sin_t = tabs

    def rows(width):
        return pl.BlockSpec((tt, width), row)

    def full(a):
        return pl.BlockSpec(a.shape, const)

    weights = [lw["attn_norm"], lw["w_rows"], lw["w_fq_t"], lw["w_fv_t"], lw["b_f"], lw["tri_incl"],
               lw["place"], lw["q_norm"], lw["wq_main_t"], lw["wq_swap_t"], lw["kv_norm"],
               lw["wk_nope"], lw["wv_t"]]
    in_specs = [rows(D)]
    args = [h]
    if has_moe_in:
        dest3, ys = moe_in
        last = B * ns - 1
        in_specs = [pl.BlockSpec((1, 1, tt), lambda b, s: (b * ns + s, 0, 0), memory_space=pltpu.SMEM),
                    pl.BlockSpec((1, 1, tt), lambda b, s: (jnp.minimum(b * ns + s + 1, last), 0, 0),
                                 memory_space=pltpu.SMEM),
                    rows(D), pl.BlockSpec(memory_space=pl.ANY)]
        args = [dest3, dest3, h, ys]
    col = lambda b, s: (0, b * ns + s)
    in_specs += [full(w) for w in weights]
    in_specs += [pl.BlockSpec((ROPE_HALF, tt), col), pl.BlockSpec((ROPE_HALF, tt), col)]
    args += weights + [cos_t, sin_t]

    bf16 = jnp.bfloat16
    qt_shape = jax.ShapeDtypeStruct((B, FOX_HEADS, ns, HEAD_PAD, tt), bf16)
    qt_spec = pl.BlockSpec((1, FOX_HEADS, 1, HEAD_PAD, tt), lambda b, s: (b, 0, s, 0, 0))
    k_shape = jax.ShapeDtypeStruct((T, FOX_HEADS * HEAD_PAD), bf16)
    k_spec = rows(FOX_HEADS * HEAD_PAD)
    vt_shape = jax.ShapeDtypeStruct((B, FOX_HEADS, ns, V_ROWS, tt), bf16)
    vt_spec = pl.BlockSpec((1, FOX_HEADS, 1, V_ROWS, tt), lambda b, s: (b, 0, s, 0, 0))
    out_shape = [qt_shape, k_shape, vt_shape] * 2
    out_specs = [qt_spec, k_spec, vt_spec] * 2
    scratch = [pltpu.VMEM((1, LANES), jnp.float32)]
    if has_moe_in:
        out_shape = [jax.ShapeDtypeStruct((T, D), jnp.float32)] + out_shape
        out_specs = [rows(D)] + out_specs
        scratch += [pltpu.VMEM((2, tt, D), jnp.float32), pltpu.SemaphoreType.DMA((2,))]

    outs = pl.pallas_call(
        functools.partial(_pre_attn_kernel, has_moe_in=has_moe_in, tt=tt),
        grid=(B, ns),
        in_specs=in_specs,
        out_specs=out_specs,
        out_shape=out_shape,
        scratch_shapes=scratch,
        compiler_params=_cparams("arbitrary", "arbitrary"),
        name="pre_attn",
    )(*args)
    if has_moe_in:
        return outs[0], outs[1:]
    return h, outs


def _attn_kernel(qt_ref, k_ref, vt_ref, g_ref, o_ref, s_scr, mc_scr, m_scr, acc_scr, *, ta, nq):
    vd = MLA_V_DIM

    def produce(q, kt, slot, masked):
        k0 = pl.multiple_of(kt * ta, ta)
        for hh in range(2):
            kk = k_ref[pl.ds(k0, ta), hh * HEAD_PAD:(hh + 1) * HEAD_PAD]
            qt = qt_ref[0, hh, q]
            if not masked:
                s = _dot(kk, qt)
                s_scr[slot, hh] = s
                mc_scr[slot, hh] = jnp.max(s, axis=0, keepdims=True)
                continue
            hk = ta // 2
            top = _dot(kk[0:hk], qt)
            low = _dot(kk[hk:], qt[:, hk:])
            top = jnp.where(lax.broadcasted_iota(jnp.int32, top.shape, 0)
                            <= lax.broadcasted_iota(jnp.int32, top.shape, 1), top, NEG_BIG)
            low = jnp.where(lax.broadcasted_iota(jnp.int32, low.shape, 0)
                            <= lax.broadcasted_iota(jnp.int32, low.shape, 1), low, NEG_BIG)
            s_scr[slot, hh, 0:hk, :] = top
            s_scr[slot, hh, hk:, 0:hk] = jnp.full((hk, hk), NEG_BIG, jnp.float32)
            s_scr[slot, hh, hk:, hk:] = low
            top_max = jnp.max(top, axis=0, keepdims=True)
            mc_scr[slot, hh, :, 0:hk] = top_max[:, 0:hk]
            mc_scr[slot, hh, :, hk:] = jnp.maximum(top_max[:, hk:], jnp.max(low, axis=0, keepdims=True))

    def consume(kt, slot):
        for hh in range(2):
            m_prev = m_scr[hh]
            m_new = jnp.maximum(m_prev, mc_scr[slot, hh])
            alpha = jnp.exp2(m_prev - m_new)
            p = jnp.exp2(s_scr[slot, hh] - m_new).astype(jnp.bfloat16)
            acc_scr[hh] = alpha * acc_scr[hh] + _dot(vt_ref[0, hh, kt], p)
            m_scr[hh] = m_new

    def reset():
        m_scr[...] = jnp.full_like(m_scr, NEG_BIG)
        acc_scr[...] = jnp.zeros_like(acc_scr)

    def finish(q):
        outs = []
        for hh in range(2):
            acc = acc_scr[hh]
            oh = acc[0:vd] / acc[vd:vd + 1]
            ms = jnp.mean(oh * oh, axis=0, keepdims=True)
            outs.append(oh * lax.rsqrt(ms + NORM_EPS))
        o_t = jnp.concatenate(outs, axis=0) * g_ref[...]
        o_ref[pl.ds(pl.multiple_of(q * ta, ta), ta), :] = o_t.T.astype(o_ref.dtype)

    reset()
    produce(0, 0, 0, True)

    def q_tile(n, carry):
        first = (n * (n + 1) // 2) % 2

        def step(j, slot):
            produce(n, j, 1 - slot, False)
            consume(jnp.where(j == 0, n, j - 1), slot)

        for par in range(2):
            @pl.when(first == par)
            def _():
                def two_steps(u, c):
                    step(2 * u, par)
                    step(2 * u + 1, 1 - par)
                    return c

                lax.fori_loop(0, n // 2, two_steps, 0)

                @pl.when(n % 2 == 1)
                def _():
                    step(n - 1, par)

        last_slot = (first + n) % 2
        last_kt = jnp.maximum(n - 1, 0)
        for par in range(2):
            @pl.when((last_slot == par) & (n < nq - 1))
            def _():
                produce(n + 1, n + 1, 1 - par, True)
                consume(last_kt, par)
                finish(n)
                reset()

        @pl.when(n == nq - 1)
        def _():
            consume(last_kt, ((nq - 1) * nq // 2 + nq - 1) % 2)
            finish(n)

        return carry

    lax.fori_loop(0, nq, q_tile, 0)


def _attention(qt, k, vt, gain_rep, B, S, name):
    T = k.shape[0]
    ta = min(ATTN_TILE, S)
    nq = S // ta
    n_pairs = qt.shape[1] // 2
    return pl.pallas_call(
        functools.partial(_attn_kernel, ta=ta, nq=nq),
        grid=(B, n_pairs),
        in_specs=[pl.BlockSpec((1, 2, nq, HEAD_PAD, ta), lambda b, j: (b, j, 0, 0, 0)),
                  pl.BlockSpec((S, 2 * HEAD_PAD), lambda b, j: (b, j)),
                  pl.BlockSpec((1, 2, nq, V_ROWS, ta), lambda b, j: (b, j, 0, 0, 0)),
                  pl.BlockSpec((LANES, ta), lambda b, j: (j, 0))],
        out_specs=pl.BlockSpec((S, LANES), lambda b, j: (b, j)),
        out_shape=jax.ShapeDtypeStruct((T, n_pairs * LANES), jnp.bfloat16),
        scratch_shapes=[pltpu.VMEM((2, 2, ta, ta), jnp.float32),
                        pltpu.VMEM((2, 2, 1, ta), jnp.float32),
                        pltpu.VMEM((2, 1, ta), jnp.float32),
                        pltpu.VMEM((2, V_ROWS, ta), jnp.float32)],
        compiler_params=_cparams("parallel", "parallel"),
        name=name,
    )(qt, k, vt, gain_rep)


def _post_attn_kernel(fox_ref, mla_ref, h_ref, wo_ref, fnorm_ref, wrh_ref, wrl_ref, tri_ref,
                      h2_ref, xr_ref, route_ref, cnt_ref, carry_ref):
    i = pl.program_id(0)
    D = h_ref.shape[1]
    h2 = h_ref[...] + _dot(fox_ref[...], wo_ref[0:FOX_WIDTH, :]) + _dot(mla_ref[...], wo_ref[FOX_WIDTH:, :])
    h2_ref[...] = h2
    xn = _rms(h2, fnorm_ref[...])
    xr_ref[:, 0:D] = xn

    x_hi = xn.astype(jnp.bfloat16)
    x_lo = (xn - x_hi.astype(jnp.float32)).astype(jnp.bfloat16)
    logits = _dot(x_hi, wrh_ref[...]) + _dot(x_lo, wrh_ref[...]) + _dot(x_hi, wrl_ref[...])

    lt = logits.T
    tt = lt.shape[1]
    ng = N_GROUPS
    gl = lt[0:ng]
    sub = lax.broadcasted_iota(jnp.int32, (ng, tt), 0).astype(jnp.float32)

    gmax = jnp.max(gl, axis=0, keepdims=True)
    group_w = 1.0 / jnp.sum(jnp.exp(gl - gmax), axis=0, keepdims=True)
    gid = jnp.min(jnp.where(gl == gmax, sub, float(ng)), axis=0, keepdims=True)
    onehot = sub == gid

    ins = [jnp.sum(jnp.where(onehot, lt[ng * (j + 1):ng * (j + 2)], 0.0), axis=0, keepdims=True)
           for j in range(EXPERTS_PER_GROUP)]

    def first_argmax(vals):
        top = functools.reduce(jnp.maximum, vals)
        idx = jnp.full_like(top, float(len(vals) - 1))
        for j in range(len(vals) - 2, -1, -1):
            idx = jnp.where(vals[j] == top, float(j), idx)
        return top, idx

    e1, i1 = first_argmax(ins)
    e2, i2 = first_argmax([jnp.where(i1 == j, NEG_BIG, v) for j, v in enumerate(ins)])
    t = jnp.exp(e2 - e1)
    g1 = group_w / (1.0 + t)
    g2 = group_w * t / (1.0 + t)
    gates = [jnp.where(i1 == j, g1, jnp.where(i2 == j, g2, 0.0)) for j in range(EXPERTS_PER_GROUP)]

    @pl.when(i == 0)
    def _():
        carry_ref[...] = jnp.zeros_like(carry_ref)

    ones = jnp.where(onehot, 1.0, 0.0)
    prefix = _dot(ones.astype(jnp.bfloat16), tri_ref[...]) + carry_ref[:, 0:1]
    rank = jnp.sum(jnp.where(onehot, prefix, 0.0), axis=0, keepdims=True)
    carry = carry_ref[...] + jnp.sum(ones, axis=1, keepdims=True)
    carry_ref[...] = carry
    cnt_ref[...] = carry

    pad_rows = jnp.zeros((ROUTE_ROWS - EXPERTS_PER_GROUP - 2, tt), jnp.float32)
    info_t = jnp.concatenate(gates + [gid, rank, pad_rows], axis=0)
    route_ref[...] = info_t
    info_pad = jnp.zeros((LANES - ROUTE_ROWS, tt), jnp.float32)
    xr_ref[:, D:D + LANES] = jnp.concatenate([info_t, info_pad], axis=0).T


def _post_attn(fox, mla, h, lw):
    T, D = h.shape
    tt = min(ROW_TILE, T)
    row = lambda i: (i, 0)
    const = lambda i: (0, 0)
    weights = [lw["w_o"], lw["ffn_norm"], lw["wr_hi"], lw["wr_lo"], lw["tri_before"]]
    return pl.pallas_call(
        _post_attn_kernel,
        grid=(T // tt,),
        in_specs=[pl.BlockSpec((tt, FOX_WIDTH), row), pl.BlockSpec((tt, MLA_WIDTH), row),
                  pl.BlockSpec((tt, D), row)] + [pl.BlockSpec(w.shape, const) for w in weights],
        out_specs=[pl.BlockSpec((tt, D), row), pl.BlockSpec((tt, D + LANES), row),
                   pl.BlockSpec((ROUTE_ROWS, tt), lambda i: (0, i)), pl.BlockSpec((N_GROUPS, LANES), const)],
        out_shape=[jax.ShapeDtypeStruct((T, D), jnp.float32),
                   jax.ShapeDtypeStruct((T, D + LANES), jnp.float32),
                   jax.ShapeDtypeStruct((ROUTE_ROWS, T), jnp.float32),
                   jax.ShapeDtypeStruct((N_GROUPS, LANES), jnp.float32)],
        scratch_shapes=[pltpu.VMEM((N_GROUPS, LANES), jnp.float32)],
        compiler_params=_cparams("arbitrary"),
        name="post_attn",
    )(fox, mla, h, *weights)


def _dispatch_kernel(fill_ref, dest_ref, xr_ref, xs_hbm, zero_ref, sem, fill_sem, *, tt, n_sorted):
    first = pl.program_id(0) == 0

    def fill_copies(act):
        for g in range(N_GROUPS):
            start, length = fill_ref[g], fill_ref[N_GROUPS + g]
            head = (-start) & (SUBLANES - 1)
            for r in range(SUBLANES - 1):
                @pl.when(r < head)
                def _(start=start, r=r):
                    act(pltpu.make_async_copy(zero_ref.at[pl.ds(0, 1)], xs_hbm.at[pl.ds(start + r, 1)], fill_sem))
            base, rest = start + head, length - head
            size = MOE_BLOCK // 2
            while size >= SUBLANES:
                @pl.when((rest & size) != 0)
                def _(base=base, rest=rest, size=size):
                    at = pl.multiple_of(base + (rest & ~(2 * size - 1)), SUBLANES)
                    act(pltpu.make_async_copy(zero_ref.at[pl.ds(0, size)], xs_hbm.at[pl.ds(at, size)], fill_sem))
                size //= 2
        used = fill_ref[2 * N_GROUPS]
        for blk in range(N_GROUPS):
            @pl.when(used + (blk + 1) * MOE_BLOCK <= n_sorted)
            def _(blk=blk):
                at = pl.multiple_of(used + blk * MOE_BLOCK, MOE_BLOCK)
                act(pltpu.make_async_copy(zero_ref, xs_hbm.at[pl.ds(at, MOE_BLOCK)], fill_sem))

    @pl.when(first)
    def _():
        zero_ref[...] = jnp.zeros_like(zero_ref)
        fill_copies(lambda cp: cp.start())

    def issue(r, carry):
        pltpu.make_async_copy(xr_ref.at[pl.ds(r, 1)], xs_hbm.at[pl.ds(dest_ref[0, 0, r], 1)], sem).start()
        return carry
    lax.fori_loop(0, tt, issue, 0, unroll=ROW_DMA_UNROLL)
    pltpu.make_async_copy(xr_ref, xs_hbm.at[pl.ds(0, tt)], sem).wait()

    @pl.when(first)
    def _():
        fill_copies(lambda cp: cp.wait())


def _dispatch(xr, dest3, fill, n_sorted):
    T, W = xr.shape
    tt = dest3.shape[2]
    grid_spec = pltpu.PrefetchScalarGridSpec(
        num_scalar_prefetch=1,
        grid=(T // tt,),
        in_specs=[pl.BlockSpec((1, 1, tt), lambda i, f: (i, 0, 0), memory_space=pltpu.SMEM),
                  pl.BlockSpec((tt, W), lambda i, f: (i, 0))],
        out_specs=pl.BlockSpec(memory_space=pl.ANY),
        scratch_shapes=[pltpu.VMEM((MOE_BLOCK, W), xr.dtype),
                        pltpu.SemaphoreType.DMA(()), pltpu.SemaphoreType.DMA(())],
    )
    return pl.pallas_call(
        functools.partial(_dispatch_kernel, tt=tt, n_sorted=n_sorted),
        grid_spec=grid_spec,
        out_shape=jax.ShapeDtypeStruct((n_sorted, W), xr.dtype),
        compiler_params=_cparams("arbitrary"),
        name="moe_dispatch",
    )(fill, dest3, xr)


def _moe_kernel(bg_ref, nv_ref, xs_ref, wg32_ref, wu32_ref, wd32_ref, ys_ref, wg_ref, wu_ref, wd_ref):
    D = ys_ref.shape[1]
    b = pl.program_id(0)
    g = bg_ref[b]
    n_valid = nv_ref[b]

    @pl.when((b == 0) | (g != bg_ref[jnp.maximum(b - 1, 0)]))
    def _():
        wg_ref[...] = wg32_ref[...].astype(jnp.bfloat16)
        wu_ref[...] = wu32_ref[...].astype(jnp.bfloat16)
        wd_ref[...] = wd32_ref[...].astype(jnp.bfloat16)

    @pl.when(n_valid == 0)
    def _():
        ys_ref[...] = jnp.zeros_like(ys_ref)

    @pl.when(n_valid > 0)
    def _():
        x = xs_ref[:, 0:D].astype(jnp.bfloat16)
        info = xs_ref[:, D:D + LANES]
        y = jnp.zeros(ys_ref.shape, jnp.float32)
        for j in range(EXPERTS_PER_GROUP):
            a = _dot(x, wg_ref[j])
            u = _dot(x, wu_ref[j])
            hj = (a * jax.nn.sigmoid(a) * u).astype(jnp.bfloat16)
            y = y + info[:, j:j + 1] * _dot(hj, wd_ref[j])
        ys_ref[...] = y


def _moe(xs, block_group, block_valid, layer, w_gate, w_up, w_down):
    n_sorted, W = xs.shape
    D = W - LANES
    nb = n_sorted // MOE_BLOCK
    epg = EXPERTS_PER_GROUP
    grid_spec = pltpu.PrefetchScalarGridSpec(
        num_scalar_prefetch=2,
        grid=(nb,),
        in_specs=[pl.BlockSpec((MOE_BLOCK, W), lambda b, bg, nv: (b, 0)),
                  pl.BlockSpec((None, epg, D, D_EXPERT), lambda b, bg, nv: (layer, bg[b], 0, 0)),
                  pl.BlockSpec((None, epg, D, D_EXPERT), lambda b, bg, nv: (layer, bg[b], 0, 0)),
                  pl.BlockSpec((None, epg, D_EXPERT, D), lambda b, bg, nv: (layer, bg[b], 0, 0))],
        out_specs=pl.BlockSpec((MOE_BLOCK, D), lambda b, bg, nv: (b, 0)),
        scratch_shapes=[pltpu.VMEM((epg, D, D_EXPERT), jnp.bfloat16),
                        pltpu.VMEM((epg, D, D_EXPERT), jnp.bfloat16),
                        pltpu.VMEM((epg, D_EXPERT, D), jnp.bfloat16)],
    )
    return pl.pallas_call(
        _moe_kernel,
        grid_spec=grid_spec,
        out_shape=jax.ShapeDtypeStruct((n_sorted, D), jnp.float32),
        compiler_params=_cparams("arbitrary"),
        name="moe_experts",
    )(block_group, block_valid, xs, w_gate, w_up, w_down)


def _final_kernel(dest_ref, dest_next_ref, h_ref, ys_hbm, gain_ref, o_ref, ybuf_ref, sems, *, tt):
    y = _gather_rows(ys_hbm, dest_ref, dest_next_ref, ybuf_ref, sems,
                     pl.program_id(0), pl.num_programs(0), tt)
    o_ref[...] = _rms(h_ref[...] + y, gain_ref[...])


def _final(h, dest3, ys, gain):
    T, D = h.shape
    tt = dest3.shape[2]
    last = T // tt - 1
    return pl.pallas_call(
        functools.partial(_final_kernel, tt=tt),
        grid=(T // tt,),
        in_specs=[pl.BlockSpec((1, 1, tt), lambda i: (i, 0, 0), memory_space=pltpu.SMEM),
                  pl.BlockSpec((1, 1, tt), lambda i: (jnp.minimum(i + 1, last), 0, 0), memory_space=pltpu.SMEM),
                  pl.BlockSpec((tt, D), lambda i: (i, 0)),
                  pl.BlockSpec(memory_space=pl.ANY),
                  pl.BlockSpec((1, D), lambda i: (0, 0))],
        out_specs=pl.BlockSpec((tt, D), lambda i: (i, 0)),
        out_shape=jax.ShapeDtypeStruct((T, D), jnp.float32),
        scratch_shapes=[pltpu.VMEM((2, tt, D), jnp.float32), pltpu.SemaphoreType.DMA((2,))],
        compiler_params=_cparams("arbitrary"),
        name="final_norm",
    )(dest3, dest3, h, ys, gain)


def _prep_layer(l, tt, w_in, b_f, w_uq, w_ukv, w_o, w_group, w_router,
                attn_norm, q_norm, kv_norm, ffn_norm):
    bf16 = jnp.bfloat16
    D = w_in.shape[1]
    fw = FOX_WIDTH
    o_f = 3 * fw
    o_cq = o_f + FOX_HEADS
    o_ckv = o_cq + Q_LORA_RANK
    o_kr = o_ckv + KV_LORA_RANK
    wi = w_in[l]
    w_f = wi[:, o_f:o_cq]
    w_f3 = jnp.concatenate([w_f] * DECAY_PARTS + [jnp.zeros((D, LANES - DECAY_PARTS * FOX_HEADS), wi.dtype)], axis=1)
    b_f3 = jnp.concatenate([b_f[l]] * DECAY_PARTS + [jnp.zeros((LANES - DECAY_PARTS * FOX_HEADS,), b_f.dtype)])
    kr = wi[:, o_kr:o_kr + MLA_ROPE_DIM]
    kr_swap = jnp.concatenate([-kr[:, ROPE_HALF:], kr[:, :ROPE_HALF]], axis=1)
    zpad = jnp.zeros((D, MLA_ROPE_DIM), wi.dtype)
    w_rows = jnp.concatenate([wi[:, fw:2 * fw], w_f3, wi[:, o_cq:o_kr], kr_swap, zpad, kr, zpad], axis=1)

    src = jnp.arange(LANES)
    dst = (src % FOX_HEADS) * HEAD_PAD + FOX_HEAD_DIM + src // FOX_HEADS
    place = ((jnp.arange(FOX_HEADS * HEAD_PAD)[None, :] == dst[:, None])
             & (src[:, None] < DECAY_PARTS * FOX_HEADS)).astype(bf16)

    qd = MLA_NOPE_DIM + MLA_ROPE_DIM
    wq = w_uq[l].reshape(Q_LORA_RANK, MLA_HEADS, qd)
    zq = jnp.zeros((Q_LORA_RANK, MLA_HEADS, HEAD_PAD - qd), wq.dtype)
    wq_main = jnp.concatenate([wq, zq], axis=2).reshape(Q_LORA_RANK, MLA_HEADS * HEAD_PAD)
    x1 = wq[:, :, MLA_NOPE_DIM:MLA_NOPE_DIM + ROPE_HALF]
    x2 = wq[:, :, MLA_NOPE_DIM + ROPE_HALF:]
    wq_swap = jnp.concatenate([-x2, x1], axis=2).reshape(Q_LORA_RANK, MLA_HEADS * MLA_ROPE_DIM)

    wkv = w_ukv[l].reshape(KV_LORA_RANK, MLA_HEADS, MLA_NOPE_DIM + MLA_V_DIM)
    wk_nope = jnp.concatenate([wkv[:, :, :MLA_NOPE_DIM],
                               jnp.zeros((KV_LORA_RANK, MLA_HEADS, HEAD_PAD - MLA_NOPE_DIM), wkv.dtype)], axis=2)
    wv = wkv[:, :, MLA_NOPE_DIM:].reshape(KV_LORA_RANK, MLA_WIDTH)

    w_exp = w_router[l].reshape(D, N_GROUPS, EXPERTS_PER_GROUP).transpose(0, 2, 1).reshape(D, N_EXPERTS)
    wr = jnp.concatenate([w_group[l], w_exp,
                          jnp.zeros((D, LANES - N_GROUPS - N_EXPERTS), w_group.dtype)], axis=1)
    wr_hi = wr.astype(bf16)
    wr_lo = (wr - wr_hi.astype(jnp.float32)).astype(bf16)

    ar = jnp.arange(tt)
    return dict(
        attn_norm=attn_norm[l].reshape(1, D),
        w_rows=w_rows.astype(bf16),
        w_fq_t=wi[:, 0:fw].T.astype(bf16),
        w_fv_t=wi[:, 2 * fw:3 * fw].T.astype(bf16),
        b_f=b_f3.reshape(1, LANES),
        tri_incl=(ar[:, None] >= ar[None, :]).astype(bf16),
        tri_before=(ar[:, None] < ar[None, :]).astype(bf16),
        place=place,
        q_norm=q_norm[l].reshape(1, Q_LORA_RANK),
        wq_main_t=wq_main.T.astype(bf16),
        wq_swap_t=wq_swap.T.astype(bf16),
        kv_norm=kv_norm[l].reshape(1, KV_LORA_RANK),
        wk_nope=wk_nope.reshape(KV_LORA_RANK, MLA_HEADS * HEAD_PAD).astype(bf16),
        wv_t=wv.T.astype(bf16),
        w_o=w_o[l].astype(bf16),
        ffn_norm=ffn_norm[l].reshape(1, D),
        wr_hi=wr_hi,
        wr_lo=wr_lo,
    )


def _routing_plan(route, counts, n_blocks):
    gid = route[GID_INDEX].astype(jnp.int32)
    rank = route[RANK_INDEX].astype(jnp.int32)
    cnt = counts[:, 0].astype(jnp.int32)
    padded = (cnt + MOE_BLOCK - 1) // MOE_BLOCK * MOE_BLOCK
    pend = jnp.cumsum(padded)
    pstart = pend - padded
    dest = pstart[gid] + rank
    block_start = jnp.arange(n_blocks, dtype=jnp.int32) * MOE_BLOCK
    block_group = jnp.sum((pend[None, :] <= block_start[:, None]).astype(jnp.int32), axis=1)
    block_group = jnp.minimum(block_group, N_GROUPS - 1)
    block_valid = jnp.clip((pstart + cnt)[block_group] - block_start, 0, MOE_BLOCK)
    fill = jnp.concatenate([pstart + cnt, padded - cnt, pend[-1:]])
    return dest, block_group, block_valid, fill


def kernel(x, positions, attn_norm, w_in, b_f, q_norm, w_uq, kv_norm, w_ukv, fox_out_norm, mla_out_norm, w_o,
           ffn_norm, w_group, w_router, w_gate, w_up, w_down, final_norm):
    B, S, D = x.shape
    T = B * S
    depth = w_in.shape[0]
    ta = min(ATTN_TILE, S)
    tt = min(ROW_TILE, S)
    n_blocks = -(-(T + N_GROUPS * (MOE_BLOCK - 1)) // MOE_BLOCK)

    cos_t, sin_t = _rope_tables(positions)
    tabs = (cos_t, sin_t)

    h = x.reshape(T, D)
    moe_in = None
    for l in range(depth):
        lw = _prep_layer(l, tt, w_in, b_f, w_uq, w_ukv, w_o, w_group, w_router,
                         attn_norm, q_norm, kv_norm, ffn_norm)
        h, (fq_t, fk, fv_t, mq_t, mk, mv_t) = _pre_attn(h, moe_in, B, S, lw, tabs)
        fox_gain = jnp.broadcast_to(fox_out_norm[l].reshape(FOX_WIDTH, 1), (FOX_WIDTH, ta))
        mla_gain = jnp.broadcast_to(mla_out_norm[l].reshape(MLA_WIDTH, 1), (MLA_WIDTH, ta))
        fox = _attention(fq_t, fk, fv_t, fox_gain, B, S, "attn_fox")
        mla = _attention(mq_t, mk, mv_t, mla_gain, B, S, "attn_mla")
        h, xr, route, counts = _post_attn(fox, mla, h, lw)
        dest, block_group, block_valid, fill = _routing_plan(route, counts, n_blocks)
        dest3 = dest.reshape(T // tt, 1, tt)
        xs = _dispatch(xr, dest3, fill, n_blocks * MOE_BLOCK)
        ys = _moe(xs, block_group, block_valid, l, w_gate, w_up, w_down)
        moe_in = (dest3, ys)
    out = _final(h, moe_in[0], moe_in[1], final_norm.reshape(1, D))
    return out.reshape(B, S, D)
```

```python
import functools
import math

import jax
import jax.numpy as jnp
from jax import lax
from jax.experimental import pallas as pl
from jax.experimental.pallas import tpu as pltpu

FOX_HEADS = 8
FOX_HEAD_DIM = 64
FOX_WIDTH = FOX_HEADS * FOX_HEAD_DIM
MLA_HEADS = 8
MLA_NOPE_DIM = 64
MLA_ROPE_DIM = 32
MLA_V_DIM = 64
MLA_WIDTH = MLA_HEADS * MLA_V_DIM
Q_LORA_RANK = 256
KV_LORA_RANK = 128
ROPE_THETA = 10000.0
N_GROUPS = 8
EXPERTS_PER_GROUP = 4
N_EXPERTS = N_GROUPS * EXPERTS_PER_GROUP
D_EXPERT = 256
NORM_EPS = 1e-6

LANES = 128
SUBLANES = 8
BF16_SUBLANES = 16
VMEM_LIMIT_BYTES = 56 * 1024 * 1024

ROW_TILE = 512
ATTN_TILE = 512
MOE_BLOCK = 512
ROW_DMA_UNROLL = 8
HEAD_PAD = 128
V_ROWS = MLA_V_DIM + BF16_SUBLANES
ROPE_HALF = MLA_ROPE_DIM // 2
NEG_BIG = -1e30
LOG2E = math.log2(math.e)
DECAY_PARTS = 3

GID_INDEX = EXPERTS_PER_GROUP
RANK_INDEX = EXPERTS_PER_GROUP + 1
ROUTE_ROWS = SUBLANES


def _cparams(*semantics):
    return pltpu.CompilerParams(dimension_semantics=semantics, vmem_limit_bytes=VMEM_LIMIT_BYTES)


def _rms(x, gain):
    ms = jnp.mean(x * x, axis=-1, keepdims=True)
    return x * lax.rsqrt(ms + NORM_EPS) * gain


def _dot(a, b):
    return jnp.dot(a, b, preferred_element_type=jnp.float32)


def _dot_nt(a, b):
    return lax.dot_general(a, b, (((1,), (1,)), ((), ())), preferred_element_type=jnp.float32)


def _split_bf16(x, parts):
    out = []
    for _ in range(parts - 1):
        piece = x.astype(jnp.bfloat16).astype(jnp.float32)
        out.append(piece)
        x = x - piece
    out.append(x.astype(jnp.bfloat16).astype(jnp.float32))
    return out


def _rope_table_kernel(pos_ref, invf_ref, cos_ref, sin_ref):
    ang = pos_ref[...].astype(jnp.float32) * invf_ref[...]
    cos_ref[...] = jnp.cos(ang)
    sin_ref[...] = jnp.sin(ang)


def _rope_tables(positions):
    T = positions.size
    inv_freq = ROPE_THETA ** (-jnp.arange(ROPE_HALF, dtype=jnp.float32) / ROPE_HALF)
    ct = min(T, 4096)
    return pl.pallas_call(
        _rope_table_kernel,
        grid=(T // ct,),
        in_specs=[pl.BlockSpec((1, ct), lambda i: (0, i)),
                  pl.BlockSpec((ROPE_HALF, 1), lambda i: (0, 0))],
        out_specs=[pl.BlockSpec((ROPE_HALF, ct), lambda i: (0, i))] * 2,
        out_shape=[jax.ShapeDtypeStruct((ROPE_HALF, T), jnp.float32)] * 2,
        compiler_params=_cparams("parallel"),
        name="rope_tables",
    )(positions.reshape(1, T), inv_freq.reshape(ROPE_HALF, 1))


def _gather_rows(src_hbm, idx_ref, idx_next_ref, buf_ref, sems, step, n_steps, n_rows):
    slot = step % 2

    def request(ref, to_slot):
        def issue(r, carry):
            pltpu.make_async_copy(src_hbm.at[pl.ds(ref[0, 0, r], 1)],
                                  buf_ref.at[to_slot, pl.ds(r, 1)], sems.at[to_slot]).start()
            return carry
        lax.fori_loop(0, n_rows, issue, 0, unroll=ROW_DMA_UNROLL)

    @pl.when(step == 0)
    def _():
        request(idx_ref, slot)

    @pl.when(step + 1 < n_steps)
    def _():
        request(idx_next_ref, 1 - slot)

    pltpu.make_async_copy(src_hbm.at[pl.ds(0, n_rows)], buf_ref.at[slot], sems.at[slot]).wait()
    return buf_ref[slot]


def _pre_attn_kernel(*refs, has_moe_in, tt):
    if has_moe_in:
        dest_ref, dest_next_ref, h_ref, ys_hbm = refs[:4]
        refs = refs[4:]
    else:
        h_ref = refs[0]
        refs = refs[1:]
    (anorm_ref, wrow_ref, wfq_ref, wfv_ref, bf_ref, tri_ref, place_ref, qnorm_ref, wqm_ref, wqs_ref,
     kvnorm_ref, wkn_ref, wv_ref, cos_ref, sin_ref) = refs[:15]
    refs = refs[15:]
    if has_moe_in:
        hout_ref = refs[0]
        refs = refs[1:]
    fq_ref, fk_ref, fv_ref, qm_ref, km_ref, vm_ref = refs[:6]
    refs = refs[6:]
    carry_ref = refs[0]
    if has_moe_in:
        ybuf_ref, sems = refs[1:3]

    si = pl.program_id(1)
    bf16 = jnp.bfloat16
    half = LANES // 2

    h = h_ref[...]
    if has_moe_in:
        step = pl.program_id(0) * pl.num_programs(1) + si
        n_steps = pl.num_programs(0) * pl.num_programs(1)
        h = h + _gather_rows(ys_hbm, dest_ref, dest_next_ref, ybuf_ref, sems, step, n_steps, tt)
        hout_ref[...] = h

    xn = _rms(h, anorm_ref[...]).astype(bf16)
    lane = lax.broadcasted_iota(jnp.int32, (tt, LANES), 1)
    sub = lax.broadcasted_iota(jnp.int32, (BF16_SUBLANES, tt), 0)
    ones_rows = jnp.ones((BF16_SUBLANES, tt), bf16)

    fq_t = _dot_nt(wfq_ref[...], xn) * (FOX_HEAD_DIM ** -0.5 * LOG2E)
    q_aug = jnp.where(sub < DECAY_PARTS, -1.0, 0.0).astype(bf16)
    q_zero = jnp.zeros((HEAD_PAD - FOX_HEAD_DIM - BF16_SUBLANES, tt), bf16)
    for hd in range(FOX_HEADS):
        fq_ref[0, hd, 0, 0:FOX_HEAD_DIM, :] = fq_t[hd * FOX_HEAD_DIM:(hd + 1) * FOX_HEAD_DIM].astype(bf16)
        fq_ref[0, hd, 0, FOX_HEAD_DIM:FOX_HEAD_DIM + BF16_SUBLANES, :] = q_aug
        fq_ref[0, hd, 0, FOX_HEAD_DIM + BF16_SUBLANES:, :] = q_zero
    fv_t = _dot_nt(wfv_ref[...], xn)
    vd = MLA_V_DIM
    for hd in range(FOX_HEADS):
        fv_ref[0, hd, 0, 0:vd, :] = fv_t[hd * vd:(hd + 1) * vd].astype(bf16)
        fv_ref[0, hd, 0, vd:, :] = ones_rows

    fw = FOX_WIDTH
    o = fw
    z = _dot(xn, wrow_ref[:, o:o + LANES]) + bf_ref[...]
    o += LANES
    logf = jnp.minimum(z, 0.0) - jnp.log1p(jnp.exp(-jnp.abs(z)))
    pieces = jnp.concatenate([p_.astype(bf16) for p_ in _split_bf16(logf, DECAY_PARTS)], axis=1)
    csum = _dot(tri_ref[...], pieces)

    @pl.when(si == 0)
    def _():
        carry_ref[...] = jnp.zeros_like(carry_ref)

    dcum = csum[:, 0:LANES] + csum[:, LANES:2 * LANES] + csum[:, 2 * LANES:] + carry_ref[...]
    carry_ref[...] = dcum[tt - 1:tt, :]
    d_parts = _split_bf16(dcum * LOG2E, DECAY_PARTS)
    d_sel = jnp.where(lane < FOX_HEADS, d_parts[0],
                      jnp.where(lane < 2 * FOX_HEADS, d_parts[1],
                                jnp.where(lane < 3 * FOX_HEADS, d_parts[2], 0.0)))
    k_aug = _dot(d_sel.astype(bf16), place_ref[...])

    fk = _dot(xn, wrow_ref[:, 0:fw])
    for p in range(FOX_HEADS // 2):
        blk = fk[:, p * LANES:(p + 1) * LANES]
        for hh, b_ in ((0, blk), (1, pltpu.roll(blk, half, axis=1))):
            sl = slice((2 * p + hh) * HEAD_PAD, (2 * p + hh + 1) * HEAD_PAD)
            fk_ref[:, sl] = jnp.where(lane < FOX_HEAD_DIM, b_, k_aug[:, sl]).astype(bf16)

    cq = _rms(_dot(xn, wrow_ref[:, o:o + Q_LORA_RANK]), qnorm_ref[...]).astype(bf16)
    o += Q_LORA_RANK
    qm_t = _dot_nt(wqm_ref[...], cq)
    qs_t = _dot_nt(wqs_ref[...], cq)
    cos_t = cos_ref[...]
    sin_t = sin_ref[...]
    scale = (MLA_NOPE_DIM + MLA_ROPE_DIM) ** -0.5 * LOG2E
    q_zero = jnp.zeros((HEAD_PAD - MLA_NOPE_DIM - MLA_ROPE_DIM, tt), jnp.float32)
    for hd in range(MLA_HEADS):
        blk = qm_t[hd * HEAD_PAD:(hd + 1) * HEAD_PAD]
        rows = [blk[0:MLA_NOPE_DIM]]
        for j in range(2):
            main = blk[MLA_NOPE_DIM + j * ROPE_HALF:MLA_NOPE_DIM + (j + 1) * ROPE_HALF]
            swap = qs_t[hd * MLA_ROPE_DIM + j * ROPE_HALF:hd * MLA_ROPE_DIM + (j + 1) * ROPE_HALF]
            rows.append(main * cos_t + swap * sin_t)
        rows.append(q_zero)
        qm_ref[0, hd, 0] = (jnp.concatenate(rows, axis=0) * scale).astype(bf16)

    ckv = _rms(_dot(xn, wrow_ref[:, o:o + KV_LORA_RANK]), kvnorm_ref[...]).astype(bf16)
    o += KV_LORA_RANK
    mv_t = _dot_nt(wv_ref[...], ckv)
    for hd in range(MLA_HEADS):
        vm_ref[0, hd, 0, 0:vd, :] = mv_t[hd * vd:(hd + 1) * vd].astype(bf16)
        vm_ref[0, hd, 0, vd:, :] = ones_rows
    k_nope = _dot(ckv, wkn_ref[...])
    z_rows = jnp.zeros((MLA_ROPE_DIM, tt), jnp.float32)
    csk = jnp.concatenate([sin_t, sin_t, z_rows, cos_t, cos_t, z_rows], axis=0).T
    kr = _dot(xn, wrow_ref[:, o:o + LANES]) * csk
    kr = kr + pltpu.roll(kr, half, axis=1)
    k_pe = jnp.where(lane >= MLA_NOPE_DIM, kr, 0.0)
    for hd in range(MLA_HEADS):
        sl = slice(hd * HEAD_PAD, (hd + 1) * HEAD_PAD)
        km_ref[:, sl] = (k_nope[:, sl] + k_pe).astype(bf16)


def _pre_attn(h, moe_in, B, S, lw, tabs):
    T, D = h.shape
    tt = min(ROW_TILE, S)
    ns = S // tt
    has_moe_in = moe_in is not None
    row = lambda b, s: (b * ns + s, 0)
    const = lambda b, s: (0, 0)
    cos_t, sin_t = tabs

    def rows(width):
        return pl.BlockSpec((tt, width), row)

    def full(a):
        return pl.BlockSpec(a.shape, const)

    weights = [lw["attn_norm"], lw["w_rows"], lw["w_fq_t"], lw["w_fv_t"], lw["b_f"], lw["tri_incl"],
               lw["place"], lw["q_norm"], lw["wq_main_t"], lw["wq_swap_t"], lw["kv_norm"],
               lw["wk_nope"], lw["wv_t"]]
    in_specs = [rows(D)]
    args = [h]
    if has_moe_in:
        dest3, ys = moe_in
        last = B * ns - 1
        in_specs = [pl.BlockSpec((1, 1, tt), lambda b, s: (b * ns + s, 0, 0), memory_space=pltpu.SMEM),
                    pl.BlockSpec((1, 1, tt), lambda b, s: (jnp.minimum(b * ns + s + 1, last), 0, 0),
                                 memory_space=pltpu.SMEM),
                    rows(D), pl.BlockSpec(memory_space=pl.ANY)]
        args = [dest3, dest3, h, ys]
    col = lambda b, s: (0, b * ns + s)
    in_specs += [full(w) for w in weights]
    in_specs += [pl.BlockSpec((ROPE_HALF, tt), col), pl.BlockSpec((ROPE_HALF, tt), col)]
    args += weights + [cos_t, sin_t]

    bf16 = jnp.bfloat16
    qt_shape = jax.ShapeDtypeStruct((B, FOX_HEADS, ns, HEAD_PAD, tt), bf16)
    qt_spec = pl.BlockSpec((1, FOX_HEADS, 1, HEAD_PAD, tt), lambda b, s: (b, 0, s, 0, 0))
    k_shape = jax.ShapeDtypeStruct((T, FOX_HEADS * HEAD_PAD), bf16)
    k_spec = rows(FOX_HEADS * HEAD_PAD)
    vt_shape = jax.ShapeDtypeStruct((B, FOX_HEADS, ns, V_ROWS, tt), bf16)
    vt_spec = pl.BlockSpec((1, FOX_HEADS, 1, V_ROWS, tt), lambda b, s: (b, 0, s, 0, 0))
    out_shape = [qt_shape, k_shape, vt_shape] * 2
    out_specs = [qt_spec, k_spec, vt_spec] * 2
    scratch = [pltpu.VMEM((1, LANES), jnp.float32)]
    if has_moe_in:
        out_shape = [jax.ShapeDtypeStruct((T, D), jnp.float32)] + out_shape
        out_specs = [rows(D)] + out_specs
        scratch += [pltpu.VMEM((2, tt, D), jnp.float32), pltpu.SemaphoreType.DMA((2,))]

    outs = pl.pallas_call(
        functools.partial(_pre_attn_kernel, has_moe_in=has_moe_in, tt=tt),
        grid=(B, ns),
        in_specs=in_specs,
        out_specs=out_specs,
        out_shape=out_shape,
        scratch_shapes=scratch,
        compiler_params=_cparams("arbitrary", "arbitrary"),
        name="pre_attn",
    )(*args)
    if has_moe_in:
        return outs[0], outs[1:]
    return h, outs


def _attn_kernel(qt_ref, k_ref, vt_ref, g_ref, o_ref, s_scr, mc_scr, m_scr, acc_scr, *, ta, nq):
    vd = MLA_V_DIM

    def produce(q, kt, slot, masked):
        k0 = pl.multiple_of(kt * ta, ta)
        for hh in range(2):
            kk = k_ref[pl.ds(k0, ta), hh * HEAD_PAD:(hh + 1) * HEAD_PAD]
            qt = qt_ref[0, hh, q]
            if not masked:
                s = _dot(kk, qt)
                s_scr[slot, hh] = s
                mc_scr[slot, hh] = jnp.max(s, axis=0, keepdims=True)
                continue
            hk = ta // 2
            top = _dot(kk[0:hk], qt)
            low = _dot(kk[hk:], qt[:, hk:])
            top = jnp.where(lax.broadcasted_iota(jnp.int32, top.shape, 0)
                            <= lax.broadcasted_iota(jnp.int32, top.shape, 1), top, NEG_BIG)
            low = jnp.where(lax.broadcasted_iota(jnp.int32, low.shape, 0)
                            <= lax.broadcasted_iota(jnp.int32, low.shape, 1), low, NEG_BIG)
            s_scr[slot, hh, 0:hk, :] = top
            s_scr[slot, hh, hk:, 0:hk] = jnp.full((hk, hk), NEG_BIG, jnp.float32)
            s_scr[slot, hh, hk:, hk:] = low
            top_max = jnp.max(top, axis=0, keepdims=True)
            mc_scr[slot, hh, :, 0:hk] = top_max[:, 0:hk]
            mc_scr[slot, hh, :, hk:] = jnp.maximum(top_max[:, hk:], jnp.max(low, axis=0, keepdims=True))

    def consume(kt, slot):
        for hh in range(2):
            m_prev = m_scr[hh]
            m_new = jnp.maximum(m_prev, mc_scr[slot, hh])
            alpha = jnp.exp2(m_prev - m_new)
            p = jnp.exp2(s_scr[slot, hh] - m_new).astype(jnp.bfloat16)
            acc_scr[hh] = alpha * acc_scr[hh] + _dot(vt_ref[0, hh, kt], p)
            m_scr[hh] = m_new

    def reset():
        m_scr[...] = jnp.full_like(m_scr, NEG_BIG)
        acc_scr[...] = jnp.zeros_like(acc_scr)

    def finish(q):
        outs = []
        for hh in range(2):
            acc = acc_scr[hh]
            oh = acc[0:vd] / acc[vd:vd + 1]
            ms = jnp.mean(oh * oh, axis=0, keepdims=True)
            outs.append(oh * lax.rsqrt(ms + NORM_EPS))
        o_t = jnp.concatenate(outs, axis=0) * g_ref[...]
        o_ref[pl.ds(pl.multiple_of(q * ta, ta), ta), :] = o_t.T.astype(o_ref.dtype)

    reset()
    produce(0, 0, 0, True)

    def q_tile(n, carry):
        first = (n * (n + 1) // 2) % 2

        def step(j, slot):
            produce(n, j, 1 - slot, False)
            consume(jnp.where(j == 0, n, j - 1), slot)

        for par in range(2):
            @pl.when(first == par)
            def _():
                def two_steps(u, c):
                    step(2 * u, par)
                    step(2 * u + 1, 1 - par)
                    return c

                lax.fori_loop(0, n // 2, two_steps, 0)

                @pl.when(n % 2 == 1)
                def _():
                    step(n - 1, par)

        last_slot = (first + n) % 2
        last_kt = jnp.maximum(n - 1, 0)
        for par in range(2):
            @pl.when((last_slot == par) & (n < nq - 1))
            def _():
                produce(n + 1, n + 1, 1 - par, True)
                consume(last_kt, par)
                finish(n)
                reset()

        @pl.when(n == nq - 1)
        def _():
            consume(last_kt, ((nq - 1) * nq // 2 + nq - 1) % 2)
            finish(n)

        return carry

    lax.fori_loop(0, nq, q_tile, 0)


def _attention(qt, k, vt, gain_rep, B, S, name):
    T = k.shape[0]
    ta = min(ATTN_TILE, S)
    nq = S // ta
    n_pairs = qt.shape[1] // 2
    return pl.pallas_call(
        functools.partial(_attn_kernel, ta=ta, nq=nq),
        grid=(B, n_pairs),
        in_specs=[pl.BlockSpec((1, 2, nq, HEAD_PAD, ta), lambda b, j: (b, j, 0, 0, 0)),
                  pl.BlockSpec((S, 2 * HEAD_PAD), lambda b, j: (b, j)),
                  pl.BlockSpec((1, 2, nq, V_ROWS, ta), lambda b, j: (b, j, 0, 0, 0)),
                  pl.BlockSpec((LANES, ta), lambda b, j: (j, 0))],
        out_specs=pl.BlockSpec((S, LANES), lambda b, j: (b, j)),
        out_shape=jax.ShapeDtypeStruct((T, n_pairs * LANES), jnp.bfloat16),
        scratch_shapes=[pltpu.VMEM((2, 2, ta, ta), jnp.float32),
                        pltpu.VMEM((2, 2, 1, ta), jnp.float32),
                        pltpu.VMEM((2, 1, ta), jnp.float32),
                        pltpu.VMEM((2, V_ROWS, ta), jnp.float32)],
        compiler_params=_cparams("parallel", "parallel"),
        name=name,
    )(qt, k, vt, gain_rep)


def _post_attn_kernel(fox_ref, mla_ref, h_ref, wo_ref, fnorm_ref, wrh_ref, wrl_ref, tri_ref,
                      h2_ref, xr_ref, route_ref, cnt_ref, carry_ref, xn_scr):
    i = pl.program_id(0)
    D = h_ref.shape[1]

    @pl.when(i == 0)
    def _():
        xn_scr[...] = jnp.zeros_like(xn_scr)

    @pl.when(i <= 1)
    def _():
        carry_ref[...] = jnp.zeros_like(carry_ref)

    xn = xn_scr[...]
    xr_ref[:, 0:D] = xn
    h2 = h_ref[...] + _dot(fox_ref[...], wo_ref[0:FOX_WIDTH, :]) + _dot(mla_ref[...], wo_ref[FOX_WIDTH:, :])
    h2_ref[...] = h2
    xn_scr[...] = _rms(h2, fnorm_ref[...])

    x_hi = xn.astype(jnp.bfloat16)
    x_lo = (xn - x_hi.astype(jnp.float32)).astype(jnp.bfloat16)
    logits = _dot(x_hi, wrh_ref[...]) + _dot(x_lo, wrh_ref[...]) + _dot(x_hi, wrl_ref[...])

    lt = logits.T
    tt = lt.shape[1]
    ng = N_GROUPS
    gl = lt[0:ng]
    sub = lax.broadcasted_iota(jnp.int32, (ng, tt), 0).astype(jnp.float32)

    gmax = jnp.max(gl, axis=0, keepdims=True)
    group_w = 1.0 / jnp.sum(jnp.exp(gl - gmax), axis=0, keepdims=True)
    gid = jnp.min(jnp.where(gl == gmax, sub, float(ng)), axis=0, keepdims=True)
    onehot = sub == gid

    ins = [jnp.sum(jnp.where(onehot, lt[ng * (j + 1):ng * (j + 2)], 0.0), axis=0, keepdims=True)
           for j in range(EXPERTS_PER_GROUP)]

    def first_argmax(vals):
        top = functools.reduce(jnp.maximum, vals)
        idx = jnp.full_like(top, float(len(vals) - 1))
        for j in range(len(vals) - 2, -1, -1):
            idx = jnp.where(vals[j] == top, float(j), idx)
        return top, idx

    e1, i1 = first_argmax(ins)
    e2, i2 = first_argmax([jnp.where(i1 == j, NEG_BIG, v) for j, v in enumerate(ins)])
    t = jnp.exp(e2 - e1)
    g1 = group_w / (1.0 + t)
    g2 = group_w * t / (1.0 + t)
    gates = [jnp.where(i1 == j, g1, jnp.where(i2 == j, g2, 0.0)) for j in range(EXPERTS_PER_GROUP)]

    ones = jnp.where(onehot, 1.0, 0.0)
    prefix = _dot(ones.astype(jnp.bfloat16), tri_ref[...]) + carry_ref[:, 0:1]
    rank = jnp.sum(jnp.where(onehot, prefix, 0.0), axis=0, keepdims=True)
    carry = carry_ref[...] + jnp.sum(ones, axis=1, keepdims=True)
    carry_ref[...] = carry
    cnt_ref[...] = carry

    pad_rows = jnp.zeros((ROUTE_ROWS - EXPERTS_PER_GROUP - 2, tt), jnp.float32)
    info_t = jnp.concatenate(gates + [gid, rank, pad_rows], axis=0)
    route_ref[...] = info_t
    info_pad = jnp.zeros((LANES - ROUTE_ROWS, tt), jnp.float32)
    xr_ref[:, D:D + LANES] = jnp.concatenate([info_t, info_pad], axis=0).T


def _post_attn(fox, mla, h, lw):
    T, D = h.shape
    tt = min(ROW_TILE, T)
    n = T // tt
    proj = lambda i: (jnp.minimum(i, n - 1), 0)
    routed = lambda i: (jnp.maximum(i - 1, 0), 0)
    const = lambda i: (0, 0)
    weights = [lw["w_o"], lw["ffn_norm"], lw["wr_hi"], lw["wr_lo"], lw["tri_before"]]
    return pl.pallas_call(
        _post_attn_kernel,
        grid=(n + 1,),
        in_specs=[pl.BlockSpec((tt, FOX_WIDTH), proj), pl.BlockSpec((tt, MLA_WIDTH), proj),
                  pl.BlockSpec((tt, D), proj)] + [pl.BlockSpec(w.shape, const) for w in weights],
        out_specs=[pl.BlockSpec((tt, D), proj), pl.BlockSpec((tt, D + LANES), routed),
                   pl.BlockSpec((ROUTE_ROWS, tt), lambda i: (0, jnp.maximum(i - 1, 0))),
                   pl.BlockSpec((N_GROUPS, LANES), const)],
        out_shape=[jax.ShapeDtypeStruct((T, D), jnp.float32),
                   jax.ShapeDtypeStruct((T, D + LANES), jnp.float32),
                   jax.ShapeDtypeStruct((ROUTE_ROWS, T), jnp.float32),
                   jax.ShapeDtypeStruct((N_GROUPS, LANES), jnp.float32)],
        scratch_shapes=[pltpu.VMEM((N_GROUPS, LANES), jnp.float32), pltpu.VMEM((tt, D), jnp.float32)],
        compiler_params=_cparams("arbitrary"),
        name="post_attn",
    )(fox, mla, h, *weights)


def _dispatch_kernel(fill_ref, dest_ref, xr_ref, xs_hbm, zero_ref, sem, fill_sem, *, tt, n_sorted):
    first = pl.program_id(0) == 0

    def fill_copies(act):
        for g in range(N_GROUPS):
            start, length = fill_ref[g], fill_ref[N_GROUPS + g]
            head = (-start) & (SUBLANES - 1)
            for r in range(SUBLANES - 1):
                @pl.when(r < head)
                def _(start=start, r=r):
                    act(pltpu.make_async_copy(zero_ref.at[pl.ds(0, 1)], xs_hbm.at[pl.ds(start + r, 1)], fill_sem))
            base, rest = start + head, length - head
            size = MOE_BLOCK // 2
            while size >= SUBLANES:
                @pl.when((rest & size) != 0)
                def _(base=base, rest=rest, size=size):
                    at = pl.multiple_of(base + (rest & ~(2 * size - 1)), SUBLANES)
                    act(pltpu.make_async_copy(zero_ref.at[pl.ds(0, size)], xs_hbm.at[pl.ds(at, size)], fill_sem))
                size //= 2
        used = fill_ref[2 * N_GROUPS]
        for blk in range(N_GROUPS):
            @pl.when(used + (blk + 1) * MOE_BLOCK <= n_sorted)
            def _(blk=blk):
                at = pl.multiple_of(used + blk * MOE_BLOCK, MOE_BLOCK)
                act(pltpu.make_async_copy(zero_ref, xs_hbm.at[pl.ds(at, MOE_BLOCK)], fill_sem))

    @pl.when(first)
    def _():
        zero_ref[...] = jnp.zeros_like(zero_ref)
        fill_copies(lambda cp: cp.start())

    def issue(r, carry):
        pltpu.make_async_copy(xr_ref.at[pl.ds(r, 1)], xs_hbm.at[pl.ds(dest_ref[0, 0, r], 1)], sem).start()
        return carry
    lax.fori_loop(0, tt, issue, 0, unroll=ROW_DMA_UNROLL)
    pltpu.make_async_copy(xr_ref, xs_hbm.at[pl.ds(0, tt)], sem).wait()

    @pl.when(first)
    def _():
        fill_copies(lambda cp: cp.wait())


def _dispatch(xr, dest3, fill, n_sorted):
    T, W = xr.shape
    tt = dest3.shape[2]
    grid_spec = pltpu.PrefetchScalarGridSpec(
        num_scalar_prefetch=1,
        grid=(T // tt,),
        in_specs=[pl.BlockSpec((1, 1, tt), lambda i, f: (i, 0, 0), memory_space=pltpu.SMEM),
                  pl.BlockSpec((tt, W), lambda i, f: (i, 0))],
        out_specs=pl.BlockSpec(memory_space=pl.ANY),
        scratch_shapes=[pltpu.VMEM((MOE_BLOCK, W), xr.dtype),
                        pltpu.SemaphoreType.DMA(()), pltpu.SemaphoreType.DMA(())],
    )
    return pl.pallas_call(
        functools.partial(_dispatch_kernel, tt=tt, n_sorted=n_sorted),
        grid_spec=grid_spec,
        out_shape=jax.ShapeDtypeStruct((n_sorted, W), xr.dtype),
        compiler_params=_cparams("arbitrary"),
        name="moe_dispatch",
    )(fill, dest3, xr)


def _moe_kernel(bg_ref, nv_ref, xs_ref, wg32_ref, wu32_ref, wd32_ref, ys_ref, wg_ref, wu_ref, wd_ref):
    D = ys_ref.shape[1]
    b = pl.program_id(0)
    g = bg_ref[b]
    n_valid = nv_ref[b]

    @pl.when((b == 0) | (g != bg_ref[jnp.maximum(b - 1, 0)]))
    def _():
        wg_ref[...] = wg32_ref[...].astype(jnp.bfloat16)
        wu_ref[...] = wu32_ref[...].astype(jnp.bfloat16)
        wd_ref[...] = wd32_ref[...].astype(jnp.bfloat16)

    @pl.when(n_valid == 0)
    def _():
        ys_ref[...] = jnp.zeros_like(ys_ref)

    @pl.when(n_valid > 0)
    def _():
        x = xs_ref[:, 0:D].astype(jnp.bfloat16)
        info = xs_ref[:, D:D + LANES]
        y = jnp.zeros(ys_ref.shape, jnp.float32)
        for j in range(EXPERTS_PER_GROUP):
            a = _dot(x, wg_ref[j])
            u = _dot(x, wu_ref[j])
            hj = (a * jax.nn.sigmoid(a) * u).astype(jnp.bfloat16)
            y = y + info[:, j:j + 1] * _dot(hj, wd_ref[j])
        ys_ref[...] = y


def _moe(xs, block_group, block_valid, layer, w_gate, w_up, w_down):
    n_sorted, W = xs.shape
    D = W - LANES
    nb = n_sorted // MOE_BLOCK
    epg = EXPERTS_PER_GROUP
    grid_spec = pltpu.PrefetchScalarGridSpec(
        num_scalar_prefetch=2,
        grid=(nb,),
        in_specs=[pl.BlockSpec((MOE_BLOCK, W), lambda b, bg, nv: (b, 0)),
                  pl.BlockSpec((None, epg, D, D_EXPERT), lambda b, bg, nv: (layer, bg[b], 0, 0)),
                  pl.BlockSpec((None, epg, D, D_EXPERT), lambda b, bg, nv: (layer, bg[b], 0, 0)),
                  pl.BlockSpec((None, epg, D_EXPERT, D), lambda b, bg, nv: (layer, bg[b], 0, 0))],
        out_specs=pl.BlockSpec((MOE_BLOCK, D), lambda b, bg, nv: (b, 0)),
        scratch_shapes=[pltpu.VMEM((epg, D, D_EXPERT), jnp.bfloat16),
                        pltpu.VMEM((epg, D, D_EXPERT), jnp.bfloat16),
                        pltpu.VMEM((epg, D_EXPERT, D), jnp.bfloat16)],
    )
    return pl.pallas_call(
        _moe_kernel,
        grid_spec=grid_spec,
        out_shape=jax.ShapeDtypeStruct((n_sorted, D), jnp.float32),
        compiler_params=_cparams("arbitrary"),
        name="moe_experts",
    )(block_group, block_valid, xs, w_gate, w_up, w_down)


def _final_kernel(dest_ref, dest_next_ref, h_ref, ys_hbm, gain_ref, o_ref, ybuf_ref, sems, *, tt):
    y = _gather_rows(ys_hbm, dest_ref, dest_next_ref, ybuf_ref, sems,
                     pl.program_id(0), pl.num_programs(0), tt)
    o_ref[...] = _rms(h_ref[...] + y, gain_ref[...])


def _final(h, dest3, ys, gain):
    T, D = h.shape
    tt = dest3.shape[2]
    last = T // tt - 1
    return pl.pallas_call(
        functools.partial(_final_kernel, tt=tt),
        grid=(T // tt,),
        in_specs=[pl.BlockSpec((1, 1, tt), lambda i: (i, 0, 0), memory_space=pltpu.SMEM),
                  pl.BlockSpec((1, 1, tt), lambda i: (jnp.minimum(i + 1, last), 0, 0), memory_space=pltpu.SMEM),
                  pl.BlockSpec((tt, D), lambda i: (i, 0)),
                  pl.BlockSpec(memory_space=pl.ANY),
                  pl.BlockSpec((1, D), lambda i: (0, 0))],
        out_specs=pl.BlockSpec((tt, D), lambda i: (i, 0)),
        out_shape=jax.ShapeDtypeStruct((T, D), jnp.float32),
        scratch_shapes=[pltpu.VMEM((2, tt, D), jnp.float32), pltpu.SemaphoreType.DMA((2,))],
        compiler_params=_cparams("arbitrary"),
        name="final_norm",
    )(dest3, dest3, h, ys, gain)


def _prep_layer(l, tt, w_in, b_f, w_uq, w_ukv, w_o, w_group, w_router,
                attn_norm, q_norm, kv_norm, ffn_norm):
    bf16 = jnp.bfloat16
    D = w_in.shape[1]
    fw = FOX_WIDTH
    o_f = 3 * fw
    o_cq = o_f + FOX_HEADS
    o_ckv = o_cq + Q_LORA_RANK
    o_kr = o_ckv + KV_LORA_RANK
    wi = w_in[l]
    w_f = wi[:, o_f:o_cq]
    w_f3 = jnp.concatenate([w_f] * DECAY_PARTS + [jnp.zeros((D, LANES - DECAY_PARTS * FOX_HEADS), wi.dtype)], axis=1)
    b_f3 = jnp.concatenate([b_f[l]] * DECAY_PARTS + [jnp.zeros((LANES - DECAY_PARTS * FOX_HEADS,), b_f.dtype)])
    kr = wi[:, o_kr:o_kr + MLA_ROPE_DIM]
    kr_swap = jnp.concatenate([-kr[:, ROPE_HALF:], kr[:, :ROPE_HALF]], axis=1)
    zpad = jnp.zeros((D, MLA_ROPE_DIM), wi.dtype)
    w_rows = jnp.concatenate([wi[:, fw:2 * fw], w_f3, wi[:, o_cq:o_kr], kr_swap, zpad, kr, zpad], axis=1)

    src = jnp.arange(LANES)
    dst = (src % FOX_HEADS) * HEAD_PAD + FOX_HEAD_DIM + src // FOX_HEADS
    place = ((jnp.arange(FOX_HEADS * HEAD_PAD)[None, :] == dst[:, None])
             & (src[:, None] < DECAY_PARTS * FOX_HEADS)).astype(bf16)

    qd = MLA_NOPE_DIM + MLA_ROPE_DIM
    wq = w_uq[l].reshape(Q_LORA_RANK, MLA_HEADS, qd)
    zq = jnp.zeros((Q_LORA_RANK, MLA_HEADS, HEAD_PAD - qd), wq.dtype)
    wq_main = jnp.concatenate([wq, zq], axis=2).reshape(Q_LORA_RANK, MLA_HEADS * HEAD_PAD)
    x1 = wq[:, :, MLA_NOPE_DIM:MLA_NOPE_DIM + ROPE_HALF]
    x2 = wq[:, :, MLA_NOPE_DIM + ROPE_HALF:]
    wq_swap = jnp.concatenate([-x2, x1], axis=2).reshape(Q_LORA_RANK, MLA_HEADS * MLA_ROPE_DIM)

    wkv = w_ukv[l].reshape(KV_LORA_RANK, MLA_HEADS, MLA_NOPE_DIM + MLA_V_DIM)
    wk_nope = jnp.concatenate([wkv[:, :, :MLA_NOPE_DIM],
                               jnp.zeros((KV_LORA_RANK, MLA_HEADS, HEAD_PAD - MLA_NOPE_DIM), wkv.dtype)], axis=2)
    wv = wkv[:, :, MLA_NOPE_DIM:].reshape(KV_LORA_RANK, MLA_WIDTH)

    w_exp = w_router[l].reshape(D, N_GROUPS, EXPERTS_PER_GROUP).transpose(0, 2, 1).reshape(D, N_EXPERTS)
    wr = jnp.concatenate([w_group[l], w_exp,
                          jnp.zeros((D, LANES - N_GROUPS - N_EXPERTS), w_group.dtype)], axis=1)
    wr_hi = wr.astype(bf16)
    wr_lo = (wr - wr_hi.astype(jnp.float32)).astype(bf16)

    ar = jnp.arange(tt)
    return dict(
        attn_norm=attn_norm[l].reshape(1, D),
        w_rows=w_rows.astype(bf16),
        w_fq_t=wi[:, 0:fw].T.astype(bf16),
        w_fv_t=wi[:, 2 * fw:3 * fw].T.astype(bf16),
        b_f=b_f3.reshape(1, LANES),
        tri_incl=(ar[:, None] >= ar[None, :]).astype(bf16),
        tri_before=(ar[:, None] < ar[None, :]).astype(bf16),
        place=place,
        q_norm=q_norm[l].reshape(1, Q_LORA_RANK),
        wq_main_t=wq_main.T.astype(bf16),
        wq_swap_t=wq_swap.T.astype(bf16),
        kv_norm=kv_norm[l].reshape(1, KV_LORA_RANK),
        wk_nope=wk_nope.reshape(KV_LORA_RANK, MLA_HEADS * HEAD_PAD).astype(bf16),
        wv_t=wv.T.astype(bf16),
        w_o=w_o[l].astype(bf16),
        ffn_norm=ffn_norm[l].reshape(1, D),
        wr_hi=wr_hi,
        wr_lo=wr_lo,
    )


def _routing_plan(route, counts, n_blocks):
    gid = route[GID_INDEX].astype(jnp.int32)
    rank = route[RANK_INDEX].astype(jnp.int32)
    cnt = counts[:, 0].astype(jnp.int32)
    padded = (cnt + MOE_BLOCK - 1) // MOE_BLOCK * MOE_BLOCK
    pend = jnp.cumsum(padded)
    pstart = pend - padded
    dest = pstart[gid] + rank
    block_start = jnp.arange(n_blocks, dtype=jnp.int32) * MOE_BLOCK
    block_group = jnp.sum((pend[None, :] <= block_start[:, None]).astype(jnp.int32), axis=1)
    block_group = jnp.minimum(block_group, N_GROUPS - 1)
    block_valid = jnp.clip((pstart + cnt)[block_group] - block_start, 0, MOE_BLOCK)
    fill = jnp.concatenate([pstart + cnt, padded - cnt, pend[-1:]])
    return dest, block_group, block_valid, fill


def kernel(x, positions, attn_norm, w_in, b_f, q_norm, w_uq, kv_norm, w_ukv, fox_out_norm, mla_out_norm, w_o,
           ffn_norm, w_group, w_router, w_gate, w_up, w_down, final_norm):
    B, S, D = x.shape
    T = B * S
    depth = w_in.shape[0]
    ta = min(ATTN_TILE, S)
    tt = min(ROW_TILE, S)
    n_blocks = -(-(T + N_GROUPS * (MOE_BLOCK - 1)) // MOE_BLOCK)

    cos_t, sin_t = _rope_tables(positions)
    tabs = (cos_t, sin_t)

    h = x.reshape(T, D)
    moe_in = None
    for l in range(depth):
        lw = _prep_layer(l, tt, w_in, b_f, w_uq, w_ukv, w_o, w_group, w_router,
                         attn_norm, q_norm, kv_norm, ffn_norm)
        h, (fq_t, fk, fv_t, mq_t, mk, mv_t) = _pre_attn(h, moe_in, B, S, lw, tabs)
        fox_gain = jnp.broadcast_to(fox_out_norm[l].reshape(FOX_WIDTH, 1), (FOX_WIDTH, ta))
        mla_gain = jnp.broadcast_to(mla_out_norm[l].reshape(MLA_WIDTH, 1), (MLA_WIDTH, ta))
        fox = _attention(fq_t, fk, fv_t, fox_gain, B, S, "attn_fox")
        mla = _attention(mq_t, mk, mv_t, mla_gain, B, S, "attn_mla")
        h, xr, route, counts = _post_attn(fox, mla, h, lw)
        dest, block_group, block_valid, fill = _routing_plan(route, counts, n_blocks)
        dest3 = dest.reshape(T // tt, 1, tt)
        xs = _dispatch(xr, dest3, fill, n_blocks * MOE_BLOCK)
        ys = _moe(xs, block_group, block_valid, l, w_gate, w_up, w_down)
        moe_in = (dest3, ys)
    out = _final(h, moe_in[0], moe_in[1], final_norm.reshape(1, D))
    return out.reshape(B, S, D)
```

```python
import functools
import math

import jax
import jax.numpy as jnp
from jax import lax
from jax.experimental import pallas as pl
from jax.experimental.pallas import tpu as pltpu

FOX_HEADS = 8
FOX_HEAD_DIM = 64
FOX_WIDTH = FOX_HEADS * FOX_HEAD_DIM
MLA_HEADS = 8
MLA_NOPE_DIM = 64
MLA_ROPE_DIM = 32
MLA_V_DIM = 64
MLA_WIDTH = MLA_HEADS * MLA_V_DIM
Q_LORA_RANK = 256
KV_LORA_RANK = 128
ROPE_THETA = 10000.0
N_GROUPS = 8
EXPERTS_PER_GROUP = 4
N_EXPERTS = N_GROUPS * EXPERTS_PER_GROUP
D_EXPERT = 256
NORM_EPS = 1e-6

LANES = 128
SUBLANES = 8
BF16_SUBLANES = 16
VMEM_LIMIT_BYTES = 56 * 1024 * 1024

ROW_TILE = 512
ATTN_TILE = 512
MOE_BLOCK = 512
ROW_DMA_UNROLL = 8
HEAD_PAD = 128
V_ROWS = MLA_V_DIM + BF16_SUBLANES
ROPE_HALF = MLA_ROPE_DIM // 2
NEG_BIG = -1e30
LOG2E = math.log2(math.e)
DECAY_PARTS = 3

GID_INDEX = EXPERTS_PER_GROUP
RANK_INDEX = EXPERTS_PER_GROUP + 1
ROUTE_ROWS = SUBLANES


def _cparams(*semantics):
    return pltpu.CompilerParams(dimension_semantics=semantics, vmem_limit_bytes=VMEM_LIMIT_BYTES)


def _rms(x, gain):
    ms = jnp.mean(x * x, axis=-1, keepdims=True)
    return x * lax.rsqrt(ms + NORM_EPS) * gain


def _dot(a, b):
    return jnp.dot(a, b, preferred_element_type=jnp.float32)


def _dot_nt(a, b):
    return lax.dot_general(a, b, (((1,), (1,)), ((), ())), preferred_element_type=jnp.float32)


def _split_bf16(x, parts):
    out = []
    for _ in range(parts - 1):
        piece = x.astype(jnp.bfloat16).astype(jnp.float32)
        out.append(piece)
        x = x - piece
    out.append(x.astype(jnp.bfloat16).astype(jnp.float32))
    return out


def _rope_table_kernel(pos_ref, invf_ref, cos_ref, sin_ref):
    ang = pos_ref[...].astype(jnp.float32) * invf_ref[...]
    cos_ref[...] = jnp.cos(ang)
    sin_ref[...] = jnp.sin(ang)


def _rope_tables(positions):
    T = positions.size
    inv_freq = ROPE_THETA ** (-jnp.arange(ROPE_HALF, dtype=jnp.float32) / ROPE_HALF)
    ct = min(T, 4096)
    return pl.pallas_call(
        _rope_table_kernel,
        grid=(T // ct,),
        in_specs=[pl.BlockSpec((1, ct), lambda i: (0, i)),
                  pl.BlockSpec((ROPE_HALF, 1), lambda i: (0, 0))],
        out_specs=[pl.BlockSpec((ROPE_HALF, ct), lambda i: (0, i))] * 2,
        out_shape=[jax.ShapeDtypeStruct((ROPE_HALF, T), jnp.float32)] * 2,
        compiler_params=_cparams("parallel"),
        name="rope_tables",
    )(positions.reshape(1, T), inv_freq.reshape(ROPE_HALF, 1))


def _gather_rows(src_hbm, idx_ref, idx_next_ref, buf_ref, sems, step, n_steps, n_rows):
    slot = step % 2
    chunks = buf_ref.shape[1] // n_rows

    def request(ref, to_slot):
        def issue(r, carry):
            src = pl.multiple_of(ref[0, 0, r] * chunks, chunks)
            dst = pl.multiple_of(r * chunks, chunks)
            pltpu.make_async_copy(src_hbm.at[pl.ds(src, chunks)], buf_ref.at[to_slot, pl.ds(dst, chunks)],
                                  sems.at[to_slot]).start()
            return carry
        lax.fori_loop(0, n_rows, issue, 0, unroll=ROW_DMA_UNROLL)

    @pl.when(step == 0)
    def _():
        request(idx_ref, slot)

    @pl.when(step + 1 < n_steps)
    def _():
        request(idx_next_ref, 1 - slot)

    pltpu.make_async_copy(src_hbm.at[pl.ds(0, n_rows * chunks)], buf_ref.at[slot], sems.at[slot]).wait()
    return jnp.concatenate([buf_ref[slot, pl.ds(c, n_rows, stride=chunks), :] for c in range(chunks)], axis=1)


def _pre_attn_kernel(*refs, has_moe_in, tt):
    if has_moe_in:
        dest_ref, dest_next_ref, h_ref, ys_hbm = refs[:4]
        refs = refs[4:]
    else:
        h_ref = refs[0]
        refs = refs[1:]
    (anorm_ref, wrow_ref, wfq_ref, wfv_ref, bf_ref, tri_ref, place_ref, qnorm_ref, wqm_ref, wqs_ref,
     kvnorm_ref, wkn_ref, wv_ref, cos_ref, sin_ref) = refs[:15]
    refs = refs[15:]
    if has_moe_in:
        hout_ref = refs[0]
        refs = refs[1:]
    fq_ref, fk_ref, fv_ref, qm_ref, km_ref, vm_ref = refs[:6]
    refs = refs[6:]
    carry_ref = refs[0]
    if has_moe_in:
        ybuf_ref, sems = refs[1:3]

    si = pl.program_id(1)
    bf16 = jnp.bfloat16
    half = LANES // 2

    h = h_ref[...]
    if has_moe_in:
        step = pl.program_id(0) * pl.num_programs(1) + si
        n_steps = pl.num_programs(0) * pl.num_programs(1)
        h = h + _gather_rows(ys_hbm, dest_ref, dest_next_ref, ybuf_ref, sems, step, n_steps, tt)
        hout_ref[...] = h

    xn = _rms(h, anorm_ref[...]).astype(bf16)
    lane = lax.broadcasted_iota(jnp.int32, (tt, LANES), 1)
    sub = lax.broadcasted_iota(jnp.int32, (BF16_SUBLANES, tt), 0)
    ones_rows = jnp.ones((BF16_SUBLANES, tt), bf16)

    fq_t = _dot_nt(wfq_ref[...], xn) * (FOX_HEAD_DIM ** -0.5 * LOG2E)
    q_aug = jnp.where(sub < DECAY_PARTS, -1.0, 0.0).astype(bf16)
    q_zero = jnp.zeros((HEAD_PAD - FOX_HEAD_DIM - BF16_SUBLANES, tt), bf16)
    for hd in range(FOX_HEADS):
        fq_ref[0, hd, 0, 0:FOX_HEAD_DIM, :] = fq_t[hd * FOX_HEAD_DIM:(hd + 1) * FOX_HEAD_DIM].astype(bf16)
        fq_ref[0, hd, 0, FOX_HEAD_DIM:FOX_HEAD_DIM + BF16_SUBLANES, :] = q_aug
        fq_ref[0, hd, 0, FOX_HEAD_DIM + BF16_SUBLANES:, :] = q_zero
    fv_t = _dot_nt(wfv_ref[...], xn)
    vd = MLA_V_DIM
    for hd in range(FOX_HEADS):
        fv_ref[0, hd, 0, 0:vd, :] = fv_t[hd * vd:(hd + 1) * vd].astype(bf16)
        fv_ref[0, hd, 0, vd:, :] = ones_rows

    fw = FOX_WIDTH
    o = fw
    z = _dot(xn, wrow_ref[:, o:o + LANES]) + bf_ref[...]
    o += LANES
    logf = jnp.minimum(z, 0.0) - jnp.log1p(jnp.exp(-jnp.abs(z)))
    pieces = jnp.concatenate([p_.astype(bf16) for p_ in _split_bf16(logf, DECAY_PARTS)], axis=1)
    csum = _dot(tri_ref[...], pieces)

    @pl.when(si == 0)
    def _():
        carry_ref[...] = jnp.zeros_like(carry_ref)

    dcum = csum[:, 0:LANES] + csum[:, LANES:2 * LANES] + csum[:, 2 * LANES:] + carry_ref[...]
    carry_ref[...] = dcum[tt - 1:tt, :]
    d_parts = _split_bf16(dcum * LOG2E, DECAY_PARTS)
    d_sel = jnp.where(lane < FOX_HEADS, d_parts[0],
                      jnp.where(lane < 2 * FOX_HEADS, d_parts[1],
                                jnp.where(lane < 3 * FOX_HEADS, d_parts[2], 0.0)))
    k_aug = _dot(d_sel.astype(bf16), place_ref[...])

    fk = _dot(xn, wrow_ref[:, 0:fw])
    for p in range(FOX_HEADS // 2):
        blk = fk[:, p * LANES:(p + 1) * LANES]
        for hh, b_ in ((0, blk), (1, pltpu.roll(blk, half, axis=1))):
            sl = slice((2 * p + hh) * HEAD_PAD, (2 * p + hh + 1) * HEAD_PAD)
            fk_ref[:, sl] = jnp.where(lane < FOX_HEAD_DIM, b_, k_aug[:, sl]).astype(bf16)

    cq = _rms(_dot(xn, wrow_ref[:, o:o + Q_LORA_RANK]), qnorm_ref[...]).astype(bf16)
    o += Q_LORA_RANK
    qm_t = _dot_nt(wqm_ref[...], cq)
    qs_t = _dot_nt(wqs_ref[...], cq)
    cos_t = cos_ref[...]
    sin_t = sin_ref[...]
    scale = (MLA_NOPE_DIM + MLA_ROPE_DIM) ** -0.5 * LOG2E
    q_zero = jnp.zeros((HEAD_PAD - MLA_NOPE_DIM - MLA_ROPE_DIM, tt), jnp.float32)
    for hd in range(MLA_HEADS):
        blk = qm_t[hd * HEAD_PAD:(hd + 1) * HEAD_PAD]
        rows = [blk[0:MLA_NOPE_DIM]]
        for j in range(2):
            main = blk[MLA_NOPE_DIM + j * ROPE_HALF:MLA_NOPE_DIM + (j + 1) * ROPE_HALF]
            swap = qs_t[hd * MLA_ROPE_DIM + j * ROPE_HALF:hd * MLA_ROPE_DIM + (j + 1) * ROPE_HALF]
            rows.append(main * cos_t + swap * sin_t)
        rows.append(q_zero)
        qm_ref[0, hd, 0] = (jnp.concatenate(rows, axis=0) * scale).astype(bf16)

    ckv = _rms(_dot(xn, wrow_ref[:, o:o + KV_LORA_RANK]), kvnorm_ref[...]).astype(bf16)
    o += KV_LORA_RANK
    mv_t = _dot_nt(wv_ref[...], ckv)
    for hd in range(MLA_HEADS):
        vm_ref[0, hd, 0, 0:vd, :] = mv_t[hd * vd:(hd + 1) * vd].astype(bf16)
        vm_ref[0, hd, 0, vd:, :] = ones_rows
    k_nope = _dot(ckv, wkn_ref[...])
    z_rows = jnp.zeros((MLA_ROPE_DIM, tt), jnp.float32)
    csk = jnp.concatenate([sin_t, sin_t, z_rows, cos_t, cos_t, z_rows], axis=0).T
    kr = _dot(xn, wrow_ref[:, o:o + LANES]) * csk
    kr = kr + pltpu.roll(kr, half, axis=1)
    k_pe = jnp.where(lane >= MLA_NOPE_DIM, kr, 0.0)
    for hd in range(MLA_HEADS):
        sl = slice(hd * HEAD_PAD, (hd + 1) * HEAD_PAD)
        km_ref[:, sl] = (k_nope[:, sl] + k_pe).astype(bf16)


def _pre_attn(h, moe_in, B, S, lw, tabs):
    T, D = h.shape
    tt = min(ROW_TILE, S)
    ns = S // tt
    has_moe_in = moe_in is not None
    row = lambda b, s: (b * ns + s, 0)
    const = lambda b, s: (0, 0)
    cos_t, sin_t = tabs

    def rows(width):
        return pl.BlockSpec((tt, width), row)

    def full(a):
        return pl.BlockSpec(a.shape, const)

    weights = [lw["attn_norm"], lw["w_rows"], lw["w_fq_t"], lw["w_fv_t"], lw["b_f"], lw["tri_incl"],
               lw["place"], lw["q_norm"], lw["wq_main_t"], lw["wq_swap_t"], lw["kv_norm"],
               lw["wk_nope"], lw["wv_t"]]
    in_specs = [rows(D)]
    args = [h]
    if has_moe_in:
        dest3, ys = moe_in
        last = B * ns - 1
        in_specs = [pl.BlockSpec((1, 1, tt), lambda b, s: (b * ns + s, 0, 0), memory_space=pltpu.SMEM),
                    pl.BlockSpec((1, 1, tt), lambda b, s: (jnp.minimum(b * ns + s + 1, last), 0, 0),
                                 memory_space=pltpu.SMEM),
                    rows(D), pl.BlockSpec(memory_space=pl.ANY)]
        args = [dest3, dest3, h, ys]
    col = lambda b, s: (0, b * ns + s)
    in_specs += [full(w) for w in weights]
    in_specs += [pl.BlockSpec((ROPE_HALF, tt), col), pl.BlockSpec((ROPE_HALF, tt), col)]
    args += weights + [cos_t, sin_t]

    bf16 = jnp.bfloat16
    qt_shape = jax.ShapeDtypeStruct((B, FOX_HEADS, ns, HEAD_PAD, tt), bf16)
    qt_spec = pl.BlockSpec((1, FOX_HEADS, 1, HEAD_PAD, tt), lambda b, s: (b, 0, s, 0, 0))
    k_shape = jax.ShapeDtypeStruct((T, FOX_HEADS * HEAD_PAD), bf16)
    k_spec = rows(FOX_HEADS * HEAD_PAD)
    vt_shape = jax.ShapeDtypeStruct((B, FOX_HEADS, ns, V_ROWS, tt), bf16)
    vt_spec = pl.BlockSpec((1, FOX_HEADS, 1, V_ROWS, tt), lambda b, s: (b, 0, s, 0, 0))
    out_shape = [qt_shape, k_shape, vt_shape] * 2
    out_specs = [qt_spec, k_spec, vt_spec] * 2
    scratch = [pltpu.VMEM((1, LANES), jnp.float32)]
    if has_moe_in:
        out_shape = [jax.ShapeDtypeStruct((T, D), jnp.float32)] + out_shape
        out_specs = [rows(D)] + out_specs
        scratch += [pltpu.VMEM((2, tt * (D // LANES), LANES), jnp.float32), pltpu.SemaphoreType.DMA((2,))]

    outs = pl.pallas_call(
        functools.partial(_pre_attn_kernel, has_moe_in=has_moe_in, tt=tt),
        grid=(B, ns),
        in_specs=in_specs,
        out_specs=out_specs,
        out_shape=out_shape,
        scratch_shapes=scratch,
        compiler_params=_cparams("arbitrary", "arbitrary"),
        name="pre_attn",
    )(*args)
    if has_moe_in:
        return outs[0], outs[1:]
    return h, outs


def _attn_kernel(qt_ref, k_ref, vt_ref, g_ref, o_ref, s_scr, mc_scr, m_scr, acc_scr, *, ta, nq):
    vd = MLA_V_DIM

    def produce(q, kt, slot, masked):
        k0 = pl.multiple_of(kt * ta, ta)
        for hh in range(2):
            kk = k_ref[pl.ds(k0, ta), hh * HEAD_PAD:(hh + 1) * HEAD_PAD]
            qt = qt_ref[0, hh, q]
            if not masked:
                s = _dot(kk, qt)
                s_scr[slot, hh] = s
                mc_scr[slot, hh] = jnp.max(s, axis=0, keepdims=True)
                continue
            hk = ta // 2
            top = _dot(kk[0:hk], qt)
            low = _dot(kk[hk:], qt[:, hk:])
            top = jnp.where(lax.broadcasted_iota(jnp.int32, top.shape, 0)
                            <= lax.broadcasted_iota(jnp.int32, top.shape, 1), top, NEG_BIG)
            low = jnp.where(lax.broadcasted_iota(jnp.int32, low.shape, 0)
                            <= lax.broadcasted_iota(jnp.int32, low.shape, 1), low, NEG_BIG)
            s_scr[slot, hh, 0:hk, :] = top
            s_scr[slot, hh, hk:, 0:hk] = jnp.full((hk, hk), NEG_BIG, jnp.float32)
            s_scr[slot, hh, hk:, hk:] = low
            top_max = jnp.max(top, axis=0, keepdims=True)
            mc_scr[slot, hh, :, 0:hk] = top_max[:, 0:hk]
            mc_scr[slot, hh, :, hk:] = jnp.maximum(top_max[:, hk:], jnp.max(low, axis=0, keepdims=True))

    def consume(kt, slot):
        for hh in range(2):
            m_prev = m_scr[hh]
            m_new = jnp.maximum(m_prev, mc_scr[slot, hh])
            alpha = jnp.exp2(m_prev - m_new)
            p = jnp.exp2(s_scr[slot, hh] - m_new).astype(jnp.bfloat16)
            acc_scr[hh] = alpha * acc_scr[hh] + _dot(vt_ref[0, hh, kt], p)
            m_scr[hh] = m_new

    def reset():
        m_scr[...] = jnp.full_like(m_scr, NEG_BIG)
        acc_scr[...] = jnp.zeros_like(acc_scr)

    def finish(q):
        outs = []
        for hh in range(2):
            acc = acc_scr[hh]
            oh = acc[0:vd] / acc[vd:vd + 1]
            ms = jnp.mean(oh * oh, axis=0, keepdims=True)
            outs.append(oh * lax.rsqrt(ms + NORM_EPS))
        o_t = jnp.concatenate(outs, axis=0) * g_ref[...]
        o_ref[pl.ds(pl.multiple_of(q * ta, ta), ta), :] = o_t.T.astype(o_ref.dtype)

    reset()
    produce(0, 0, 0, True)

    def q_tile(n, carry):
        first = (n * (n + 1) // 2) % 2

        def step(j, slot):
            produce(n, j, 1 - slot, False)
            consume(jnp.where(j == 0, n, j - 1), slot)

        for par in range(2):
            @pl.when(first == par)
            def _():
                def two_steps(u, c):
                    step(2 * u, par)
                    step(2 * u + 1, 1 - par)
                    return c

                lax.fori_loop(0, n // 2, two_steps, 0)

                @pl.when(n % 2 == 1)
                def _():
                    step(n - 1, par)

        last_slot = (first + n) % 2
        last_kt = jnp.maximum(n - 1, 0)
        for par in range(2):
            @pl.when((last_slot == par) & (n < nq - 1))
            def _():
                produce(n + 1, n + 1, 1 - par, True)
                consume(last_kt, par)
                finish(n)
                reset()

        @pl.when(n == nq - 1)
        def _():
            consume(last_kt, ((nq - 1) * nq // 2 + nq - 1) % 2)
            finish(n)

        return carry

    lax.fori_loop(0, nq, q_tile, 0)


def _attention(qt, k, vt, gain_rep, B, S, name):
    T = k.shape[0]
    ta = min(ATTN_TILE, S)
    nq = S // ta
    n_pairs = qt.shape[1] // 2
    return pl.pallas_call(
        functools.partial(_attn_kernel, ta=ta, nq=nq),
        grid=(B, n_pairs),
        in_specs=[pl.BlockSpec((1, 2, nq, HEAD_PAD, ta), lambda b, j: (b, j, 0, 0, 0)),
                  pl.BlockSpec((S, 2 * HEAD_PAD), lambda b, j: (b, j)),
                  pl.BlockSpec((1, 2, nq, V_ROWS, ta), lambda b, j: (b, j, 0, 0, 0)),
                  pl.BlockSpec((LANES, ta), lambda b, j: (j, 0))],
        out_specs=pl.BlockSpec((S, LANES), lambda b, j: (b, j)),
        out_shape=jax.ShapeDtypeStruct((T, n_pairs * LANES), jnp.bfloat16),
        scratch_shapes=[pltpu.VMEM((2, 2, ta, ta), jnp.float32),
                        pltpu.VMEM((2, 2, 1, ta), jnp.float32),
                        pltpu.VMEM((2, 1, ta), jnp.float32),
                        pltpu.VMEM((2, V_ROWS, ta), jnp.float32)],
        compiler_params=_cparams("parallel", "parallel"),
        name=name,
    )(qt, k, vt, gain_rep)


def _post_attn_kernel(fox_ref, mla_ref, h_ref, wo_ref, fnorm_ref, wrh_ref, wrl_ref, tri_ref,
                      h2_ref, xr_ref, route_ref, cnt_ref, carry_ref, xn_scr):
    i = pl.program_id(0)
    D = h_ref.shape[1]

    @pl.when(i == 0)
    def _():
        xn_scr[...] = jnp.zeros_like(xn_scr)

    @pl.when(i <= 1)
    def _():
        carry_ref[...] = jnp.zeros_like(carry_ref)

    xn = xn_scr[...]
    xr_ref[:, 0:D] = xn
    h2 = h_ref[...] + _dot(fox_ref[...], wo_ref[0:FOX_WIDTH, :]) + _dot(mla_ref[...], wo_ref[FOX_WIDTH:, :])
    h2_ref[...] = h2
    xn_scr[...] = _rms(h2, fnorm_ref[...])

    x_hi = xn.astype(jnp.bfloat16)
    x_lo = (xn - x_hi.astype(jnp.float32)).astype(jnp.bfloat16)
    logits = _dot(x_hi, wrh_ref[...]) + _dot(x_lo, wrh_ref[...]) + _dot(x_hi, wrl_ref[...])

    lt = logits.T
    tt = lt.shape[1]
    ng = N_GROUPS
    gl = lt[0:ng]
    sub = lax.broadcasted_iota(jnp.int32, (ng, tt), 0).astype(jnp.float32)

    gmax = jnp.max(gl, axis=0, keepdims=True)
    group_w = 1.0 / jnp.sum(jnp.exp(gl - gmax), axis=0, keepdims=True)
    gid = jnp.min(jnp.where(gl == gmax, sub, float(ng)), axis=0, keepdims=True)
    onehot = sub == gid

    ins = [jnp.sum(jnp.where(onehot, lt[ng * (j + 1):ng * (j + 2)], 0.0), axis=0, keepdims=True)
           for j in range(EXPERTS_PER_GROUP)]

    def first_argmax(vals):
        top = functools.reduce(jnp.maximum, vals)
        idx = jnp.full_like(top, float(len(vals) - 1))
        for j in range(len(vals) - 2, -1, -1):
            idx = jnp.where(vals[j] == top, float(j), idx)
        return top, idx

    e1, i1 = first_argmax(ins)
    e2, i2 = first_argmax([jnp.where(i1 == j, NEG_BIG, v) for j, v in enumerate(ins)])
    t = jnp.exp(e2 - e1)
    g1 = group_w / (1.0 + t)
    g2 = group_w * t / (1.0 + t)
    gates = [jnp.where(i1 == j, g1, jnp.where(i2 == j, g2, 0.0)) for j in range(EXPERTS_PER_GROUP)]

    ones = jnp.where(onehot, 1.0, 0.0)
    prefix = _dot(ones.astype(jnp.bfloat16), tri_ref[...]) + carry_ref[:, 0:1]
    rank = jnp.sum(jnp.where(onehot, prefix, 0.0), axis=0, keepdims=True)
    carry = carry_ref[...] + jnp.sum(ones, axis=1, keepdims=True)
    carry_ref[...] = carry
    cnt_ref[...] = carry

    pad_rows = jnp.zeros((ROUTE_ROWS - EXPERTS_PER_GROUP - 2, tt), jnp.float32)
    info_t = jnp.concatenate(gates + [gid, rank, pad_rows], axis=0)
    route_ref[...] = info_t
    info_pad = jnp.zeros((LANES - ROUTE_ROWS, tt), jnp.float32)
    xr_ref[:, D:D + LANES] = jnp.concatenate([info_t, info_pad], axis=0).T


def _post_attn(fox, mla, h, lw):
    T, D = h.shape
    tt = min(ROW_TILE, T)
    n = T // tt
    proj = lambda i: (jnp.minimum(i, n - 1), 0)
    routed = lambda i: (jnp.maximum(i - 1, 0), 0)
    const = lambda i: (0, 0)
    weights = [lw["w_o"], lw["ffn_norm"], lw["wr_hi"], lw["wr_lo"], lw["tri_before"]]
    return pl.pallas_call(
        _post_attn_kernel,
        grid=(n + 1,),
        in_specs=[pl.BlockSpec((tt, FOX_WIDTH), proj), pl.BlockSpec((tt, MLA_WIDTH), proj),
                  pl.BlockSpec((tt, D), proj)] + [pl.BlockSpec(w.shape, const) for w in weights],
        out_specs=[pl.BlockSpec((tt, D), proj), pl.BlockSpec((tt, D + LANES), routed),
                   pl.BlockSpec((ROUTE_ROWS, tt), lambda i: (0, jnp.maximum(i - 1, 0))),
                   pl.BlockSpec((N_GROUPS, LANES), const)],
        out_shape=[jax.ShapeDtypeStruct((T, D), jnp.float32),
                   jax.ShapeDtypeStruct((T, D + LANES), jnp.float32),
                   jax.ShapeDtypeStruct((ROUTE_ROWS, T), jnp.float32),
                   jax.ShapeDtypeStruct((N_GROUPS, LANES), jnp.float32)],
        scratch_shapes=[pltpu.VMEM((N_GROUPS, LANES), jnp.float32), pltpu.VMEM((tt, D), jnp.float32)],
        compiler_params=_cparams("arbitrary"),
        name="post_attn",
    )(fox, mla, h, *weights)


def _dispatch_kernel(fill_ref, dest_ref, xr_ref, xs_hbm, zero_ref, sem, fill_sem, *, tt, n_sorted):
    first = pl.program_id(0) == 0

    def fill_copies(act):
        for g in range(N_GROUPS):
            start, length = fill_ref[g], fill_ref[N_GROUPS + g]
            head = (-start) & (SUBLANES - 1)
            for r in range(SUBLANES - 1):
                @pl.when(r < head)
                def _(start=start, r=r):
                    act(pltpu.make_async_copy(zero_ref.at[pl.ds(0, 1)], xs_hbm.at[pl.ds(start + r, 1)], fill_sem))
            base, rest = start + head, length - head
            size = MOE_BLOCK // 2
            while size >= SUBLANES:
                @pl.when((rest & size) != 0)
                def _(base=base, rest=rest, size=size):
                    at = pl.multiple_of(base + (rest & ~(2 * size - 1)), SUBLANES)
                    act(pltpu.make_async_copy(zero_ref.at[pl.ds(0, size)], xs_hbm.at[pl.ds(at, size)], fill_sem))
                size //= 2
        used = fill_ref[2 * N_GROUPS]
        for blk in range(N_GROUPS):
            @pl.when(used + (blk + 1) * MOE_BLOCK <= n_sorted)
            def _(blk=blk):
                at = pl.multiple_of(used + blk * MOE_BLOCK, MOE_BLOCK)
                act(pltpu.make_async_copy(zero_ref, xs_hbm.at[pl.ds(at, MOE_BLOCK)], fill_sem))

    @pl.when(first)
    def _():
        zero_ref[...] = jnp.zeros_like(zero_ref)
        fill_copies(lambda cp: cp.start())

    def issue(r, carry):
        pltpu.make_async_copy(xr_ref.at[pl.ds(r, 1)], xs_hbm.at[pl.ds(dest_ref[0, 0, r], 1)], sem).start()
        return carry
    lax.fori_loop(0, tt, issue, 0, unroll=ROW_DMA_UNROLL)
    pltpu.make_async_copy(xr_ref, xs_hbm.at[pl.ds(0, tt)], sem).wait()

    @pl.when(first)
    def _():
        fill_copies(lambda cp: cp.wait())


def _dispatch(xr, dest3, fill, n_sorted):
    T, W = xr.shape
    tt = dest3.shape[2]
    grid_spec = pltpu.PrefetchScalarGridSpec(
        num_scalar_prefetch=1,
        grid=(T // tt,),
        in_specs=[pl.BlockSpec((1, 1, tt), lambda i, f: (i, 0, 0), memory_space=pltpu.SMEM),
                  pl.BlockSpec((tt, W), lambda i, f: (i, 0))],
        out_specs=pl.BlockSpec(memory_space=pl.ANY),
        scratch_shapes=[pltpu.VMEM((MOE_BLOCK, W), xr.dtype),
                        pltpu.SemaphoreType.DMA(()), pltpu.SemaphoreType.DMA(())],
    )
    return pl.pallas_call(
        functools.partial(_dispatch_kernel, tt=tt, n_sorted=n_sorted),
        grid_spec=grid_spec,
        out_shape=jax.ShapeDtypeStruct((n_sorted, W), xr.dtype),
        compiler_params=_cparams("arbitrary"),
        name="moe_dispatch",
    )(fill, dest3, xr)


def _moe_kernel(bg_ref, nv_ref, xs_ref, wg32_ref, wu32_ref, wd32_ref, ys_ref, wg_ref, wu_ref, wd_ref):
    D = xs_ref.shape[1] - LANES
    rows = xs_ref.shape[0]
    chunks = D // LANES
    b = pl.program_id(0)
    g = bg_ref[b]
    n_valid = nv_ref[b]

    @pl.when((b == 0) | (g != bg_ref[jnp.maximum(b - 1, 0)]))
    def _():
        wg_ref[...] = wg32_ref[...].astype(jnp.bfloat16)
        wu_ref[...] = wu32_ref[...].astype(jnp.bfloat16)
        wd_ref[...] = wd32_ref[...].astype(jnp.bfloat16)

    @pl.when(n_valid == 0)
    def _():
        ys_ref[...] = jnp.zeros_like(ys_ref)

    @pl.when(n_valid > 0)
    def _():
        x = xs_ref[:, 0:D].astype(jnp.bfloat16)
        info = xs_ref[:, D:D + LANES]
        y = jnp.zeros((rows, D), jnp.float32)
        for j in range(EXPERTS_PER_GROUP):
            a = _dot(x, wg_ref[j])
            u = _dot(x, wu_ref[j])
            hj = (a * jax.nn.sigmoid(a) * u).astype(jnp.bfloat16)
            y = y + info[:, j:j + 1] * _dot(hj, wd_ref[j])
        for c in range(chunks):
            ys_ref[pl.ds(c, rows, stride=chunks), :] = y[:, c * LANES:(c + 1) * LANES]


def _moe(xs, block_group, block_valid, layer, w_gate, w_up, w_down):
    n_sorted, W = xs.shape
    D = W - LANES
    nb = n_sorted // MOE_BLOCK
    epg = EXPERTS_PER_GROUP
    chunks = D // LANES
    grid_spec = pltpu.PrefetchScalarGridSpec(
        num_scalar_prefetch=2,
        grid=(nb,),
        in_specs=[pl.BlockSpec((MOE_BLOCK, W), lambda b, bg, nv: (b, 0)),
                  pl.BlockSpec((None, epg, D, D_EXPERT), lambda b, bg, nv: (layer, bg[b], 0, 0)),
                  pl.BlockSpec((None, epg, D, D_EXPERT), lambda b, bg, nv: (layer, bg[b], 0, 0)),
                  pl.BlockSpec((None, epg, D_EXPERT, D), lambda b, bg, nv: (layer, bg[b], 0, 0))],
        out_specs=pl.BlockSpec((MOE_BLOCK * chunks, LANES), lambda b, bg, nv: (b, 0)),
        scratch_shapes=[pltpu.VMEM((epg, D, D_EXPERT), jnp.bfloat16),
                        pltpu.VMEM((epg, D, D_EXPERT), jnp.bfloat16),
                        pltpu.VMEM((epg, D_EXPERT, D), jnp.bfloat16)],
    )
    return pl.pallas_call(
        _moe_kernel,
        grid_spec=grid_spec,
        out_shape=jax.ShapeDtypeStruct((n_sorted * chunks, LANES), jnp.float32),
        compiler_params=_cparams("arbitrary"),
        name="moe_experts",
    )(block_group, block_valid, xs, w_gate, w_up, w_down)


def _final_kernel(dest_ref, dest_next_ref, h_ref, ys_hbm, gain_ref, o_ref, ybuf_ref, sems, *, tt):
    y = _gather_rows(ys_hbm, dest_ref, dest_next_ref, ybuf_ref, sems,
                     pl.program_id(0), pl.num_programs(0), tt)
    o_ref[...] = _rms(h_ref[...] + y, gain_ref[...])


def _final(h, dest3, ys, gain):
    T, D = h.shape
    tt = dest3.shape[2]
    last = T // tt - 1
    return pl.pallas_call(
        functools.partial(_final_kernel, tt=tt),
        grid=(T // tt,),
        in_specs=[pl.BlockSpec((1, 1, tt), lambda i: (i, 0, 0), memory_space=pltpu.SMEM),
                  pl.BlockSpec((1, 1, tt), lambda i: (jnp.minimum(i + 1, last), 0, 0), memory_space=pltpu.SMEM),
                  pl.BlockSpec((tt, D), lambda i: (i, 0)),
                  pl.BlockSpec(memory_space=pl.ANY),
                  pl.BlockSpec((1, D), lambda i: (0, 0))],
        out_specs=pl.BlockSpec((tt, D), lambda i: (i, 0)),
        out_shape=jax.ShapeDtypeStruct((T, D), jnp.float32),
        scratch_shapes=[pltpu.VMEM((2, tt * (D // LANES), LANES), jnp.float32), pltpu.SemaphoreType.DMA((2,))],
        compiler_params=_cparams("arbitrary"),
        name="final_norm",
    )(dest3, dest3, h, ys, gain)


def _prep_layer(l, tt, w_in, b_f, w_uq, w_ukv, w_o, w_group, w_router,
                attn_norm, q_norm, kv_norm, ffn_norm):
    bf16 = jnp.bfloat16
    D = w_in.shape[1]
    fw = FOX_WIDTH
    o_f = 3 * fw
    o_cq = o_f + FOX_HEADS
    o_ckv = o_cq + Q_LORA_RANK
    o_kr = o_ckv + KV_LORA_RANK
    wi = w_in[l]
    w_f = wi[:, o_f:o_cq]
    w_f3 = jnp.concatenate([w_f] * DECAY_PARTS + [jnp.zeros((D, LANES - DECAY_PARTS * FOX_HEADS), wi.dtype)], axis=1)
    b_f3 = jnp.concatenate([b_f[l]] * DECAY_PARTS + [jnp.zeros((LANES - DECAY_PARTS * FOX_HEADS,), b_f.dtype)])
    kr = wi[:, o_kr:o_kr + MLA_ROPE_DIM]
    kr_swap = jnp.concatenate([-kr[:, ROPE_HALF:], kr[:, :ROPE_HALF]], axis=1)
    zpad = jnp.zeros((D, MLA_ROPE_DIM), wi.dtype)
    w_rows = jnp.concatenate([wi[:, fw:2 * fw], w_f3, wi[:, o_cq:o_kr], kr_swap, zpad, kr, zpad], axis=1)

    src = jnp.arange(LANES)
    dst = (src % FOX_HEADS) * HEAD_PAD + FOX_HEAD_DIM + src // FOX_HEADS
    place = ((jnp.arange(FOX_HEADS * HEAD_PAD)[None, :] == dst[:, None])
             & (src[:, None] < DECAY_PARTS * FOX_HEADS)).astype(bf16)

    qd = MLA_NOPE_DIM + MLA_ROPE_DIM
    wq = w_uq[l].reshape(Q_LORA_RANK, MLA_HEADS, qd)
    zq = jnp.zeros((Q_LORA_RANK, MLA_HEADS, HEAD_PAD - qd), wq.dtype)
    wq_main = jnp.concatenate([wq, zq], axis=2).reshape(Q_LORA_RANK, MLA_HEADS * HEAD_PAD)
    x1 = wq[:, :, MLA_NOPE_DIM:MLA_NOPE_DIM + ROPE_HALF]
    x2 = wq[:, :, MLA_NOPE_DIM + ROPE_HALF:]
    wq_swap = jnp.concatenate([-x2, x1], axis=2).reshape(Q_LORA_RANK, MLA_HEADS * MLA_ROPE_DIM)

    wkv = w_ukv[l].reshape(KV_LORA_RANK, MLA_HEADS, MLA_NOPE_DIM + MLA_V_DIM)
    wk_nope = jnp.concatenate([wkv[:, :, :MLA_NOPE_DIM],
                               jnp.zeros((KV_LORA_RANK, MLA_HEADS, HEAD_PAD - MLA_NOPE_DIM), wkv.dtype)], axis=2)
    wv = wkv[:, :, MLA_NOPE_DIM:].reshape(KV_LORA_RANK, MLA_WIDTH)

    w_exp = w_router[l].reshape(D, N_GROUPS, EXPERTS_PER_GROUP).transpose(0, 2, 1).reshape(D, N_EXPERTS)
    wr = jnp.concatenate([w_group[l], w_exp,
                          jnp.zeros((D, LANES - N_GROUPS - N_EXPERTS), w_group.dtype)], axis=1)
    wr_hi = wr.astype(bf16)
    wr_lo = (wr - wr_hi.astype(jnp.float32)).astype(bf16)

    ar = jnp.arange(tt)
    return dict(
        attn_norm=attn_norm[l].reshape(1, D),
        w_rows=w_rows.astype(bf16),
        w_fq_t=wi[:, 0:fw].T.astype(bf16),
        w_fv_t=wi[:, 2 * fw:3 * fw].T.astype(bf16),
        b_f=b_f3.reshape(1, LANES),
        tri_incl=(ar[:, None] >= ar[None, :]).astype(bf16),
        tri_before=(ar[:, None] < ar[None, :]).astype(bf16),
        place=place,
        q_norm=q_norm[l].reshape(1, Q_LORA_RANK),
        wq_main_t=wq_main.T.astype(bf16),
        wq_swap_t=wq_swap.T.astype(bf16),
        kv_norm=kv_norm[l].reshape(1, KV_LORA_RANK),
        wk_nope=wk_nope.reshape(KV_LORA_RANK, MLA_HEADS * HEAD_PAD).astype(bf16),
        wv_t=wv.T.astype(bf16),
        w_o=w_o[l].astype(bf16),
        ffn_norm=ffn_norm[l].reshape(1, D),
        wr_hi=wr_hi,
        wr_lo=wr_lo,
    )


def _routing_plan(route, counts, n_blocks):
    gid = route[GID_INDEX].astype(jnp.int32)
    rank = route[RANK_INDEX].astype(jnp.int32)
    cnt = counts[:, 0].astype(jnp.int32)
    padded = (cnt + MOE_BLOCK - 1) // MOE_BLOCK * MOE_BLOCK
    pend = jnp.cumsum(padded)
    pstart = pend - padded
    dest = pstart[gid] + rank
    block_start = jnp.arange(n_blocks, dtype=jnp.int32) * MOE_BLOCK
    block_group = jnp.sum((pend[None, :] <= block_start[:, None]).astype(jnp.int32), axis=1)
    block_group = jnp.minimum(block_group, N_GROUPS - 1)
    block_valid = jnp.clip((pstart + cnt)[block_group] - block_start, 0, MOE_BLOCK)
    fill = jnp.concatenate([pstart + cnt, padded - cnt, pend[-1:]])
    return dest, block_group, block_valid, fill


def kernel(x, positions, attn_norm, w_in, b_f, q_norm, w_uq, kv_norm, w_ukv, fox_out_norm, mla_out_norm, w_o,
           ffn_norm, w_group, w_router, w_gate, w_up, w_down, final_norm):
    B, S, D = x.shape
    T = B * S
    depth = w_in.shape[0]
    ta = min(ATTN_TILE, S)
    tt = min(ROW_TILE, S)
    n_blocks = -(-(T + N_GROUPS * (MOE_BLOCK - 1)) // MOE_BLOCK)

    cos_t, sin_t = _rope_tables(positions)
    tabs = (cos_t, sin_t)

    h = x.reshape(T, D)
    moe_in = None
    for l in range(depth):
        lw = _prep_layer(l, tt, w_in, b_f, w_uq, w_ukv, w_o, w_group, w_router,
                         attn_norm, q_norm, kv_norm, ffn_norm)
        h, (fq_t, fk, fv_t, mq_t, mk, mv_t) = _pre_attn(h, moe_in, B, S, lw, tabs)
        fox_gain = jnp.broadcast_to(fox_out_norm[l].reshape(FOX_WIDTH, 1), (FOX_WIDTH, ta))
        mla_gain = jnp.broadcast_to(mla_out_norm[l].reshape(MLA_WIDTH, 1), (MLA_WIDTH, ta))
        fox = _attention(fq_t, fk, fv_t, fox_gain, B, S, "attn_fox")
        mla = _attention(mq_t, mk, mv_t, mla_gain, B, S, "attn_mla")
        h, xr, route, counts = _post_attn(fox, mla, h, lw)
        dest, block_group, block_valid, fill = _routing_plan(route, counts, n_blocks)
        dest3 = dest.reshape(T // tt, 1, tt)
        xs = _dispatch(xr, dest3, fill, n_blocks * MOE_BLOCK)
        ys = _moe(xs, block_group, block_valid, l, w_gate, w_up, w_down)
        moe_in = (dest3, ys)
    out = _final(h, moe_in[0], moe_in[1], final_norm.reshape(1, D))
    return out.reshape(B, S, D)
```

```python
import functools
import math

import jax
import jax.numpy as jnp
from jax import lax
from jax.experimental import pallas as pl
from jax.experimental.pallas import tpu as pltpu

FOX_HEADS = 8
FOX_HEAD_DIM = 64
FOX_WIDTH = FOX_HEADS * FOX_HEAD_DIM
MLA_HEADS = 8
MLA_NOPE_DIM = 64
MLA_ROPE_DIM = 32
MLA_V_DIM = 64
MLA_WIDTH = MLA_HEADS * MLA_V_DIM
Q_LORA_RANK = 256
KV_LORA_RANK = 128
ROPE_THETA = 10000.0
N_GROUPS = 8
EXPERTS_PER_GROUP = 4
N_EXPERTS = N_GROUPS * EXPERTS_PER_GROUP
D_EXPERT = 256
NORM_EPS = 1e-6

LANES = 128
SUBLANES = 8
BF16_SUBLANES = 16
VMEM_LIMIT_BYTES = 56 * 1024 * 1024

ROW_TILE = 512
DISPATCH_TILE = 1024
ATTN_TILE = 512
MOE_BLOCK = 512
ROW_DMA_UNROLL = 8
HEAD_PAD = 128
V_ROWS = MLA_V_DIM + BF16_SUBLANES
ROPE_HALF = MLA_ROPE_DIM // 2
NEG_BIG = -1e30
LOG2E = math.log2(math.e)
DECAY_PARTS = 3

GID_INDEX = EXPERTS_PER_GROUP
RANK_INDEX = EXPERTS_PER_GROUP + 1
ROUTE_ROWS = SUBLANES


def _cparams(*semantics):
    return pltpu.CompilerParams(dimension_semantics=semantics, vmem_limit_bytes=VMEM_LIMIT_BYTES)


def _rms(x, gain):
    ms = jnp.mean(x * x, axis=-1, keepdims=True)
    return x * lax.rsqrt(ms + NORM_EPS) * gain


def _dot(a, b):
    return jnp.dot(a, b, preferred_element_type=jnp.float32)


def _dot_nt(a, b):
    return lax.dot_general(a, b, (((1,), (1,)), ((), ())), preferred_element_type=jnp.float32)


def _split_bf16(x, parts):
    out = []
    for _ in range(parts - 1):
        piece = x.astype(jnp.bfloat16).astype(jnp.float32)
        out.append(piece)
        x = x - piece
    out.append(x.astype(jnp.bfloat16).astype(jnp.float32))
    return out


def _rope_table_kernel(pos_ref, invf_ref, cos_ref, sin_ref):
    ang = pos_ref[...].astype(jnp.float32) * invf_ref[...]
    cos_ref[...] = jnp.cos(ang)
    sin_ref[...] = jnp.sin(ang)


def _rope_tables(positions):
    T = positions.size
    inv_freq = ROPE_THETA ** (-jnp.arange(ROPE_HALF, dtype=jnp.float32) / ROPE_HALF)
    ct = min(T, 4096)
    return pl.pallas_call(
        _rope_table_kernel,
        grid=(T // ct,),
        in_specs=[pl.BlockSpec((1, ct), lambda i: (0, i)),
                  pl.BlockSpec((ROPE_HALF, 1), lambda i: (0, 0))],
        out_specs=[pl.BlockSpec((ROPE_HALF, ct), lambda i: (0, i))] * 2,
        out_shape=[jax.ShapeDtypeStruct((ROPE_HALF, T), jnp.float32)] * 2,
        compiler_params=_cparams("parallel"),
        name="rope_tables",
    )(positions.reshape(1, T), inv_freq.reshape(ROPE_HALF, 1))


def _gather_rows(src_hbm, idx_ref, idx_next_ref, buf_ref, sems, step, n_steps, n_rows):
    slot = step % 2
    chunks = buf_ref.shape[1] // n_rows

    def request(ref, to_slot):
        def issue(r, carry):
            src = pl.multiple_of(ref[0, 0, r] * chunks, chunks)
            dst = pl.multiple_of(r * chunks, chunks)
            pltpu.make_async_copy(src_hbm.at[pl.ds(src, chunks)], buf_ref.at[to_slot, pl.ds(dst, chunks)],
                                  sems.at[to_slot]).start()
            return carry
        lax.fori_loop(0, n_rows, issue, 0, unroll=ROW_DMA_UNROLL)

    @pl.when(step == 0)
    def _():
        request(idx_ref, slot)

    @pl.when(step + 1 < n_steps)
    def _():
        request(idx_next_ref, 1 - slot)

    pltpu.make_async_copy(src_hbm.at[pl.ds(0, n_rows * chunks)], buf_ref.at[slot], sems.at[slot]).wait()
    return jnp.concatenate([buf_ref[slot, pl.ds(c, n_rows, stride=chunks), :] for c in range(chunks)], axis=1)


def _pre_attn_kernel(*refs, has_moe_in, tt):
    if has_moe_in:
        dest_ref, dest_next_ref, h_ref, ys_hbm = refs[:4]
        refs = refs[4:]
    else:
        h_ref = refs[0]
        refs = refs[1:]
    (anorm_ref, wrow_ref, wfq_ref, wfv_ref, bf_ref, tri_ref, place_ref, qnorm_ref, wqm_ref, wqs_ref,
     kvnorm_ref, wkn_ref, wv_ref, cos_ref, sin_ref) = refs[:15]
    refs = refs[15:]
    if has_moe_in:
        hout_ref = refs[0]
        refs = refs[1:]
    fq_ref, fk_ref, fv_ref, qm_ref, km_ref, vm_ref = refs[:6]
    refs = refs[6:]
    carry_ref = refs[0]
    if has_moe_in:
        ybuf_ref, sems = refs[1:3]

    si = pl.program_id(1)
    bf16 = jnp.bfloat16
    half = LANES // 2

    @pl.when(si == 0)
    def _():
        carry_ref[...] = jnp.zeros_like(carry_ref)

    h = h_ref[...]
    if has_moe_in:
        step = pl.program_id(0) * pl.num_programs(1) + si
        n_steps = pl.num_programs(0) * pl.num_programs(1)
        h = h + _gather_rows(ys_hbm, dest_ref, dest_next_ref, ybuf_ref, sems, step, n_steps, tt)
        hout_ref[...] = h

    xn = _rms(h, anorm_ref[...]).astype(bf16)
    lane = lax.broadcasted_iota(jnp.int32, (tt, LANES), 1)
    sub = lax.broadcasted_iota(jnp.int32, (BF16_SUBLANES, tt), 0)
    ones_rows = jnp.ones((BF16_SUBLANES, tt), bf16)

    fw = FOX_WIDTH
    o = fw
    z = _dot(xn, wrow_ref[:, o:o + LANES]) + bf_ref[...]
    o += LANES
    cq_raw = _dot(xn, wrow_ref[:, o:o + Q_LORA_RANK])
    o += Q_LORA_RANK
    ckv_raw = _dot(xn, wrow_ref[:, o:o + KV_LORA_RANK])
    o += KV_LORA_RANK
    kr_raw = _dot(xn, wrow_ref[:, o:o + LANES])

    fq_t = _dot_nt(wfq_ref[...], xn) * (FOX_HEAD_DIM ** -0.5 * LOG2E)
    q_aug = jnp.where(sub < DECAY_PARTS, -1.0, 0.0).astype(bf16)
    q_zero = jnp.zeros((HEAD_PAD - FOX_HEAD_DIM - BF16_SUBLANES, tt), bf16)
    for hd in range(FOX_HEADS):
        fq_ref[0, hd, 0, 0:FOX_HEAD_DIM, :] = fq_t[hd * FOX_HEAD_DIM:(hd + 1) * FOX_HEAD_DIM].astype(bf16)
        fq_ref[0, hd, 0, FOX_HEAD_DIM:FOX_HEAD_DIM + BF16_SUBLANES, :] = q_aug
        fq_ref[0, hd, 0, FOX_HEAD_DIM + BF16_SUBLANES:, :] = q_zero
    fv_t = _dot_nt(wfv_ref[...], xn)
    vd = MLA_V_DIM
    for hd in range(FOX_HEADS):
        fv_ref[0, hd, 0, 0:vd, :] = fv_t[hd * vd:(hd + 1) * vd].astype(bf16)
        fv_ref[0, hd, 0, vd:, :] = ones_rows

    logf = jnp.minimum(z, 0.0) - jnp.log1p(jnp.exp(-jnp.abs(z)))
    pieces = jnp.concatenate([p_.astype(bf16) for p_ in _split_bf16(logf, DECAY_PARTS)], axis=1)
    csum = _dot(tri_ref[...], pieces)

    dcum = csum[:, 0:LANES] + csum[:, LANES:2 * LANES] + csum[:, 2 * LANES:] + carry_ref[...]
    carry_ref[...] = dcum[tt - 1:tt, :]
    d_parts = _split_bf16(dcum * LOG2E, DECAY_PARTS)
    d_sel = jnp.where(lane < FOX_HEADS, d_parts[0],
                      jnp.where(lane < 2 * FOX_HEADS, d_parts[1],
                                jnp.where(lane < 3 * FOX_HEADS, d_parts[2], 0.0)))
    k_aug = _dot(d_sel.astype(bf16), place_ref[...])

    fk = _dot(xn, wrow_ref[:, 0:fw])
    for p in range(FOX_HEADS // 2):
        blk = fk[:, p * LANES:(p + 1) * LANES]
        for hh, b_ in ((0, blk), (1, pltpu.roll(blk, half, axis=1))):
            sl = slice((2 * p + hh) * HEAD_PAD, (2 * p + hh + 1) * HEAD_PAD)
            fk_ref[:, sl] = jnp.where(lane < FOX_HEAD_DIM, b_, k_aug[:, sl]).astype(bf16)

    cq = _rms(cq_raw, qnorm_ref[...]).astype(bf16)
    qm_t = _dot_nt(wqm_ref[...], cq)
    qs_t = _dot_nt(wqs_ref[...], cq)
    cos_t = cos_ref[...]
    sin_t = sin_ref[...]
    scale = (MLA_NOPE_DIM + MLA_ROPE_DIM) ** -0.5 * LOG2E
    q_zero = jnp.zeros((HEAD_PAD - MLA_NOPE_DIM - MLA_ROPE_DIM, tt), jnp.float32)
    for hd in range(MLA_HEADS):
        blk = qm_t[hd * HEAD_PAD:(hd + 1) * HEAD_PAD]
        rows = [blk[0:MLA_NOPE_DIM]]
        for j in range(2):
            main = blk[MLA_NOPE_DIM + j * ROPE_HALF:MLA_NOPE_DIM + (j + 1) * ROPE_HALF]
            swap = qs_t[hd * MLA_ROPE_DIM + j * ROPE_HALF:hd * MLA_ROPE_DIM + (j + 1) * ROPE_HALF]
            rows.append(main * cos_t + swap * sin_t)
        rows.append(q_zero)
        qm_ref[0, hd, 0] = (jnp.concatenate(rows, axis=0) * scale).astype(bf16)

    ckv = _rms(ckv_raw, kvnorm_ref[...]).astype(bf16)
    mv_t = _dot_nt(wv_ref[...], ckv)
    for hd in range(MLA_HEADS):
        vm_ref[0, hd, 0, 0:vd, :] = mv_t[hd * vd:(hd + 1) * vd].astype(bf16)
        vm_ref[0, hd, 0, vd:, :] = ones_rows
    k_nope = _dot(ckv, wkn_ref[...])
    z_rows = jnp.zeros((MLA_ROPE_DIM, tt), jnp.float32)
    csk = jnp.concatenate([sin_t, sin_t, z_rows, cos_t, cos_t, z_rows], axis=0).T
    kr = kr_raw * csk
    kr = kr + pltpu.roll(kr, half, axis=1)
    k_pe = jnp.where(lane >= MLA_NOPE_DIM, kr, 0.0)
    for hd in range(MLA_HEADS):
        sl = slice(hd * HEAD_PAD, (hd + 1) * HEAD_PAD)
        km_ref[:, sl] = (k_nope[:, sl] + k_pe).astype(bf16)


def _pre_attn(h, moe_in, B, S, lw, tabs):
    T, D = h.shape
    tt = min(ROW_TILE, S)
    ns = S // tt
    has_moe_in = moe_in is not None
    row = lambda b, s: (b * ns + s, 0)
    const = lambda b, s: (0, 0)
    cos_t, sin_t = tabs

    def rows(width):
        return pl.BlockSpec((tt, width), row)

    def full(a):
        return pl.BlockSpec(a.shape, const)

    weights = [lw["attn_norm"], lw["w_rows"], lw["w_fq_t"], lw["w_fv_t"], lw["b_f"], lw["tri_incl"],
               lw["place"], lw["q_norm"], lw["wq_main_t"], lw["wq_swap_t"], lw["kv_norm"],
               lw["wk_nope"], lw["wv_t"]]
    in_specs = [rows(D)]
    args = [h]
    if has_moe_in:
        dest3, ys = moe_in
        last = B * ns - 1
        in_specs = [pl.BlockSpec((1, 1, tt), lambda b, s: (b * ns + s, 0, 0), memory_space=pltpu.SMEM),
                    pl.BlockSpec((1, 1, tt), lambda b, s: (jnp.minimum(b * ns + s + 1, last), 0, 0),
                                 memory_space=pltpu.SMEM),
                    rows(D), pl.BlockSpec(memory_space=pl.ANY)]
        args = [dest3, dest3, h, ys]
    col = lambda b, s: (0, b * ns + s)
    in_specs += [full(w) for w in weights]
    in_specs += [pl.BlockSpec((ROPE_HALF, tt), col), pl.BlockSpec((ROPE_HALF, tt), col)]
    args += weights + [cos_t, sin_t]

    bf16 = jnp.bfloat16
    qt_shape = jax.ShapeDtypeStruct((B, FOX_HEADS, ns, HEAD_PAD, tt), bf16)
    qt_spec = pl.BlockSpec((1, FOX_HEADS, 1, HEAD_PAD, tt), lambda b, s: (b, 0, s, 0, 0))
    k_shape = jax.ShapeDtypeStruct((T, FOX_HEADS * HEAD_PAD), bf16)
    k_spec = rows(FOX_HEADS * HEAD_PAD)
    vt_shape = jax.ShapeDtypeStruct((B, FOX_HEADS, ns, V_ROWS, tt), bf16)
    vt_spec = pl.BlockSpec((1, FOX_HEADS, 1, V_ROWS, tt), lambda b, s: (b, 0, s, 0, 0))
    out_shape = [qt_shape, k_shape, vt_shape] * 2
    out_specs = [qt_spec, k_spec, vt_spec] * 2
    scratch = [pltpu.VMEM((1, LANES), jnp.float32)]
    if has_moe_in:
        out_shape = [jax.ShapeDtypeStruct((T, D), jnp.float32)] + out_shape
        out_specs = [rows(D)] + out_specs
        scratch += [pltpu.VMEM((2, tt * (D // LANES), LANES), jnp.float32), pltpu.SemaphoreType.DMA((2,))]

    outs = pl.pallas_call(
        functools.partial(_pre_attn_kernel, has_moe_in=has_moe_in, tt=tt),
        grid=(B, ns),
        in_specs=in_specs,
        out_specs=out_specs,
        out_shape=out_shape,
        scratch_shapes=scratch,
        compiler_params=_cparams("arbitrary", "arbitrary"),
        name="pre_attn",
    )(*args)
    if has_moe_in:
        return outs[0], outs[1:]
    return h, outs


def _attn_kernel(qt_ref, k_ref, vt_ref, g_ref, o_ref, s_scr, mc_scr, m_scr, acc_scr, *, ta, nq):
    vd = MLA_V_DIM

    def produce(q, kt, slot, masked):
        k0 = pl.multiple_of(kt * ta, ta)
        for hh in range(2):
            kk = k_ref[pl.ds(k0, ta), hh * HEAD_PAD:(hh + 1) * HEAD_PAD]
            qt = qt_ref[0, hh, q]
            if not masked:
                s = _dot(kk, qt)
                s_scr[slot, hh] = s
                mc_scr[slot, hh] = jnp.max(s, axis=0, keepdims=True)
                continue
            hk = ta // 2
            top = _dot(kk[0:hk], qt)
            low = _dot(kk[hk:], qt[:, hk:])
            top = jnp.where(lax.broadcasted_iota(jnp.int32, top.shape, 0)
                            <= lax.broadcasted_iota(jnp.int32, top.shape, 1), top, NEG_BIG)
            low = jnp.where(lax.broadcasted_iota(jnp.int32, low.shape, 0)
                            <= lax.broadcasted_iota(jnp.int32, low.shape, 1), low, NEG_BIG)
            s_scr[slot, hh, 0:hk, :] = top
            s_scr[slot, hh, hk:, 0:hk] = jnp.full((hk, hk), NEG_BIG, jnp.float32)
            s_scr[slot, hh, hk:, hk:] = low
            top_max = jnp.max(top, axis=0, keepdims=True)
            mc_scr[slot, hh, :, 0:hk] = top_max[:, 0:hk]
            mc_scr[slot, hh, :, hk:] = jnp.maximum(top_max[:, hk:], jnp.max(low, axis=0, keepdims=True))

    def consume(kt, slot):
        for hh in range(2):
            m_prev = m_scr[hh]
            m_new = jnp.maximum(m_prev, mc_scr[slot, hh])
            alpha = jnp.exp2(m_prev - m_new)
            p = jnp.exp2(s_scr[slot, hh] - m_new).astype(jnp.bfloat16)
            acc_scr[hh] = alpha * acc_scr[hh] + _dot(vt_ref[0, hh, kt], p)
            m_scr[hh] = m_new

    def reset():
        m_scr[...] = jnp.full_like(m_scr, NEG_BIG)
        acc_scr[...] = jnp.zeros_like(acc_scr)

    def finish(q):
        outs = []
        for hh in range(2):
            acc = acc_scr[hh]
            oh = acc[0:vd] / acc[vd:vd + 1]
            ms = jnp.mean(oh * oh, axis=0, keepdims=True)
            outs.append(oh * lax.rsqrt(ms + NORM_EPS))
        o_t = jnp.concatenate(outs, axis=0) * g_ref[...]
        o_ref[pl.ds(pl.multiple_of(q * ta, ta), ta), :] = o_t.T.astype(o_ref.dtype)

    reset()
    produce(0, 0, 0, True)

    def q_tile(n, carry):
        first = (n * (n + 1) // 2) % 2

        def step(j, slot):
            produce(n, j, 1 - slot, False)
            consume(jnp.where(j == 0, n, j - 1), slot)

        for par in range(2):
            @pl.when(first == par)
            def _():
                def two_steps(u, c):
                    step(2 * u, par)
                    step(2 * u + 1, 1 - par)
                    return c

                lax.fori_loop(0, n // 2, two_steps, 0)

                @pl.when(n % 2 == 1)
                def _():
                    step(n - 1, par)

        last_slot = (first + n) % 2
        last_kt = jnp.maximum(n - 1, 0)
        for par in range(2):
            @pl.when((last_slot == par) & (n < nq - 1))
            def _():
                produce(n + 1, n + 1, 1 - par, True)
                consume(last_kt, par)
                finish(n)
                reset()

        @pl.when(n == nq - 1)
        def _():
            consume(last_kt, ((nq - 1) * nq // 2 + nq - 1) % 2)
            finish(n)

        return carry

    lax.fori_loop(0, nq, q_tile, 0)


def _attention(qt, k, vt, gain_rep, B, S, name):
    T = k.shape[0]
    ta = min(ATTN_TILE, S)
    nq = S // ta
    n_pairs = qt.shape[1] // 2
    return pl.pallas_call(
        functools.partial(_attn_kernel, ta=ta, nq=nq),
        grid=(B, n_pairs),
        in_specs=[pl.BlockSpec((1, 2, nq, HEAD_PAD, ta), lambda b, j: (b, j, 0, 0, 0)),
                  pl.BlockSpec((S, 2 * HEAD_PAD), lambda b, j: (b, j)),
                  pl.BlockSpec((1, 2, nq, V_ROWS, ta), lambda b, j: (b, j, 0, 0, 0)),
                  pl.BlockSpec((LANES, ta), lambda b, j: (j, 0))],
        out_specs=pl.BlockSpec((S, LANES), lambda b, j: (b, j)),
        out_shape=jax.ShapeDtypeStruct((T, n_pairs * LANES), jnp.bfloat16),
        scratch_shapes=[pltpu.VMEM((2, 2, ta, ta), jnp.float32),
                        pltpu.VMEM((2, 2, 1, ta), jnp.float32),
                        pltpu.VMEM((2, 1, ta), jnp.float32),
                        pltpu.VMEM((2, V_ROWS, ta), jnp.float32)],
        compiler_params=_cparams("parallel", "parallel"),
        name=name,
    )(qt, k, vt, gain_rep)


def _post_attn_kernel(fox_ref, mla_ref, h_ref, wo_ref, fnorm_ref, wrh_ref, wrl_ref, tri_ref,
                      h2_ref, xr_ref, route_ref, cnt_ref, carry_ref, xn_scr):
    i = pl.program_id(0)
    D = h_ref.shape[1]

    @pl.when(i == 0)
    def _():
        xn_scr[...] = jnp.zeros_like(xn_scr)

    @pl.when(i <= 1)
    def _():
        carry_ref[...] = jnp.zeros_like(carry_ref)

    xn = xn_scr[...]
    xr_ref[:, 0:D] = xn
    h2 = h_ref[...] + _dot(fox_ref[...], wo_ref[0:FOX_WIDTH, :]) + _dot(mla_ref[...], wo_ref[FOX_WIDTH:, :])
    h2_ref[...] = h2
    xn_scr[...] = _rms(h2, fnorm_ref[...])

    x_hi = xn.astype(jnp.bfloat16)
    x_lo = (xn - x_hi.astype(jnp.float32)).astype(jnp.bfloat16)
    logits = _dot(x_hi, wrh_ref[...]) + _dot(x_lo, wrh_ref[...]) + _dot(x_hi, wrl_ref[...])

    lt = logits.T
    tt = lt.shape[1]
    ng = N_GROUPS
    gl = lt[0:ng]
    sub = lax.broadcasted_iota(jnp.int32, (ng, tt), 0).astype(jnp.float32)

    gmax = jnp.max(gl, axis=0, keepdims=True)
    group_w = 1.0 / jnp.sum(jnp.exp(gl - gmax), axis=0, keepdims=True)
    gid = jnp.min(jnp.where(gl == gmax, sub, float(ng)), axis=0, keepdims=True)
    onehot = sub == gid

    ins = [jnp.sum(jnp.where(onehot, lt[ng * (j + 1):ng * (j + 2)], 0.0), axis=0, keepdims=True)
           for j in range(EXPERTS_PER_GROUP)]

    def first_argmax(vals):
        top = functools.reduce(jnp.maximum, vals)
        idx = jnp.full_like(top, float(len(vals) - 1))
        for j in range(len(vals) - 2, -1, -1):
            idx = jnp.where(vals[j] == top, float(j), idx)
        return top, idx

    e1, i1 = first_argmax(ins)
    e2, i2 = first_argmax([jnp.where(i1 == j, NEG_BIG, v) for j, v in enumerate(ins)])
    t = jnp.exp(e2 - e1)
    g1 = group_w / (1.0 + t)
    g2 = group_w * t / (1.0 + t)
    gates = [jnp.where(i1 == j, g1, jnp.where(i2 == j, g2, 0.0)) for j in range(EXPERTS_PER_GROUP)]

    ones = jnp.where(onehot, 1.0, 0.0)
    prefix = _dot(ones.astype(jnp.bfloat16), tri_ref[...]) + carry_ref[:, 0:1]
    rank = jnp.sum(jnp.where(onehot, prefix, 0.0), axis=0, keepdims=True)
    carry = carry_ref[...] + jnp.sum(ones, axis=1, keepdims=True)
    carry_ref[...] = carry
    cnt_ref[...] = carry

    pad_rows = jnp.zeros((ROUTE_ROWS - EXPERTS_PER_GROUP - 2, tt), jnp.float32)
    info_t = jnp.concatenate(gates + [gid, rank, pad_rows], axis=0)
    route_ref[...] = info_t
    info_pad = jnp.zeros((LANES - ROUTE_ROWS, tt), jnp.float32)
    xr_ref[:, D:D + LANES] = jnp.concatenate([info_t, info_pad], axis=0).T


def _post_attn(fox, mla, h, lw):
    T, D = h.shape
    tt = min(ROW_TILE, T)
    n = T // tt
    proj = lambda i: (jnp.minimum(i, n - 1), 0)
    routed = lambda i: (jnp.maximum(i - 1, 0), 0)
    const = lambda i: (0, 0)
    weights = [lw["w_o"], lw["ffn_norm"], lw["wr_hi"], lw["wr_lo"], lw["tri_before"]]
    return pl.pallas_call(
        _post_attn_kernel,
        grid=(n + 1,),
        in_specs=[pl.BlockSpec((tt, FOX_WIDTH), proj), pl.BlockSpec((tt, MLA_WIDTH), proj),
                  pl.BlockSpec((tt, D), proj)] + [pl.BlockSpec(w.shape, const) for w in weights],
        out_specs=[pl.BlockSpec((tt, D), proj), pl.BlockSpec((tt, D + LANES), routed),
                   pl.BlockSpec((ROUTE_ROWS, tt), lambda i: (0, jnp.maximum(i - 1, 0))),
                   pl.BlockSpec((N_GROUPS, LANES), const)],
        out_shape=[jax.ShapeDtypeStruct((T, D), jnp.float32),
                   jax.ShapeDtypeStruct((T, D + LANES), jnp.float32),
                   jax.ShapeDtypeStruct((ROUTE_ROWS, T), jnp.float32),
                   jax.ShapeDtypeStruct((N_GROUPS, LANES), jnp.float32)],
        scratch_shapes=[pltpu.VMEM((N_GROUPS, LANES), jnp.float32), pltpu.VMEM((tt, D), jnp.float32)],
        compiler_params=_cparams("arbitrary"),
        name="post_attn",
    )(fox, mla, h, *weights)


def _dispatch_kernel(fill_ref, dest_ref, xr_ref, xs_hbm, zero_ref, sem, fill_sem, *, tt, n_sorted):
    first = pl.program_id(0) == 0

    def fill_copies(act):
        for g in range(N_GROUPS):
            start, length = fill_ref[g], fill_ref[N_GROUPS + g]
            head = (-start) & (SUBLANES - 1)
            for r in range(SUBLANES - 1):
                @pl.when(r < head)
                def _(start=start, r=r):
                    act(pltpu.make_async_copy(zero_ref.at[pl.ds(0, 1)], xs_hbm.at[pl.ds(start + r, 1)], fill_sem))
            base, rest = start + head, length - head
            size = MOE_BLOCK // 2
            while size >= SUBLANES:
                @pl.when((rest & size) != 0)
                def _(base=base, rest=rest, size=size):
                    at = pl.multiple_of(base + (rest & ~(2 * size - 1)), SUBLANES)
                    act(pltpu.make_async_copy(zero_ref.at[pl.ds(0, size)], xs_hbm.at[pl.ds(at, size)], fill_sem))
                size //= 2
        used = fill_ref[2 * N_GROUPS]
        for blk in range(N_GROUPS):
            @pl.when(used + (blk + 1) * MOE_BLOCK <= n_sorted)
            def _(blk=blk):
                at = pl.multiple_of(used + blk * MOE_BLOCK, MOE_BLOCK)
                act(pltpu.make_async_copy(zero_ref, xs_hbm.at[pl.ds(at, MOE_BLOCK)], fill_sem))

    @pl.when(first)
    def _():
        zero_ref[...] = jnp.zeros_like(zero_ref)
        fill_copies(lambda cp: cp.start())

    def issue(r, carry):
        pltpu.make_async_copy(xr_ref.at[pl.ds(r, 1)], xs_hbm.at[pl.ds(dest_ref[0, 0, r], 1)], sem).start()
        return carry
    lax.fori_loop(0, tt, issue, 0, unroll=ROW_DMA_UNROLL)
    pltpu.make_async_copy(xr_ref, xs_hbm.at[pl.ds(0, tt)], sem).wait()

    @pl.when(first)
    def _():
        fill_copies(lambda cp: cp.wait())


def _dispatch(xr, dest3, fill, n_sorted):
    T, W = xr.shape
    tt = dest3.shape[2]
    grid_spec = pltpu.PrefetchScalarGridSpec(
        num_scalar_prefetch=1,
        grid=(T // tt,),
        in_specs=[pl.BlockSpec((1, 1, tt), lambda i, f: (i, 0, 0), memory_space=pltpu.SMEM),
                  pl.BlockSpec((tt, W), lambda i, f: (i, 0))],
        out_specs=pl.BlockSpec(memory_space=pl.ANY),
        scratch_shapes=[pltpu.VMEM((MOE_BLOCK, W), xr.dtype),
                        pltpu.SemaphoreType.DMA(()), pltpu.SemaphoreType.DMA(())],
    )
    return pl.pallas_call(
        functools.partial(_dispatch_kernel, tt=tt, n_sorted=n_sorted),
        grid_spec=grid_spec,
        out_shape=jax.ShapeDtypeStruct((n_sorted, W), xr.dtype),
        compiler_params=_cparams("arbitrary"),
        name="moe_dispatch",
    )(fill, dest3, xr)


def _moe_kernel(bg_ref, nv_ref, xs_ref, wg32_ref, wu32_ref, wd32_ref, ys_ref, wg_ref, wu_ref, wd_ref):
    D = xs_ref.shape[1] - LANES
    rows = xs_ref.shape[0]
    chunks = D // LANES
    b = pl.program_id(0)
    g = bg_ref[b]
    n_valid = nv_ref[b]

    @pl.when((b == 0) | (g != bg_ref[jnp.maximum(b - 1, 0)]))
    def _():
        wg_ref[...] = wg32_ref[...].astype(jnp.bfloat16)
        wu_ref[...] = wu32_ref[...].astype(jnp.bfloat16)
        wd_ref[...] = wd32_ref[...].astype(jnp.bfloat16)

    @pl.when(n_valid == 0)
    def _():
        ys_ref[...] = jnp.zeros_like(ys_ref)

    @pl.when(n_valid > 0)
    def _():
        x = xs_ref[:, 0:D].astype(jnp.bfloat16)
        info = xs_ref[:, D:D + LANES]
        y = jnp.zeros((rows, D), jnp.float32)
        for j in range(EXPERTS_PER_GROUP):
            a = _dot(x, wg_ref[j])
            u = _dot(x, wu_ref[j])
            hj = (a * jax.nn.sigmoid(a) * u).astype(jnp.bfloat16)
            y = y + info[:, j:j + 1] * _dot(hj, wd_ref[j])
        for c in range(chunks):
            ys_ref[pl.ds(c, rows, stride=chunks), :] = y[:, c * LANES:(c + 1) * LANES]


def _moe(xs, block_group, block_valid, layer, w_gate, w_up, w_down):
    n_sorted, W = xs.shape
    D = W - LANES
    nb = n_sorted // MOE_BLOCK
    epg = EXPERTS_PER_GROUP
    chunks = D // LANES
    grid_spec = pltpu.PrefetchScalarGridSpec(
        num_scalar_prefetch=2,
        grid=(nb,),
        in_specs=[pl.BlockSpec((MOE_BLOCK, W), lambda b, bg, nv: (b, 0)),
                  pl.BlockSpec((None, epg, D, D_EXPERT), lambda b, bg, nv: (layer, bg[b], 0, 0)),
                  pl.BlockSpec((None, epg, D, D_EXPERT), lambda b, bg, nv: (layer, bg[b], 0, 0)),
                  pl.BlockSpec((None, epg, D_EXPERT, D), lambda b, bg, nv: (layer, bg[b], 0, 0))],
        out_specs=pl.BlockSpec((MOE_BLOCK * chunks, LANES), lambda b, bg, nv: (b, 0)),
        scratch_shapes=[pltpu.VMEM((epg, D, D_EXPERT), jnp.bfloat16),
                        pltpu.VMEM((epg, D, D_EXPERT), jnp.bfloat16),
                        pltpu.VMEM((epg, D_EXPERT, D), jnp.bfloat16)],
    )
    return pl.pallas_call(
        _moe_kernel,
        grid_spec=grid_spec,
        out_shape=jax.ShapeDtypeStruct((n_sorted * chunks, LANES), jnp.float32),
        compiler_params=_cparams("arbitrary"),
        name="moe_experts",
    )(block_group, block_valid, xs, w_gate, w_up, w_down)


def _final_kernel(dest_ref, dest_next_ref, h_ref, ys_hbm, gain_ref, o_ref, ybuf_ref, sems, *, tt):
    y = _gather_rows(ys_hbm, dest_ref, dest_next_ref, ybuf_ref, sems,
                     pl.program_id(0), pl.num_programs(0), tt)
    o_ref[...] = _rms(h_ref[...] + y, gain_ref[...])


def _final(h, dest3, ys, gain):
    T, D = h.shape
    tt = dest3.shape[2]
    last = T // tt - 1
    return pl.pallas_call(
        functools.partial(_final_kernel, tt=tt),
        grid=(T // tt,),
        in_specs=[pl.BlockSpec((1, 1, tt), lambda i: (i, 0, 0), memory_space=pltpu.SMEM),
                  pl.BlockSpec((1, 1, tt), lambda i: (jnp.minimum(i + 1, last), 0, 0), memory_space=pltpu.SMEM),
                  pl.BlockSpec((tt, D), lambda i: (i, 0)),
                  pl.BlockSpec(memory_space=pl.ANY),
                  pl.BlockSpec((1, D), lambda i: (0, 0))],
        out_specs=pl.BlockSpec((tt, D), lambda i: (i, 0)),
        out_shape=jax.ShapeDtypeStruct((T, D), jnp.float32),
        scratch_shapes=[pltpu.VMEM((2, tt * (D // LANES), LANES), jnp.float32), pltpu.SemaphoreType.DMA((2,))],
        compiler_params=_cparams("arbitrary"),
        name="final_norm",
    )(dest3, dest3, h, ys, gain)


def _prep_layer(l, tt, w_in, b_f, w_uq, w_ukv, w_o, w_group, w_router,
                attn_norm, q_norm, kv_norm, ffn_norm):
    bf16 = jnp.bfloat16
    D = w_in.shape[1]
    fw = FOX_WIDTH
    o_f = 3 * fw
    o_cq = o_f + FOX_HEADS
    o_ckv = o_cq + Q_LORA_RANK
    o_kr = o_ckv + KV_LORA_RANK
    wi = w_in[l]
    w_f = wi[:, o_f:o_cq]
    w_f3 = jnp.concatenate([w_f] * DECAY_PARTS + [jnp.zeros((D, LANES - DECAY_PARTS * FOX_HEADS), wi.dtype)], axis=1)
    b_f3 = jnp.concatenate([b_f[l]] * DECAY_PARTS + [jnp.zeros((LANES - DECAY_PARTS * FOX_HEADS,), b_f.dtype)])
    kr = wi[:, o_kr:o_kr + MLA_ROPE_DIM]
    kr_swap = jnp.concatenate([-kr[:, ROPE_HALF:], kr[:, :ROPE_HALF]], axis=1)
    zpad = jnp.zeros((D, MLA_ROPE_DIM), wi.dtype)
    w_rows = jnp.concatenate([wi[:, fw:2 * fw], w_f3, wi[:, o_cq:o_kr], kr_swap, zpad, kr, zpad], axis=1)

    src = jnp.arange(LANES)
    dst = (src % FOX_HEADS) * HEAD_PAD + FOX_HEAD_DIM + src // FOX_HEADS
    place = ((jnp.arange(FOX_HEADS * HEAD_PAD)[None, :] == dst[:, None])
             & (src[:, None] < DECAY_PARTS * FOX_HEADS)).astype(bf16)

    qd = MLA_NOPE_DIM + MLA_ROPE_DIM
    wq = w_uq[l].reshape(Q_LORA_RANK, MLA_HEADS, qd)
    zq = jnp.zeros((Q_LORA_RANK, MLA_HEADS, HEAD_PAD - qd), wq.dtype)
    wq_main = jnp.concatenate([wq, zq], axis=2).reshape(Q_LORA_RANK, MLA_HEADS * HEAD_PAD)
    x1 = wq[:, :, MLA_NOPE_DIM:MLA_NOPE_DIM + ROPE_HALF]
    x2 = wq[:, :, MLA_NOPE_DIM + ROPE_HALF:]
    wq_swap = jnp.concatenate([-x2, x1], axis=2).reshape(Q_LORA_RANK, MLA_HEADS * MLA_ROPE_DIM)

    wkv = w_ukv[l].reshape(KV_LORA_RANK, MLA_HEADS, MLA_NOPE_DIM + MLA_V_DIM)
    wk_nope = jnp.concatenate([wkv[:, :, :MLA_NOPE_DIM],
                               jnp.zeros((KV_LORA_RANK, MLA_HEADS, HEAD_PAD - MLA_NOPE_DIM), wkv.dtype)], axis=2)
    wv = wkv[:, :, MLA_NOPE_DIM:].reshape(KV_LORA_RANK, MLA_WIDTH)

    w_exp = w_router[l].reshape(D, N_GROUPS, EXPERTS_PER_GROUP).transpose(0, 2, 1).reshape(D, N_EXPERTS)
    wr = jnp.concatenate([w_group[l], w_exp,
                          jnp.zeros((D, LANES - N_GROUPS - N_EXPERTS), w_group.dtype)], axis=1)
    wr_hi = wr.astype(bf16)
    wr_lo = (wr - wr_hi.astype(jnp.float32)).astype(bf16)

    ar = jnp.arange(tt)
    return dict(
        attn_norm=attn_norm[l].reshape(1, D),
        w_rows=w_rows.astype(bf16),
        w_fq_t=wi[:, 0:fw].T.astype(bf16),
        w_fv_t=wi[:, 2 * fw:3 * fw].T.astype(bf16),
        b_f=b_f3.reshape(1, LANES),
        tri_incl=(ar[:, None] >= ar[None, :]).astype(bf16),
        tri_before=(ar[:, None] < ar[None, :]).astype(bf16),
        place=place,
        q_norm=q_norm[l].reshape(1, Q_LORA_RANK),
        wq_main_t=wq_main.T.astype(bf16),
        wq_swap_t=wq_swap.T.astype(bf16),
        kv_norm=kv_norm[l].reshape(1, KV_LORA_RANK),
        wk_nope=wk_nope.reshape(KV_LORA_RANK, MLA_HEADS * HEAD_PAD).astype(bf16),
        wv_t=wv.T.astype(bf16),
        w_o=w_o[l].astype(bf16),
        ffn_norm=ffn_norm[l].reshape(1, D),
        wr_hi=wr_hi,
        wr_lo=wr_lo,
    )


def _routing_plan(route, counts, n_blocks):
    gid = route[GID_INDEX].astype(jnp.int32)
    rank = route[RANK_INDEX].astype(jnp.int32)
    cnt = counts[:, 0].astype(jnp.int32)
    padded = (cnt + MOE_BLOCK - 1) // MOE_BLOCK * MOE_BLOCK
    pend = jnp.cumsum(padded)
    pstart = pend - padded
    dest = pstart[gid] + rank
    block_start = jnp.arange(n_blocks, dtype=jnp.int32) * MOE_BLOCK
    block_group = jnp.sum((pend[None, :] <= block_start[:, None]).astype(jnp.int32), axis=1)
    block_group = jnp.minimum(block_group, N_GROUPS - 1)
    block_valid = jnp.clip((pstart + cnt)[block_group] - block_start, 0, MOE_BLOCK)
    fill = jnp.concatenate([pstart + cnt, padded - cnt, pend[-1:]])
    return dest, block_group, block_valid, fill


def kernel(x, positions, attn_norm, w_in, b_f, q_norm, w_uq, kv_norm, w_ukv, fox_out_norm, mla_out_norm, w_o,
           ffn_norm, w_group, w_router, w_gate, w_up, w_down, final_norm):
    B, S, D = x.shape
    T = B * S
    depth = w_in.shape[0]
    ta = min(ATTN_TILE, S)
    tt = min(ROW_TILE, S)
    n_blocks = -(-(T + N_GROUPS * (MOE_BLOCK - 1)) // MOE_BLOCK)

    cos_t, sin_t = _rope_tables(positions)
    tabs = (cos_t, sin_t)

    h = x.reshape(T, D)
    moe_in = None
    for l in range(depth):
        lw = _prep_layer(l, tt, w_in, b_f, w_uq, w_ukv, w_o, w_group, w_router,
                         attn_norm, q_norm, kv_norm, ffn_norm)
        h, (fq_t, fk, fv_t, mq_t, mk, mv_t) = _pre_attn(h, moe_in, B, S, lw, tabs)
        fox_gain = jnp.broadcast_to(fox_out_norm[l].reshape(FOX_WIDTH, 1), (FOX_WIDTH, ta))
        mla_gain = jnp.broadcast_to(mla_out_norm[l].reshape(MLA_WIDTH, 1), (MLA_WIDTH, ta))
        fox = _attention(fq_t, fk, fv_t, fox_gain, B, S, "attn_fox")
        mla = _attention(mq_t, mk, mv_t, mla_gain, B, S, "attn_mla")
        h, xr, route, counts = _post_attn(fox, mla, h, lw)
        dest, block_group, block_valid, fill = _routing_plan(route, counts, n_blocks)
        dest3 = dest.reshape(T // tt, 1, tt)
        dt = min(DISPATCH_TILE, T)
        xs = _dispatch(xr, dest.reshape(T // dt, 1, dt), fill, n_blocks * MOE_BLOCK)
        ys = _moe(xs, block_group, block_valid, l, w_gate, w_up, w_down)
        moe_in = (dest3, ys)
    out = _final(h, moe_in[0], moe_in[1], final_norm.reshape(1, D))
    return out.reshape(B, S, D)
```

```python
import functools
import math

import jax
import jax.numpy as jnp
from jax import lax
from jax.experimental import pallas as pl
from jax.experimental.pallas import tpu as pltpu

FOX_HEADS = 8
FOX_HEAD_DIM = 64
FOX_WIDTH = FOX_HEADS * FOX_HEAD_DIM
MLA_HEADS = 8
MLA_NOPE_DIM = 64
MLA_ROPE_DIM = 32
MLA_V_DIM = 64
MLA_WIDTH = MLA_HEADS * MLA_V_DIM
Q_LORA_RANK = 256
KV_LORA_RANK = 128
ROPE_THETA = 10000.0
N_GROUPS = 8
EXPERTS_PER_GROUP = 4
N_EXPERTS = N_GROUPS * EXPERTS_PER_GROUP
D_EXPERT = 256
NORM_EPS = 1e-6

LANES = 128
SUBLANES = 8
BF16_SUBLANES = 16
VMEM_LIMIT_BYTES = 56 * 1024 * 1024

ROW_TILE = 512
DISPATCH_TILE = 1024
ATTN_TILE = 512
MOE_BLOCK = 512
ROW_DMA_UNROLL = 8
HEAD_PAD = 128
V_ROWS = MLA_V_DIM + BF16_SUBLANES
ROPE_HALF = MLA_ROPE_DIM // 2
NEG_BIG = -1e30
LOG2E = math.log2(math.e)
DECAY_PARTS = 3

GID_INDEX = EXPERTS_PER_GROUP
RANK_INDEX = EXPERTS_PER_GROUP + 1
ROUTE_ROWS = SUBLANES


def _cparams(*semantics):
    return pltpu.CompilerParams(dimension_semantics=semantics, vmem_limit_bytes=VMEM_LIMIT_BYTES)


def _rms(x, gain):
    ms = jnp.mean(x * x, axis=-1, keepdims=True)
    return x * lax.rsqrt(ms + NORM_EPS) * gain


def _dot(a, b):
    return jnp.dot(a, b, preferred_element_type=jnp.float32)


def _dot_nt(a, b):
    return lax.dot_general(a, b, (((1,), (1,)), ((), ())), preferred_element_type=jnp.float32)


def _split_bf16(x, parts):
    out = []
    for _ in range(parts - 1):
        piece = x.astype(jnp.bfloat16).astype(jnp.float32)
        out.append(piece)
        x = x - piece
    out.append(x.astype(jnp.bfloat16).astype(jnp.float32))
    return out


def _rope_table_kernel(pos_ref, invf_ref, cos_ref, sin_ref):
    ang = pos_ref[...].astype(jnp.float32) * invf_ref[...]
    cos_ref[...] = jnp.cos(ang)
    sin_ref[...] = jnp.sin(ang)


def _rope_tables(positions):
    T = positions.size
    inv_freq = ROPE_THETA ** (-jnp.arange(ROPE_HALF, dtype=jnp.float32) / ROPE_HALF)
    ct = min(T, 4096)
    return pl.pallas_call(
        _rope_table_kernel,
        grid=(T // ct,),
        in_specs=[pl.BlockSpec((1, ct), lambda i: (0, i)),
                  pl.BlockSpec((ROPE_HALF, 1), lambda i: (0, 0))],
        out_specs=[pl.BlockSpec((ROPE_HALF, ct), lambda i: (0, i))] * 2,
        out_shape=[jax.ShapeDtypeStruct((ROPE_HALF, T), jnp.float32)] * 2,
        compiler_params=_cparams("parallel"),
        name="rope_tables",
    )(positions.reshape(1, T), inv_freq.reshape(ROPE_HALF, 1))


def _gather_rows(src_hbm, idx_ref, idx_next_ref, buf_ref, sems, step, n_steps, n_rows):
    slot = step % 2
    chunks = buf_ref.shape[1] // n_rows

    def request(ref, to_slot):
        def issue(r, carry):
            src = pl.multiple_of(ref[0, 0, r] * chunks, chunks)
            dst = pl.multiple_of(r * chunks, chunks)
            pltpu.make_async_copy(src_hbm.at[pl.ds(src, chunks)], buf_ref.at[to_slot, pl.ds(dst, chunks)],
                                  sems.at[to_slot]).start()
            return carry
        lax.fori_loop(0, n_rows, issue, 0, unroll=ROW_DMA_UNROLL)

    @pl.when(step == 0)
    def _():
        request(idx_ref, slot)

    @pl.when(step + 1 < n_steps)
    def _():
        request(idx_next_ref, 1 - slot)

    pltpu.make_async_copy(src_hbm.at[pl.ds(0, n_rows * chunks)], buf_ref.at[slot], sems.at[slot]).wait()
    return jnp.concatenate([buf_ref[slot, pl.ds(c, n_rows, stride=chunks), :] for c in range(chunks)], axis=1)


def _pre_attn_kernel(*refs, has_moe_in, tt):
    if has_moe_in:
        dest_ref, dest_next_ref, h_ref, ys_hbm = refs[:4]
        refs = refs[4:]
    else:
        h_ref = refs[0]
        refs = refs[1:]
    (anorm_ref, wrow_ref, wfq_ref, wfv_ref, bf_ref, tri_ref, place_ref, qnorm_ref, wqm_ref, wqs_ref,
     kvnorm_ref, wkn_ref, wv_ref, cos_ref, sin_ref) = refs[:15]
    refs = refs[15:]
    if has_moe_in:
        hout_ref = refs[0]
        refs = refs[1:]
    fq_ref, fk_ref, fv_ref, qm_ref, km_ref, vm_ref = refs[:6]
    refs = refs[6:]
    carry_ref = refs[0]
    if has_moe_in:
        ybuf_ref, sems = refs[1:3]

    si = pl.program_id(1)
    bf16 = jnp.bfloat16
    half = LANES // 2

    @pl.when(si == 0)
    def _():
        carry_ref[...] = jnp.zeros_like(carry_ref)

    h = h_ref[...]
    if has_moe_in:
        step = pl.program_id(0) * pl.num_programs(1) + si
        n_steps = pl.num_programs(0) * pl.num_programs(1)
        h = h + _gather_rows(ys_hbm, dest_ref, dest_next_ref, ybuf_ref, sems, step, n_steps, tt)
        hout_ref[...] = h

    xn = _rms(h, anorm_ref[...]).astype(bf16)
    lane = lax.broadcasted_iota(jnp.int32, (tt, LANES), 1)
    sub = lax.broadcasted_iota(jnp.int32, (BF16_SUBLANES, tt), 0)
    ones_rows = jnp.ones((BF16_SUBLANES, tt), bf16)

    fw = FOX_WIDTH
    o = fw
    z = _dot(xn, wrow_ref[:, o:o + LANES]) + bf_ref[...]
    o += LANES
    cq_raw = _dot(xn, wrow_ref[:, o:o + Q_LORA_RANK])
    o += Q_LORA_RANK
    ckv_raw = _dot(xn, wrow_ref[:, o:o + KV_LORA_RANK])
    o += KV_LORA_RANK
    kr_raw = _dot(xn, wrow_ref[:, o:o + LANES])

    fq_t = _dot_nt(wfq_ref[...], xn) * (FOX_HEAD_DIM ** -0.5 * LOG2E)
    q_aug = jnp.where(sub < DECAY_PARTS, -1.0, 0.0).astype(bf16)
    q_zero = jnp.zeros((HEAD_PAD - FOX_HEAD_DIM - BF16_SUBLANES, tt), bf16)
    for hd in range(FOX_HEADS):
        fq_ref[0, hd, 0, 0:FOX_HEAD_DIM, :] = fq_t[hd * FOX_HEAD_DIM:(hd + 1) * FOX_HEAD_DIM].astype(bf16)
        fq_ref[0, hd, 0, FOX_HEAD_DIM:FOX_HEAD_DIM + BF16_SUBLANES, :] = q_aug
        fq_ref[0, hd, 0, FOX_HEAD_DIM + BF16_SUBLANES:, :] = q_zero
    fv_t = _dot_nt(wfv_ref[...], xn)
    vd = MLA_V_DIM
    for hd in range(FOX_HEADS):
        fv_ref[0, hd, 0, 0:vd, :] = fv_t[hd * vd:(hd + 1) * vd].astype(bf16)
        fv_ref[0, hd, 0, vd:, :] = ones_rows

    logf = jnp.minimum(z, 0.0) - jnp.log1p(jnp.exp(-jnp.abs(z)))
    pieces = jnp.concatenate([p_.astype(bf16) for p_ in _split_bf16(logf, DECAY_PARTS)], axis=1)
    csum = _dot(tri_ref[...], pieces)

    dcum = csum[:, 0:LANES] + csum[:, LANES:2 * LANES] + csum[:, 2 * LANES:] + carry_ref[...]
    carry_ref[...] = dcum[tt - 1:tt, :]
    d_parts = _split_bf16(dcum * LOG2E, DECAY_PARTS)
    d_sel = jnp.where(lane < FOX_HEADS, d_parts[0],
                      jnp.where(lane < 2 * FOX_HEADS, d_parts[1],
                                jnp.where(lane < 3 * FOX_HEADS, d_parts[2], 0.0)))
    k_aug = _dot(d_sel.astype(bf16), place_ref[...])

    fk = _dot(xn, wrow_ref[:, 0:fw])
    for p in range(FOX_HEADS // 2):
        blk = fk[:, p * LANES:(p + 1) * LANES]
        for hh, b_ in ((0, blk), (1, pltpu.roll(blk, half, axis=1))):
            sl = slice((2 * p + hh) * HEAD_PAD, (2 * p + hh + 1) * HEAD_PAD)
            fk_ref[:, sl] = jnp.where(lane < FOX_HEAD_DIM, b_, k_aug[:, sl]).astype(bf16)

    cq = _rms(cq_raw, qnorm_ref[...]).astype(bf16)
    qm_t = _dot_nt(wqm_ref[...], cq)
    qs_t = _dot_nt(wqs_ref[...], cq)
    cos_t = cos_ref[...]
    sin_t = sin_ref[...]
    scale = (MLA_NOPE_DIM + MLA_ROPE_DIM) ** -0.5 * LOG2E
    q_zero = jnp.zeros((HEAD_PAD - MLA_NOPE_DIM - MLA_ROPE_DIM, tt), jnp.float32)
    for hd in range(MLA_HEADS):
        blk = qm_t[hd * HEAD_PAD:(hd + 1) * HEAD_PAD]
        rows = [blk[0:MLA_NOPE_DIM]]
        for j in range(2):
            main = blk[MLA_NOPE_DIM + j * ROPE_HALF:MLA_NOPE_DIM + (j + 1) * ROPE_HALF]
            swap = qs_t[hd * MLA_ROPE_DIM + j * ROPE_HALF:hd * MLA_ROPE_DIM + (j + 1) * ROPE_HALF]
            rows.append(main * cos_t + swap * sin_t)
        rows.append(q_zero)
        qm_ref[0, hd, 0] = (jnp.concatenate(rows, axis=0) * scale).astype(bf16)

    ckv = _rms(ckv_raw, kvnorm_ref[...]).astype(bf16)
    mv_t = _dot_nt(wv_ref[...], ckv)
    for hd in range(MLA_HEADS):
        vm_ref[0, hd, 0, 0:vd, :] = mv_t[hd * vd:(hd + 1) * vd].astype(bf16)
        vm_ref[0, hd, 0, vd:, :] = ones_rows
    k_nope = _dot(ckv, wkn_ref[...])
    z_rows = jnp.zeros((MLA_ROPE_DIM, tt), jnp.float32)
    csk = jnp.concatenate([sin_t, sin_t, z_rows, cos_t, cos_t, z_rows], axis=0).T
    kr = kr_raw * csk
    kr = kr + pltpu.roll(kr, half, axis=1)
    k_pe = jnp.where(lane >= MLA_NOPE_DIM, kr, 0.0)
    for hd in range(MLA_HEADS):
        sl = slice(hd * HEAD_PAD, (hd + 1) * HEAD_PAD)
        km_ref[:, sl] = (k_nope[:, sl] + k_pe).astype(bf16)


def _pre_attn(h, moe_in, B, S, lw, tabs):
    T, D = h.shape
    tt = min(ROW_TILE, S)
    ns = S // tt
    has_moe_in = moe_in is not None
    row = lambda b, s: (b * ns + s, 0)
    const = lambda b, s: (0, 0)
    cos_t, sin_t = tabs

    def rows(width):
        return pl.BlockSpec((tt, width), row)

    def full(a):
        return pl.BlockSpec(a.shape, const)

    weights = [lw["attn_norm"], lw["w_rows"], lw["w_fq_t"], lw["w_fv_t"], lw["b_f"], lw["tri_incl"],
               lw["place"], lw["q_norm"], lw["wq_main_t"], lw["wq_swap_t"], lw["kv_norm"],
               lw["wk_nope"], lw["wv_t"]]
    in_specs = [rows(D)]
    args = [h]
    if has_moe_in:
        dest3, ys = moe_in
        last = B * ns - 1
        in_specs = [pl.BlockSpec((1, 1, tt), lambda b, s: (b * ns + s, 0, 0), memory_space=pltpu.SMEM),
                    pl.BlockSpec((1, 1, tt), lambda b, s: (jnp.minimum(b * ns + s + 1, last), 0, 0),
                                 memory_space=pltpu.SMEM),
                    rows(D), pl.BlockSpec(memory_space=pl.ANY)]
        args = [dest3, dest3, h, ys]
    col = lambda b, s: (0, b * ns + s)
    in_specs += [full(w) for w in weights]
    in_specs += [pl.BlockSpec((ROPE_HALF, tt), col), pl.BlockSpec((ROPE_HALF, tt), col)]
    args += weights + [cos_t, sin_t]

    bf16 = jnp.bfloat16
    qt_shape = jax.ShapeDtypeStruct((B, FOX_HEADS, ns, HEAD_PAD, tt), bf16)
    qt_spec = pl.BlockSpec((1, FOX_HEADS, 1, HEAD_PAD, tt), lambda b, s: (b, 0, s, 0, 0))
    k_shape = jax.ShapeDtypeStruct((T, FOX_HEADS * HEAD_PAD), bf16)
    k_spec = rows(FOX_HEADS * HEAD_PAD)
    vt_shape = jax.ShapeDtypeStruct((B, FOX_HEADS, ns, V_ROWS, tt), bf16)
    vt_spec = pl.BlockSpec((1, FOX_HEADS, 1, V_ROWS, tt), lambda b, s: (b, 0, s, 0, 0))
    out_shape = [qt_shape, k_shape, vt_shape] * 2
    out_specs = [qt_spec, k_spec, vt_spec] * 2
    scratch = [pltpu.VMEM((1, LANES), jnp.float32)]
    if has_moe_in:
        out_shape = [jax.ShapeDtypeStruct((T, D), jnp.float32)] + out_shape
        out_specs = [rows(D)] + out_specs
        scratch += [pltpu.VMEM((2, tt * (D // LANES), LANES), jnp.float32), pltpu.SemaphoreType.DMA((2,))]

    outs = pl.pallas_call(
        functools.partial(_pre_attn_kernel, has_moe_in=has_moe_in, tt=tt),
        grid=(B, ns),
        in_specs=in_specs,
        out_specs=out_specs,
        out_shape=out_shape,
        scratch_shapes=scratch,
        compiler_params=_cparams("arbitrary", "arbitrary"),
        name="pre_attn",
    )(*args)
    if has_moe_in:
        return outs[0], outs[1:]
    return h, outs


def _attn_kernel(qt_ref, k_ref, vt_ref, g_ref, o_ref, s_scr, mc_scr, m_scr, acc_scr, *, ta, nq):
    vd = MLA_V_DIM

    def produce(q, kt, slot, masked):
        k0 = pl.multiple_of(kt * ta, ta)
        for hh in range(2):
            kk = k_ref[pl.ds(k0, ta), hh * HEAD_PAD:(hh + 1) * HEAD_PAD]
            qt = qt_ref[0, hh, q]
            if not masked:
                s = _dot(kk, qt)
                s_scr[slot, hh] = s
                mc_scr[slot, hh] = jnp.max(s, axis=0, keepdims=True)
                continue
            hk = ta // 2
            top = _dot(kk[0:hk], qt)
            low = _dot(kk[hk:], qt[:, hk:])
            top = jnp.where(lax.broadcasted_iota(jnp.int32, top.shape, 0)
                            <= lax.broadcasted_iota(jnp.int32, top.shape, 1), top, NEG_BIG)
            low = jnp.where(lax.broadcasted_iota(jnp.int32, low.shape, 0)
                            <= lax.broadcasted_iota(jnp.int32, low.shape, 1), low, NEG_BIG)
            s_scr[slot, hh, 0:hk, :] = top
            s_scr[slot, hh, hk:, 0:hk] = jnp.full((hk, hk), NEG_BIG, jnp.float32)
            s_scr[slot, hh, hk:, hk:] = low
            top_max = jnp.max(top, axis=0, keepdims=True)
            mc_scr[slot, hh, :, 0:hk] = top_max[:, 0:hk]
            mc_scr[slot, hh, :, hk:] = jnp.maximum(top_max[:, hk:], jnp.max(low, axis=0, keepdims=True))

    def consume(kt, slot):
        for hh in range(2):
            m_prev = m_scr[hh]
            m_new = jnp.maximum(m_prev, mc_scr[slot, hh])
            alpha = jnp.exp2(m_prev - m_new)
            p = jnp.exp2(s_scr[slot, hh] - m_new).astype(jnp.bfloat16)
            acc_scr[hh] = alpha * acc_scr[hh] + _dot(vt_ref[0, hh, kt], p)
            m_scr[hh] = m_new

    def reset():
        m_scr[...] = jnp.full_like(m_scr, NEG_BIG)
        acc_scr[...] = jnp.zeros_like(acc_scr)

    def finish(q):
        outs = []
        for hh in range(2):
            acc = acc_scr[hh]
            oh = acc[0:vd] / acc[vd:vd + 1]
            ms = jnp.mean(oh * oh, axis=0, keepdims=True)
            outs.append(oh * lax.rsqrt(ms + NORM_EPS))
        o_t = jnp.concatenate(outs, axis=0) * g_ref[...]
        o_ref[pl.ds(pl.multiple_of(q * ta, ta), ta), :] = o_t.T.astype(o_ref.dtype)

    reset()
    produce(0, 0, 0, True)

    def q_tile(n, carry):
        first = (n * (n + 1) // 2) % 2

        def step(j, slot):
            produce(n, j, 1 - slot, False)
            consume(jnp.where(j == 0, n, j - 1), slot)

        for par in range(2):
            @pl.when(first == par)
            def _():
                def two_steps(u, c):
                    step(2 * u, par)
                    step(2 * u + 1, 1 - par)
                    return c

                lax.fori_loop(0, n // 2, two_steps, 0)

                @pl.when((n % 2 == 1) & (n < nq - 1))
                def _():
                    step(n - 1, par)
                    produce(n + 1, n + 1, par, True)
                    consume(n - 1, 1 - par)
                    finish(n)
                    reset()

                @pl.when((n % 2 == 0) & (n < nq - 1))
                def _():
                    produce(n + 1, n + 1, 1 - par, True)
                    consume(jnp.maximum(n - 1, 0), par)
                    finish(n)
                    reset()

        n_last = nq - 1
        first_last = (n_last * (n_last + 1) // 2) % 2

        @pl.when(n == n_last)
        def _():
            if n_last % 2 == 1:
                step(n - 1, first_last)
                consume(n - 1, 1 - first_last)
            else:
                consume(jnp.maximum(n - 1, 0), first_last)
            finish(n)

        return carry

    lax.fori_loop(0, nq, q_tile, 0)


def _attention(qt, k, vt, gain_rep, B, S, name):
    T = k.shape[0]
    ta = min(ATTN_TILE, S)
    nq = S // ta
    n_pairs = qt.shape[1] // 2
    return pl.pallas_call(
        functools.partial(_attn_kernel, ta=ta, nq=nq),
        grid=(B, n_pairs),
        in_specs=[pl.BlockSpec((1, 2, nq, HEAD_PAD, ta), lambda b, j: (b, j, 0, 0, 0)),
                  pl.BlockSpec((S, 2 * HEAD_PAD), lambda b, j: (b, j)),
                  pl.BlockSpec((1, 2, nq, V_ROWS, ta), lambda b, j: (b, j, 0, 0, 0)),
                  pl.BlockSpec((LANES, ta), lambda b, j: (j, 0))],
        out_specs=pl.BlockSpec((S, LANES), lambda b, j: (b, j)),
        out_shape=jax.ShapeDtypeStruct((T, n_pairs * LANES), jnp.bfloat16),
        scratch_shapes=[pltpu.VMEM((2, 2, ta, ta), jnp.float32),
                        pltpu.VMEM((2, 2, 1, ta), jnp.float32),
                        pltpu.VMEM((2, 1, ta), jnp.float32),
                        pltpu.VMEM((2, V_ROWS, ta), jnp.float32)],
        compiler_params=_cparams("parallel", "parallel"),
        name=name,
    )(qt, k, vt, gain_rep)


def _post_attn_kernel(fox_ref, mla_ref, h_ref, wo_ref, fnorm_ref, wrh_ref, wrl_ref, tri_ref,
                      h2_ref, xr_ref, route_ref, cnt_ref, carry_ref, xn_scr):
    i = pl.program_id(0)
    D = h_ref.shape[1]

    @pl.when(i == 0)
    def _():
        xn_scr[...] = jnp.zeros_like(xn_scr)

    @pl.when(i <= 1)
    def _():
        carry_ref[...] = jnp.zeros_like(carry_ref)

    xn = xn_scr[...]
    xr_ref[:, 0:D] = xn
    h2 = h_ref[...] + _dot(fox_ref[...], wo_ref[0:FOX_WIDTH, :]) + _dot(mla_ref[...], wo_ref[FOX_WIDTH:, :])
    h2_ref[...] = h2
    xn_scr[...] = _rms(h2, fnorm_ref[...])

    x_hi = xn.astype(jnp.bfloat16)
    x_lo = (xn - x_hi.astype(jnp.float32)).astype(jnp.bfloat16)
    logits = _dot(x_hi, wrh_ref[...]) + _dot(x_lo, wrh_ref[...]) + _dot(x_hi, wrl_ref[...])

    lt = logits.T
    tt = lt.shape[1]
    ng = N_GROUPS
    gl = lt[0:ng]
    sub = lax.broadcasted_iota(jnp.int32, (ng, tt), 0).astype(jnp.float32)

    gmax = jnp.max(gl, axis=0, keepdims=True)
    group_w = 1.0 / jnp.sum(jnp.exp(gl - gmax), axis=0, keepdims=True)
    gid = jnp.min(jnp.where(gl == gmax, sub, float(ng)), axis=0, keepdims=True)
    onehot = sub == gid

    ins = [jnp.sum(jnp.where(onehot, lt[ng * (j + 1):ng * (j + 2)], 0.0), axis=0, keepdims=True)
           for j in range(EXPERTS_PER_GROUP)]

    def first_argmax(vals):
        top = functools.reduce(jnp.maximum, vals)
        idx = jnp.full_like(top, float(len(vals) - 1))
        for j in range(len(vals) - 2, -1, -1):
            idx = jnp.where(vals[j] == top, float(j), idx)
        return top, idx

    e1, i1 = first_argmax(ins)
    e2, i2 = first_argmax([jnp.where(i1 == j, NEG_BIG, v) for j, v in enumerate(ins)])
    t = jnp.exp(e2 - e1)
    g1 = group_w / (1.0 + t)
    g2 = group_w * t / (1.0 + t)
    gates = [jnp.where(i1 == j, g1, jnp.where(i2 == j, g2, 0.0)) for j in range(EXPERTS_PER_GROUP)]

    ones = jnp.where(onehot, 1.0, 0.0)
    prefix = _dot(ones.astype(jnp.bfloat16), tri_ref[...]) + carry_ref[:, 0:1]
    rank = jnp.sum(jnp.where(onehot, prefix, 0.0), axis=0, keepdims=True)
    carry = carry_ref[...] + jnp.sum(ones, axis=1, keepdims=True)
    carry_ref[...] = carry
    cnt_ref[...] = carry

    pad_rows = jnp.zeros((ROUTE_ROWS - EXPERTS_PER_GROUP - 2, tt), jnp.float32)
    info_t = jnp.concatenate(gates + [gid, rank, pad_rows], axis=0)
    route_ref[...] = info_t
    info_pad = jnp.zeros((LANES - ROUTE_ROWS, tt), jnp.float32)
    xr_ref[:, D:D + LANES] = jnp.concatenate([info_t, info_pad], axis=0).T


def _post_attn(fox, mla, h, lw):
    T, D = h.shape
    tt = min(ROW_TILE, T)
    n = T // tt
    proj = lambda i: (jnp.minimum(i, n - 1), 0)
    routed = lambda i: (jnp.maximum(i - 1, 0), 0)
    const = lambda i: (0, 0)
    weights = [lw["w_o"], lw["ffn_norm"], lw["wr_hi"], lw["wr_lo"], lw["tri_before"]]
    return pl.pallas_call(
        _post_attn_kernel,
        grid=(n + 1,),
        in_specs=[pl.BlockSpec((tt, FOX_WIDTH), proj), pl.BlockSpec((tt, MLA_WIDTH), proj),
                  pl.BlockSpec((tt, D), proj)] + [pl.BlockSpec(w.shape, const) for w in weights],
        out_specs=[pl.BlockSpec((tt, D), proj), pl.BlockSpec((tt, D + LANES), routed),
                   pl.BlockSpec((ROUTE_ROWS, tt), lambda i: (0, jnp.maximum(i - 1, 0))),
                   pl.BlockSpec((N_GROUPS, LANES), const)],
        out_shape=[jax.ShapeDtypeStruct((T, D), jnp.float32),
                   jax.ShapeDtypeStruct((T, D + LANES), jnp.float32),
                   jax.ShapeDtypeStruct((ROUTE_ROWS, T), jnp.float32),
                   jax.ShapeDtypeStruct((N_GROUPS, LANES), jnp.float32)],
        scratch_shapes=[pltpu.VMEM((N_GROUPS, LANES), jnp.float32), pltpu.VMEM((tt, D), jnp.float32)],
        compiler_params=_cparams("arbitrary"),
        name="post_attn",
    )(fox, mla, h, *weights)


def _dispatch_kernel(fill_ref, dest_ref, xr_ref, xs_hbm, zero_ref, sem, fill_sem, *, tt, n_sorted):
    first = pl.program_id(0) == 0

    def fill_copies(act):
        for g in range(N_GROUPS):
            start, length = fill_ref[g], fill_ref[N_GROUPS + g]
            head = (-start) & (SUBLANES - 1)
            for r in range(SUBLANES - 1):
                @pl.when(r < head)
                def _(start=start, r=r):
                    act(pltpu.make_async_copy(zero_ref.at[pl.ds(0, 1)], xs_hbm.at[pl.ds(start + r, 1)], fill_sem))
            base, rest = start + head, length - head
            size = MOE_BLOCK // 2
            while size >= SUBLANES:
                @pl.when((rest & size) != 0)
                def _(base=base, rest=rest, size=size):
                    at = pl.multiple_of(base + (rest & ~(2 * size - 1)), SUBLANES)
                    act(pltpu.make_async_copy(zero_ref.at[pl.ds(0, size)], xs_hbm.at[pl.ds(at, size)], fill_sem))
                size //= 2
        used = fill_ref[2 * N_GROUPS]
        for blk in range(N_GROUPS):
            @pl.when(used + (blk + 1) * MOE_BLOCK <= n_sorted)
            def _(blk=blk):
                at = pl.multiple_of(used + blk * MOE_BLOCK, MOE_BLOCK)
                act(pltpu.make_async_copy(zero_ref, xs_hbm.at[pl.ds(at, MOE_BLOCK)], fill_sem))

    @pl.when(first)
    def _():
        zero_ref[...] = jnp.zeros_like(zero_ref)
        fill_copies(lambda cp: cp.start())

    def issue(r, carry):
        pltpu.make_async_copy(xr_ref.at[pl.ds(r, 1)], xs_hbm.at[pl.ds(dest_ref[0, 0, r], 1)], sem).start()
        return carry
    lax.fori_loop(0, tt, issue, 0, unroll=ROW_DMA_UNROLL)
    pltpu.make_async_copy(xr_ref, xs_hbm.at[pl.ds(0, tt)], sem).wait()

    @pl.when(first)
    def _():
        fill_copies(lambda cp: cp.wait())


def _dispatch(xr, dest3, fill, n_sorted):
    T, W = xr.shape
    tt = dest3.shape[2]
    grid_spec = pltpu.PrefetchScalarGridSpec(
        num_scalar_prefetch=1,
        grid=(T // tt,),
        in_specs=[pl.BlockSpec((1, 1, tt), lambda i, f: (i, 0, 0), memory_space=pltpu.SMEM),
                  pl.BlockSpec((tt, W), lambda i, f: (i, 0))],
        out_specs=pl.BlockSpec(memory_space=pl.ANY),
        scratch_shapes=[pltpu.VMEM((MOE_BLOCK, W), xr.dtype),
                        pltpu.SemaphoreType.DMA(()), pltpu.SemaphoreType.DMA(())],
    )
    return pl.pallas_call(
        functools.partial(_dispatch_kernel, tt=tt, n_sorted=n_sorted),
        grid_spec=grid_spec,
        out_shape=jax.ShapeDtypeStruct((n_sorted, W), xr.dtype),
        compiler_params=_cparams("arbitrary"),
        name="moe_dispatch",
    )(fill, dest3, xr)


def _moe_kernel(bg_ref, nv_ref, xs_ref, wg32_ref, wu32_ref, wd32_ref, ys_ref, wg_ref, wu_ref, wd_ref):
    D = xs_ref.shape[1] - LANES
    rows = xs_ref.shape[0]
    chunks = D // LANES
    b = pl.program_id(0)
    g = bg_ref[b]
    n_valid = nv_ref[b]

    @pl.when((b == 0) | (g != bg_ref[jnp.maximum(b - 1, 0)]))
    def _():
        wg_ref[...] = wg32_ref[...].astype(jnp.bfloat16)
        wu_ref[...] = wu32_ref[...].astype(jnp.bfloat16)
        wd_ref[...] = wd32_ref[...].astype(jnp.bfloat16)

    @pl.when(n_valid == 0)
    def _():
        ys_ref[...] = jnp.zeros_like(ys_ref)

    @pl.when(n_valid > 0)
    def _():
        x = xs_ref[:, 0:D].astype(jnp.bfloat16)
        info = xs_ref[:, D:D + LANES]
        y = jnp.zeros((rows, D), jnp.float32)
        for j in range(EXPERTS_PER_GROUP):
            a = _dot(x, wg_ref[j])
            u = _dot(x, wu_ref[j])
            hj = (a * jax.nn.sigmoid(a) * u).astype(jnp.bfloat16)
            y = y + info[:, j:j + 1] * _dot(hj, wd_ref[j])
        for c in range(chunks):
            ys_ref[pl.ds(c, rows, stride=chunks), :] = y[:, c * LANES:(c + 1) * LANES]


def _moe(xs, block_group, block_valid, layer, w_gate, w_up, w_down):
    n_sorted, W = xs.shape
    D = W - LANES
    nb = n_sorted // MOE_BLOCK
    epg = EXPERTS_PER_GROUP
    chunks = D // LANES
    grid_spec = pltpu.PrefetchScalarGridSpec(
        num_scalar_prefetch=2,
        grid=(nb,),
        in_specs=[pl.BlockSpec((MOE_BLOCK, W), lambda b, bg, nv: (b, 0)),
                  pl.BlockSpec((None, epg, D, D_EXPERT), lambda b, bg, nv: (layer, bg[b], 0, 0)),
                  pl.BlockSpec((None, epg, D, D_EXPERT), lambda b, bg, nv: (layer, bg[b], 0, 0)),
                  pl.BlockSpec((None, epg, D_EXPERT, D), lambda b, bg, nv: (layer, bg[b], 0, 0))],
        out_specs=pl.BlockSpec((MOE_BLOCK * chunks, LANES), lambda b, bg, nv: (b, 0)),
        scratch_shapes=[pltpu.VMEM((epg, D, D_EXPERT), jnp.bfloat16),
                        pltpu.VMEM((epg, D, D_EXPERT), jnp.bfloat16),
                        pltpu.VMEM((epg, D_EXPERT, D), jnp.bfloat16)],
    )
    return pl.pallas_call(
        _moe_kernel,
        grid_spec=grid_spec,
        out_shape=jax.ShapeDtypeStruct((n_sorted * chunks, LANES), jnp.float32),
        compiler_params=_cparams("arbitrary"),
        name="moe_experts",
    )(block_group, block_valid, xs, w_gate, w_up, w_down)


def _final_kernel(dest_ref, dest_next_ref, h_ref, ys_hbm, gain_ref, o_ref, ybuf_ref, sems, *, tt):
    y = _gather_rows(ys_hbm, dest_ref, dest_next_ref, ybuf_ref, sems,
                     pl.program_id(0), pl.num_programs(0), tt)
    o_ref[...] = _rms(h_ref[...] + y, gain_ref[...])


def _final(h, dest3, ys, gain):
    T, D = h.shape
    tt = dest3.shape[2]
    last = T // tt - 1
    return pl.pallas_call(
        functools.partial(_final_kernel, tt=tt),
        grid=(T // tt,),
        in_specs=[pl.BlockSpec((1, 1, tt), lambda i: (i, 0, 0), memory_space=pltpu.SMEM),
                  pl.BlockSpec((1, 1, tt), lambda i: (jnp.minimum(i + 1, last), 0, 0), memory_space=pltpu.SMEM),
                  pl.BlockSpec((tt, D), lambda i: (i, 0)),
                  pl.BlockSpec(memory_space=pl.ANY),
                  pl.BlockSpec((1, D), lambda i: (0, 0))],
        out_specs=pl.BlockSpec((tt, D), lambda i: (i, 0)),
        out_shape=jax.ShapeDtypeStruct((T, D), jnp.float32),
        scratch_shapes=[pltpu.VMEM((2, tt * (D // LANES), LANES), jnp.float32), pltpu.SemaphoreType.DMA((2,))],
        compiler_params=_cparams("arbitrary"),
        name="final_norm",
    )(dest3, dest3, h, ys, gain)


def _prep_layer(l, tt, w_in, b_f, w_uq, w_ukv, w_o, w_group, w_router,
                attn_norm, q_norm, kv_norm, ffn_norm):
    bf16 = jnp.bfloat16
    D = w_in.shape[1]
    fw = FOX_WIDTH
    o_f = 3 * fw
    o_cq = o_f + FOX_HEADS
    o_ckv = o_cq + Q_LORA_RANK
    o_kr = o_ckv + KV_LORA_RANK
    wi = w_in[l]
    w_f = wi[:, o_f:o_cq]
    w_f3 = jnp.concatenate([w_f] * DECAY_PARTS + [jnp.zeros((D, LANES - DECAY_PARTS * FOX_HEADS), wi.dtype)], axis=1)
    b_f3 = jnp.concatenate([b_f[l]] * DECAY_PARTS + [jnp.zeros((LANES - DECAY_PARTS * FOX_HEADS,), b_f.dtype)])
    kr = wi[:, o_kr:o_kr + MLA_ROPE_DIM]
    kr_swap = jnp.concatenate([-kr[:, ROPE_HALF:], kr[:, :ROPE_HALF]], axis=1)
    zpad = jnp.zeros((D, MLA_ROPE_DIM), wi.dtype)
    w_rows = jnp.concatenate([wi[:, fw:2 * fw], w_f3, wi[:, o_cq:o_kr], kr_swap, zpad, kr, zpad], axis=1)

    src = jnp.arange(LANES)
    dst = (src % FOX_HEADS) * HEAD_PAD + FOX_HEAD_DIM + src // FOX_HEADS
    place = ((jnp.arange(FOX_HEADS * HEAD_PAD)[None, :] == dst[:, None])
             & (src[:, None] < DECAY_PARTS * FOX_HEADS)).astype(bf16)

    qd = MLA_NOPE_DIM + MLA_ROPE_DIM
    wq = w_uq[l].reshape(Q_LORA_RANK, MLA_HEADS, qd)
    zq = jnp.zeros((Q_LORA_RANK, MLA_HEADS, HEAD_PAD - qd), wq.dtype)
    wq_main = jnp.concatenate([wq, zq], axis=2).reshape(Q_LORA_RANK, MLA_HEADS * HEAD_PAD)
    x1 = wq[:, :, MLA_NOPE_DIM:MLA_NOPE_DIM + ROPE_HALF]
    x2 = wq[:, :, MLA_NOPE_DIM + ROPE_HALF:]
    wq_swap = jnp.concatenate([-x2, x1], axis=2).reshape(Q_LORA_RANK, MLA_HEADS * MLA_ROPE_DIM)

    wkv = w_ukv[l].reshape(KV_LORA_RANK, MLA_HEADS, MLA_NOPE_DIM + MLA_V_DIM)
    wk_nope = jnp.concatenate([wkv[:, :, :MLA_NOPE_DIM],
                               jnp.zeros((KV_LORA_RANK, MLA_HEADS, HEAD_PAD - MLA_NOPE_DIM), wkv.dtype)], axis=2)
    wv = wkv[:, :, MLA_NOPE_DIM:].reshape(KV_LORA_RANK, MLA_WIDTH)

    w_exp = w_router[l].reshape(D, N_GROUPS, EXPERTS_PER_GROUP).transpose(0, 2, 1).reshape(D, N_EXPERTS)
    wr = jnp.concatenate([w_group[l], w_exp,
                          jnp.zeros((D, LANES - N_GROUPS - N_EXPERTS), w_group.dtype)], axis=1)
    wr_hi = wr.astype(bf16)
    wr_lo = (wr - wr_hi.astype(jnp.float32)).astype(bf16)

    ar = jnp.arange(tt)
    return dict(
        attn_norm=attn_norm[l].reshape(1, D),
        w_rows=w_rows.astype(bf16),
        w_fq_t=wi[:, 0:fw].T.astype(bf16),
        w_fv_t=wi[:, 2 * fw:3 * fw].T.astype(bf16),
        b_f=b_f3.reshape(1, LANES),
        tri_incl=(ar[:, None] >= ar[None, :]).astype(bf16),
        tri_before=(ar[:, None] < ar[None, :]).astype(bf16),
        place=place,
        q_norm=q_norm[l].reshape(1, Q_LORA_RANK),
        wq_main_t=wq_main.T.astype(bf16),
        wq_swap_t=wq_swap.T.astype(bf16),
        kv_norm=kv_norm[l].reshape(1, KV_LORA_RANK),
        wk_nope=wk_nope.reshape(KV_LORA_RANK, MLA_HEADS * HEAD_PAD).astype(bf16),
        wv_t=wv.T.astype(bf16),
        w_o=w_o[l].astype(bf16),
        ffn_norm=ffn_norm[l].reshape(1, D),
        wr_hi=wr_hi,
        wr_lo=wr_lo,
    )


def _routing_plan(route, counts, n_blocks):
    gid = route[GID_INDEX].astype(jnp.int32)
    rank = route[RANK_INDEX].astype(jnp.int32)
    cnt = counts[:, 0].astype(jnp.int32)
    padded = (cnt + MOE_BLOCK - 1) // MOE_BLOCK * MOE_BLOCK
    pend = jnp.cumsum(padded)
    pstart = pend - padded
    dest = pstart[gid] + rank
    block_start = jnp.arange(n_blocks, dtype=jnp.int32) * MOE_BLOCK
    block_group = jnp.sum((pend[None, :] <= block_start[:, None]).astype(jnp.int32), axis=1)
    block_group = jnp.minimum(block_group, N_GROUPS - 1)
    block_valid = jnp.clip((pstart + cnt)[block_group] - block_start, 0, MOE_BLOCK)
    fill = jnp.concatenate([pstart + cnt, padded - cnt, pend[-1:]])
    return dest, block_group, block_valid, fill


def kernel(x, positions, attn_norm, w_in, b_f, q_norm, w_uq, kv_norm, w_ukv, fox_out_norm, mla_out_norm, w_o,
           ffn_norm, w_group, w_router, w_gate, w_up, w_down, final_norm):
    B, S, D = x.shape
    T = B * S
    depth = w_in.shape[0]
    ta = min(ATTN_TILE, S)
    tt = min(ROW_TILE, S)
    n_blocks = -(-(T + N_GROUPS * (MOE_BLOCK - 1)) // MOE_BLOCK)

    cos_t, sin_t = _rope_tables(positions)
    tabs = (cos_t, sin_t)

    h = x.reshape(T, D)
    moe_in = None
    for l in range(depth):
        lw = _prep_layer(l, tt, w_in, b_f, w_uq, w_ukv, w_o, w_group, w_router,
                         attn_norm, q_norm, kv_norm, ffn_norm)
        h, (fq_t, fk, fv_t, mq_t, mk, mv_t) = _pre_attn(h, moe_in, B, S, lw, tabs)
        fox_gain = jnp.broadcast_to(fox_out_norm[l].reshape(FOX_WIDTH, 1), (FOX_WIDTH, ta))
        mla_gain = jnp.broadcast_to(mla_out_norm[l].reshape(MLA_WIDTH, 1), (MLA_WIDTH, ta))
        fox = _attention(fq_t, fk, fv_t, fox_gain, B, S, "attn_fox")
        mla = _attention(mq_t, mk, mv_t, mla_gain, B, S, "attn_mla")
        h, xr, route, counts = _post_attn(fox, mla, h, lw)
        dest, block_group, block_valid, fill = _routing_plan(route, counts, n_blocks)
        dest3 = dest.reshape(T // tt, 1, tt)
        dt = DISPATCH_TILE if T % DISPATCH_TILE == 0 else tt
        xs = _dispatch(xr, dest.reshape(T // dt, 1, dt), fill, n_blocks * MOE_BLOCK)
        ys = _moe(xs, block_group, block_valid, l, w_gate, w_up, w_down)
        moe_in = (dest3, ys)
    out = _final(h, moe_in[0], moe_in[1], final_norm.reshape(1, D))
    return out.reshape(B, S, D)
```

```python
import functools
import math

import jax
import jax.numpy as jnp
from jax import lax
from jax.experimental import pallas as pl
from jax.experimental.pallas import tpu as pltpu

FOX_HEADS = 8
FOX_HEAD_DIM = 64
FOX_WIDTH = FOX_HEADS * FOX_HEAD_DIM
MLA_HEADS = 8
MLA_NOPE_DIM = 64
MLA_ROPE_DIM = 32
MLA_V_DIM = 64
MLA_WIDTH = MLA_HEADS * MLA_V_DIM
Q_LORA_RANK = 256
KV_LORA_RANK = 128
ROPE_THETA = 10000.0
N_GROUPS = 8
EXPERTS_PER_GROUP = 4
N_EXPERTS = N_GROUPS * EXPERTS_PER_GROUP
D_EXPERT = 256
NORM_EPS = 1e-6

LANES = 128
SUBLANES = 8
BF16_SUBLANES = 16
VMEM_LIMIT_BYTES = 56 * 1024 * 1024

ROW_TILE = 512
DISPATCH_TILE = 1024
ATTN_TILE = 512
MOE_BLOCK = 512
ROW_DMA_UNROLL = 8
HEAD_PAD = 128
V_ROWS = MLA_V_DIM + BF16_SUBLANES
ROPE_HALF = MLA_ROPE_DIM // 2
NEG_BIG = -1e30
LOG2E = math.log2(math.e)
DECAY_PARTS = 3

GID_INDEX = EXPERTS_PER_GROUP
RANK_INDEX = EXPERTS_PER_GROUP + 1
ROUTE_ROWS = SUBLANES


def _cparams(*semantics):
    return pltpu.CompilerParams(dimension_semantics=semantics, vmem_limit_bytes=VMEM_LIMIT_BYTES)


def _rms(x, gain):
    ms = jnp.mean(x * x, axis=-1, keepdims=True)
    return x * lax.rsqrt(ms + NORM_EPS) * gain


def _dot(a, b):
    return jnp.dot(a, b, preferred_element_type=jnp.float32)


def _dot_nt(a, b):
    return lax.dot_general(a, b, (((1,), (1,)), ((), ())), preferred_element_type=jnp.float32)


def _split_bf16(x, parts):
    out = []
    for _ in range(parts - 1):
        piece = x.astype(jnp.bfloat16).astype(jnp.float32)
        out.append(piece)
        x = x - piece
    out.append(x.astype(jnp.bfloat16).astype(jnp.float32))
    return out


def _rope_table_kernel(pos_ref, invf_ref, cos_ref, sin_ref):
    ang = pos_ref[...].astype(jnp.float32) * invf_ref[...]
    cos_ref[...] = jnp.cos(ang)
    sin_ref[...] = jnp.sin(ang)


def _rope_tables(positions):
    T = positions.size
    inv_freq = ROPE_THETA ** (-jnp.arange(ROPE_HALF, dtype=jnp.float32) / ROPE_HALF)
    ct = min(T, 4096)
    return pl.pallas_call(
        _rope_table_kernel,
        grid=(T // ct,),
        in_specs=[pl.BlockSpec((1, ct), lambda i: (0, i)),
                  pl.BlockSpec((ROPE_HALF, 1), lambda i: (0, 0))],
        out_specs=[pl.BlockSpec((ROPE_HALF, ct), lambda i: (0, i))] * 2,
        out_shape=[jax.ShapeDtypeStruct((ROPE_HALF, T), jnp.float32)] * 2,
        compiler_params=_cparams("parallel"),
        name="rope_tables",
    )(positions.reshape(1, T), inv_freq.reshape(ROPE_HALF, 1))


def _gather_rows(src_hbm, idx_ref, idx_next_ref, buf_ref, sems, step, n_steps, n_rows):
    slot = step % 2
    chunks = buf_ref.shape[1] // n_rows

    def request(ref, to_slot):
        def issue(r, carry):
            src = pl.multiple_of(ref[0, 0, r] * chunks, chunks)
            dst = pl.multiple_of(r * chunks, chunks)
            pltpu.make_async_copy(src_hbm.at[pl.ds(src, chunks)], buf_ref.at[to_slot, pl.ds(dst, chunks)],
                                  sems.at[to_slot]).start()
            return carry
        lax.fori_loop(0, n_rows, issue, 0, unroll=ROW_DMA_UNROLL)

    @pl.when(step == 0)
    def _():
        request(idx_ref, slot)

    @pl.when(step + 1 < n_steps)
    def _():
        request(idx_next_ref, 1 - slot)

    pltpu.make_async_copy(src_hbm.at[pl.ds(0, n_rows * chunks)], buf_ref.at[slot], sems.at[slot]).wait()
    return jnp.concatenate([buf_ref[slot, pl.ds(c, n_rows, stride=chunks), :] for c in range(chunks)], axis=1)


def _pre_attn_kernel(*refs, has_moe_in, tt):
    if has_moe_in:
        dest_ref, dest_next_ref, h_ref, ys_hbm = refs[:4]
        refs = refs[4:]
    else:
        h_ref = refs[0]
        refs = refs[1:]
    (anorm_ref, wrow_ref, wfq_ref, wfv_ref, bf_ref, tri_ref, place_ref, qnorm_ref, wqm_ref, wqs_ref,
     kvnorm_ref, wkn_ref, wv_ref, cos_ref, sin_ref) = refs[:15]
    refs = refs[15:]
    if has_moe_in:
        hout_ref = refs[0]
        refs = refs[1:]
    fq_ref, fk_ref, fv_ref, qm_ref, km_ref, vm_ref = refs[:6]
    refs = refs[6:]
    carry_ref = refs[0]
    if has_moe_in:
        ybuf_ref, sems = refs[1:3]

    si = pl.program_id(1)
    bf16 = jnp.bfloat16
    half = LANES // 2

    @pl.when(si == 0)
    def _():
        carry_ref[...] = jnp.zeros_like(carry_ref)

    h = h_ref[...]
    if has_moe_in:
        step = pl.program_id(0) * pl.num_programs(1) + si
        n_steps = pl.num_programs(0) * pl.num_programs(1)
        h = h + _gather_rows(ys_hbm, dest_ref, dest_next_ref, ybuf_ref, sems, step, n_steps, tt)
        hout_ref[...] = h

    xn = _rms(h, anorm_ref[...]).astype(bf16)
    lane = lax.broadcasted_iota(jnp.int32, (tt, LANES), 1)
    sub = lax.broadcasted_iota(jnp.int32, (BF16_SUBLANES, tt), 0)
    ones_rows = jnp.ones((BF16_SUBLANES, tt), bf16)

    fw = FOX_WIDTH
    o = fw
    z = _dot(xn, wrow_ref[:, o:o + LANES]) + bf_ref[...]
    o += LANES
    cq_raw = _dot(xn, wrow_ref[:, o:o + Q_LORA_RANK])
    o += Q_LORA_RANK
    ckv_raw = _dot(xn, wrow_ref[:, o:o + KV_LORA_RANK])
    o += KV_LORA_RANK
    kr_raw = _dot(xn, wrow_ref[:, o:o + LANES])

    fq_t = _dot_nt(wfq_ref[...], xn) * (FOX_HEAD_DIM ** -0.5 * LOG2E)
    q_aug = jnp.where(sub < DECAY_PARTS, -1.0, 0.0).astype(bf16)
    q_zero = jnp.zeros((HEAD_PAD - FOX_HEAD_DIM - BF16_SUBLANES, tt), bf16)
    for hd in range(FOX_HEADS):
        fq_ref[0, hd, 0, 0:FOX_HEAD_DIM, :] = fq_t[hd * FOX_HEAD_DIM:(hd + 1) * FOX_HEAD_DIM].astype(bf16)
        fq_ref[0, hd, 0, FOX_HEAD_DIM:FOX_HEAD_DIM + BF16_SUBLANES, :] = q_aug
        fq_ref[0, hd, 0, FOX_HEAD_DIM + BF16_SUBLANES:, :] = q_zero
    fv_t = _dot_nt(wfv_ref[...], xn)
    vd = MLA_V_DIM
    for hd in range(FOX_HEADS):
        fv_ref[0, hd, 0, 0:vd, :] = fv_t[hd * vd:(hd + 1) * vd].astype(bf16)
        fv_ref[0, hd, 0, vd:, :] = ones_rows

    logf = jnp.minimum(z, 0.0) - jnp.log1p(jnp.exp(-jnp.abs(z)))
    pieces = jnp.concatenate([p_.astype(bf16) for p_ in _split_bf16(logf, DECAY_PARTS)], axis=1)
    csum = _dot(tri_ref[...], pieces)

    dcum = csum[:, 0:LANES] + csum[:, LANES:2 * LANES] + csum[:, 2 * LANES:] + carry_ref[...]
    carry_ref[...] = dcum[tt - 1:tt, :]
    d_parts = _split_bf16(dcum * LOG2E, DECAY_PARTS)
    d_sel = jnp.where(lane < FOX_HEADS, d_parts[0],
                      jnp.where(lane < 2 * FOX_HEADS, d_parts[1],
                                jnp.where(lane < 3 * FOX_HEADS, d_parts[2], 0.0)))
    k_aug = _dot(d_sel.astype(bf16), place_ref[...])

    fk = _dot(xn, wrow_ref[:, 0:fw])
    for p in range(FOX_HEADS // 2):
        blk = fk[:, p * LANES:(p + 1) * LANES]
        for hh, b_ in ((0, blk), (1, pltpu.roll(blk, half, axis=1))):
            sl = slice((2 * p + hh) * HEAD_PAD, (2 * p + hh + 1) * HEAD_PAD)
            fk_ref[:, sl] = jnp.where(lane < FOX_HEAD_DIM, b_, k_aug[:, sl]).astype(bf16)

    cq = _rms(cq_raw, qnorm_ref[...]).astype(bf16)
    qm_t = _dot_nt(wqm_ref[...], cq)
    qs_t = _dot_nt(wqs_ref[...], cq)
    cos_t = cos_ref[...]
    sin_t = sin_ref[...]
    scale = (MLA_NOPE_DIM + MLA_ROPE_DIM) ** -0.5 * LOG2E
    q_zero = jnp.zeros((HEAD_PAD - MLA_NOPE_DIM - MLA_ROPE_DIM, tt), jnp.float32)
    for hd in range(MLA_HEADS):
        blk = qm_t[hd * HEAD_PAD:(hd + 1) * HEAD_PAD]
        rows = [blk[0:MLA_NOPE_DIM]]
        for j in range(2):
            main = blk[MLA_NOPE_DIM + j * ROPE_HALF:MLA_NOPE_DIM + (j + 1) * ROPE_HALF]
            swap = qs_t[hd * MLA_ROPE_DIM + j * ROPE_HALF:hd * MLA_ROPE_DIM + (j + 1) * ROPE_HALF]
            rows.append(main * cos_t + swap * sin_t)
        rows.append(q_zero)
        qm_ref[0, hd, 0] = (jnp.concatenate(rows, axis=0) * scale).astype(bf16)

    ckv = _rms(ckv_raw, kvnorm_ref[...]).astype(bf16)
    mv_t = _dot_nt(wv_ref[...], ckv)
    for hd in range(MLA_HEADS):
        vm_ref[0, hd, 0, 0:vd, :] = mv_t[hd * vd:(hd + 1) * vd].astype(bf16)
        vm_ref[0, hd, 0, vd:, :] = ones_rows
    k_nope = _dot(ckv, wkn_ref[...])
    z_rows = jnp.zeros((MLA_ROPE_DIM, tt), jnp.float32)
    csk = jnp.concatenate([sin_t, sin_t, z_rows, cos_t, cos_t, z_rows], axis=0).T
    kr = kr_raw * csk
    kr = kr + pltpu.roll(kr, half, axis=1)
    k_pe = jnp.where(lane >= MLA_NOPE_DIM, kr, 0.0)
    for hd in range(MLA_HEADS):
        sl = slice(hd * HEAD_PAD, (hd + 1) * HEAD_PAD)
        km_ref[:, sl] = (k_nope[:, sl] + k_pe).astype(bf16)


def _pre_attn(h, moe_in, B, S, lw, tabs):
    T, D = h.shape
    tt = min(ROW_TILE, S)
    ns = S // tt
    has_moe_in = moe_in is not None
    row = lambda b, s: (b * ns + s, 0)
    const = lambda b, s: (0, 0)
    cos_t, sin_t = tabs

    def rows(width):
        return pl.BlockSpec((tt, width), row)

    def full(a):
        return pl.BlockSpec(a.shape, const)

    weights = [lw["attn_norm"], lw["w_rows"], lw["w_fq_t"], lw["w_fv_t"], lw["b_f"], lw["tri_incl"],
               lw["place"], lw["q_norm"], lw["wq_main_t"], lw["wq_swap_t"], lw["kv_norm"],
               lw["wk_nope"], lw["wv_t"]]
    in_specs = [rows(D)]
    args = [h]
    if has_moe_in:
        dest3, ys = moe_in
        last = B * ns - 1
        in_specs = [pl.BlockSpec((1, 1, tt), lambda b, s: (b * ns + s, 0, 0), memory_space=pltpu.SMEM),
                    pl.BlockSpec((1, 1, tt), lambda b, s: (jnp.minimum(b * ns + s + 1, last), 0, 0),
                                 memory_space=pltpu.SMEM),
                    rows(D), pl.BlockSpec(memory_space=pl.ANY)]
        args = [dest3, dest3, h, ys]
    col = lambda b, s: (0, b * ns + s)
    in_specs += [full(w) for w in weights]
    in_specs += [pl.BlockSpec((ROPE_HALF, tt), col), pl.BlockSpec((ROPE_HALF, tt), col)]
    args += weights + [cos_t, sin_t]

    bf16 = jnp.bfloat16
    qt_shape = jax.ShapeDtypeStruct((B, FOX_HEADS, ns, HEAD_PAD, tt), bf16)
    qt_spec = pl.BlockSpec((1, FOX_HEADS, 1, HEAD_PAD, tt), lambda b, s: (b, 0, s, 0, 0))
    k_shape = jax.ShapeDtypeStruct((T, FOX_HEADS * HEAD_PAD), bf16)
    k_spec = rows(FOX_HEADS * HEAD_PAD)
    vt_shape = jax.ShapeDtypeStruct((B, FOX_HEADS, ns, V_ROWS, tt), bf16)
    vt_spec = pl.BlockSpec((1, FOX_HEADS, 1, V_ROWS, tt), lambda b, s: (b, 0, s, 0, 0))
    out_shape = [qt_shape, k_shape, vt_shape] * 2
    out_specs = [qt_spec, k_spec, vt_spec] * 2
    scratch = [pltpu.VMEM((1, LANES), jnp.float32)]
    if has_moe_in:
        out_shape = [jax.ShapeDtypeStruct((T, D), jnp.float32)] + out_shape
        out_specs = [rows(D)] + out_specs
        scratch += [pltpu.VMEM((2, tt * (D // LANES), LANES), jnp.float32), pltpu.SemaphoreType.DMA((2,))]

    outs = pl.pallas_call(
        functools.partial(_pre_attn_kernel, has_moe_in=has_moe_in, tt=tt),
        grid=(B, ns),
        in_specs=in_specs,
        out_specs=out_specs,
        out_shape=out_shape,
        scratch_shapes=scratch,
        compiler_params=_cparams("arbitrary", "arbitrary"),
        name="pre_attn",
    )(*args)
    if has_moe_in:
        return outs[0], outs[1:]
    return h, outs


def _attn_kernel(*refs, ta, nq, n_streams):
    ins, refs = refs[:4 * n_streams], refs[4 * n_streams:]
    o_refs, (s_scr, mc_scr, m_scr, acc_scr) = refs[:n_streams], refs[n_streams:]
    qt_refs, k_refs, vt_refs, g_refs = ins[0::4], ins[1::4], ins[2::4], ins[3::4]
    heads = [(st, hh) for st in range(n_streams) for hh in range(2)]
    vd = MLA_V_DIM

    def produce(q, kt, slot, masked):
        k0 = pl.multiple_of(kt * ta, ta)
        for hi, (st, hh) in enumerate(heads):
            kk = k_refs[st][pl.ds(k0, ta), hh * HEAD_PAD:(hh + 1) * HEAD_PAD]
            qt = qt_refs[st][0, hh, q]
            if not masked:
                s = _dot(kk, qt)
                s_scr[slot, hi] = s
                mc_scr[slot, hi] = jnp.max(s, axis=0, keepdims=True)
                continue
            hk = ta // 2
            top = _dot(kk[0:hk], qt)
            low = _dot(kk[hk:], qt[:, hk:])
            top = jnp.where(lax.broadcasted_iota(jnp.int32, top.shape, 0)
                            <= lax.broadcasted_iota(jnp.int32, top.shape, 1), top, NEG_BIG)
            low = jnp.where(lax.broadcasted_iota(jnp.int32, low.shape, 0)
                            <= lax.broadcasted_iota(jnp.int32, low.shape, 1), low, NEG_BIG)
            s_scr[slot, hi, 0:hk, :] = top
            s_scr[slot, hi, hk:, 0:hk] = jnp.full((hk, hk), NEG_BIG, jnp.float32)
            s_scr[slot, hi, hk:, hk:] = low
            top_max = jnp.max(top, axis=0, keepdims=True)
            mc_scr[slot, hi, :, 0:hk] = top_max[:, 0:hk]
            mc_scr[slot, hi, :, hk:] = jnp.maximum(top_max[:, hk:], jnp.max(low, axis=0, keepdims=True))

    def consume(kt, slot):
        for hi, (st, hh) in enumerate(heads):
            m_prev = m_scr[hi]
            m_new = jnp.maximum(m_prev, mc_scr[slot, hi])
            alpha = jnp.exp2(m_prev - m_new)
            p = jnp.exp2(s_scr[slot, hi] - m_new).astype(jnp.bfloat16)
            acc_scr[hi] = alpha * acc_scr[hi] + _dot(vt_refs[st][0, hh, kt], p)
            m_scr[hi] = m_new

    def reset():
        m_scr[...] = jnp.full_like(m_scr, NEG_BIG)
        acc_scr[...] = jnp.zeros_like(acc_scr)

    def finish(q):
        for st in range(n_streams):
            outs = []
            for hh in range(2):
                acc = acc_scr[2 * st + hh]
                oh = acc[0:vd] / acc[vd:vd + 1]
                ms = jnp.mean(oh * oh, axis=0, keepdims=True)
                outs.append(oh * lax.rsqrt(ms + NORM_EPS))
            o_t = jnp.concatenate(outs, axis=0) * g_refs[st][...]
            o_refs[st][pl.ds(pl.multiple_of(q * ta, ta), ta), :] = o_t.T.astype(o_refs[st].dtype)

    reset()
    produce(0, 0, 0, True)

    def q_tile(n, carry):
        first = (n * (n + 1) // 2) % 2

        def step(j, slot):
            produce(n, j, 1 - slot, False)
            consume(jnp.where(j == 0, n, j - 1), slot)

        for par in range(2):
            @pl.when(first == par)
            def _():
                def two_steps(u, c):
                    step(2 * u, par)
                    step(2 * u + 1, 1 - par)
                    return c

                lax.fori_loop(0, n // 2, two_steps, 0)

                @pl.when((n % 2 == 1) & (n < nq - 1))
                def _():
                    step(n - 1, par)
                    produce(n + 1, n + 1, par, True)
                    consume(n - 1, 1 - par)
                    finish(n)
                    reset()

                @pl.when((n % 2 == 0) & (n < nq - 1))
                def _():
                    produce(n + 1, n + 1, 1 - par, True)
                    consume(jnp.maximum(n - 1, 0), par)
                    finish(n)
                    reset()

        n_last = nq - 1
        first_last = (n_last * (n_last + 1) // 2) % 2

        @pl.when(n == n_last)
        def _():
            if n_last % 2 == 1:
                step(n - 1, first_last)
                consume(n - 1, 1 - first_last)
            else:
                consume(jnp.maximum(n - 1, 0), first_last)
            finish(n)

        return carry

    lax.fori_loop(0, nq, q_tile, 0)


def _attention(streams, B, S):
    T = streams[0][1].shape[0]
    ta = min(ATTN_TILE, S)
    nq = S // ta
    n_pairs = streams[0][0].shape[1] // 2
    n_streams = len(streams)
    n_heads = 2 * n_streams
    stream_specs = [pl.BlockSpec((1, 2, nq, HEAD_PAD, ta), lambda b, j: (b, j, 0, 0, 0)),
                    pl.BlockSpec((S, 2 * HEAD_PAD), lambda b, j: (b, j)),
                    pl.BlockSpec((1, 2, nq, V_ROWS, ta), lambda b, j: (b, j, 0, 0, 0)),
                    pl.BlockSpec((LANES, ta), lambda b, j: (j, 0))]
    return pl.pallas_call(
        functools.partial(_attn_kernel, ta=ta, nq=nq, n_streams=n_streams),
        grid=(B, n_pairs),
        in_specs=stream_specs * n_streams,
        out_specs=[pl.BlockSpec((S, LANES), lambda b, j: (b, j))] * n_streams,
        out_shape=[jax.ShapeDtypeStruct((T, n_pairs * LANES), jnp.bfloat16)] * n_streams,
        scratch_shapes=[pltpu.VMEM((2, n_heads, ta, ta), jnp.float32),
                        pltpu.VMEM((2, n_heads, 1, ta), jnp.float32),
                        pltpu.VMEM((n_heads, 1, ta), jnp.float32),
                        pltpu.VMEM((n_heads, V_ROWS, ta), jnp.float32)],
        compiler_params=_cparams("parallel", "parallel"),
        name="attention",
    )(*[a for stream in streams for a in stream])


def _post_attn_kernel(fox_ref, mla_ref, h_ref, wo_ref, fnorm_ref, wrh_ref, wrl_ref, tri_ref,
                      h2_ref, xr_ref, route_ref, cnt_ref, carry_ref, xn_scr):
    i = pl.program_id(0)
    D = h_ref.shape[1]

    @pl.when(i == 0)
    def _():
        xn_scr[...] = jnp.zeros_like(xn_scr)

    @pl.when(i <= 1)
    def _():
        carry_ref[...] = jnp.zeros_like(carry_ref)

    xn = xn_scr[...]
    xr_ref[:, 0:D] = xn
    h2 = h_ref[...] + _dot(fox_ref[...], wo_ref[0:FOX_WIDTH, :]) + _dot(mla_ref[...], wo_ref[FOX_WIDTH:, :])
    h2_ref[...] = h2
    xn_scr[...] = _rms(h2, fnorm_ref[...])

    x_hi = xn.astype(jnp.bfloat16)
    x_lo = (xn - x_hi.astype(jnp.float32)).astype(jnp.bfloat16)
    logits = _dot(x_hi, wrh_ref[...]) + _dot(x_lo, wrh_ref[...]) + _dot(x_hi, wrl_ref[...])

    lt = logits.T
    tt = lt.shape[1]
    ng = N_GROUPS
    gl = lt[0:ng]
    sub = lax.broadcasted_iota(jnp.int32, (ng, tt), 0).astype(jnp.float32)

    gmax = jnp.max(gl, axis=0, keepdims=True)
    group_w = 1.0 / jnp.sum(jnp.exp(gl - gmax), axis=0, keepdims=True)
    gid = jnp.min(jnp.where(gl == gmax, sub, float(ng)), axis=0, keepdims=True)
    onehot = sub == gid

    ins = [jnp.sum(jnp.where(onehot, lt[ng * (j + 1):ng * (j + 2)], 0.0), axis=0, keepdims=True)
           for j in range(EXPERTS_PER_GROUP)]

    def first_argmax(vals):
        top = functools.reduce(jnp.maximum, vals)
        idx = jnp.full_like(top, float(len(vals) - 1))
        for j in range(len(vals) - 2, -1, -1):
            idx = jnp.where(vals[j] == top, float(j), idx)
        return top, idx

    e1, i1 = first_argmax(ins)
    e2, i2 = first_argmax([jnp.where(i1 == j, NEG_BIG, v) for j, v in enumerate(ins)])
    t = jnp.exp(e2 - e1)
    g1 = group_w / (1.0 + t)
    g2 = group_w * t / (1.0 + t)
    gates = [jnp.where(i1 == j, g1, jnp.where(i2 == j, g2, 0.0)) for j in range(EXPERTS_PER_GROUP)]

    ones = jnp.where(onehot, 1.0, 0.0)
    prefix = _dot(ones.astype(jnp.bfloat16), tri_ref[...]) + carry_ref[:, 0:1]
    rank = jnp.sum(jnp.where(onehot, prefix, 0.0), axis=0, keepdims=True)
    carry = carry_ref[...] + jnp.sum(ones, axis=1, keepdims=True)
    carry_ref[...] = carry
    cnt_ref[...] = carry

    pad_rows = jnp.zeros((ROUTE_ROWS - EXPERTS_PER_GROUP - 2, tt), jnp.float32)
    info_t = jnp.concatenate(gates + [gid, rank, pad_rows], axis=0)
    route_ref[...] = info_t
    info_pad = jnp.zeros((LANES - ROUTE_ROWS, tt), jnp.float32)
    xr_ref[:, D:D + LANES] = jnp.concatenate([info_t, info_pad], axis=0).T


def _post_attn(fox, mla, h, lw):
    T, D = h.shape
    tt = min(ROW_TILE, T)
    n = T // tt
    proj = lambda i: (jnp.minimum(i, n - 1), 0)
    routed = lambda i: (jnp.maximum(i - 1, 0), 0)
    const = lambda i: (0, 0)
    weights = [lw["w_o"], lw["ffn_norm"], lw["wr_hi"], lw["wr_lo"], lw["tri_before"]]
    return pl.pallas_call(
        _post_attn_kernel,
        grid=(n + 1,),
        in_specs=[pl.BlockSpec((tt, FOX_WIDTH), proj), pl.BlockSpec((tt, MLA_WIDTH), proj),
                  pl.BlockSpec((tt, D), proj)] + [pl.BlockSpec(w.shape, const) for w in weights],
        out_specs=[pl.BlockSpec((tt, D), proj), pl.BlockSpec((tt, D + LANES), routed),
                   pl.BlockSpec((ROUTE_ROWS, tt), lambda i: (0, jnp.maximum(i - 1, 0))),
                   pl.BlockSpec((N_GROUPS, LANES), const)],
        out_shape=[jax.ShapeDtypeStruct((T, D), jnp.float32),
                   jax.ShapeDtypeStruct((T, D + LANES), jnp.float32),
                   jax.ShapeDtypeStruct((ROUTE_ROWS, T), jnp.float32),
                   jax.ShapeDtypeStruct((N_GROUPS, LANES), jnp.float32)],
        scratch_shapes=[pltpu.VMEM((N_GROUPS, LANES), jnp.float32), pltpu.VMEM((tt, D), jnp.float32)],
        compiler_params=_cparams("arbitrary"),
        name="post_attn",
    )(fox, mla, h, *weights)


def _dispatch_kernel(fill_ref, dest_ref, xr_ref, xs_hbm, zero_ref, sem, fill_sem, *, tt, n_sorted):
    first = pl.program_id(0) == 0

    def fill_copies(act):
        for g in range(N_GROUPS):
            start, length = fill_ref[g], fill_ref[N_GROUPS + g]
            head = (-start) & (SUBLANES - 1)
            for r in range(SUBLANES - 1):
                @pl.when(r < head)
                def _(start=start, r=r):
                    act(pltpu.make_async_copy(zero_ref.at[pl.ds(0, 1)], xs_hbm.at[pl.ds(start + r, 1)], fill_sem))
            base, rest = start + head, length - head
            size = MOE_BLOCK // 2
            while size >= SUBLANES:
                @pl.when((rest & size) != 0)
                def _(base=base, rest=rest, size=size):
                    at = pl.multiple_of(base + (rest & ~(2 * size - 1)), SUBLANES)
                    act(pltpu.make_async_copy(zero_ref.at[pl.ds(0, size)], xs_hbm.at[pl.ds(at, size)], fill_sem))
                size //= 2
        used = fill_ref[2 * N_GROUPS]
        for blk in range(N_GROUPS):
            @pl.when(used + (blk + 1) * MOE_BLOCK <= n_sorted)
            def _(blk=blk):
                at = pl.multiple_of(used + blk * MOE_BLOCK, MOE_BLOCK)
                act(pltpu.make_async_copy(zero_ref, xs_hbm.at[pl.ds(at, MOE_BLOCK)], fill_sem))

    @pl.when(first)
    def _():
        zero_ref[...] = jnp.zeros_like(zero_ref)
        fill_copies(lambda cp: cp.start())

    def issue(r, carry):
        pltpu.make_async_copy(xr_ref.at[pl.ds(r, 1)], xs_hbm.at[pl.ds(dest_ref[0, 0, r], 1)], sem).start()
        return carry
    lax.fori_loop(0, tt, issue, 0, unroll=ROW_DMA_UNROLL)
    pltpu.make_async_copy(xr_ref, xs_hbm.at[pl.ds(0, tt)], sem).wait()

    @pl.when(first)
    def _():
        fill_copies(lambda cp: cp.wait())


def _dispatch(xr, dest3, fill, n_sorted):
    T, W = xr.shape
    tt = dest3.shape[2]
    grid_spec = pltpu.PrefetchScalarGridSpec(
        num_scalar_prefetch=1,
        grid=(T // tt,),
        in_specs=[pl.BlockSpec((1, 1, tt), lambda i, f: (i, 0, 0), memory_space=pltpu.SMEM),
                  pl.BlockSpec((tt, W), lambda i, f: (i, 0))],
        out_specs=pl.BlockSpec(memory_space=pl.ANY),
        scratch_shapes=[pltpu.VMEM((MOE_BLOCK, W), xr.dtype),
                        pltpu.SemaphoreType.DMA(()), pltpu.SemaphoreType.DMA(())],
    )
    return pl.pallas_call(
        functools.partial(_dispatch_kernel, tt=tt, n_sorted=n_sorted),
        grid_spec=grid_spec,
        out_shape=jax.ShapeDtypeStruct((n_sorted, W), xr.dtype),
        compiler_params=_cparams("arbitrary"),
        name="moe_dispatch",
    )(fill, dest3, xr)


def _moe_kernel(bg_ref, nv_ref, xs_ref, wg32_ref, wu32_ref, wd32_ref, ys_ref, wg_ref, wu_ref, wd_ref):
    D = xs_ref.shape[1] - LANES
    rows = xs_ref.shape[0]
    chunks = D // LANES
    b = pl.program_id(0)
    g = bg_ref[b]
    n_valid = nv_ref[b]

    @pl.when((b == 0) | (g != bg_ref[jnp.maximum(b - 1, 0)]))
    def _():
        wg_ref[...] = wg32_ref[...].astype(jnp.bfloat16)
        wu_ref[...] = wu32_ref[...].astype(jnp.bfloat16)
        wd_ref[...] = wd32_ref[...].astype(jnp.bfloat16)

    @pl.when(n_valid == 0)
    def _():
        ys_ref[...] = jnp.zeros_like(ys_ref)

    @pl.when(n_valid > 0)
    def _():
        x = xs_ref[:, 0:D].astype(jnp.bfloat16)
        info = xs_ref[:, D:D + LANES]
        y = jnp.zeros((rows, D), jnp.float32)
        for j in range(EXPERTS_PER_GROUP):
            a = _dot(x, wg_ref[j])
            u = _dot(x, wu_ref[j])
            hj = (a * jax.nn.sigmoid(a) * u).astype(jnp.bfloat16)
            y = y + info[:, j:j + 1] * _dot(hj, wd_ref[j])
        for c in range(chunks):
            ys_ref[pl.ds(c, rows, stride=chunks), :] = y[:, c * LANES:(c + 1) * LANES]


def _moe(xs, block_group, block_valid, layer, w_gate, w_up, w_down):
    n_sorted, W = xs.shape
    D = W - LANES
    nb = n_sorted // MOE_BLOCK
    epg = EXPERTS_PER_GROUP
    chunks = D // LANES
    grid_spec = pltpu.PrefetchScalarGridSpec(
        num_scalar_prefetch=2,
        grid=(nb,),
        in_specs=[pl.BlockSpec((MOE_BLOCK, W), lambda b, bg, nv: (b, 0)),
                  pl.BlockSpec((None, epg, D, D_EXPERT), lambda b, bg, nv: (layer, bg[b], 0, 0)),
                  pl.BlockSpec((None, epg, D, D_EXPERT), lambda b, bg, nv: (layer, bg[b], 0, 0)),
                  pl.BlockSpec((None, epg, D_EXPERT, D), lambda b, bg, nv: (layer, bg[b], 0, 0))],
        out_specs=pl.BlockSpec((MOE_BLOCK * chunks, LANES), lambda b, bg, nv: (b, 0)),
        scratch_shapes=[pltpu.VMEM((epg, D, D_EXPERT), jnp.bfloat16),
                        pltpu.VMEM((epg, D, D_EXPERT), jnp.bfloat16),
                        pltpu.VMEM((epg, D_EXPERT, D), jnp.bfloat16)],
    )
    return pl.pallas_call(
        _moe_kernel,
        grid_spec=grid_spec,
        out_shape=jax.ShapeDtypeStruct((n_sorted * chunks, LANES), jnp.float32),
        compiler_params=_cparams("arbitrary"),
        name="moe_experts",
    )(block_group, block_valid, xs, w_gate, w_up, w_down)


def _final_kernel(dest_ref, dest_next_ref, h_ref, ys_hbm, gain_ref, o_ref, ybuf_ref, sems, *, tt):
    y = _gather_rows(ys_hbm, dest_ref, dest_next_ref, ybuf_ref, sems,
                     pl.program_id(0), pl.num_programs(0), tt)
    o_ref[...] = _rms(h_ref[...] + y, gain_ref[...])


def _final(h, dest3, ys, gain):
    T, D = h.shape
    tt = dest3.shape[2]
    last = T // tt - 1
    return pl.pallas_call(
        functools.partial(_final_kernel, tt=tt),
        grid=(T // tt,),
        in_specs=[pl.BlockSpec((1, 1, tt), lambda i: (i, 0, 0), memory_space=pltpu.SMEM),
                  pl.BlockSpec((1, 1, tt), lambda i: (jnp.minimum(i + 1, last), 0, 0), memory_space=pltpu.SMEM),
                  pl.BlockSpec((tt, D), lambda i: (i, 0)),
                  pl.BlockSpec(memory_space=pl.ANY),
                  pl.BlockSpec((1, D), lambda i: (0, 0))],
        out_specs=pl.BlockSpec((tt, D), lambda i: (i, 0)),
        out_shape=jax.ShapeDtypeStruct((T, D), jnp.float32),
        scratch_shapes=[pltpu.VMEM((2, tt * (D // LANES), LANES), jnp.float32), pltpu.SemaphoreType.DMA((2,))],
        compiler_params=_cparams("arbitrary"),
        name="final_norm",
    )(dest3, dest3, h, ys, gain)


def _prep_layer(l, tt, w_in, b_f, w_uq, w_ukv, w_o, w_group, w_router,
                attn_norm, q_norm, kv_norm, ffn_norm):
    bf16 = jnp.bfloat16
    D = w_in.shape[1]
    fw = FOX_WIDTH
    o_f = 3 * fw
    o_cq = o_f + FOX_HEADS
    o_ckv = o_cq + Q_LORA_RANK
    o_kr = o_ckv + KV_LORA_RANK
    wi = w_in[l]
    w_f = wi[:, o_f:o_cq]
    w_f3 = jnp.concatenate([w_f] * DECAY_PARTS + [jnp.zeros((D, LANES - DECAY_PARTS * FOX_HEADS), wi.dtype)], axis=1)
    b_f3 = jnp.concatenate([b_f[l]] * DECAY_PARTS + [jnp.zeros((LANES - DECAY_PARTS * FOX_HEADS,), b_f.dtype)])
    kr = wi[:, o_kr:o_kr + MLA_ROPE_DIM]
    kr_swap = jnp.concatenate([-kr[:, ROPE_HALF:], kr[:, :ROPE_HALF]], axis=1)
    zpad = jnp.zeros((D, MLA_ROPE_DIM), wi.dtype)
    w_rows = jnp.concatenate([wi[:, fw:2 * fw], w_f3, wi[:, o_cq:o_kr], kr_swap, zpad, kr, zpad], axis=1)

    src = jnp.arange(LANES)
    dst = (src % FOX_HEADS) * HEAD_PAD + FOX_HEAD_DIM + src // FOX_HEADS
    place = ((jnp.arange(FOX_HEADS * HEAD_PAD)[None, :] == dst[:, None])
             & (src[:, None] < DECAY_PARTS * FOX_HEADS)).astype(bf16)

    qd = MLA_NOPE_DIM + MLA_ROPE_DIM
    wq = w_uq[l].reshape(Q_LORA_RANK, MLA_HEADS, qd)
    zq = jnp.zeros((Q_LORA_RANK, MLA_HEADS, HEAD_PAD - qd), wq.dtype)
    wq_main = jnp.concatenate([wq, zq], axis=2).reshape(Q_LORA_RANK, MLA_HEADS * HEAD_PAD)
    x1 = wq[:, :, MLA_NOPE_DIM:MLA_NOPE_DIM + ROPE_HALF]
    x2 = wq[:, :, MLA_NOPE_DIM + ROPE_HALF:]
    wq_swap = jnp.concatenate([-x2, x1], axis=2).reshape(Q_LORA_RANK, MLA_HEADS * MLA_ROPE_DIM)

    wkv = w_ukv[l].reshape(KV_LORA_RANK, MLA_HEADS, MLA_NOPE_DIM + MLA_V_DIM)
    wk_nope = jnp.concatenate([wkv[:, :, :MLA_NOPE_DIM],
                               jnp.zeros((KV_LORA_RANK, MLA_HEADS, HEAD_PAD - MLA_NOPE_DIM), wkv.dtype)], axis=2)
    wv = wkv[:, :, MLA_NOPE_DIM:].reshape(KV_LORA_RANK, MLA_WIDTH)

    w_exp = w_router[l].reshape(D, N_GROUPS, EXPERTS_PER_GROUP).transpose(0, 2, 1).reshape(D, N_EXPERTS)
    wr = jnp.concatenate([w_group[l], w_exp,
                          jnp.zeros((D, LANES - N_GROUPS - N_EXPERTS), w_group.dtype)], axis=1)
    wr_hi = wr.astype(bf16)
    wr_lo = (wr - wr_hi.astype(jnp.float32)).astype(bf16)

    ar = jnp.arange(tt)
    return dict(
        attn_norm=attn_norm[l].reshape(1, D),
        w_rows=w_rows.astype(bf16),
        w_fq_t=wi[:, 0:fw].T.astype(bf16),
        w_fv_t=wi[:, 2 * fw:3 * fw].T.astype(bf16),
        b_f=b_f3.reshape(1, LANES),
        tri_incl=(ar[:, None] >= ar[None, :]).astype(bf16),
        tri_before=(ar[:, None] < ar[None, :]).astype(bf16),
        place=place,
        q_norm=q_norm[l].reshape(1, Q_LORA_RANK),
        wq_main_t=wq_main.T.astype(bf16),
        wq_swap_t=wq_swap.T.astype(bf16),
        kv_norm=kv_norm[l].reshape(1, KV_LORA_RANK),
        wk_nope=wk_nope.reshape(KV_LORA_RANK, MLA_HEADS * HEAD_PAD).astype(bf16),
        wv_t=wv.T.astype(bf16),
        w_o=w_o[l].astype(bf16),
        ffn_norm=ffn_norm[l].reshape(1, D),
        wr_hi=wr_hi,
        wr_lo=wr_lo,
    )


def _routing_plan(route, counts, n_blocks):
    gid = route[GID_INDEX].astype(jnp.int32)
    rank = route[RANK_INDEX].astype(jnp.int32)
    cnt = counts[:, 0].astype(jnp.int32)
    padded = (cnt + MOE_BLOCK - 1) // MOE_BLOCK * MOE_BLOCK
    pend = jnp.cumsum(padded)
    pstart = pend - padded
    dest = pstart[gid] + rank
    block_start = jnp.arange(n_blocks, dtype=jnp.int32) * MOE_BLOCK
    block_group = jnp.sum((pend[None, :] <= block_start[:, None]).astype(jnp.int32), axis=1)
    block_group = jnp.minimum(block_group, N_GROUPS - 1)
    block_valid = jnp.clip((pstart + cnt)[block_group] - block_start, 0, MOE_BLOCK)
    fill = jnp.concatenate([pstart + cnt, padded - cnt, pend[-1:]])
    return dest, block_group, block_valid, fill


def kernel(x, positions, attn_norm, w_in, b_f, q_norm, w_uq, kv_norm, w_ukv, fox_out_norm, mla_out_norm, w_o,
           ffn_norm, w_group, w_router, w_gate, w_up, w_down, final_norm):
    B, S, D = x.shape
    T = B * S
    depth = w_in.shape[0]
    ta = min(ATTN_TILE, S)
    tt = min(ROW_TILE, S)
    n_blocks = -(-(T + N_GROUPS * (MOE_BLOCK - 1)) // MOE_BLOCK)

    cos_t, sin_t = _rope_tables(positions)
    tabs = (cos_t, sin_t)

    h = x.reshape(T, D)
    moe_in = None
    for l in range(depth):
        lw = _prep_layer(l, tt, w_in, b_f, w_uq, w_ukv, w_o, w_group, w_router,
                         attn_norm, q_norm, kv_norm, ffn_norm)
        h, (fq_t, fk, fv_t, mq_t, mk, mv_t) = _pre_attn(h, moe_in, B, S, lw, tabs)
        fox_gain = jnp.broadcast_to(fox_out_norm[l].reshape(FOX_WIDTH, 1), (FOX_WIDTH, ta))
        mla_gain = jnp.broadcast_to(mla_out_norm[l].reshape(MLA_WIDTH, 1), (MLA_WIDTH, ta))
        fox, mla = _attention([(fq_t, fk, fv_t, fox_gain), (mq_t, mk, mv_t, mla_gain)], B, S)
        h, xr, route, counts = _post_attn(fox, mla, h, lw)
        dest, block_group, block_valid, fill = _routing_plan(route, counts, n_blocks)
        dest3 = dest.reshape(T // tt, 1, tt)
        dt = DISPATCH_TILE if T % DISPATCH_TILE == 0 else tt
        xs = _dispatch(xr, dest.reshape(T // dt, 1, dt), fill, n_blocks * MOE_BLOCK)
        ys = _moe(xs, block_group, block_valid, l, w_gate, w_up, w_down)
        moe_in = (dest3, ys)
    out = _final(h, moe_in[0], moe_in[1], final_norm.reshape(1, D))
    return out.reshape(B, S, D)
```

```python
import functools
import math

import jax
import jax.numpy as jnp
from jax import lax
from jax.experimental import pallas as pl
from jax.experimental.pallas import tpu as pltpu

FOX_HEADS = 8
FOX_HEAD_DIM = 64
FOX_WIDTH = FOX_HEADS * FOX_HEAD_DIM
MLA_HEADS = 8
MLA_NOPE_DIM = 64
MLA_ROPE_DIM = 32
MLA_V_DIM = 64
MLA_WIDTH = MLA_HEADS * MLA_V_DIM
Q_LORA_RANK = 256
KV_LORA_RANK = 128
ROPE_THETA = 10000.0
N_GROUPS = 8
EXPERTS_PER_GROUP = 4
N_EXPERTS = N_GROUPS * EXPERTS_PER_GROUP
D_EXPERT = 256
NORM_EPS = 1e-6

LANES = 128
SUBLANES = 8
BF16_SUBLANES = 16
VMEM_LIMIT_BYTES = 56 * 1024 * 1024

ROW_TILE = 512
DISPATCH_TILE = 1024
ATTN_TILE = 512
MOE_BLOCK = 512
ROW_DMA_UNROLL = 8
HEAD_PAD = 128
V_ROWS = MLA_V_DIM + BF16_SUBLANES
ROPE_HALF = MLA_ROPE_DIM // 2
NEG_BIG = -1e30
LOG2E = math.log2(math.e)
DECAY_PARTS = 3

GID_INDEX = EXPERTS_PER_GROUP
RANK_INDEX = EXPERTS_PER_GROUP + 1
ROUTE_ROWS = SUBLANES


def _cparams(*semantics):
    return pltpu.CompilerParams(dimension_semantics=semantics, vmem_limit_bytes=VMEM_LIMIT_BYTES)


def _rms(x, gain):
    ms = jnp.mean(x * x, axis=-1, keepdims=True)
    return x * lax.rsqrt(ms + NORM_EPS) * gain


def _dot(a, b):
    return jnp.dot(a, b, preferred_element_type=jnp.float32)


def _dot_nt(a, b):
    return lax.dot_general(a, b, (((1,), (1,)), ((), ())), preferred_element_type=jnp.float32)


def _split_bf16(x, parts):
    out = []
    for _ in range(parts - 1):
        piece = x.astype(jnp.bfloat16).astype(jnp.float32)
        out.append(piece)
        x = x - piece
    out.append(x.astype(jnp.bfloat16).astype(jnp.float32))
    return out


def _rope_table_kernel(pos_ref, invf_ref, cos_ref, sin_ref):
    ang = pos_ref[...].astype(jnp.float32) * invf_ref[...]
    cos_ref[...] = jnp.cos(ang)
    sin_ref[...] = jnp.sin(ang)


def _rope_tables(positions):
    T = positions.size
    inv_freq = ROPE_THETA ** (-jnp.arange(ROPE_HALF, dtype=jnp.float32) / ROPE_HALF)
    ct = min(T, 4096)
    return pl.pallas_call(
        _rope_table_kernel,
        grid=(T // ct,),
        in_specs=[pl.BlockSpec((1, ct), lambda i: (0, i)),
                  pl.BlockSpec((ROPE_HALF, 1), lambda i: (0, 0))],
        out_specs=[pl.BlockSpec((ROPE_HALF, ct), lambda i: (0, i))] * 2,
        out_shape=[jax.ShapeDtypeStruct((ROPE_HALF, T), jnp.float32)] * 2,
        compiler_params=_cparams("parallel"),
        name="rope_tables",
    )(positions.reshape(1, T), inv_freq.reshape(ROPE_HALF, 1))


def _gather_rows(src_hbm, idx_ref, idx_next_ref, buf_ref, sems, step, n_steps, n_rows):
    slot = step % 2
    chunks = buf_ref.shape[1] // n_rows

    def request(ref, to_slot):
        def issue(r, carry):
            src = pl.multiple_of(ref[0, 0, r] * chunks, chunks)
            dst = pl.multiple_of(r * chunks, chunks)
            pltpu.make_async_copy(src_hbm.at[pl.ds(src, chunks)], buf_ref.at[to_slot, pl.ds(dst, chunks)],
                                  sems.at[to_slot]).start()
            return carry
        lax.fori_loop(0, n_rows, issue, 0, unroll=ROW_DMA_UNROLL)

    @pl.when(step == 0)
    def _():
        request(idx_ref, slot)

    @pl.when(step + 1 < n_steps)
    def _():
        request(idx_next_ref, 1 - slot)

    pltpu.make_async_copy(src_hbm.at[pl.ds(0, n_rows * chunks)], buf_ref.at[slot], sems.at[slot]).wait()
    return jnp.concatenate([buf_ref[slot, pl.ds(c, n_rows, stride=chunks), :] for c in range(chunks)], axis=1)


def _pre_attn_kernel(*refs, has_moe_in, tt):
    if has_moe_in:
        dest_ref, dest_next_ref, h_ref, ys_hbm = refs[:4]
        refs = refs[4:]
    else:
        h_ref = refs[0]
        refs = refs[1:]
    (anorm_ref, wrow_ref, wfq_ref, wfv_ref, bf_ref, tri_ref, place_ref, qnorm_ref, wqm_ref, wqs_ref,
     kvnorm_ref, wkn_ref, wv_ref, cos_ref, sin_ref) = refs[:15]
    refs = refs[15:]
    if has_moe_in:
        hout_ref = refs[0]
        refs = refs[1:]
    fq_ref, fk_ref, fv_ref, qm_ref, km_ref, vm_ref = refs[:6]
    refs = refs[6:]
    carry_ref = refs[0]
    if has_moe_in:
        ybuf_ref, sems = refs[1:3]

    si = pl.program_id(1)
    bf16 = jnp.bfloat16
    half = LANES // 2

    @pl.when(si == 0)
    def _():
        carry_ref[...] = jnp.zeros_like(carry_ref)

    h = h_ref[...]
    if has_moe_in:
        step = pl.program_id(0) * pl.num_programs(1) + si
        n_steps = pl.num_programs(0) * pl.num_programs(1)
        h = h + _gather_rows(ys_hbm, dest_ref, dest_next_ref, ybuf_ref, sems, step, n_steps, tt)
        hout_ref[...] = h

    xn = _rms(h, anorm_ref[...]).astype(bf16)
    lane = lax.broadcasted_iota(jnp.int32, (tt, LANES), 1)
    sub = lax.broadcasted_iota(jnp.int32, (BF16_SUBLANES, tt), 0)
    ones_rows = jnp.ones((BF16_SUBLANES, tt), bf16)

    fw = FOX_WIDTH
    o = fw
    z = _dot(xn, wrow_ref[:, o:o + LANES]) + bf_ref[...]
    o += LANES
    cq_raw = _dot(xn, wrow_ref[:, o:o + Q_LORA_RANK])
    o += Q_LORA_RANK
    ckv_raw = _dot(xn, wrow_ref[:, o:o + KV_LORA_RANK])
    o += KV_LORA_RANK
    kr_raw = _dot(xn, wrow_ref[:, o:o + LANES])

    fq_t = _dot_nt(wfq_ref[...], xn) * (FOX_HEAD_DIM ** -0.5 * LOG2E)
    q_aug = jnp.where(sub < DECAY_PARTS, -1.0, 0.0).astype(bf16)
    q_zero = jnp.zeros((HEAD_PAD - FOX_HEAD_DIM - BF16_SUBLANES, tt), bf16)
    for hd in range(FOX_HEADS):
        fq_ref[0, hd, 0, 0:FOX_HEAD_DIM, :] = fq_t[hd * FOX_HEAD_DIM:(hd + 1) * FOX_HEAD_DIM].astype(bf16)
        fq_ref[0, hd, 0, FOX_HEAD_DIM:FOX_HEAD_DIM + BF16_SUBLANES, :] = q_aug
        fq_ref[0, hd, 0, FOX_HEAD_DIM + BF16_SUBLANES:, :] = q_zero
    fv_t = _dot_nt(wfv_ref[...], xn)
    vd = MLA_V_DIM
    for hd in range(FOX_HEADS):
        fv_ref[0, hd, 0, 0:vd, :] = fv_t[hd * vd:(hd + 1) * vd].astype(bf16)
        fv_ref[0, hd, 0, vd:, :] = ones_rows

    logf = jnp.minimum(z, 0.0) - jnp.log1p(jnp.exp(-jnp.abs(z)))
    pieces = jnp.concatenate([p_.astype(bf16) for p_ in _split_bf16(logf, DECAY_PARTS)], axis=1)
    csum = _dot(tri_ref[...], pieces)

    dcum = csum[:, 0:LANES] + csum[:, LANES:2 * LANES] + csum[:, 2 * LANES:] + carry_ref[...]
    carry_ref[...] = dcum[tt - 1:tt, :]
    d_parts = _split_bf16(dcum * LOG2E, DECAY_PARTS)
    d_sel = jnp.where(lane < FOX_HEADS, d_parts[0],
                      jnp.where(lane < 2 * FOX_HEADS, d_parts[1],
                                jnp.where(lane < 3 * FOX_HEADS, d_parts[2], 0.0)))
    k_aug = _dot(d_sel.astype(bf16), place_ref[...])

    fk = _dot(xn, wrow_ref[:, 0:fw])
    for p in range(FOX_HEADS // 2):
        blk = fk[:, p * LANES:(p + 1) * LANES]
        for hh, b_ in ((0, blk), (1, pltpu.roll(blk, half, axis=1))):
            sl = slice((2 * p + hh) * HEAD_PAD, (2 * p + hh + 1) * HEAD_PAD)
            fk_ref[:, sl] = jnp.where(lane < FOX_HEAD_DIM, b_, k_aug[:, sl]).astype(bf16)

    cq = _rms(cq_raw, qnorm_ref[...]).astype(bf16)
    qm_t = _dot_nt(wqm_ref[...], cq)
    qs_t = _dot_nt(wqs_ref[...], cq)
    cos_t = cos_ref[...]
    sin_t = sin_ref[...]
    scale = (MLA_NOPE_DIM + MLA_ROPE_DIM) ** -0.5 * LOG2E
    q_zero = jnp.zeros((HEAD_PAD - MLA_NOPE_DIM - MLA_ROPE_DIM, tt), jnp.float32)
    for hd in range(MLA_HEADS):
        blk = qm_t[hd * HEAD_PAD:(hd + 1) * HEAD_PAD]
        rows = [blk[0:MLA_NOPE_DIM]]
        for j in range(2):
            main = blk[MLA_NOPE_DIM + j * ROPE_HALF:MLA_NOPE_DIM + (j + 1) * ROPE_HALF]
            swap = qs_t[hd * MLA_ROPE_DIM + j * ROPE_HALF:hd * MLA_ROPE_DIM + (j + 1) * ROPE_HALF]
            rows.append(main * cos_t + swap * sin_t)
        rows.append(q_zero)
        qm_ref[0, hd, 0] = (jnp.concatenate(rows, axis=0) * scale).astype(bf16)

    ckv = _rms(ckv_raw, kvnorm_ref[...]).astype(bf16)
    mv_t = _dot_nt(wv_ref[...], ckv)
    for hd in range(MLA_HEADS):
        vm_ref[0, hd, 0, 0:vd, :] = mv_t[hd * vd:(hd + 1) * vd].astype(bf16)
        vm_ref[0, hd, 0, vd:, :] = ones_rows
    k_nope = _dot(ckv, wkn_ref[...])
    z_rows = jnp.zeros((MLA_ROPE_DIM, tt), jnp.float32)
    csk = jnp.concatenate([sin_t, sin_t, z_rows, cos_t, cos_t, z_rows], axis=0).T
    kr = kr_raw * csk
    kr = kr + pltpu.roll(kr, half, axis=1)
    k_pe = jnp.where(lane >= MLA_NOPE_DIM, kr, 0.0)
    for hd in range(MLA_HEADS):
        sl = slice(hd * HEAD_PAD, (hd + 1) * HEAD_PAD)
        km_ref[:, sl] = (k_nope[:, sl] + k_pe).astype(bf16)


def _pre_attn(h, moe_in, B, S, lw, tabs):
    T, D = h.shape
    tt = min(ROW_TILE, S)
    ns = S // tt
    has_moe_in = moe_in is not None
    row = lambda b, s: (b * ns + s, 0)
    const = lambda b, s: (0, 0)
    cos_t, sin_t = tabs

    def rows(width):
        return pl.BlockSpec((tt, width), row)

    def full(a):
        return pl.BlockSpec(a.shape, const)

    weights = [lw["attn_norm"], lw["w_rows"], lw["w_fq_t"], lw["w_fv_t"], lw["b_f"], lw["tri_incl"],
               lw["place"], lw["q_norm"], lw["wq_main_t"], lw["wq_swap_t"], lw["kv_norm"],
               lw["wk_nope"], lw["wv_t"]]
    in_specs = [rows(D)]
    args = [h]
    if has_moe_in:
        dest3, ys = moe_in
        last = B * ns - 1
        in_specs = [pl.BlockSpec((1, 1, tt), lambda b, s: (b * ns + s, 0, 0), memory_space=pltpu.SMEM),
                    pl.BlockSpec((1, 1, tt), lambda b, s: (jnp.minimum(b * ns + s + 1, last), 0, 0),
                                 memory_space=pltpu.SMEM),
                    rows(D), pl.BlockSpec(memory_space=pl.ANY)]
        args = [dest3, dest3, h, ys]
    col = lambda b, s: (0, b * ns + s)
    in_specs += [full(w) for w in weights]
    in_specs += [pl.BlockSpec((ROPE_HALF, tt), col), pl.BlockSpec((ROPE_HALF, tt), col)]
    args += weights + [cos_t, sin_t]

    bf16 = jnp.bfloat16
    qt_shape = jax.ShapeDtypeStruct((B, FOX_HEADS, ns, HEAD_PAD, tt), bf16)
    qt_spec = pl.BlockSpec((1, FOX_HEADS, 1, HEAD_PAD, tt), lambda b, s: (b, 0, s, 0, 0))
    k_shape = jax.ShapeDtypeStruct((T, FOX_HEADS * HEAD_PAD), bf16)
    k_spec = rows(FOX_HEADS * HEAD_PAD)
    vt_shape = jax.ShapeDtypeStruct((B, FOX_HEADS, ns, V_ROWS, tt), bf16)
    vt_spec = pl.BlockSpec((1, FOX_HEADS, 1, V_ROWS, tt), lambda b, s: (b, 0, s, 0, 0))
    out_shape = [qt_shape, k_shape, vt_shape] * 2
    out_specs = [qt_spec, k_spec, vt_spec] * 2
    scratch = [pltpu.VMEM((1, LANES), jnp.float32)]
    if has_moe_in:
        out_shape = [jax.ShapeDtypeStruct((T, D), jnp.float32)] + out_shape
        out_specs = [rows(D)] + out_specs
        scratch += [pltpu.VMEM((2, tt * (D // LANES), LANES), jnp.float32), pltpu.SemaphoreType.DMA((2,))]

    outs = pl.pallas_call(
        functools.partial(_pre_attn_kernel, has_moe_in=has_moe_in, tt=tt),
        grid=(B, ns),
        in_specs=in_specs,
        out_specs=out_specs,
        out_shape=out_shape,
        scratch_shapes=scratch,
        compiler_params=_cparams("arbitrary", "arbitrary"),
        name="pre_attn",
    )(*args)
    if has_moe_in:
        return outs[0], outs[1:]
    return h, outs


def _attn_kernel(*refs, ta, nq, n_streams):
    ins, refs = refs[:4 * n_streams], refs[4 * n_streams:]
    o_refs, (s_scr, mc_scr, m_scr, acc_scr) = refs[:n_streams], refs[n_streams:]
    qt_refs, k_refs, vt_refs, g_refs = ins[0::4], ins[1::4], ins[2::4], ins[3::4]
    heads = [(st, hh) for st in range(n_streams) for hh in range(2)]
    vd = MLA_V_DIM

    def produce_head(hi, q, kt, slot, masked):
        st, hh = heads[hi]
        k0 = pl.multiple_of(kt * ta, ta)
        kk = k_refs[st][pl.ds(k0, ta), hh * HEAD_PAD:(hh + 1) * HEAD_PAD]
        qt = qt_refs[st][0, hh, q]
        if not masked:
            s = _dot(kk, qt)
            s_scr[slot, hi] = s
            mc_scr[slot, hi] = jnp.max(s, axis=0, keepdims=True)
            return
        hk = ta // 2
        top = _dot(kk[0:hk], qt)
        low = _dot(kk[hk:], qt[:, hk:])
        top = jnp.where(lax.broadcasted_iota(jnp.int32, top.shape, 0)
                        <= lax.broadcasted_iota(jnp.int32, top.shape, 1), top, NEG_BIG)
        low = jnp.where(lax.broadcasted_iota(jnp.int32, low.shape, 0)
                        <= lax.broadcasted_iota(jnp.int32, low.shape, 1), low, NEG_BIG)
        s_scr[slot, hi, 0:hk, :] = top
        s_scr[slot, hi, hk:, 0:hk] = jnp.full((hk, hk), NEG_BIG, jnp.float32)
        s_scr[slot, hi, hk:, hk:] = low
        top_max = jnp.max(top, axis=0, keepdims=True)
        mc_scr[slot, hi, :, 0:hk] = top_max[:, 0:hk]
        mc_scr[slot, hi, :, hk:] = jnp.maximum(top_max[:, hk:], jnp.max(low, axis=0, keepdims=True))

    def consume_head(hi, kt, slot):
        st, hh = heads[hi]
        m_prev = m_scr[hi]
        m_new = jnp.maximum(m_prev, mc_scr[slot, hi])
        alpha = jnp.exp2(m_prev - m_new)
        p = jnp.exp2(s_scr[slot, hi] - m_new).astype(jnp.bfloat16)
        acc_scr[hi] = alpha * acc_scr[hi] + _dot(vt_refs[st][0, hh, kt], p)
        m_scr[hi] = m_new

    def produce(q, kt, slot, masked):
        for hi in range(len(heads)):
            produce_head(hi, q, kt, slot, masked)

    def consume(kt, slot):
        for hi in range(len(heads)):
            consume_head(hi, kt, slot)

    def produce_consume(produce_args, consume_args):
        for hi in range(len(heads)):
            produce_head(hi, *produce_args)
            consume_head(hi, *consume_args)

    def reset():
        m_scr[...] = jnp.full_like(m_scr, NEG_BIG)
        acc_scr[...] = jnp.zeros_like(acc_scr)

    def finish(q):
        for st in range(n_streams):
            outs = []
            for hh in range(2):
                acc = acc_scr[2 * st + hh]
                oh = acc[0:vd] / acc[vd:vd + 1]
                ms = jnp.mean(oh * oh, axis=0, keepdims=True)
                outs.append(oh * lax.rsqrt(ms + NORM_EPS))
            o_t = jnp.concatenate(outs, axis=0) * g_refs[st][...]
            o_refs[st][pl.ds(pl.multiple_of(q * ta, ta), ta), :] = o_t.T.astype(o_refs[st].dtype)

    reset()
    produce(0, 0, 0, True)

    def q_tile(n, carry):
        first = (n * (n + 1) // 2) % 2

        def step(j, slot):
            produce_consume((n, j, 1 - slot, False), (jnp.where(j == 0, n, j - 1), slot))

        for par in range(2):
            @pl.when(first == par)
            def _():
                def two_steps(u, c):
                    step(2 * u, par)
                    step(2 * u + 1, 1 - par)
                    return c

                lax.fori_loop(0, n // 2, two_steps, 0)

                @pl.when((n % 2 == 1) & (n < nq - 1))
                def _():
                    step(n - 1, par)
                    produce_consume((n + 1, n + 1, par, True), (n - 1, 1 - par))
                    finish(n)
                    reset()

                @pl.when((n % 2 == 0) & (n < nq - 1))
                def _():
                    produce_consume((n + 1, n + 1, 1 - par, True), (jnp.maximum(n - 1, 0), par))
                    finish(n)
                    reset()

        n_last = nq - 1
        first_last = (n_last * (n_last + 1) // 2) % 2

        @pl.when(n == n_last)
        def _():
            if n_last % 2 == 1:
                step(n - 1, first_last)
                consume(n - 1, 1 - first_last)
            else:
                consume(jnp.maximum(n - 1, 0), first_last)
            finish(n)

        return carry

    lax.fori_loop(0, nq, q_tile, 0)


def _attention(streams, B, S):
    T = streams[0][1].shape[0]
    ta = min(ATTN_TILE, S)
    nq = S // ta
    n_pairs = streams[0][0].shape[1] // 2
    n_streams = len(streams)
    n_heads = 2 * n_streams
    stream_specs = [pl.BlockSpec((1, 2, nq, HEAD_PAD, ta), lambda b, j: (b, j, 0, 0, 0)),
                    pl.BlockSpec((S, 2 * HEAD_PAD), lambda b, j: (b, j)),
                    pl.BlockSpec((1, 2, nq, V_ROWS, ta), lambda b, j: (b, j, 0, 0, 0)),
                    pl.BlockSpec((LANES, ta), lambda b, j: (j, 0))]
    return pl.pallas_call(
        functools.partial(_attn_kernel, ta=ta, nq=nq, n_streams=n_streams),
        grid=(B, n_pairs),
        in_specs=stream_specs * n_streams,
        out_specs=[pl.BlockSpec((S, LANES), lambda b, j: (b, j))] * n_streams,
        out_shape=[jax.ShapeDtypeStruct((T, n_pairs * LANES), jnp.bfloat16)] * n_streams,
        scratch_shapes=[pltpu.VMEM((2, n_heads, ta, ta), jnp.float32),
                        pltpu.VMEM((2, n_heads, 1, ta), jnp.float32),
                        pltpu.VMEM((n_heads, 1, ta), jnp.float32),
                        pltpu.VMEM((n_heads, V_ROWS, ta), jnp.float32)],
        compiler_params=_cparams("parallel", "parallel"),
        name="attention",
    )(*[a for stream in streams for a in stream])


def _post_attn_kernel(fox_ref, mla_ref, h_ref, wo_ref, fnorm_ref, wrh_ref, wrl_ref, tri_ref,
                      h2_ref, xr_ref, route_ref, cnt_ref, carry_ref, xn_scr):
    i = pl.program_id(0)
    D = h_ref.shape[1]

    @pl.when(i == 0)
    def _():
        xn_scr[...] = jnp.zeros_like(xn_scr)

    @pl.when(i <= 1)
    def _():
        carry_ref[...] = jnp.zeros_like(carry_ref)

    xn = xn_scr[...]
    xr_ref[:, 0:D] = xn
    h2 = h_ref[...] + _dot(fox_ref[...], wo_ref[0:FOX_WIDTH, :]) + _dot(mla_ref[...], wo_ref[FOX_WIDTH:, :])
    h2_ref[...] = h2
    xn_scr[...] = _rms(h2, fnorm_ref[...])

    x_hi = xn.astype(jnp.bfloat16)
    x_lo = (xn - x_hi.astype(jnp.float32)).astype(jnp.bfloat16)
    logits = _dot(x_hi, wrh_ref[...]) + _dot(x_lo, wrh_ref[...]) + _dot(x_hi, wrl_ref[...])

    lt = logits.T
    tt = lt.shape[1]
    ng = N_GROUPS
    gl = lt[0:ng]
    sub = lax.broadcasted_iota(jnp.int32, (ng, tt), 0).astype(jnp.float32)

    gmax = jnp.max(gl, axis=0, keepdims=True)
    group_w = 1.0 / jnp.sum(jnp.exp(gl - gmax), axis=0, keepdims=True)
    gid = jnp.min(jnp.where(gl == gmax, sub, float(ng)), axis=0, keepdims=True)
    onehot = sub == gid

    ins = [jnp.sum(jnp.where(onehot, lt[ng * (j + 1):ng * (j + 2)], 0.0), axis=0, keepdims=True)
           for j in range(EXPERTS_PER_GROUP)]

    def first_argmax(vals):
        top = functools.reduce(jnp.maximum, vals)
        idx = jnp.full_like(top, float(len(vals) - 1))
        for j in range(len(vals) - 2, -1, -1):
            idx = jnp.where(vals[j] == top, float(j), idx)
        return top, idx

    e1, i1 = first_argmax(ins)
    e2, i2 = first_argmax([jnp.where(i1 == j, NEG_BIG, v) for j, v in enumerate(ins)])
    t = jnp.exp(e2 - e1)
    g1 = group_w / (1.0 + t)
    g2 = group_w * t / (1.0 + t)
    gates = [jnp.where(i1 == j, g1, jnp.where(i2 == j, g2, 0.0)) for j in range(EXPERTS_PER_GROUP)]

    ones = jnp.where(onehot, 1.0, 0.0)
    prefix = _dot(ones.astype(jnp.bfloat16), tri_ref[...]) + carry_ref[:, 0:1]
    rank = jnp.sum(jnp.where(onehot, prefix, 0.0), axis=0, keepdims=True)
    carry = carry_ref[...] + jnp.sum(ones, axis=1, keepdims=True)
    carry_ref[...] = carry
    cnt_ref[...] = carry

    pad_rows = jnp.zeros((ROUTE_ROWS - EXPERTS_PER_GROUP - 2, tt), jnp.float32)
    info_t = jnp.concatenate(gates + [gid, rank, pad_rows], axis=0)
    route_ref[...] = info_t
    info_pad = jnp.zeros((LANES - ROUTE_ROWS, tt), jnp.float32)
    xr_ref[:, D:D + LANES] = jnp.concatenate([info_t, info_pad], axis=0).T


def _post_attn(fox, mla, h, lw):
    T, D = h.shape
    tt = min(ROW_TILE, T)
    n = T // tt
    proj = lambda i: (jnp.minimum(i, n - 1), 0)
    routed = lambda i: (jnp.maximum(i - 1, 0), 0)
    const = lambda i: (0, 0)
    weights = [lw["w_o"], lw["ffn_norm"], lw["wr_hi"], lw["wr_lo"], lw["tri_before"]]
    return pl.pallas_call(
        _post_attn_kernel,
        grid=(n + 1,),
        in_specs=[pl.BlockSpec((tt, FOX_WIDTH), proj), pl.BlockSpec((tt, MLA_WIDTH), proj),
                  pl.BlockSpec((tt, D), proj)] + [pl.BlockSpec(w.shape, const) for w in weights],
        out_specs=[pl.BlockSpec((tt, D), proj), pl.BlockSpec((tt, D + LANES), routed),
                   pl.BlockSpec((ROUTE_ROWS, tt), lambda i: (0, jnp.maximum(i - 1, 0))),
                   pl.BlockSpec((N_GROUPS, LANES), const)],
        out_shape=[jax.ShapeDtypeStruct((T, D), jnp.float32),
                   jax.ShapeDtypeStruct((T, D + LANES), jnp.float32),
                   jax.ShapeDtypeStruct((ROUTE_ROWS, T), jnp.float32),
                   jax.ShapeDtypeStruct((N_GROUPS, LANES), jnp.float32)],
        scratch_shapes=[pltpu.VMEM((N_GROUPS, LANES), jnp.float32), pltpu.VMEM((tt, D), jnp.float32)],
        compiler_params=_cparams("arbitrary"),
        name="post_attn",
    )(fox, mla, h, *weights)


def _dispatch_kernel(fill_ref, dest_ref, xr_ref, xs_hbm, zero_ref, sem, fill_sem, *, tt, n_sorted):
    first = pl.program_id(0) == 0

    def fill_copies(act):
        for g in range(N_GROUPS):
            start, length = fill_ref[g], fill_ref[N_GROUPS + g]
            head = (-start) & (SUBLANES - 1)
            for r in range(SUBLANES - 1):
                @pl.when(r < head)
                def _(start=start, r=r):
                    act(pltpu.make_async_copy(zero_ref.at[pl.ds(0, 1)], xs_hbm.at[pl.ds(start + r, 1)], fill_sem))
            base, rest = start + head, length - head
            size = MOE_BLOCK // 2
            while size >= SUBLANES:
                @pl.when((rest & size) != 0)
                def _(base=base, rest=rest, size=size):
                    at = pl.multiple_of(base + (rest & ~(2 * size - 1)), SUBLANES)
                    act(pltpu.make_async_copy(zero_ref.at[pl.ds(0, size)], xs_hbm.at[pl.ds(at, size)], fill_sem))
                size //= 2
        used = fill_ref[2 * N_GROUPS]
        for blk in range(N_GROUPS):
            @pl.when(used + (blk + 1) * MOE_BLOCK <= n_sorted)
            def _(blk=blk):
                at = pl.multiple_of(used + blk * MOE_BLOCK, MOE_BLOCK)
                act(pltpu.make_async_copy(zero_ref, xs_hbm.at[pl.ds(at, MOE_BLOCK)], fill_sem))

    @pl.when(first)
    def _():
        zero_ref[...] = jnp.zeros_like(zero_ref)
        fill_copies(lambda cp: cp.start())

    def issue(r, carry):
        pltpu.make_async_copy(xr_ref.at[pl.ds(r, 1)], xs_hbm.at[pl.ds(dest_ref[0, 0, r], 1)], sem).start()
        return carry
    lax.fori_loop(0, tt, issue, 0, unroll=ROW_DMA_UNROLL)
    pltpu.make_async_copy(xr_ref, xs_hbm.at[pl.ds(0, tt)], sem).wait()

    @pl.when(first)
    def _():
        fill_copies(lambda cp: cp.wait())


def _dispatch(xr, dest3, fill, n_sorted):
    T, W = xr.shape
    tt = dest3.shape[2]
    grid_spec = pltpu.PrefetchScalarGridSpec(
        num_scalar_prefetch=1,
        grid=(T // tt,),
        in_specs=[pl.BlockSpec((1, 1, tt), lambda i, f: (i, 0, 0), memory_space=pltpu.SMEM),
                  pl.BlockSpec((tt, W), lambda i, f: (i, 0))],
        out_specs=pl.BlockSpec(memory_space=pl.ANY),
        scratch_shapes=[pltpu.VMEM((MOE_BLOCK, W), xr.dtype),
                        pltpu.SemaphoreType.DMA(()), pltpu.SemaphoreType.DMA(())],
    )
    return pl.pallas_call(
        functools.partial(_dispatch_kernel, tt=tt, n_sorted=n_sorted),
        grid_spec=grid_spec,
        out_shape=jax.ShapeDtypeStruct((n_sorted, W), xr.dtype),
        compiler_params=_cparams("arbitrary"),
        name="moe_dispatch",
    )(fill, dest3, xr)


def _moe_kernel(bg_ref, nv_ref, xs_ref, wg32_ref, wu32_ref, wd32_ref, ys_ref, wg_ref, wu_ref, wd_ref):
    D = xs_ref.shape[1] - LANES
    rows = xs_ref.shape[0]
    chunks = D // LANES
    b = pl.program_id(0)
    g = bg_ref[b]
    n_valid = nv_ref[b]

    @pl.when((b == 0) | (g != bg_ref[jnp.maximum(b - 1, 0)]))
    def _():
        wg_ref[...] = wg32_ref[...].astype(jnp.bfloat16)
        wu_ref[...] = wu32_ref[...].astype(jnp.bfloat16)
        wd_ref[...] = wd32_ref[...].astype(jnp.bfloat16)

    @pl.when(n_valid == 0)
    def _():
        ys_ref[...] = jnp.zeros_like(ys_ref)

    @pl.when(n_valid > 0)
    def _():
        x = xs_ref[:, 0:D].astype(jnp.bfloat16)
        info = xs_ref[:, D:D + LANES]
        y = jnp.zeros((rows, D), jnp.float32)
        for j in range(EXPERTS_PER_GROUP):
            a = _dot(x, wg_ref[j])
            u = _dot(x, wu_ref[j])
            hj = (a * jax.nn.sigmoid(a) * u).astype(jnp.bfloat16)
            y = y + info[:, j:j + 1] * _dot(hj, wd_ref[j])
        for c in range(chunks):
            ys_ref[pl.ds(c, rows, stride=chunks), :] = y[:, c * LANES:(c + 1) * LANES]


def _moe(xs, block_group, block_valid, layer, w_gate, w_up, w_down):
    n_sorted, W = xs.shape
    D = W - LANES
    nb = n_sorted // MOE_BLOCK
    epg = EXPERTS_PER_GROUP
    chunks = D // LANES
    grid_spec = pltpu.PrefetchScalarGridSpec(
        num_scalar_prefetch=2,
        grid=(nb,),
        in_specs=[pl.BlockSpec((MOE_BLOCK, W), lambda b, bg, nv: (b, 0)),
                  pl.BlockSpec((None, epg, D, D_EXPERT), lambda b, bg, nv: (layer, bg[b], 0, 0)),
                  pl.BlockSpec((None, epg, D, D_EXPERT), lambda b, bg, nv: (layer, bg[b], 0, 0)),
                  pl.BlockSpec((None, epg, D_EXPERT, D), lambda b, bg, nv: (layer, bg[b], 0, 0))],
        out_specs=pl.BlockSpec((MOE_BLOCK * chunks, LANES), lambda b, bg, nv: (b, 0)),
        scratch_shapes=[pltpu.VMEM((epg, D, D_EXPERT), jnp.bfloat16),
                        pltpu.VMEM((epg, D, D_EXPERT), jnp.bfloat16),
                        pltpu.VMEM((epg, D_EXPERT, D), jnp.bfloat16)],
    )
    return pl.pallas_call(
        _moe_kernel,
        grid_spec=grid_spec,
        out_shape=jax.ShapeDtypeStruct((n_sorted * chunks, LANES), jnp.float32),
        compiler_params=_cparams("arbitrary"),
        name="moe_experts",
    )(block_group, block_valid, xs, w_gate, w_up, w_down)


def _final_kernel(dest_ref, dest_next_ref, h_ref, ys_hbm, gain_ref, o_ref, ybuf_ref, sems, *, tt):
    y = _gather_rows(ys_hbm, dest_ref, dest_next_ref, ybuf_ref, sems,
                     pl.program_id(0), pl.num_programs(0), tt)
    o_ref[...] = _rms(h_ref[...] + y, gain_ref[...])


def _final(h, dest3, ys, gain):
    T, D = h.shape
    tt = dest3.shape[2]
    last = T // tt - 1
    return pl.pallas_call(
        functools.partial(_final_kernel, tt=tt),
        grid=(T // tt,),
        in_specs=[pl.BlockSpec((1, 1, tt), lambda i: (i, 0, 0), memory_space=pltpu.SMEM),
                  pl.BlockSpec((1, 1, tt), lambda i: (jnp.minimum(i + 1, last), 0, 0), memory_space=pltpu.SMEM),
                  pl.BlockSpec((tt, D), lambda i: (i, 0)),
                  pl.BlockSpec(memory_space=pl.ANY),
                  pl.BlockSpec((1, D), lambda i: (0, 0))],
        out_specs=pl.BlockSpec((tt, D), lambda i: (i, 0)),
        out_shape=jax.ShapeDtypeStruct((T, D), jnp.float32),
        scratch_shapes=[pltpu.VMEM((2, tt * (D // LANES), LANES), jnp.float32), pltpu.SemaphoreType.DMA((2,))],
        compiler_params=_cparams("arbitrary"),
        name="final_norm",
    )(dest3, dest3, h, ys, gain)


def _prep_layer(l, tt, w_in, b_f, w_uq, w_ukv, w_o, w_group, w_router,
                attn_norm, q_norm, kv_norm, ffn_norm):
    bf16 = jnp.bfloat16
    D = w_in.shape[1]
    fw = FOX_WIDTH
    o_f = 3 * fw
    o_cq = o_f + FOX_HEADS
    o_ckv = o_cq + Q_LORA_RANK
    o_kr = o_ckv + KV_LORA_RANK
    wi = w_in[l]
    w_f = wi[:, o_f:o_cq]
    w_f3 = jnp.concatenate([w_f] * DECAY_PARTS + [jnp.zeros((D, LANES - DECAY_PARTS * FOX_HEADS), wi.dtype)], axis=1)
    b_f3 = jnp.concatenate([b_f[l]] * DECAY_PARTS + [jnp.zeros((LANES - DECAY_PARTS * FOX_HEADS,), b_f.dtype)])
    kr = wi[:, o_kr:o_kr + MLA_ROPE_DIM]
    kr_swap = jnp.concatenate([-kr[:, ROPE_HALF:], kr[:, :ROPE_HALF]], axis=1)
    zpad = jnp.zeros((D, MLA_ROPE_DIM), wi.dtype)
    w_rows = jnp.concatenate([wi[:, fw:2 * fw], w_f3, wi[:, o_cq:o_kr], kr_swap, zpad, kr, zpad], axis=1)

    src = jnp.arange(LANES)
    dst = (src % FOX_HEADS) * HEAD_PAD + FOX_HEAD_DIM + src // FOX_HEADS
    place = ((jnp.arange(FOX_HEADS * HEAD_PAD)[None, :] == dst[:, None])
             & (src[:, None] < DECAY_PARTS * FOX_HEADS)).astype(bf16)

    qd = MLA_NOPE_DIM + MLA_ROPE_DIM
    wq = w_uq[l].reshape(Q_LORA_RANK, MLA_HEADS, qd)
    zq = jnp.zeros((Q_LORA_RANK, MLA_HEADS, HEAD_PAD - qd), wq.dtype)
    wq_main = jnp.concatenate([wq, zq], axis=2).reshape(Q_LORA_RANK, MLA_HEADS * HEAD_PAD)
    x1 = wq[:, :, MLA_NOPE_DIM:MLA_NOPE_DIM + ROPE_HALF]
    x2 = wq[:, :, MLA_NOPE_DIM + ROPE_HALF:]
    wq_swap = jnp.concatenate([-x2, x1], axis=2).reshape(Q_LORA_RANK, MLA_HEADS * MLA_ROPE_DIM)

    wkv = w_ukv[l].reshape(KV_LORA_RANK, MLA_HEADS, MLA_NOPE_DIM + MLA_V_DIM)
    wk_nope = jnp.concatenate([wkv[:, :, :MLA_NOPE_DIM],
                               jnp.zeros((KV_LORA_RANK, MLA_HEADS, HEAD_PAD - MLA_NOPE_DIM), wkv.dtype)], axis=2)
    wv = wkv[:, :, MLA_NOPE_DIM:].reshape(KV_LORA_RANK, MLA_WIDTH)

    w_exp = w_router[l].reshape(D, N_GROUPS, EXPERTS_PER_GROUP).transpose(0, 2, 1).reshape(D, N_EXPERTS)
    wr = jnp.concatenate([w_group[l], w_exp,
                          jnp.zeros((D, LANES - N_GROUPS - N_EXPERTS), w_group.dtype)], axis=1)
    wr_hi = wr.astype(bf16)
    wr_lo = (wr - wr_hi.astype(jnp.float32)).astype(bf16)

    ar = jnp.arange(tt)
    return dict(
        attn_norm=attn_norm[l].reshape(1, D),
        w_rows=w_rows.astype(bf16),
        w_fq_t=wi[:, 0:fw].T.astype(bf16),
        w_fv_t=wi[:, 2 * fw:3 * fw].T.astype(bf16),
        b_f=b_f3.reshape(1, LANES),
        tri_incl=(ar[:, None] >= ar[None, :]).astype(bf16),
        tri_before=(ar[:, None] < ar[None, :]).astype(bf16),
        place=place,
        q_norm=q_norm[l].reshape(1, Q_LORA_RANK),
        wq_main_t=wq_main.T.astype(bf16),
        wq_swap_t=wq_swap.T.astype(bf16),
        kv_norm=kv_norm[l].reshape(1, KV_LORA_RANK),
        wk_nope=wk_nope.reshape(KV_LORA_RANK, MLA_HEADS * HEAD_PAD).astype(bf16),
        wv_t=wv.T.astype(bf16),
        w_o=w_o[l].astype(bf16),
        ffn_norm=ffn_norm[l].reshape(1, D),
        wr_hi=wr_hi,
        wr_lo=wr_lo,
    )


def _routing_plan(route, counts, n_blocks):
    gid = route[GID_INDEX].astype(jnp.int32)
    rank = route[RANK_INDEX].astype(jnp.int32)
    cnt = counts[:, 0].astype(jnp.int32)
    padded = (cnt + MOE_BLOCK - 1) // MOE_BLOCK * MOE_BLOCK
    pend = jnp.cumsum(padded)
    pstart = pend - padded
    dest = pstart[gid] + rank
    block_start = jnp.arange(n_blocks, dtype=jnp.int32) * MOE_BLOCK
    block_group = jnp.sum((pend[None, :] <= block_start[:, None]).astype(jnp.int32), axis=1)
    block_group = jnp.minimum(block_group, N_GROUPS - 1)
    block_valid = jnp.clip((pstart + cnt)[block_group] - block_start, 0, MOE_BLOCK)
    fill = jnp.concatenate([pstart + cnt, padded - cnt, pend[-1:]])
    return dest, block_group, block_valid, fill


def kernel(x, positions, attn_norm, w_in, b_f, q_norm, w_uq, kv_norm, w_ukv, fox_out_norm, mla_out_norm, w_o,
           ffn_norm, w_group, w_router, w_gate, w_up, w_down, final_norm):
    B, S, D = x.shape
    T = B * S
    depth = w_in.shape[0]
    ta = min(ATTN_TILE, S)
    tt = min(ROW_TILE, S)
    n_blocks = -(-(T + N_GROUPS * (MOE_BLOCK - 1)) // MOE_BLOCK)

    cos_t, sin_t = _rope_tables(positions)
    tabs = (cos_t, sin_t)

    h = x.reshape(T, D)
    moe_in = None
    for l in range(depth):
        lw = _prep_layer(l, tt, w_in, b_f, w_uq, w_ukv, w_o, w_group, w_router,
                         attn_norm, q_norm, kv_norm, ffn_norm)
        h, (fq_t, fk, fv_t, mq_t, mk, mv_t) = _pre_attn(h, moe_in, B, S, lw, tabs)
        fox_gain = jnp.broadcast_to(fox_out_norm[l].reshape(FOX_WIDTH, 1), (FOX_WIDTH, ta))
        mla_gain = jnp.broadcast_to(mla_out_norm[l].reshape(MLA_WIDTH, 1), (MLA_WIDTH, ta))
        fox, mla = _attention([(fq_t, fk, fv_t, fox_gain), (mq_t, mk, mv_t, mla_gain)], B, S)
        h, xr, route, counts = _post_attn(fox, mla, h, lw)
        dest, block_group, block_valid, fill = _routing_plan(route, counts, n_blocks)
        dest3 = dest.reshape(T // tt, 1, tt)
        dt = DISPATCH_TILE if T % DISPATCH_TILE == 0 else tt
        xs = _dispatch(xr, dest.reshape(T // dt, 1, dt), fill, n_blocks * MOE_BLOCK)
        ys = _moe(xs, block_group, block_valid, l, w_gate, w_up, w_down)
        moe_in = (dest3, ys)
    out = _final(h, moe_in[0], moe_in[1], final_norm.reshape(1, D))
    return out.reshape(B, S, D)
```

```python
import functools
import math

import jax
import jax.numpy as jnp
from jax import lax
from jax.experimental import pallas as pl
from jax.experimental.pallas import tpu as pltpu

FOX_HEADS = 8
FOX_HEAD_DIM = 64
FOX_WIDTH = FOX_HEADS * FOX_HEAD_DIM
MLA_HEADS = 8
MLA_NOPE_DIM = 64
MLA_ROPE_DIM = 32
MLA_V_DIM = 64
MLA_WIDTH = MLA_HEADS * MLA_V_DIM
Q_LORA_RANK = 256
KV_LORA_RANK = 128
ROPE_THETA = 10000.0
N_GROUPS = 8
EXPERTS_PER_GROUP = 4
N_EXPERTS = N_GROUPS * EXPERTS_PER_GROUP
D_EXPERT = 256
NORM_EPS = 1e-6

LANES = 128
SUBLANES = 8
BF16_SUBLANES = 16
VMEM_LIMIT_BYTES = 56 * 1024 * 1024

ROW_TILE = 512
DISPATCH_TILE = 1024
ATTN_TILE = 512
MOE_BLOCK = 512
ROW_DMA_UNROLL = 32
HEAD_PAD = 128
V_ROWS = MLA_V_DIM + BF16_SUBLANES
ROPE_HALF = MLA_ROPE_DIM // 2
NEG_BIG = -1e30
LOG2E = math.log2(math.e)
DECAY_PARTS = 3

GID_INDEX = EXPERTS_PER_GROUP
RANK_INDEX = EXPERTS_PER_GROUP + 1
ROUTE_ROWS = SUBLANES


def _cparams(*semantics):
    return pltpu.CompilerParams(dimension_semantics=semantics, vmem_limit_bytes=VMEM_LIMIT_BYTES)


def _rms(x, gain):
    ms = jnp.mean(x * x, axis=-1, keepdims=True)
    return x * lax.rsqrt(ms + NORM_EPS) * gain


def _dot(a, b):
    return jnp.dot(a, b, preferred_element_type=jnp.float32)


def _dot_nt(a, b):
    return lax.dot_general(a, b, (((1,), (1,)), ((), ())), preferred_element_type=jnp.float32)


def _split_bf16(x, parts):
    out = []
    for _ in range(parts - 1):
        piece = x.astype(jnp.bfloat16).astype(jnp.float32)
        out.append(piece)
        x = x - piece
    out.append(x.astype(jnp.bfloat16).astype(jnp.float32))
    return out


def _rope_table_kernel(pos_ref, invf_ref, cos_ref, sin_ref):
    ang = pos_ref[...].astype(jnp.float32) * invf_ref[...]
    cos_ref[...] = jnp.cos(ang)
    sin_ref[...] = jnp.sin(ang)


def _rope_tables(positions):
    T = positions.size
    inv_freq = ROPE_THETA ** (-jnp.arange(ROPE_HALF, dtype=jnp.float32) / ROPE_HALF)
    ct = min(T, 4096)
    return pl.pallas_call(
        _rope_table_kernel,
        grid=(T // ct,),
        in_specs=[pl.BlockSpec((1, ct), lambda i: (0, i)),
                  pl.BlockSpec((ROPE_HALF, 1), lambda i: (0, 0))],
        out_specs=[pl.BlockSpec((ROPE_HALF, ct), lambda i: (0, i))] * 2,
        out_shape=[jax.ShapeDtypeStruct((ROPE_HALF, T), jnp.float32)] * 2,
        compiler_params=_cparams("parallel"),
        name="rope_tables",
    )(positions.reshape(1, T), inv_freq.reshape(ROPE_HALF, 1))


def _gather_rows(src_hbm, idx_ref, idx_next_ref, buf_ref, sems, step, n_steps, n_rows):
    slot = step % 2
    chunks = buf_ref.shape[1] // n_rows

    def request(ref, to_slot):
        def issue(r, carry):
            src = pl.multiple_of(ref[0, 0, r] * chunks, chunks)
            dst = pl.multiple_of(r * chunks, chunks)
            pltpu.make_async_copy(src_hbm.at[pl.ds(src, chunks)], buf_ref.at[to_slot, pl.ds(dst, chunks)],
                                  sems.at[to_slot]).start()
            return carry
        lax.fori_loop(0, n_rows, issue, 0, unroll=ROW_DMA_UNROLL)

    @pl.when(step == 0)
    def _():
        request(idx_ref, slot)

    @pl.when(step + 1 < n_steps)
    def _():
        request(idx_next_ref, 1 - slot)

    pltpu.make_async_copy(src_hbm.at[pl.ds(0, n_rows * chunks)], buf_ref.at[slot], sems.at[slot]).wait()
    return jnp.concatenate([buf_ref[slot, pl.ds(c, n_rows, stride=chunks), :] for c in range(chunks)], axis=1)


def _pre_attn_kernel(*refs, has_moe_in, tt):
    if has_moe_in:
        dest_ref, dest_next_ref, h_ref, ys_hbm = refs[:4]
        refs = refs[4:]
    else:
        h_ref = refs[0]
        refs = refs[1:]
    (anorm_ref, wrow_ref, wfq_ref, wfv_ref, bf_ref, tri_ref, place_ref, qnorm_ref, wqm_ref, wqs_ref,
     kvnorm_ref, wkn_ref, wv_ref, cos_ref, sin_ref) = refs[:15]
    refs = refs[15:]
    if has_moe_in:
        hout_ref = refs[0]
        refs = refs[1:]
    fq_ref, fk_ref, fv_ref, qm_ref, km_ref, vm_ref = refs[:6]
    refs = refs[6:]
    carry_ref = refs[0]
    if has_moe_in:
        ybuf_ref, sems = refs[1:3]

    si = pl.program_id(1)
    bf16 = jnp.bfloat16
    half = LANES // 2

    @pl.when(si == 0)
    def _():
        carry_ref[...] = jnp.zeros_like(carry_ref)

    h = h_ref[...]
    if has_moe_in:
        step = pl.program_id(0) * pl.num_programs(1) + si
        n_steps = pl.num_programs(0) * pl.num_programs(1)
        h = h + _gather_rows(ys_hbm, dest_ref, dest_next_ref, ybuf_ref, sems, step, n_steps, tt)
        hout_ref[...] = h

    xn = _rms(h, anorm_ref[...]).astype(bf16)
    lane = lax.broadcasted_iota(jnp.int32, (tt, LANES), 1)
    sub = lax.broadcasted_iota(jnp.int32, (BF16_SUBLANES, tt), 0)
    ones_rows = jnp.ones((BF16_SUBLANES, tt), bf16)

    fw = FOX_WIDTH
    o = fw
    z = _dot(xn, wrow_ref[:, o:o + LANES]) + bf_ref[...]
    o += LANES
    cq_raw = _dot(xn, wrow_ref[:, o:o + Q_LORA_RANK])
    o += Q_LORA_RANK
    ckv_raw = _dot(xn, wrow_ref[:, o:o + KV_LORA_RANK])
    o += KV_LORA_RANK
    kr_raw = _dot(xn, wrow_ref[:, o:o + LANES])

    fq_t = _dot_nt(wfq_ref[...], xn) * (FOX_HEAD_DIM ** -0.5 * LOG2E)
    q_aug = jnp.where(sub < DECAY_PARTS, -1.0, 0.0).astype(bf16)
    q_zero = jnp.zeros((HEAD_PAD - FOX_HEAD_DIM - BF16_SUBLANES, tt), bf16)
    for hd in range(FOX_HEADS):
        fq_ref[0, hd, 0, 0:FOX_HEAD_DIM, :] = fq_t[hd * FOX_HEAD_DIM:(hd + 1) * FOX_HEAD_DIM].astype(bf16)
        fq_ref[0, hd, 0, FOX_HEAD_DIM:FOX_HEAD_DIM + BF16_SUBLANES, :] = q_aug
        fq_ref[0, hd, 0, FOX_HEAD_DIM + BF16_SUBLANES:, :] = q_zero
    fv_t = _dot_nt(wfv_ref[...], xn)
    vd = MLA_V_DIM
    for hd in range(FOX_HEADS):
        fv_ref[0, hd, 0, 0:vd, :] = fv_t[hd * vd:(hd + 1) * vd].astype(bf16)
        fv_ref[0, hd, 0, vd:, :] = ones_rows

    logf = jnp.minimum(z, 0.0) - jnp.log1p(jnp.exp(-jnp.abs(z)))
    pieces = jnp.concatenate([p_.astype(bf16) for p_ in _split_bf16(logf, DECAY_PARTS)], axis=1)
    csum = _dot(tri_ref[...], pieces)

    dcum = csum[:, 0:LANES] + csum[:, LANES:2 * LANES] + csum[:, 2 * LANES:] + carry_ref[...]
    carry_ref[...] = dcum[tt - 1:tt, :]
    d_parts = _split_bf16(dcum * LOG2E, DECAY_PARTS)
    d_sel = jnp.where(lane < FOX_HEADS, d_parts[0],
                      jnp.where(lane < 2 * FOX_HEADS, d_parts[1],
                                jnp.where(lane < 3 * FOX_HEADS, d_parts[2], 0.0)))
    k_aug = _dot(d_sel.astype(bf16), place_ref[...])

    fk = _dot(xn, wrow_ref[:, 0:fw])
    for p in range(FOX_HEADS // 2):
        blk = fk[:, p * LANES:(p + 1) * LANES]
        for hh, b_ in ((0, blk), (1, pltpu.roll(blk, half, axis=1))):
            sl = slice((2 * p + hh) * HEAD_PAD, (2 * p + hh + 1) * HEAD_PAD)
            fk_ref[:, sl] = jnp.where(lane < FOX_HEAD_DIM, b_, k_aug[:, sl]).astype(bf16)

    cq = _rms(cq_raw, qnorm_ref[...]).astype(bf16)
    qm_t = _dot_nt(wqm_ref[...], cq)
    qs_t = _dot_nt(wqs_ref[...], cq)
    cos_t = cos_ref[...]
    sin_t = sin_ref[...]
    scale = (MLA_NOPE_DIM + MLA_ROPE_DIM) ** -0.5 * LOG2E
    q_zero = jnp.zeros((HEAD_PAD - MLA_NOPE_DIM - MLA_ROPE_DIM, tt), jnp.float32)
    for hd in range(MLA_HEADS):
        blk = qm_t[hd * HEAD_PAD:(hd + 1) * HEAD_PAD]
        rows = [blk[0:MLA_NOPE_DIM]]
        for j in range(2):
            main = blk[MLA_NOPE_DIM + j * ROPE_HALF:MLA_NOPE_DIM + (j + 1) * ROPE_HALF]
            swap = qs_t[hd * MLA_ROPE_DIM + j * ROPE_HALF:hd * MLA_ROPE_DIM + (j + 1) * ROPE_HALF]
            rows.append(main * cos_t + swap * sin_t)
        rows.append(q_zero)
        qm_ref[0, hd, 0] = (jnp.concatenate(rows, axis=0) * scale).astype(bf16)

    ckv = _rms(ckv_raw, kvnorm_ref[...]).astype(bf16)
    mv_t = _dot_nt(wv_ref[...], ckv)
    for hd in range(MLA_HEADS):
        vm_ref[0, hd, 0, 0:vd, :] = mv_t[hd * vd:(hd + 1) * vd].astype(bf16)
        vm_ref[0, hd, 0, vd:, :] = ones_rows
    k_nope = _dot(ckv, wkn_ref[...])
    z_rows = jnp.zeros((MLA_ROPE_DIM, tt), jnp.float32)
    csk = jnp.concatenate([sin_t, sin_t, z_rows, cos_t, cos_t, z_rows], axis=0).T
    kr = kr_raw * csk
    kr = kr + pltpu.roll(kr, half, axis=1)
    k_pe = jnp.where(lane >= MLA_NOPE_DIM, kr, 0.0)
    for hd in range(MLA_HEADS):
        sl = slice(hd * HEAD_PAD, (hd + 1) * HEAD_PAD)
        km_ref[:, sl] = (k_nope[:, sl] + k_pe).astype(bf16)


def _pre_attn(h, moe_in, B, S, lw, tabs):
    T, D = h.shape
    tt = min(ROW_TILE, S)
    ns = S // tt
    has_moe_in = moe_in is not None
    row = lambda b, s: (b * ns + s, 0)
    const = lambda b, s: (0, 0)
    cos_t, sin_t = tabs

    def rows(width):
        return pl.BlockSpec((tt, width), row)

    def full(a):
        return pl.BlockSpec(a.shape, const)

    weights = [lw["attn_norm"], lw["w_rows"], lw["w_fq_t"], lw["w_fv_t"], lw["b_f"], lw["tri_incl"],
               lw["place"], lw["q_norm"], lw["wq_main_t"], lw["wq_swap_t"], lw["kv_norm"],
               lw["wk_nope"], lw["wv_t"]]
    in_specs = [rows(D)]
    args = [h]
    if has_moe_in:
        dest3, ys = moe_in
        last = B * ns - 1
        in_specs = [pl.BlockSpec((1, 1, tt), lambda b, s: (b * ns + s, 0, 0), memory_space=pltpu.SMEM),
                    pl.BlockSpec((1, 1, tt), lambda b, s: (jnp.minimum(b * ns + s + 1, last), 0, 0),
                                 memory_space=pltpu.SMEM),
                    rows(D), pl.BlockSpec(memory_space=pl.ANY)]
        args = [dest3, dest3, h, ys]
    col = lambda b, s: (0, b * ns + s)
    in_specs += [full(w) for w in weights]
    in_specs += [pl.BlockSpec((ROPE_HALF, tt), col), pl.BlockSpec((ROPE_HALF, tt), col)]
    args += weights + [cos_t, sin_t]

    bf16 = jnp.bfloat16
    qt_shape = jax.ShapeDtypeStruct((B, FOX_HEADS, ns, HEAD_PAD, tt), bf16)
    qt_spec = pl.BlockSpec((1, FOX_HEADS, 1, HEAD_PAD, tt), lambda b, s: (b, 0, s, 0, 0))
    k_shape = jax.ShapeDtypeStruct((T, FOX_HEADS * HEAD_PAD), bf16)
    k_spec = rows(FOX_HEADS * HEAD_PAD)
    vt_shape = jax.ShapeDtypeStruct((B, FOX_HEADS, ns, V_ROWS, tt), bf16)
    vt_spec = pl.BlockSpec((1, FOX_HEADS, 1, V_ROWS, tt), lambda b, s: (b, 0, s, 0, 0))
    out_shape = [qt_shape, k_shape, vt_shape] * 2
    out_specs = [qt_spec, k_spec, vt_spec] * 2
    scratch = [pltpu.VMEM((1, LANES), jnp.float32)]
    if has_moe_in:
        out_shape = [jax.ShapeDtypeStruct((T, D), jnp.float32)] + out_shape
        out_specs = [rows(D)] + out_specs
        scratch += [pltpu.VMEM((2, tt * (D // LANES), LANES), jnp.float32), pltpu.SemaphoreType.DMA((2,))]

    outs = pl.pallas_call(
        functools.partial(_pre_attn_kernel, has_moe_in=has_moe_in, tt=tt),
        grid=(B, ns),
        in_specs=in_specs,
        out_specs=out_specs,
        out_shape=out_shape,
        scratch_shapes=scratch,
        compiler_params=_cparams("arbitrary", "arbitrary"),
        name="pre_attn",
    )(*args)
    if has_moe_in:
        return outs[0], outs[1:]
    return h, outs


def _attn_kernel(*refs, ta, nq, n_streams):
    ins, refs = refs[:4 * n_streams], refs[4 * n_streams:]
    o_refs, (s_scr, mc_scr, m_scr, acc_scr) = refs[:n_streams], refs[n_streams:]
    qt_refs, k_refs, vt_refs, g_refs = ins[0::4], ins[1::4], ins[2::4], ins[3::4]
    heads = [(st, hh) for st in range(n_streams) for hh in range(2)]
    vd = MLA_V_DIM

    def produce_head(hi, q, kt, slot, masked):
        st, hh = heads[hi]
        k0 = pl.multiple_of(kt * ta, ta)
        kk = k_refs[st][pl.ds(k0, ta), hh * HEAD_PAD:(hh + 1) * HEAD_PAD]
        qt = qt_refs[st][0, hh, q]
        if not masked:
            s = _dot(kk, qt)
            s_scr[slot, hi] = s
            mc_scr[slot, hi] = jnp.max(s, axis=0, keepdims=True)
            return
        hk = ta // 2
        top = _dot(kk[0:hk], qt)
        low = _dot(kk[hk:], qt[:, hk:])
        top = jnp.where(lax.broadcasted_iota(jnp.int32, top.shape, 0)
                        <= lax.broadcasted_iota(jnp.int32, top.shape, 1), top, NEG_BIG)
        low = jnp.where(lax.broadcasted_iota(jnp.int32, low.shape, 0)
                        <= lax.broadcasted_iota(jnp.int32, low.shape, 1), low, NEG_BIG)
        s_scr[slot, hi, 0:hk, :] = top
        s_scr[slot, hi, hk:, 0:hk] = jnp.full((hk, hk), NEG_BIG, jnp.float32)
        s_scr[slot, hi, hk:, hk:] = low
        top_max = jnp.max(top, axis=0, keepdims=True)
        mc_scr[slot, hi, :, 0:hk] = top_max[:, 0:hk]
        mc_scr[slot, hi, :, hk:] = jnp.maximum(top_max[:, hk:], jnp.max(low, axis=0, keepdims=True))

    def consume_head(hi, kt, slot):
        st, hh = heads[hi]
        m_prev = m_scr[hi]
        m_new = jnp.maximum(m_prev, mc_scr[slot, hi])
        alpha = jnp.exp2(m_prev - m_new)
        p = jnp.exp2(s_scr[slot, hi] - m_new).astype(jnp.bfloat16)
        acc_scr[hi] = alpha * acc_scr[hi] + _dot(vt_refs[st][0, hh, kt], p)
        m_scr[hi] = m_new

    def produce(q, kt, slot, masked):
        for hi in range(len(heads)):
            produce_head(hi, q, kt, slot, masked)

    def consume(kt, slot):
        for hi in range(len(heads)):
            consume_head(hi, kt, slot)

    def produce_consume(produce_args, consume_args):
        for hi in range(len(heads)):
            produce_head(hi, *produce_args)
            consume_head(hi, *consume_args)

    def reset():
        m_scr[...] = jnp.full_like(m_scr, NEG_BIG)
        acc_scr[...] = jnp.zeros_like(acc_scr)

    def finish(q):
        for st in range(n_streams):
            outs = []
            for hh in range(2):
                acc = acc_scr[2 * st + hh]
                oh = acc[0:vd] / acc[vd:vd + 1]
                ms = jnp.mean(oh * oh, axis=0, keepdims=True)
                outs.append(oh * lax.rsqrt(ms + NORM_EPS))
            o_t = jnp.concatenate(outs, axis=0) * g_refs[st][...]
            o_refs[st][pl.ds(pl.multiple_of(q * ta, ta), ta), :] = o_t.T.astype(o_refs[st].dtype)

    reset()
    produce(0, 0, 0, True)

    def q_tile(n, carry):
        first = (n * (n + 1) // 2) % 2

        def step(j, slot):
            produce_consume((n, j, 1 - slot, False), (jnp.where(j == 0, n, j - 1), slot))

        for par in range(2):
            @pl.when(first == par)
            def _():
                def two_steps(u, c):
                    step(2 * u, par)
                    step(2 * u + 1, 1 - par)
                    return c

                lax.fori_loop(0, n // 2, two_steps, 0)

                @pl.when((n % 2 == 1) & (n < nq - 1))
                def _():
                    step(n - 1, par)
                    produce_consume((n + 1, n + 1, par, True), (n - 1, 1 - par))
                    finish(n)
                    reset()

                @pl.when((n % 2 == 0) & (n < nq - 1))
                def _():
                    produce_consume((n + 1, n + 1, 1 - par, True), (jnp.maximum(n - 1, 0), par))
                    finish(n)
                    reset()

        n_last = nq - 1
        first_last = (n_last * (n_last + 1) // 2) % 2

        @pl.when(n == n_last)
        def _():
            if n_last % 2 == 1:
                step(n - 1, first_last)
                consume(n - 1, 1 - first_last)
            else:
                consume(jnp.maximum(n - 1, 0), first_last)
            finish(n)

        return carry

    lax.fori_loop(0, nq, q_tile, 0)


def _attention(streams, B, S):
    T = streams[0][1].shape[0]
    ta = min(ATTN_TILE, S)
    nq = S // ta
    n_pairs = streams[0][0].shape[1] // 2
    n_streams = len(streams)
    n_heads = 2 * n_streams
    stream_specs = [pl.BlockSpec((1, 2, nq, HEAD_PAD, ta), lambda b, j: (b, j, 0, 0, 0)),
                    pl.BlockSpec((S, 2 * HEAD_PAD), lambda b, j: (b, j)),
                    pl.BlockSpec((1, 2, nq, V_ROWS, ta), lambda b, j: (b, j, 0, 0, 0)),
                    pl.BlockSpec((LANES, ta), lambda b, j: (j, 0))]
    return pl.pallas_call(
        functools.partial(_attn_kernel, ta=ta, nq=nq, n_streams=n_streams),
        grid=(B, n_pairs),
        in_specs=stream_specs * n_streams,
        out_specs=[pl.BlockSpec((S, LANES), lambda b, j: (b, j))] * n_streams,
        out_shape=[jax.ShapeDtypeStruct((T, n_pairs * LANES), jnp.bfloat16)] * n_streams,
        scratch_shapes=[pltpu.VMEM((2, n_heads, ta, ta), jnp.float32),
                        pltpu.VMEM((2, n_heads, 1, ta), jnp.float32),
                        pltpu.VMEM((n_heads, 1, ta), jnp.float32),
                        pltpu.VMEM((n_heads, V_ROWS, ta), jnp.float32)],
        compiler_params=_cparams("parallel", "parallel"),
        name="attention",
    )(*[a for stream in streams for a in stream])


def _post_attn_kernel(fox_ref, mla_ref, h_ref, wo_ref, fnorm_ref, wrh_ref, wrl_ref, tri_ref,
                      h2_ref, xr_ref, route_ref, cnt_ref, carry_ref, xn_scr):
    i = pl.program_id(0)
    D = h_ref.shape[1]

    @pl.when(i == 0)
    def _():
        xn_scr[...] = jnp.zeros_like(xn_scr)

    @pl.when(i <= 1)
    def _():
        carry_ref[...] = jnp.zeros_like(carry_ref)

    xn = xn_scr[...]
    xr_ref[:, 0:D] = xn

    x_hi = xn.astype(jnp.bfloat16)
    x_lo = (xn - x_hi.astype(jnp.float32)).astype(jnp.bfloat16)
    logits = _dot(x_hi, wrh_ref[...]) + _dot(x_lo, wrh_ref[...]) + _dot(x_hi, wrl_ref[...])

    h2 = h_ref[...] + _dot(fox_ref[...], wo_ref[0:FOX_WIDTH, :]) + _dot(mla_ref[...], wo_ref[FOX_WIDTH:, :])
    h2_ref[...] = h2
    xn_scr[...] = _rms(h2, fnorm_ref[...])

    lt = logits.T
    tt = lt.shape[1]
    ng = N_GROUPS
    gl = lt[0:ng]
    sub = lax.broadcasted_iota(jnp.int32, (ng, tt), 0).astype(jnp.float32)

    gmax = jnp.max(gl, axis=0, keepdims=True)
    group_w = 1.0 / jnp.sum(jnp.exp(gl - gmax), axis=0, keepdims=True)
    gid = jnp.min(jnp.where(gl == gmax, sub, float(ng)), axis=0, keepdims=True)
    onehot = sub == gid

    ins = [jnp.sum(jnp.where(onehot, lt[ng * (j + 1):ng * (j + 2)], 0.0), axis=0, keepdims=True)
           for j in range(EXPERTS_PER_GROUP)]

    def first_argmax(vals):
        top = functools.reduce(jnp.maximum, vals)
        idx = jnp.full_like(top, float(len(vals) - 1))
        for j in range(len(vals) - 2, -1, -1):
            idx = jnp.where(vals[j] == top, float(j), idx)
        return top, idx

    e1, i1 = first_argmax(ins)
    e2, i2 = first_argmax([jnp.where(i1 == j, NEG_BIG, v) for j, v in enumerate(ins)])
    t = jnp.exp(e2 - e1)
    g1 = group_w / (1.0 + t)
    g2 = group_w * t / (1.0 + t)
    gates = [jnp.where(i1 == j, g1, jnp.where(i2 == j, g2, 0.0)) for j in range(EXPERTS_PER_GROUP)]

    ones = jnp.where(onehot, 1.0, 0.0)
    prefix = _dot(ones.astype(jnp.bfloat16), tri_ref[...]) + carry_ref[:, 0:1]
    rank = jnp.sum(jnp.where(onehot, prefix, 0.0), axis=0, keepdims=True)
    carry = carry_ref[...] + jnp.sum(ones, axis=1, keepdims=True)
    carry_ref[...] = carry
    cnt_ref[...] = carry

    pad_rows = jnp.zeros((ROUTE_ROWS - EXPERTS_PER_GROUP - 2, tt), jnp.float32)
    info_t = jnp.concatenate(gates + [gid, rank, pad_rows], axis=0)
    route_ref[...] = info_t
    info_pad = jnp.zeros((LANES - ROUTE_ROWS, tt), jnp.float32)
    xr_ref[:, D:D + LANES] = jnp.concatenate([info_t, info_pad], axis=0).T


def _post_attn(fox, mla, h, lw):
    T, D = h.shape
    tt = min(ROW_TILE, T)
    n = T // tt
    proj = lambda i: (jnp.minimum(i, n - 1), 0)
    routed = lambda i: (jnp.maximum(i - 1, 0), 0)
    const = lambda i: (0, 0)
    weights = [lw["w_o"], lw["ffn_norm"], lw["wr_hi"], lw["wr_lo"], lw["tri_before"]]
    return pl.pallas_call(
        _post_attn_kernel,
        grid=(n + 1,),
        in_specs=[pl.BlockSpec((tt, FOX_WIDTH), proj), pl.BlockSpec((tt, MLA_WIDTH), proj),
                  pl.BlockSpec((tt, D), proj)] + [pl.BlockSpec(w.shape, const) for w in weights],
        out_specs=[pl.BlockSpec((tt, D), proj), pl.BlockSpec((tt, D + LANES), routed),
                   pl.BlockSpec((ROUTE_ROWS, tt), lambda i: (0, jnp.maximum(i - 1, 0))),
                   pl.BlockSpec((N_GROUPS, LANES), const)],
        out_shape=[jax.ShapeDtypeStruct((T, D), jnp.float32),
                   jax.ShapeDtypeStruct((T, D + LANES), jnp.float32),
                   jax.ShapeDtypeStruct((ROUTE_ROWS, T), jnp.float32),
                   jax.ShapeDtypeStruct((N_GROUPS, LANES), jnp.float32)],
        scratch_shapes=[pltpu.VMEM((N_GROUPS, LANES), jnp.float32), pltpu.VMEM((tt, D), jnp.float32)],
        compiler_params=_cparams("arbitrary"),
        name="post_attn",
    )(fox, mla, h, *weights)


def _dispatch_kernel(fill_ref, dest_ref, xr_ref, xs_hbm, zero_ref, sem, fill_sem, *, tt, n_sorted):
    first = pl.program_id(0) == 0

    def fill_copies(act):
        for g in range(N_GROUPS):
            start, length = fill_ref[g], fill_ref[N_GROUPS + g]
            head = (-start) & (SUBLANES - 1)
            for r in range(SUBLANES - 1):
                @pl.when(r < head)
                def _(start=start, r=r):
                    act(pltpu.make_async_copy(zero_ref.at[pl.ds(0, 1)], xs_hbm.at[pl.ds(start + r, 1)], fill_sem))
            base, rest = start + head, length - head
            size = MOE_BLOCK // 2
            while size >= SUBLANES:
                @pl.when((rest & size) != 0)
                def _(base=base, rest=rest, size=size):
                    at = pl.multiple_of(base + (rest & ~(2 * size - 1)), SUBLANES)
                    act(pltpu.make_async_copy(zero_ref.at[pl.ds(0, size)], xs_hbm.at[pl.ds(at, size)], fill_sem))
                size //= 2
        used = fill_ref[2 * N_GROUPS]
        for blk in range(N_GROUPS):
            @pl.when(used + (blk + 1) * MOE_BLOCK <= n_sorted)
            def _(blk=blk):
                at = pl.multiple_of(used + blk * MOE_BLOCK, MOE_BLOCK)
                act(pltpu.make_async_copy(zero_ref, xs_hbm.at[pl.ds(at, MOE_BLOCK)], fill_sem))

    @pl.when(first)
    def _():
        zero_ref[...] = jnp.zeros_like(zero_ref)
        fill_copies(lambda cp: cp.start())

    def issue(r, carry):
        pltpu.make_async_copy(xr_ref.at[pl.ds(r, 1)], xs_hbm.at[pl.ds(dest_ref[0, 0, r], 1)], sem).start()
        return carry
    lax.fori_loop(0, tt, issue, 0, unroll=ROW_DMA_UNROLL)
    pltpu.make_async_copy(xr_ref, xs_hbm.at[pl.ds(0, tt)], sem).wait()

    @pl.when(first)
    def _():
        fill_copies(lambda cp: cp.wait())


def _dispatch(xr, dest3, fill, n_sorted):
    T, W = xr.shape
    tt = dest3.shape[2]
    grid_spec = pltpu.PrefetchScalarGridSpec(
        num_scalar_prefetch=1,
        grid=(T // tt,),
        in_specs=[pl.BlockSpec((1, 1, tt), lambda i, f: (i, 0, 0), memory_space=pltpu.SMEM),
                  pl.BlockSpec((tt, W), lambda i, f: (i, 0))],
        out_specs=pl.BlockSpec(memory_space=pl.ANY),
        scratch_shapes=[pltpu.VMEM((MOE_BLOCK, W), xr.dtype),
                        pltpu.SemaphoreType.DMA(()), pltpu.SemaphoreType.DMA(())],
    )
    return pl.pallas_call(
        functools.partial(_dispatch_kernel, tt=tt, n_sorted=n_sorted),
        grid_spec=grid_spec,
        out_shape=jax.ShapeDtypeStruct((n_sorted, W), xr.dtype),
        compiler_params=_cparams("arbitrary"),
        name="moe_dispatch",
    )(fill, dest3, xr)


def _moe_kernel(bg_ref, nv_ref, xs_ref, wg32_ref, wu32_ref, wd32_ref, ys_ref, wg_ref, wu_ref, wd_ref):
    D = xs_ref.shape[1] - LANES
    rows = xs_ref.shape[0]
    chunks = D // LANES
    b = pl.program_id(0)
    g = bg_ref[b]
    n_valid = nv_ref[b]

    @pl.when((b == 0) | (g != bg_ref[jnp.maximum(b - 1, 0)]))
    def _():
        wg_ref[...] = wg32_ref[...].astype(jnp.bfloat16)
        wu_ref[...] = wu32_ref[...].astype(jnp.bfloat16)
        wd_ref[...] = wd32_ref[...].astype(jnp.bfloat16)

    @pl.when(n_valid == 0)
    def _():
        ys_ref[...] = jnp.zeros_like(ys_ref)

    @pl.when(n_valid > 0)
    def _():
        x = xs_ref[:, 0:D].astype(jnp.bfloat16)
        info = xs_ref[:, D:D + LANES]
        y = jnp.zeros((rows, D), jnp.float32)
        for j in range(EXPERTS_PER_GROUP):
            a = _dot(x, wg_ref[j])
            u = _dot(x, wu_ref[j])
            hj = (a * jax.nn.sigmoid(a) * u).astype(jnp.bfloat16)
            y = y + info[:, j:j + 1] * _dot(hj, wd_ref[j])
        for c in range(chunks):
            ys_ref[pl.ds(c, rows, stride=chunks), :] = y[:, c * LANES:(c + 1) * LANES]


def _moe(xs, block_group, block_valid, layer, w_gate, w_up, w_down):
    n_sorted, W = xs.shape
    D = W - LANES
    nb = n_sorted // MOE_BLOCK
    epg = EXPERTS_PER_GROUP
    chunks = D // LANES
    grid_spec = pltpu.PrefetchScalarGridSpec(
        num_scalar_prefetch=2,
        grid=(nb,),
        in_specs=[pl.BlockSpec((MOE_BLOCK, W), lambda b, bg, nv: (b, 0)),
                  pl.BlockSpec((None, epg, D, D_EXPERT), lambda b, bg, nv: (layer, bg[b], 0, 0)),
                  pl.BlockSpec((None, epg, D, D_EXPERT), lambda b, bg, nv: (layer, bg[b], 0, 0)),
                  pl.BlockSpec((None, epg, D_EXPERT, D), lambda b, bg, nv: (layer, bg[b], 0, 0))],
        out_specs=pl.BlockSpec((MOE_BLOCK * chunks, LANES), lambda b, bg, nv: (b, 0)),
        scratch_shapes=[pltpu.VMEM((epg, D, D_EXPERT), jnp.bfloat16),
                        pltpu.VMEM((epg, D, D_EXPERT), jnp.bfloat16),
                        pltpu.VMEM((epg, D_EXPERT, D), jnp.bfloat16)],
    )
    return pl.pallas_call(
        _moe_kernel,
        grid_spec=grid_spec,
        out_shape=jax.ShapeDtypeStruct((n_sorted * chunks, LANES), jnp.float32),
        compiler_params=_cparams("arbitrary"),
        name="moe_experts",
    )(block_group, block_valid, xs, w_gate, w_up, w_down)


def _final_kernel(dest_ref, dest_next_ref, h_ref, ys_hbm, gain_ref, o_ref, ybuf_ref, sems, *, tt):
    y = _gather_rows(ys_hbm, dest_ref, dest_next_ref, ybuf_ref, sems,
                     pl.program_id(0), pl.num_programs(0), tt)
    o_ref[...] = _rms(h_ref[...] + y, gain_ref[...])


def _final(h, dest3, ys, gain):
    T, D = h.shape
    tt = dest3.shape[2]
    last = T // tt - 1
    return pl.pallas_call(
        functools.partial(_final_kernel, tt=tt),
        grid=(T // tt,),
        in_specs=[pl.BlockSpec((1, 1, tt), lambda i: (i, 0, 0), memory_space=pltpu.SMEM),
                  pl.BlockSpec((1, 1, tt), lambda i: (jnp.minimum(i + 1, last), 0, 0), memory_space=pltpu.SMEM),
                  pl.BlockSpec((tt, D), lambda i: (i, 0)),
                  pl.BlockSpec(memory_space=pl.ANY),
                  pl.BlockSpec((1, D), lambda i: (0, 0))],
        out_specs=pl.BlockSpec((tt, D), lambda i: (i, 0)),
        out_shape=jax.ShapeDtypeStruct((T, D), jnp.float32),
        scratch_shapes=[pltpu.VMEM((2, tt * (D // LANES), LANES), jnp.float32), pltpu.SemaphoreType.DMA((2,))],
        compiler_params=_cparams("arbitrary"),
        name="final_norm",
    )(dest3, dest3, h, ys, gain)


def _prep_layer(l, tt, w_in, b_f, w_uq, w_ukv, w_o, w_group, w_router,
                attn_norm, q_norm, kv_norm, ffn_norm):
    bf16 = jnp.bfloat16
    D = w_in.shape[1]
    fw = FOX_WIDTH
    o_f = 3 * fw
    o_cq = o_f + FOX_HEADS
    o_ckv = o_cq + Q_LORA_RANK
    o_kr = o_ckv + KV_LORA_RANK
    wi = w_in[l]
    w_f = wi[:, o_f:o_cq]
    w_f3 = jnp.concatenate([w_f] * DECAY_PARTS + [jnp.zeros((D, LANES - DECAY_PARTS * FOX_HEADS), wi.dtype)], axis=1)
    b_f3 = jnp.concatenate([b_f[l]] * DECAY_PARTS + [jnp.zeros((LANES - DECAY_PARTS * FOX_HEADS,), b_f.dtype)])
    kr = wi[:, o_kr:o_kr + MLA_ROPE_DIM]
    kr_swap = jnp.concatenate([-kr[:, ROPE_HALF:], kr[:, :ROPE_HALF]], axis=1)
    zpad = jnp.zeros((D, MLA_ROPE_DIM), wi.dtype)
    w_rows = jnp.concatenate([wi[:, fw:2 * fw], w_f3, wi[:, o_cq:o_kr], kr_swap, zpad, kr, zpad], axis=1)

    src = jnp.arange(LANES)
    dst = (src % FOX_HEADS) * HEAD_PAD + FOX_HEAD_DIM + src // FOX_HEADS
    place = ((jnp.arange(FOX_HEADS * HEAD_PAD)[None, :] == dst[:, None])
             & (src[:, None] < DECAY_PARTS * FOX_HEADS)).astype(bf16)

    qd = MLA_NOPE_DIM + MLA_ROPE_DIM
    wq = w_uq[l].reshape(Q_LORA_RANK, MLA_HEADS, qd)
    zq = jnp.zeros((Q_LORA_RANK, MLA_HEADS, HEAD_PAD - qd), wq.dtype)
    wq_main = jnp.concatenate([wq, zq], axis=2).reshape(Q_LORA_RANK, MLA_HEADS * HEAD_PAD)
    x1 = wq[:, :, MLA_NOPE_DIM:MLA_NOPE_DIM + ROPE_HALF]
    x2 = wq[:, :, MLA_NOPE_DIM + ROPE_HALF:]
    wq_swap = jnp.concatenate([-x2, x1], axis=2).reshape(Q_LORA_RANK, MLA_HEADS * MLA_ROPE_DIM)

    wkv = w_ukv[l].reshape(KV_LORA_RANK, MLA_HEADS, MLA_NOPE_DIM + MLA_V_DIM)
    wk_nope = jnp.concatenate([wkv[:, :, :MLA_NOPE_DIM],
                               jnp.zeros((KV_LORA_RANK, MLA_HEADS, HEAD_PAD - MLA_NOPE_DIM), wkv.dtype)], axis=2)
    wv = wkv[:, :, MLA_NOPE_DIM:].reshape(KV_LORA_RANK, MLA_WIDTH)

    w_exp = w_router[l].reshape(D, N_GROUPS, EXPERTS_PER_GROUP).transpose(0, 2, 1).reshape(D, N_EXPERTS)
    wr = jnp.concatenate([w_group[l], w_exp,
                          jnp.zeros((D, LANES - N_GROUPS - N_EXPERTS), w_group.dtype)], axis=1)
    wr_hi = wr.astype(bf16)
    wr_lo = (wr - wr_hi.astype(jnp.float32)).astype(bf16)

    ar = jnp.arange(tt)
    return dict(
        attn_norm=attn_norm[l].reshape(1, D),
        w_rows=w_rows.astype(bf16),
        w_fq_t=wi[:, 0:fw].T.astype(bf16),
        w_fv_t=wi[:, 2 * fw:3 * fw].T.astype(bf16),
        b_f=b_f3.reshape(1, LANES),
        tri_incl=(ar[:, None] >= ar[None, :]).astype(bf16),
        tri_before=(ar[:, None] < ar[None, :]).astype(bf16),
        place=place,
        q_norm=q_norm[l].reshape(1, Q_LORA_RANK),
        wq_main_t=wq_main.T.astype(bf16),
        wq_swap_t=wq_swap.T.astype(bf16),
        kv_norm=kv_norm[l].reshape(1, KV_LORA_RANK),
        wk_nope=wk_nope.reshape(KV_LORA_RANK, MLA_HEADS * HEAD_PAD).astype(bf16),
        wv_t=wv.T.astype(bf16),
        w_o=w_o[l].astype(bf16),
        ffn_norm=ffn_norm[l].reshape(1, D),
        wr_hi=wr_hi,
        wr_lo=wr_lo,
    )


def _routing_plan(route, counts, n_blocks):
    gid = route[GID_INDEX].astype(jnp.int32)
    rank = route[RANK_INDEX].astype(jnp.int32)
    cnt = counts[:, 0].astype(jnp.int32)
    padded = (cnt + MOE_BLOCK - 1) // MOE_BLOCK * MOE_BLOCK
    pend = jnp.cumsum(padded)
    pstart = pend - padded
    dest = pstart[gid] + rank
    block_start = jnp.arange(n_blocks, dtype=jnp.int32) * MOE_BLOCK
    block_group = jnp.sum((pend[None, :] <= block_start[:, None]).astype(jnp.int32), axis=1)
    block_group = jnp.minimum(block_group, N_GROUPS - 1)
    block_valid = jnp.clip((pstart + cnt)[block_group] - block_start, 0, MOE_BLOCK)
    fill = jnp.concatenate([pstart + cnt, padded - cnt, pend[-1:]])
    return dest, block_group, block_valid, fill


def kernel(x, positions, attn_norm, w_in, b_f, q_norm, w_uq, kv_norm, w_ukv, fox_out_norm, mla_out_norm, w_o,
           ffn_norm, w_group, w_router, w_gate, w_up, w_down, final_norm):
    B, S, D = x.shape
    T = B * S
    depth = w_in.shape[0]
    ta = min(ATTN_TILE, S)
    tt = min(ROW_TILE, S)
    n_blocks = -(-(T + N_GROUPS * (MOE_BLOCK - 1)) // MOE_BLOCK)

    cos_t, sin_t = _rope_tables(positions)
    tabs = (cos_t, sin_t)

    h = x.reshape(T, D)
    moe_in = None
    for l in range(depth):
        lw = _prep_layer(l, tt, w_in, b_f, w_uq, w_ukv, w_o, w_group, w_router,
                         attn_norm, q_norm, kv_norm, ffn_norm)
        h, (fq_t, fk, fv_t, mq_t, mk, mv_t) = _pre_attn(h, moe_in, B, S, lw, tabs)
        fox_gain = jnp.broadcast_to(fox_out_norm[l].reshape(FOX_WIDTH, 1), (FOX_WIDTH, ta))
        mla_gain = jnp.broadcast_to(mla_out_norm[l].reshape(MLA_WIDTH, 1), (MLA_WIDTH, ta))
        fox, mla = _attention([(fq_t, fk, fv_t, fox_gain), (mq_t, mk, mv_t, mla_gain)], B, S)
        h, xr, route, counts = _post_attn(fox, mla, h, lw)
        dest, block_group, block_valid, fill = _routing_plan(route, counts, n_blocks)
        dest3 = dest.reshape(T // tt, 1, tt)
        dt = DISPATCH_TILE if T % DISPATCH_TILE == 0 else tt
        xs = _dispatch(xr, dest.reshape(T // dt, 1, dt), fill, n_blocks * MOE_BLOCK)
        ys = _moe(xs, block_group, block_valid, l, w_gate, w_up, w_down)
        moe_in = (dest3, ys)
    out = _final(h, moe_in[0], moe_in[1], final_norm.reshape(1, D))
    return out.reshape(B, S, D)
```

```python
import functools
import math

import jax
import jax.numpy as jnp
from jax import lax
from jax.experimental import pallas as pl
from jax.experimental.pallas import tpu as pltpu

FOX_HEADS = 8
FOX_HEAD_DIM = 64
FOX_WIDTH = FOX_HEADS * FOX_HEAD_DIM
MLA_HEADS = 8
MLA_NOPE_DIM = 64
MLA_ROPE_DIM = 32
MLA_V_DIM = 64
MLA_WIDTH = MLA_HEADS * MLA_V_DIM
Q_LORA_RANK = 256
KV_LORA_RANK = 128
ROPE_THETA = 10000.0
N_GROUPS = 8
EXPERTS_PER_GROUP = 4
N_EXPERTS = N_GROUPS * EXPERTS_PER_GROUP
D_EXPERT = 256
NORM_EPS = 1e-6

LANES = 128
SUBLANES = 8
BF16_SUBLANES = 16
VMEM_LIMIT_BYTES = 56 * 1024 * 1024

ROW_TILE = 512
DISPATCH_TILE = 2048
ATTN_TILE = 512
MOE_BLOCK = 512
ROW_DMA_UNROLL = 32
HEAD_PAD = 128
V_ROWS = MLA_V_DIM + BF16_SUBLANES
ROPE_HALF = MLA_ROPE_DIM // 2
NEG_BIG = -1e30
LOG2E = math.log2(math.e)
DECAY_PARTS = 3

GID_INDEX = EXPERTS_PER_GROUP
RANK_INDEX = EXPERTS_PER_GROUP + 1
ROUTE_ROWS = SUBLANES


def _cparams(*semantics):
    return pltpu.CompilerParams(dimension_semantics=semantics, vmem_limit_bytes=VMEM_LIMIT_BYTES)


def _rms(x, gain):
    ms = jnp.mean(x * x, axis=-1, keepdims=True)
    return x * lax.rsqrt(ms + NORM_EPS) * gain


def _dot(a, b):
    return jnp.dot(a, b, preferred_element_type=jnp.float32)


def _dot_nt(a, b):
    return lax.dot_general(a, b, (((1,), (1,)), ((), ())), preferred_element_type=jnp.float32)


def _split_bf16(x, parts):
    out = []
    for _ in range(parts - 1):
        piece = x.astype(jnp.bfloat16).astype(jnp.float32)
        out.append(piece)
        x = x - piece
    out.append(x.astype(jnp.bfloat16).astype(jnp.float32))
    return out


def _rope_table_kernel(pos_ref, invf_ref, cos_ref, sin_ref):
    ang = pos_ref[...].astype(jnp.float32) * invf_ref[...]
    cos_ref[...] = jnp.cos(ang)
    sin_ref[...] = jnp.sin(ang)


def _rope_tables(positions):
    T = positions.size
    inv_freq = ROPE_THETA ** (-jnp.arange(ROPE_HALF, dtype=jnp.float32) / ROPE_HALF)
    ct = min(T, 4096)
    return pl.pallas_call(
        _rope_table_kernel,
        grid=(T // ct,),
        in_specs=[pl.BlockSpec((1, ct), lambda i: (0, i)),
                  pl.BlockSpec((ROPE_HALF, 1), lambda i: (0, 0))],
        out_specs=[pl.BlockSpec((ROPE_HALF, ct), lambda i: (0, i))] * 2,
        out_shape=[jax.ShapeDtypeStruct((ROPE_HALF, T), jnp.float32)] * 2,
        compiler_params=_cparams("parallel"),
        name="rope_tables",
    )(positions.reshape(1, T), inv_freq.reshape(ROPE_HALF, 1))


def _gather_rows(src_hbm, idx_ref, idx_next_ref, buf_ref, sems, step, n_steps, n_rows):
    slot = step % 2
    chunks = buf_ref.shape[1] // n_rows

    def request(ref, to_slot):
        def issue(r, carry):
            src = pl.multiple_of(ref[0, 0, r] * chunks, chunks)
            dst = pl.multiple_of(r * chunks, chunks)
            pltpu.make_async_copy(src_hbm.at[pl.ds(src, chunks)], buf_ref.at[to_slot, pl.ds(dst, chunks)],
                                  sems.at[to_slot]).start()
            return carry
        lax.fori_loop(0, n_rows, issue, 0, unroll=ROW_DMA_UNROLL)

    @pl.when(step == 0)
    def _():
        request(idx_ref, slot)

    @pl.when(step + 1 < n_steps)
    def _():
        request(idx_next_ref, 1 - slot)

    pltpu.make_async_copy(src_hbm.at[pl.ds(0, n_rows * chunks)], buf_ref.at[slot], sems.at[slot]).wait()
    return jnp.concatenate([buf_ref[slot, pl.ds(c, n_rows, stride=chunks), :] for c in range(chunks)], axis=1)


def _pre_attn_kernel(*refs, has_moe_in, tt):
    if has_moe_in:
        dest_ref, dest_next_ref, h_ref, ys_hbm = refs[:4]
        refs = refs[4:]
    else:
        h_ref = refs[0]
        refs = refs[1:]
    (anorm_ref, wrow_ref, wfq_ref, wfv_ref, bf_ref, tri_ref, place_ref, qnorm_ref, wqm_ref, wqs_ref,
     kvnorm_ref, wkn_ref, wv_ref, cos_ref, sin_ref) = refs[:15]
    refs = refs[15:]
    if has_moe_in:
        hout_ref = refs[0]
        refs = refs[1:]
    fq_ref, fk_ref, fv_ref, qm_ref, km_ref, vm_ref = refs[:6]
    refs = refs[6:]
    carry_ref = refs[0]
    if has_moe_in:
        ybuf_ref, sems = refs[1:3]

    si = pl.program_id(1)
    bf16 = jnp.bfloat16
    half = LANES // 2

    @pl.when(si == 0)
    def _():
        carry_ref[...] = jnp.zeros_like(carry_ref)

    h = h_ref[...]
    if has_moe_in:
        step = pl.program_id(0) * pl.num_programs(1) + si
        n_steps = pl.num_programs(0) * pl.num_programs(1)
        h = h + _gather_rows(ys_hbm, dest_ref, dest_next_ref, ybuf_ref, sems, step, n_steps, tt)
        hout_ref[...] = h

    xn = _rms(h, anorm_ref[...]).astype(bf16)
    lane = lax.broadcasted_iota(jnp.int32, (tt, LANES), 1)
    sub = lax.broadcasted_iota(jnp.int32, (BF16_SUBLANES, tt), 0)
    ones_rows = jnp.ones((BF16_SUBLANES, tt), bf16)

    fw = FOX_WIDTH
    o = fw
    z = _dot(xn, wrow_ref[:, o:o + LANES]) + bf_ref[...]
    o += LANES
    cq_raw = _dot(xn, wrow_ref[:, o:o + Q_LORA_RANK])
    o += Q_LORA_RANK
    ckv_raw = _dot(xn, wrow_ref[:, o:o + KV_LORA_RANK])
    o += KV_LORA_RANK
    kr_raw = _dot(xn, wrow_ref[:, o:o + LANES])

    fq_t = _dot_nt(wfq_ref[...], xn) * (FOX_HEAD_DIM ** -0.5 * LOG2E)
    q_aug = jnp.where(sub < DECAY_PARTS, -1.0, 0.0).astype(bf16)
    q_zero = jnp.zeros((HEAD_PAD - FOX_HEAD_DIM - BF16_SUBLANES, tt), bf16)
    for hd in range(FOX_HEADS):
        fq_ref[0, hd, 0, 0:FOX_HEAD_DIM, :] = fq_t[hd * FOX_HEAD_DIM:(hd + 1) * FOX_HEAD_DIM].astype(bf16)
        fq_ref[0, hd, 0, FOX_HEAD_DIM:FOX_HEAD_DIM + BF16_SUBLANES, :] = q_aug
        fq_ref[0, hd, 0, FOX_HEAD_DIM + BF16_SUBLANES:, :] = q_zero
    fv_t = _dot_nt(wfv_ref[...], xn)
    vd = MLA_V_DIM
    for hd in range(FOX_HEADS):
        fv_ref[0, hd, 0, 0:vd, :] = fv_t[hd * vd:(hd + 1) * vd].astype(bf16)
        fv_ref[0, hd, 0, vd:, :] = ones_rows

    logf = jnp.minimum(z, 0.0) - jnp.log1p(jnp.exp(-jnp.abs(z)))
    pieces = jnp.concatenate([p_.astype(bf16) for p_ in _split_bf16(logf, DECAY_PARTS)], axis=1)
    csum = _dot(tri_ref[...], pieces)

    dcum = csum[:, 0:LANES] + csum[:, LANES:2 * LANES] + csum[:, 2 * LANES:] + carry_ref[...]
    carry_ref[...] = dcum[tt - 1:tt, :]
    d_parts = _split_bf16(dcum * LOG2E, DECAY_PARTS)
    d_sel = jnp.where(lane < FOX_HEADS, d_parts[0],
                      jnp.where(lane < 2 * FOX_HEADS, d_parts[1],
                                jnp.where(lane < 3 * FOX_HEADS, d_parts[2], 0.0)))
    k_aug = _dot(d_sel.astype(bf16), place_ref[...])

    fk = _dot(xn, wrow_ref[:, 0:fw])
    for p in range(FOX_HEADS // 2):
        blk = fk[:, p * LANES:(p + 1) * LANES]
        for hh, b_ in ((0, blk), (1, pltpu.roll(blk, half, axis=1))):
            sl = slice((2 * p + hh) * HEAD_PAD, (2 * p + hh + 1) * HEAD_PAD)
            fk_ref[:, sl] = jnp.where(lane < FOX_HEAD_DIM, b_, k_aug[:, sl]).astype(bf16)

    cq = _rms(cq_raw, qnorm_ref[...]).astype(bf16)
    qm_t = _dot_nt(wqm_ref[...], cq)
    qs_t = _dot_nt(wqs_ref[...], cq)
    cos_t = cos_ref[...]
    sin_t = sin_ref[...]
    scale = (MLA_NOPE_DIM + MLA_ROPE_DIM) ** -0.5 * LOG2E
    q_zero = jnp.zeros((HEAD_PAD - MLA_NOPE_DIM - MLA_ROPE_DIM, tt), jnp.float32)
    for hd in range(MLA_HEADS):
        blk = qm_t[hd * HEAD_PAD:(hd + 1) * HEAD_PAD]
        rows = [blk[0:MLA_NOPE_DIM]]
        for j in range(2):
            main = blk[MLA_NOPE_DIM + j * ROPE_HALF:MLA_NOPE_DIM + (j + 1) * ROPE_HALF]
            swap = qs_t[hd * MLA_ROPE_DIM + j * ROPE_HALF:hd * MLA_ROPE_DIM + (j + 1) * ROPE_HALF]
            rows.append(main * cos_t + swap * sin_t)
        rows.append(q_zero)
        qm_ref[0, hd, 0] = (jnp.concatenate(rows, axis=0) * scale).astype(bf16)

    ckv = _rms(ckv_raw, kvnorm_ref[...]).astype(bf16)
    mv_t = _dot_nt(wv_ref[...], ckv)
    for hd in range(MLA_HEADS):
        vm_ref[0, hd, 0, 0:vd, :] = mv_t[hd * vd:(hd + 1) * vd].astype(bf16)
        vm_ref[0, hd, 0, vd:, :] = ones_rows
    k_nope = _dot(ckv, wkn_ref[...])
    z_rows = jnp.zeros((MLA_ROPE_DIM, tt), jnp.float32)
    csk = jnp.concatenate([sin_t, sin_t, z_rows, cos_t, cos_t, z_rows], axis=0).T
    kr = kr_raw * csk
    kr = kr + pltpu.roll(kr, half, axis=1)
    k_pe = jnp.where(lane >= MLA_NOPE_DIM, kr, 0.0)
    for hd in range(MLA_HEADS):
        sl = slice(hd * HEAD_PAD, (hd + 1) * HEAD_PAD)
        km_ref[:, sl] = (k_nope[:, sl] + k_pe).astype(bf16)


def _pre_attn(h, moe_in, B, S, lw, tabs):
    T, D = h.shape
    tt = min(ROW_TILE, S)
    ns = S // tt
    has_moe_in = moe_in is not None
    row = lambda b, s: (b * ns + s, 0)
    const = lambda b, s: (0, 0)
    cos_t, sin_t = tabs

    def rows(width):
        return pl.BlockSpec((tt, width), row)

    def full(a):
        return pl.BlockSpec(a.shape, const)

    weights = [lw["attn_norm"], lw["w_rows"], lw["w_fq_t"], lw["w_fv_t"], lw["b_f"], lw["tri_incl"],
               lw["place"], lw["q_norm"], lw["wq_main_t"], lw["wq_swap_t"], lw["kv_norm"],
               lw["wk_nope"], lw["wv_t"]]
    in_specs = [rows(D)]
    args = [h]
    if has_moe_in:
        dest3, ys = moe_in
        last = B * ns - 1
        in_specs = [pl.BlockSpec((1, 1, tt), lambda b, s: (b * ns + s, 0, 0), memory_space=pltpu.SMEM),
                    pl.BlockSpec((1, 1, tt), lambda b, s: (jnp.minimum(b * ns + s + 1, last), 0, 0),
                                 memory_space=pltpu.SMEM),
                    rows(D), pl.BlockSpec(memory_space=pl.ANY)]
        args = [dest3, dest3, h, ys]
    col = lambda b, s: (0, b * ns + s)
    in_specs += [full(w) for w in weights]
    in_specs += [pl.BlockSpec((ROPE_HALF, tt), col), pl.BlockSpec((ROPE_HALF, tt), col)]
    args += weights + [cos_t, sin_t]

    bf16 = jnp.bfloat16
    qt_shape = jax.ShapeDtypeStruct((B, FOX_HEADS, ns, HEAD_PAD, tt), bf16)
    qt_spec = pl.BlockSpec((1, FOX_HEADS, 1, HEAD_PAD, tt), lambda b, s: (b, 0, s, 0, 0))
    k_shape = jax.ShapeDtypeStruct((T, FOX_HEADS * HEAD_PAD), bf16)
    k_spec = rows(FOX_HEADS * HEAD_PAD)
    vt_shape = jax.ShapeDtypeStruct((B, FOX_HEADS, ns, V_ROWS, tt), bf16)
    vt_spec = pl.BlockSpec((1, FOX_HEADS, 1, V_ROWS, tt), lambda b, s: (b, 0, s, 0, 0))
    out_shape = [qt_shape, k_shape, vt_shape] * 2
    out_specs = [qt_spec, k_spec, vt_spec] * 2
    scratch = [pltpu.VMEM((1, LANES), jnp.float32)]
    if has_moe_in:
        out_shape = [jax.ShapeDtypeStruct((T, D), jnp.float32)] + out_shape
        out_specs = [rows(D)] + out_specs
        scratch += [pltpu.VMEM((2, tt * (D // LANES), LANES), jnp.float32), pltpu.SemaphoreType.DMA((2,))]

    outs = pl.pallas_call(
        functools.partial(_pre_attn_kernel, has_moe_in=has_moe_in, tt=tt),
        grid=(B, ns),
        in_specs=in_specs,
        out_specs=out_specs,
        out_shape=out_shape,
        scratch_shapes=scratch,
        compiler_params=_cparams("arbitrary", "arbitrary"),
        name="pre_attn",
    )(*args)
    if has_moe_in:
        return outs[0], outs[1:]
    return h, outs


def _attn_kernel(*refs, ta, nq, n_streams):
    ins, refs = refs[:4 * n_streams], refs[4 * n_streams:]
    o_refs, (s_scr, mc_scr, m_scr, acc_scr) = refs[:n_streams], refs[n_streams:]
    qt_refs, k_refs, vt_refs, g_refs = ins[0::4], ins[1::4], ins[2::4], ins[3::4]
    heads = [(st, hh) for st in range(n_streams) for hh in range(2)]
    vd = MLA_V_DIM

    def produce_head(hi, q, kt, slot, masked):
        st, hh = heads[hi]
        k0 = pl.multiple_of(kt * ta, ta)
        kk = k_refs[st][pl.ds(k0, ta), hh * HEAD_PAD:(hh + 1) * HEAD_PAD]
        qt = qt_refs[st][0, hh, q]
        if not masked:
            s = _dot(kk, qt)
            s_scr[slot, hi] = s
            mc_scr[slot, hi] = jnp.max(s, axis=0, keepdims=True)
            return
        hk = ta // 2
        top = _dot(kk[0:hk], qt)
        low = _dot(kk[hk:], qt[:, hk:])
        top = jnp.where(lax.broadcasted_iota(jnp.int32, top.shape, 0)
                        <= lax.broadcasted_iota(jnp.int32, top.shape, 1), top, NEG_BIG)
        low = jnp.where(lax.broadcasted_iota(jnp.int32, low.shape, 0)
                        <= lax.broadcasted_iota(jnp.int32, low.shape, 1), low, NEG_BIG)
        s_scr[slot, hi, 0:hk, :] = top
        s_scr[slot, hi, hk:, 0:hk] = jnp.full((hk, hk), NEG_BIG, jnp.float32)
        s_scr[slot, hi, hk:, hk:] = low
        top_max = jnp.max(top, axis=0, keepdims=True)
        mc_scr[slot, hi, :, 0:hk] = top_max[:, 0:hk]
        mc_scr[slot, hi, :, hk:] = jnp.maximum(top_max[:, hk:], jnp.max(low, axis=0, keepdims=True))

    def consume_head(hi, kt, slot):
        st, hh = heads[hi]
        m_prev = m_scr[hi]
        m_new = jnp.maximum(m_prev, mc_scr[slot, hi])
        alpha = jnp.exp2(m_prev - m_new)
        p = jnp.exp2(s_scr[slot, hi] - m_new).astype(jnp.bfloat16)
        acc_scr[hi] = alpha * acc_scr[hi] + _dot(vt_refs[st][0, hh, kt], p)
        m_scr[hi] = m_new

    def produce(q, kt, slot, masked):
        for hi in range(len(heads)):
            produce_head(hi, q, kt, slot, masked)

    def consume(kt, slot):
        for hi in range(len(heads)):
            consume_head(hi, kt, slot)

    def produce_consume(produce_args, consume_args):
        for hi in range(len(heads)):
            produce_head(hi, *produce_args)
            consume_head(hi, *consume_args)

    def reset():
        m_scr[...] = jnp.full_like(m_scr, NEG_BIG)
        acc_scr[...] = jnp.zeros_like(acc_scr)

    def finish(q):
        for st in range(n_streams):
            outs = []
            for hh in range(2):
                acc = acc_scr[2 * st + hh]
                oh = acc[0:vd] / acc[vd:vd + 1]
                ms = jnp.mean(oh * oh, axis=0, keepdims=True)
                outs.append(oh * lax.rsqrt(ms + NORM_EPS))
            o_t = jnp.concatenate(outs, axis=0) * g_refs[st][...]
            o_refs[st][pl.ds(pl.multiple_of(q * ta, ta), ta), :] = o_t.T.astype(o_refs[st].dtype)

    reset()
    produce(0, 0, 0, True)

    def q_tile(n, carry):
        first = (n * (n + 1) // 2) % 2

        def step(j, slot):
            produce_consume((n, j, 1 - slot, False), (jnp.where(j == 0, n, j - 1), slot))

        for par in range(2):
            @pl.when(first == par)
            def _():
                def two_steps(u, c):
                    step(2 * u, par)
                    step(2 * u + 1, 1 - par)
                    return c

                lax.fori_loop(0, n // 2, two_steps, 0)

                @pl.when((n % 2 == 1) & (n < nq - 1))
                def _():
                    step(n - 1, par)
                    produce_consume((n + 1, n + 1, par, True), (n - 1, 1 - par))
                    finish(n)
                    reset()

                @pl.when((n % 2 == 0) & (n < nq - 1))
                def _():
                    produce_consume((n + 1, n + 1, 1 - par, True), (jnp.maximum(n - 1, 0), par))
                    finish(n)
                    reset()

        n_last = nq - 1
        first_last = (n_last * (n_last + 1) // 2) % 2

        @pl.when(n == n_last)
        def _():
            if n_last % 2 == 1:
                step(n - 1, first_last)
                consume(n - 1, 1 - first_last)
            else:
                consume(jnp.maximum(n - 1, 0), first_last)
            finish(n)

        return carry

    lax.fori_loop(0, nq, q_tile, 0)


def _attention(streams, B, S):
    T = streams[0][1].shape[0]
    ta = min(ATTN_TILE, S)
    nq = S // ta
    n_pairs = streams[0][0].shape[1] // 2
    n_streams = len(streams)
    n_heads = 2 * n_streams
    stream_specs = [pl.BlockSpec((1, 2, nq, HEAD_PAD, ta), lambda b, j: (b, j, 0, 0, 0)),
                    pl.BlockSpec((S, 2 * HEAD_PAD), lambda b, j: (b, j)),
                    pl.BlockSpec((1, 2, nq, V_ROWS, ta), lambda b, j: (b, j, 0, 0, 0)),
                    pl.BlockSpec((LANES, ta), lambda b, j: (j, 0))]
    return pl.pallas_call(
        functools.partial(_attn_kernel, ta=ta, nq=nq, n_streams=n_streams),
        grid=(B, n_pairs),
        in_specs=stream_specs * n_streams,
        out_specs=[pl.BlockSpec((S, LANES), lambda b, j: (b, j))] * n_streams,
        out_shape=[jax.ShapeDtypeStruct((T, n_pairs * LANES), jnp.bfloat16)] * n_streams,
        scratch_shapes=[pltpu.VMEM((2, n_heads, ta, ta), jnp.float32),
                        pltpu.VMEM((2, n_heads, 1, ta), jnp.float32),
                        pltpu.VMEM((n_heads, 1, ta), jnp.float32),
                        pltpu.VMEM((n_heads, V_ROWS, ta), jnp.float32)],
        compiler_params=_cparams("parallel", "parallel"),
        name="attention",
    )(*[a for stream in streams for a in stream])


def _post_attn_kernel(fox_ref, mla_ref, h_ref, wo_ref, fnorm_ref, wrh_ref, wrl_ref, tri_ref,
                      h2_ref, xr_ref, route_ref, cnt_ref, carry_ref, xn_scr):
    i = pl.program_id(0)
    D = h_ref.shape[1]

    @pl.when(i == 0)
    def _():
        xn_scr[...] = jnp.zeros_like(xn_scr)

    @pl.when(i <= 1)
    def _():
        carry_ref[...] = jnp.zeros_like(carry_ref)

    xn = xn_scr[...]
    xr_ref[:, 0:D] = xn

    x_hi = xn.astype(jnp.bfloat16)
    x_lo = (xn - x_hi.astype(jnp.float32)).astype(jnp.bfloat16)
    logits = _dot(x_hi, wrh_ref[...]) + _dot(x_lo, wrh_ref[...]) + _dot(x_hi, wrl_ref[...])

    h2 = h_ref[...] + _dot(fox_ref[...], wo_ref[0:FOX_WIDTH, :]) + _dot(mla_ref[...], wo_ref[FOX_WIDTH:, :])
    h2_ref[...] = h2
    xn_scr[...] = _rms(h2, fnorm_ref[...])

    lt = logits.T
    tt = lt.shape[1]
    ng = N_GROUPS
    gl = lt[0:ng]
    sub = lax.broadcasted_iota(jnp.int32, (ng, tt), 0).astype(jnp.float32)

    gmax = jnp.max(gl, axis=0, keepdims=True)
    group_w = 1.0 / jnp.sum(jnp.exp(gl - gmax), axis=0, keepdims=True)
    gid = jnp.min(jnp.where(gl == gmax, sub, float(ng)), axis=0, keepdims=True)
    onehot = sub == gid

    ins = [jnp.sum(jnp.where(onehot, lt[ng * (j + 1):ng * (j + 2)], 0.0), axis=0, keepdims=True)
           for j in range(EXPERTS_PER_GROUP)]

    def first_argmax(vals):
        top = functools.reduce(jnp.maximum, vals)
        idx = jnp.full_like(top, float(len(vals) - 1))
        for j in range(len(vals) - 2, -1, -1):
            idx = jnp.where(vals[j] == top, float(j), idx)
        return top, idx

    e1, i1 = first_argmax(ins)
    e2, i2 = first_argmax([jnp.where(i1 == j, NEG_BIG, v) for j, v in enumerate(ins)])
    t = jnp.exp(e2 - e1)
    g1 = group_w / (1.0 + t)
    g2 = group_w * t / (1.0 + t)
    gates = [jnp.where(i1 == j, g1, jnp.where(i2 == j, g2, 0.0)) for j in range(EXPERTS_PER_GROUP)]

    ones = jnp.where(onehot, 1.0, 0.0)
    prefix = _dot(ones.astype(jnp.bfloat16), tri_ref[...]) + carry_ref[:, 0:1]
    rank = jnp.sum(jnp.where(onehot, prefix, 0.0), axis=0, keepdims=True)
    carry = carry_ref[...] + jnp.sum(ones, axis=1, keepdims=True)
    carry_ref[...] = carry
    cnt_ref[...] = carry

    pad_rows = jnp.zeros((ROUTE_ROWS - EXPERTS_PER_GROUP - 2, tt), jnp.float32)
    info_t = jnp.concatenate(gates + [gid, rank, pad_rows], axis=0)
    route_ref[...] = info_t
    info_pad = jnp.zeros((LANES - ROUTE_ROWS, tt), jnp.float32)
    xr_ref[:, D:D + LANES] = jnp.concatenate([info_t, info_pad], axis=0).T


def _post_attn(fox, mla, h, lw):
    T, D = h.shape
    tt = min(ROW_TILE, T)
    n = T // tt
    proj = lambda i: (jnp.minimum(i, n - 1), 0)
    routed = lambda i: (jnp.maximum(i - 1, 0), 0)
    const = lambda i: (0, 0)
    weights = [lw["w_o"], lw["ffn_norm"], lw["wr_hi"], lw["wr_lo"], lw["tri_before"]]
    return pl.pallas_call(
        _post_attn_kernel,
        grid=(n + 1,),
        in_specs=[pl.BlockSpec((tt, FOX_WIDTH), proj), pl.BlockSpec((tt, MLA_WIDTH), proj),
                  pl.BlockSpec((tt, D), proj)] + [pl.BlockSpec(w.shape, const) for w in weights],
        out_specs=[pl.BlockSpec((tt, D), proj), pl.BlockSpec((tt, D + LANES), routed),
                   pl.BlockSpec((ROUTE_ROWS, tt), lambda i: (0, jnp.maximum(i - 1, 0))),
                   pl.BlockSpec((N_GROUPS, LANES), const)],
        out_shape=[jax.ShapeDtypeStruct((T, D), jnp.float32),
                   jax.ShapeDtypeStruct((T, D + LANES), jnp.float32),
                   jax.ShapeDtypeStruct((ROUTE_ROWS, T), jnp.float32),
                   jax.ShapeDtypeStruct((N_GROUPS, LANES), jnp.float32)],
        scratch_shapes=[pltpu.VMEM((N_GROUPS, LANES), jnp.float32), pltpu.VMEM((tt, D), jnp.float32)],
        compiler_params=_cparams("arbitrary"),
        name="post_attn",
    )(fox, mla, h, *weights)


def _dispatch_kernel(fill_ref, dest_ref, xr_ref, xs_hbm, zero_ref, sem, fill_sem, *, tt, n_sorted):
    first = pl.program_id(0) == 0

    def fill_copies(act):
        for g in range(N_GROUPS):
            start, length = fill_ref[g], fill_ref[N_GROUPS + g]
            head = (-start) & (SUBLANES - 1)
            for r in range(SUBLANES - 1):
                @pl.when(r < head)
                def _(start=start, r=r):
                    act(pltpu.make_async_copy(zero_ref.at[pl.ds(0, 1)], xs_hbm.at[pl.ds(start + r, 1)], fill_sem))
            base, rest = start + head, length - head
            size = MOE_BLOCK // 2
            while size >= SUBLANES:
                @pl.when((rest & size) != 0)
                def _(base=base, rest=rest, size=size):
                    at = pl.multiple_of(base + (rest & ~(2 * size - 1)), SUBLANES)
                    act(pltpu.make_async_copy(zero_ref.at[pl.ds(0, size)], xs_hbm.at[pl.ds(at, size)], fill_sem))
                size //= 2
        used = fill_ref[2 * N_GROUPS]
        for blk in range(N_GROUPS):
            @pl.when(used + (blk + 1) * MOE_BLOCK <= n_sorted)
            def _(blk=blk):
                at = pl.multiple_of(used + blk * MOE_BLOCK, MOE_BLOCK)
                act(pltpu.make_async_copy(zero_ref, xs_hbm.at[pl.ds(at, MOE_BLOCK)], fill_sem))

    @pl.when(first)
    def _():
        zero_ref[...] = jnp.zeros_like(zero_ref)
        fill_copies(lambda cp: cp.start())

    def issue(r, carry):
        pltpu.make_async_copy(xr_ref.at[pl.ds(r, 1)], xs_hbm.at[pl.ds(dest_ref[0, 0, r], 1)], sem).start()
        return carry
    lax.fori_loop(0, tt, issue, 0, unroll=ROW_DMA_UNROLL)
    pltpu.make_async_copy(xr_ref, xs_hbm.at[pl.ds(0, tt)], sem).wait()

    @pl.when(first)
    def _():
        fill_copies(lambda cp: cp.wait())


def _dispatch(xr, dest3, fill, n_sorted):
    T, W = xr.shape
    tt = dest3.shape[2]
    grid_spec = pltpu.PrefetchScalarGridSpec(
        num_scalar_prefetch=1,
        grid=(T // tt,),
        in_specs=[pl.BlockSpec((1, 1, tt), lambda i, f: (i, 0, 0), memory_space=pltpu.SMEM),
                  pl.BlockSpec((tt, W), lambda i, f: (i, 0))],
        out_specs=pl.BlockSpec(memory_space=pl.ANY),
        scratch_shapes=[pltpu.VMEM((MOE_BLOCK, W), xr.dtype),
                        pltpu.SemaphoreType.DMA(()), pltpu.SemaphoreType.DMA(())],
    )
    return pl.pallas_call(
        functools.partial(_dispatch_kernel, tt=tt, n_sorted=n_sorted),
        grid_spec=grid_spec,
        out_shape=jax.ShapeDtypeStruct((n_sorted, W), xr.dtype),
        compiler_params=_cparams("arbitrary"),
        name="moe_dispatch",
    )(fill, dest3, xr)


def _moe_kernel(bg_ref, nv_ref, xs_ref, wg32_ref, wu32_ref, wd32_ref, ys_ref, wg_ref, wu_ref, wd_ref):
    D = xs_ref.shape[1] - LANES
    rows = xs_ref.shape[0]
    chunks = D // LANES
    b = pl.program_id(0)
    g = bg_ref[b]
    n_valid = nv_ref[b]

    @pl.when((b == 0) | (g != bg_ref[jnp.maximum(b - 1, 0)]))
    def _():
        wg_ref[...] = wg32_ref[...].astype(jnp.bfloat16)
        wu_ref[...] = wu32_ref[...].astype(jnp.bfloat16)
        wd_ref[...] = wd32_ref[...].astype(jnp.bfloat16)

    @pl.when(n_valid == 0)
    def _():
        ys_ref[...] = jnp.zeros_like(ys_ref)

    @pl.when(n_valid > 0)
    def _():
        x = xs_ref[:, 0:D].astype(jnp.bfloat16)
        info = xs_ref[:, D:D + LANES]
        y = jnp.zeros((rows, D), jnp.float32)
        for j in range(EXPERTS_PER_GROUP):
            a = _dot(x, wg_ref[j])
            u = _dot(x, wu_ref[j])
            hj = (a * jax.nn.sigmoid(a) * u).astype(jnp.bfloat16)
            y = y + info[:, j:j + 1] * _dot(hj, wd_ref[j])
        for c in range(chunks):
            ys_ref[pl.ds(c, rows, stride=chunks), :] = y[:, c * LANES:(c + 1) * LANES]


def _moe(xs, block_group, block_valid, layer, w_gate, w_up, w_down):
    n_sorted, W = xs.shape
    D = W - LANES
    nb = n_sorted // MOE_BLOCK
    epg = EXPERTS_PER_GROUP
    chunks = D // LANES
    grid_spec = pltpu.PrefetchScalarGridSpec(
        num_scalar_prefetch=2,
        grid=(nb,),
        in_specs=[pl.BlockSpec((MOE_BLOCK, W), lambda b, bg, nv: (b, 0)),
                  pl.BlockSpec((None, epg, D, D_EXPERT), lambda b, bg, nv: (layer, bg[b], 0, 0)),
                  pl.BlockSpec((None, epg, D, D_EXPERT), lambda b, bg, nv: (layer, bg[b], 0, 0)),
                  pl.BlockSpec((None, epg, D_EXPERT, D), lambda b, bg, nv: (layer, bg[b], 0, 0))],
        out_specs=pl.BlockSpec((MOE_BLOCK * chunks, LANES), lambda b, bg, nv: (b, 0)),
        scratch_shapes=[pltpu.VMEM((epg, D, D_EXPERT), jnp.bfloat16),
                        pltpu.VMEM((epg, D, D_EXPERT), jnp.bfloat16),
                        pltpu.VMEM((epg, D_EXPERT, D), jnp.bfloat16)],
    )
    return pl.pallas_call(
        _moe_kernel,
        grid_spec=grid_spec,
        out_shape=jax.ShapeDtypeStruct((n_sorted * chunks, LANES), jnp.float32),
        compiler_params=_cparams("arbitrary"),
        name="moe_experts",
    )(block_group, block_valid, xs, w_gate, w_up, w_down)


def _final_kernel(dest_ref, dest_next_ref, h_ref, ys_hbm, gain_ref, o_ref, ybuf_ref, sems, *, tt):
    y = _gather_rows(ys_hbm, dest_ref, dest_next_ref, ybuf_ref, sems,
                     pl.program_id(0), pl.num_programs(0), tt)
    o_ref[...] = _rms(h_ref[...] + y, gain_ref[...])


def _final(h, dest3, ys, gain):
    T, D = h.shape
    tt = dest3.shape[2]
    last = T // tt - 1
    return pl.pallas_call(
        functools.partial(_final_kernel, tt=tt),
        grid=(T // tt,),
        in_specs=[pl.BlockSpec((1, 1, tt), lambda i: (i, 0, 0), memory_space=pltpu.SMEM),
                  pl.BlockSpec((1, 1, tt), lambda i: (jnp.minimum(i + 1, last), 0, 0), memory_space=pltpu.SMEM),
                  pl.BlockSpec((tt, D), lambda i: (i, 0)),
                  pl.BlockSpec(memory_space=pl.ANY),
                  pl.BlockSpec((1, D), lambda i: (0, 0))],
        out_specs=pl.BlockSpec((tt, D), lambda i: (i, 0)),
        out_shape=jax.ShapeDtypeStruct((T, D), jnp.float32),
        scratch_shapes=[pltpu.VMEM((2, tt * (D // LANES), LANES), jnp.float32), pltpu.SemaphoreType.DMA((2,))],
        compiler_params=_cparams("arbitrary"),
        name="final_norm",
    )(dest3, dest3, h, ys, gain)


def _prep_layer(l, tt, w_in, b_f, w_uq, w_ukv, w_o, w_group, w_router,
                attn_norm, q_norm, kv_norm, ffn_norm):
    bf16 = jnp.bfloat16
    D = w_in.shape[1]
    fw = FOX_WIDTH
    o_f = 3 * fw
    o_cq = o_f + FOX_HEADS
    o_ckv = o_cq + Q_LORA_RANK
    o_kr = o_ckv + KV_LORA_RANK
    wi = w_in[l]
    w_f = wi[:, o_f:o_cq]
    w_f3 = jnp.concatenate([w_f] * DECAY_PARTS + [jnp.zeros((D, LANES - DECAY_PARTS * FOX_HEADS), wi.dtype)], axis=1)
    b_f3 = jnp.concatenate([b_f[l]] * DECAY_PARTS + [jnp.zeros((LANES - DECAY_PARTS * FOX_HEADS,), b_f.dtype)])
    kr = wi[:, o_kr:o_kr + MLA_ROPE_DIM]
    kr_swap = jnp.concatenate([-kr[:, ROPE_HALF:], kr[:, :ROPE_HALF]], axis=1)
    zpad = jnp.zeros((D, MLA_ROPE_DIM), wi.dtype)
    w_rows = jnp.concatenate([wi[:, fw:2 * fw], w_f3, wi[:, o_cq:o_kr], kr_swap, zpad, kr, zpad], axis=1)

    src = jnp.arange(LANES)
    dst = (src % FOX_HEADS) * HEAD_PAD + FOX_HEAD_DIM + src // FOX_HEADS
    place = ((jnp.arange(FOX_HEADS * HEAD_PAD)[None, :] == dst[:, None])
             & (src[:, None] < DECAY_PARTS * FOX_HEADS)).astype(bf16)

    qd = MLA_NOPE_DIM + MLA_ROPE_DIM
    wq = w_uq[l].reshape(Q_LORA_RANK, MLA_HEADS, qd)
    zq = jnp.zeros((Q_LORA_RANK, MLA_HEADS, HEAD_PAD - qd), wq.dtype)
    wq_main = jnp.concatenate([wq, zq], axis=2).reshape(Q_LORA_RANK, MLA_HEADS * HEAD_PAD)
    x1 = wq[:, :, MLA_NOPE_DIM:MLA_NOPE_DIM + ROPE_HALF]
    x2 = wq[:, :, MLA_NOPE_DIM + ROPE_HALF:]
    wq_swap = jnp.concatenate([-x2, x1], axis=2).reshape(Q_LORA_RANK, MLA_HEADS * MLA_ROPE_DIM)

    wkv = w_ukv[l].reshape(KV_LORA_RANK, MLA_HEADS, MLA_NOPE_DIM + MLA_V_DIM)
    wk_nope = jnp.concatenate([wkv[:, :, :MLA_NOPE_DIM],
                               jnp.zeros((KV_LORA_RANK, MLA_HEADS, HEAD_PAD - MLA_NOPE_DIM), wkv.dtype)], axis=2)
    wv = wkv[:, :, MLA_NOPE_DIM:].reshape(KV_LORA_RANK, MLA_WIDTH)

    w_exp = w_router[l].reshape(D, N_GROUPS, EXPERTS_PER_GROUP).transpose(0, 2, 1).reshape(D, N_EXPERTS)
    wr = jnp.concatenate([w_group[l], w_exp,
                          jnp.zeros((D, LANES - N_GROUPS - N_EXPERTS), w_group.dtype)], axis=1)
    wr_hi = wr.astype(bf16)
    wr_lo = (wr - wr_hi.astype(jnp.float32)).astype(bf16)

    ar = jnp.arange(tt)
    return dict(
        attn_norm=attn_norm[l].reshape(1, D),
        w_rows=w_rows.astype(bf16),
        w_fq_t=wi[:, 0:fw].T.astype(bf16),
        w_fv_t=wi[:, 2 * fw:3 * fw].T.astype(bf16),
        b_f=b_f3.reshape(1, LANES),
        tri_incl=(ar[:, None] >= ar[None, :]).astype(bf16),
        tri_before=(ar[:, None] < ar[None, :]).astype(bf16),
        place=place,
        q_norm=q_norm[l].reshape(1, Q_LORA_RANK),
        wq_main_t=wq_main.T.astype(bf16),
        wq_swap_t=wq_swap.T.astype(bf16),
        kv_norm=kv_norm[l].reshape(1, KV_LORA_RANK),
        wk_nope=wk_nope.reshape(KV_LORA_RANK, MLA_HEADS * HEAD_PAD).astype(bf16),
        wv_t=wv.T.astype(bf16),
        w_o=w_o[l].astype(bf16),
        ffn_norm=ffn_norm[l].reshape(1, D),
        wr_hi=wr_hi,
        wr_lo=wr_lo,
    )


def _routing_plan(route, counts, n_blocks):
    gid = route[GID_INDEX].astype(jnp.int32)
    rank = route[RANK_INDEX].astype(jnp.int32)
    cnt = counts[:, 0].astype(jnp.int32)
    padded = (cnt + MOE_BLOCK - 1) // MOE_BLOCK * MOE_BLOCK
    pend = jnp.cumsum(padded)
    pstart = pend - padded
    dest = pstart[gid] + rank
    block_start = jnp.arange(n_blocks, dtype=jnp.int32) * MOE_BLOCK
    block_group = jnp.sum((pend[None, :] <= block_start[:, None]).astype(jnp.int32), axis=1)
    block_group = jnp.minimum(block_group, N_GROUPS - 1)
    block_valid = jnp.clip((pstart + cnt)[block_group] - block_start, 0, MOE_BLOCK)
    fill = jnp.concatenate([pstart + cnt, padded - cnt, pend[-1:]])
    return dest, block_group, block_valid, fill


def kernel(x, positions, attn_norm, w_in, b_f, q_norm, w_uq, kv_norm, w_ukv, fox_out_norm, mla_out_norm, w_o,
           ffn_norm, w_group, w_router, w_gate, w_up, w_down, final_norm):
    B, S, D = x.shape
    T = B * S
    depth = w_in.shape[0]
    ta = min(ATTN_TILE, S)
    tt = min(ROW_TILE, S)
    n_blocks = -(-(T + N_GROUPS * (MOE_BLOCK - 1)) // MOE_BLOCK)

    cos_t, sin_t = _rope_tables(positions)
    tabs = (cos_t, sin_t)

    h = x.reshape(T, D)
    moe_in = None
    for l in range(depth):
        lw = _prep_layer(l, tt, w_in, b_f, w_uq, w_ukv, w_o, w_group, w_router,
                         attn_norm, q_norm, kv_norm, ffn_norm)
        h, (fq_t, fk, fv_t, mq_t, mk, mv_t) = _pre_attn(h, moe_in, B, S, lw, tabs)
        fox_gain = jnp.broadcast_to(fox_out_norm[l].reshape(FOX_WIDTH, 1), (FOX_WIDTH, ta))
        mla_gain = jnp.broadcast_to(mla_out_norm[l].reshape(MLA_WIDTH, 1), (MLA_WIDTH, ta))
        fox, mla = _attention([(fq_t, fk, fv_t, fox_gain), (mq_t, mk, mv_t, mla_gain)], B, S)
        h, xr, route, counts = _post_attn(fox, mla, h, lw)
        dest, block_group, block_valid, fill = _routing_plan(route, counts, n_blocks)
        dest3 = dest.reshape(T // tt, 1, tt)
        dt = DISPATCH_TILE if T % DISPATCH_TILE == 0 else tt
        xs = _dispatch(xr, dest.reshape(T // dt, 1, dt), fill, n_blocks * MOE_BLOCK)
        ys = _moe(xs, block_group, block_valid, l, w_gate, w_up, w_down)
        moe_in = (dest3, ys)
    out = _final(h, moe_in[0], moe_in[1], final_norm.reshape(1, D))
    return out.reshape(B, S, D)
```

```python
import functools
import math

import jax
import jax.numpy as jnp
from jax import lax
from jax.experimental import pallas as pl
from jax.experimental.pallas import tpu as pltpu

FOX_HEADS = 8
FOX_HEAD_DIM = 64
FOX_WIDTH = FOX_HEADS * FOX_HEAD_DIM
MLA_HEADS = 8
MLA_NOPE_DIM = 64
MLA_ROPE_DIM = 32
MLA_V_DIM = 64
MLA_WIDTH = MLA_HEADS * MLA_V_DIM
Q_LORA_RANK = 256
KV_LORA_RANK = 128
ROPE_THETA = 10000.0
N_GROUPS = 8
EXPERTS_PER_GROUP = 4
N_EXPERTS = N_GROUPS * EXPERTS_PER_GROUP
D_EXPERT = 256
NORM_EPS = 1e-6

LANES = 128
SUBLANES = 8
BF16_SUBLANES = 16
VMEM_LIMIT_BYTES = 56 * 1024 * 1024

ROW_TILE = 512
DISPATCH_TILE = 2048
ATTN_TILE = 512
MOE_BLOCK = 512
ROW_DMA_UNROLL = 32
DMA_PRIORITIES = 2
HEAD_PAD = 128
V_ROWS = MLA_V_DIM + BF16_SUBLANES
ROPE_HALF = MLA_ROPE_DIM // 2
NEG_BIG = -1e30
LOG2E = math.log2(math.e)
DECAY_PARTS = 3

GID_INDEX = EXPERTS_PER_GROUP
RANK_INDEX = EXPERTS_PER_GROUP + 1
ROUTE_ROWS = SUBLANES


def _cparams(*semantics):
    return pltpu.CompilerParams(dimension_semantics=semantics, vmem_limit_bytes=VMEM_LIMIT_BYTES)


def _rms(x, gain):
    ms = jnp.mean(x * x, axis=-1, keepdims=True)
    return x * lax.rsqrt(ms + NORM_EPS) * gain


def _dot(a, b):
    return jnp.dot(a, b, preferred_element_type=jnp.float32)


def _dot_nt(a, b):
    return lax.dot_general(a, b, (((1,), (1,)), ((), ())), preferred_element_type=jnp.float32)


def _split_bf16(x, parts):
    out = []
    for _ in range(parts - 1):
        piece = x.astype(jnp.bfloat16).astype(jnp.float32)
        out.append(piece)
        x = x - piece
    out.append(x.astype(jnp.bfloat16).astype(jnp.float32))
    return out


def _rope_table_kernel(pos_ref, invf_ref, cos_ref, sin_ref):
    ang = pos_ref[...].astype(jnp.float32) * invf_ref[...]
    cos_ref[...] = jnp.cos(ang)
    sin_ref[...] = jnp.sin(ang)


def _rope_tables(positions):
    T = positions.size
    inv_freq = ROPE_THETA ** (-jnp.arange(ROPE_HALF, dtype=jnp.float32) / ROPE_HALF)
    ct = min(T, 4096)
    return pl.pallas_call(
        _rope_table_kernel,
        grid=(T // ct,),
        in_specs=[pl.BlockSpec((1, ct), lambda i: (0, i)),
                  pl.BlockSpec((ROPE_HALF, 1), lambda i: (0, 0))],
        out_specs=[pl.BlockSpec((ROPE_HALF, ct), lambda i: (0, i))] * 2,
        out_shape=[jax.ShapeDtypeStruct((ROPE_HALF, T), jnp.float32)] * 2,
        compiler_params=_cparams("parallel"),
        name="rope_tables",
    )(positions.reshape(1, T), inv_freq.reshape(ROPE_HALF, 1))


def _gather_rows(src_hbm, idx_ref, idx_next_ref, buf_ref, sems, step, n_steps, n_rows):
    slot = step % 2
    chunks = buf_ref.shape[1] // n_rows

    def request(ref, to_slot):
        def issue(u, carry):
            for prio in range(DMA_PRIORITIES):
                r = u * DMA_PRIORITIES + prio
                src = pl.multiple_of(ref[0, 0, r] * chunks, chunks)
                dst = pl.multiple_of(r * chunks, chunks)
                pltpu.make_async_copy(src_hbm.at[pl.ds(src, chunks)], buf_ref.at[to_slot, pl.ds(dst, chunks)],
                                      sems.at[to_slot]).start(priority=prio)
            return carry
        lax.fori_loop(0, n_rows // DMA_PRIORITIES, issue, 0, unroll=ROW_DMA_UNROLL // DMA_PRIORITIES)

    @pl.when(step == 0)
    def _():
        request(idx_ref, slot)

    @pl.when(step + 1 < n_steps)
    def _():
        request(idx_next_ref, 1 - slot)

    pltpu.make_async_copy(src_hbm.at[pl.ds(0, n_rows * chunks)], buf_ref.at[slot], sems.at[slot]).wait()
    return jnp.concatenate([buf_ref[slot, pl.ds(c, n_rows, stride=chunks), :] for c in range(chunks)], axis=1)


def _pre_attn_kernel(*refs, has_moe_in, tt):
    if has_moe_in:
        dest_ref, dest_next_ref, h_ref, ys_hbm = refs[:4]
        refs = refs[4:]
    else:
        h_ref = refs[0]
        refs = refs[1:]
    (anorm_ref, wrow_ref, wfq_ref, wfv_ref, bf_ref, tri_ref, place_ref, qnorm_ref, wqm_ref, wqs_ref,
     kvnorm_ref, wkn_ref, wv_ref, cos_ref, sin_ref) = refs[:15]
    refs = refs[15:]
    if has_moe_in:
        hout_ref = refs[0]
        refs = refs[1:]
    fq_ref, fk_ref, fv_ref, qm_ref, km_ref, vm_ref = refs[:6]
    refs = refs[6:]
    carry_ref = refs[0]
    if has_moe_in:
        ybuf_ref, sems = refs[1:3]

    si = pl.program_id(1)
    bf16 = jnp.bfloat16
    half = LANES // 2

    @pl.when(si == 0)
    def _():
        carry_ref[...] = jnp.zeros_like(carry_ref)

    h = h_ref[...]
    if has_moe_in:
        step = pl.program_id(0) * pl.num_programs(1) + si
        n_steps = pl.num_programs(0) * pl.num_programs(1)
        h = h + _gather_rows(ys_hbm, dest_ref, dest_next_ref, ybuf_ref, sems, step, n_steps, tt)
        hout_ref[...] = h

    xn = _rms(h, anorm_ref[...]).astype(bf16)
    lane = lax.broadcasted_iota(jnp.int32, (tt, LANES), 1)
    sub = lax.broadcasted_iota(jnp.int32, (BF16_SUBLANES, tt), 0)
    ones_rows = jnp.ones((BF16_SUBLANES, tt), bf16)

    fw = FOX_WIDTH
    o = fw
    z = _dot(xn, wrow_ref[:, o:o + LANES]) + bf_ref[...]
    o += LANES
    cq_raw = _dot(xn, wrow_ref[:, o:o + Q_LORA_RANK])
    o += Q_LORA_RANK
    ckv_raw = _dot(xn, wrow_ref[:, o:o + KV_LORA_RANK])
    o += KV_LORA_RANK
    kr_raw = _dot(xn, wrow_ref[:, o:o + LANES])

    fq_t = _dot_nt(wfq_ref[...], xn) * (FOX_HEAD_DIM ** -0.5 * LOG2E)
    q_aug = jnp.where(sub < DECAY_PARTS, -1.0, 0.0).astype(bf16)
    q_zero = jnp.zeros((HEAD_PAD - FOX_HEAD_DIM - BF16_SUBLANES, tt), bf16)
    for hd in range(FOX_HEADS):
        fq_ref[0, hd, 0, 0:FOX_HEAD_DIM, :] = fq_t[hd * FOX_HEAD_DIM:(hd + 1) * FOX_HEAD_DIM].astype(bf16)
        fq_ref[0, hd, 0, FOX_HEAD_DIM:FOX_HEAD_DIM + BF16_SUBLANES, :] = q_aug
        fq_ref[0, hd, 0, FOX_HEAD_DIM + BF16_SUBLANES:, :] = q_zero
    fv_t = _dot_nt(wfv_ref[...], xn)
    vd = MLA_V_DIM
    for hd in range(FOX_HEADS):
        fv_ref[0, hd, 0, 0:vd, :] = fv_t[hd * vd:(hd + 1) * vd].astype(bf16)
        fv_ref[0, hd, 0, vd:, :] = ones_rows

    logf = jnp.minimum(z, 0.0) - jnp.log1p(jnp.exp(-jnp.abs(z)))
    pieces = jnp.concatenate([p_.astype(bf16) for p_ in _split_bf16(logf, DECAY_PARTS)], axis=1)
    csum = _dot(tri_ref[...], pieces)

    dcum = csum[:, 0:LANES] + csum[:, LANES:2 * LANES] + csum[:, 2 * LANES:] + carry_ref[...]
    carry_ref[...] = dcum[tt - 1:tt, :]
    d_parts = _split_bf16(dcum * LOG2E, DECAY_PARTS)
    d_sel = jnp.where(lane < FOX_HEADS, d_parts[0],
                      jnp.where(lane < 2 * FOX_HEADS, d_parts[1],
                                jnp.where(lane < 3 * FOX_HEADS, d_parts[2], 0.0)))
    k_aug = _dot(d_sel.astype(bf16), place_ref[...])

    fk = _dot(xn, wrow_ref[:, 0:fw])
    for p in range(FOX_HEADS // 2):
        blk = fk[:, p * LANES:(p + 1) * LANES]
        for hh, b_ in ((0, blk), (1, pltpu.roll(blk, half, axis=1))):
            sl = slice((2 * p + hh) * HEAD_PAD, (2 * p + hh + 1) * HEAD_PAD)
            fk_ref[:, sl] = jnp.where(lane < FOX_HEAD_DIM, b_, k_aug[:, sl]).astype(bf16)

    cq = _rms(cq_raw, qnorm_ref[...]).astype(bf16)
    qm_t = _dot_nt(wqm_ref[...], cq)
    qs_t = _dot_nt(wqs_ref[...], cq)
    cos_t = cos_ref[...]
    sin_t = sin_ref[...]
    scale = (MLA_NOPE_DIM + MLA_ROPE_DIM) ** -0.5 * LOG2E
    q_zero = jnp.zeros((HEAD_PAD - MLA_NOPE_DIM - MLA_ROPE_DIM, tt), jnp.float32)
    for hd in range(MLA_HEADS):
        blk = qm_t[hd * HEAD_PAD:(hd + 1) * HEAD_PAD]
        rows = [blk[0:MLA_NOPE_DIM]]
        for j in range(2):
            main = blk[MLA_NOPE_DIM + j * ROPE_HALF:MLA_NOPE_DIM + (j + 1) * ROPE_HALF]
            swap = qs_t[hd * MLA_ROPE_DIM + j * ROPE_HALF:hd * MLA_ROPE_DIM + (j + 1) * ROPE_HALF]
            rows.append(main * cos_t + swap * sin_t)
        rows.append(q_zero)
        qm_ref[0, hd, 0] = (jnp.concatenate(rows, axis=0) * scale).astype(bf16)

    ckv = _rms(ckv_raw, kvnorm_ref[...]).astype(bf16)
    mv_t = _dot_nt(wv_ref[...], ckv)
    for hd in range(MLA_HEADS):
        vm_ref[0, hd, 0, 0:vd, :] = mv_t[hd * vd:(hd + 1) * vd].astype(bf16)
        vm_ref[0, hd, 0, vd:, :] = ones_rows
    k_nope = _dot(ckv, wkn_ref[...])
    z_rows = jnp.zeros((MLA_ROPE_DIM, tt), jnp.float32)
    csk = jnp.concatenate([sin_t, sin_t, z_rows, cos_t, cos_t, z_rows], axis=0).T
    kr = kr_raw * csk
    kr = kr + pltpu.roll(kr, half, axis=1)
    k_pe = jnp.where(lane >= MLA_NOPE_DIM, kr, 0.0)
    for hd in range(MLA_HEADS):
        sl = slice(hd * HEAD_PAD, (hd + 1) * HEAD_PAD)
        km_ref[:, sl] = (k_nope[:, sl] + k_pe).astype(bf16)


def _pre_attn(h, moe_in, B, S, lw, tabs):
    T, D = h.shape
    tt = min(ROW_TILE, S)
    ns = S // tt
    has_moe_in = moe_in is not None
    row = lambda b, s: (b * ns + s, 0)
    const = lambda b, s: (0, 0)
    cos_t, sin_t = tabs

    def rows(width):
        return pl.BlockSpec((tt, width), row)

    def full(a):
        return pl.BlockSpec(a.shape, const)

    weights = [lw["attn_norm"], lw["w_rows"], lw["w_fq_t"], lw["w_fv_t"], lw["b_f"], lw["tri_incl"],
               lw["place"], lw["q_norm"], lw["wq_main_t"], lw["wq_swap_t"], lw["kv_norm"],
               lw["wk_nope"], lw["wv_t"]]
    in_specs = [rows(D)]
    args = [h]
    if has_moe_in:
        dest3, ys = moe_in
        last = B * ns - 1
        in_specs = [pl.BlockSpec((1, 1, tt), lambda b, s: (b * ns + s, 0, 0), memory_space=pltpu.SMEM),
                    pl.BlockSpec((1, 1, tt), lambda b, s: (jnp.minimum(b * ns + s + 1, last), 0, 0),
                                 memory_space=pltpu.SMEM),
                    rows(D), pl.BlockSpec(memory_space=pl.ANY)]
        args = [dest3, dest3, h, ys]
    col = lambda b, s: (0, b * ns + s)
    in_specs += [full(w) for w in weights]
    in_specs += [pl.BlockSpec((ROPE_HALF, tt), col), pl.BlockSpec((ROPE_HALF, tt), col)]
    args += weights + [cos_t, sin_t]

    bf16 = jnp.bfloat16
    qt_shape = jax.ShapeDtypeStruct((B, FOX_HEADS, ns, HEAD_PAD, tt), bf16)
    qt_spec = pl.BlockSpec((1, FOX_HEADS, 1, HEAD_PAD, tt), lambda b, s: (b, 0, s, 0, 0))
    k_shape = jax.ShapeDtypeStruct((T, FOX_HEADS * HEAD_PAD), bf16)
    k_spec = rows(FOX_HEADS * HEAD_PAD)
    vt_shape = jax.ShapeDtypeStruct((B, FOX_HEADS, ns, V_ROWS, tt), bf16)
    vt_spec = pl.BlockSpec((1, FOX_HEADS, 1, V_ROWS, tt), lambda b, s: (b, 0, s, 0, 0))
    out_shape = [qt_shape, k_shape, vt_shape] * 2
    out_specs = [qt_spec, k_spec, vt_spec] * 2
    scratch = [pltpu.VMEM((1, LANES), jnp.float32)]
    if has_moe_in:
        out_shape = [jax.ShapeDtypeStruct((T, D), jnp.float32)] + out_shape
        out_specs = [rows(D)] + out_specs
        scratch += [pltpu.VMEM((2, tt * (D // LANES), LANES), jnp.float32), pltpu.SemaphoreType.DMA((2,))]

    outs = pl.pallas_call(
        functools.partial(_pre_attn_kernel, has_moe_in=has_moe_in, tt=tt),
        grid=(B, ns),
        in_specs=in_specs,
        out_specs=out_specs,
        out_shape=out_shape,
        scratch_shapes=scratch,
        compiler_params=_cparams("arbitrary", "arbitrary"),
        name="pre_attn",
    )(*args)
    if has_moe_in:
        return outs[0], outs[1:]
    return h, outs


def _attn_kernel(*refs, ta, nq, n_streams):
    ins, refs = refs[:4 * n_streams], refs[4 * n_streams:]
    o_refs, (s_scr, mc_scr, m_scr, acc_scr) = refs[:n_streams], refs[n_streams:]
    qt_refs, k_refs, vt_refs, g_refs = ins[0::4], ins[1::4], ins[2::4], ins[3::4]
    heads = [(st, hh) for st in range(n_streams) for hh in range(2)]
    vd = MLA_V_DIM

    def produce_head(hi, q, kt, slot, masked):
        st, hh = heads[hi]
        k0 = pl.multiple_of(kt * ta, ta)
        kk = k_refs[st][pl.ds(k0, ta), hh * HEAD_PAD:(hh + 1) * HEAD_PAD]
        qt = qt_refs[st][0, hh, q]
        if not masked:
            s = _dot(kk, qt)
            s_scr[slot, hi] = s
            mc_scr[slot, hi] = jnp.max(s, axis=0, keepdims=True)
            return
        hk = ta // 2
        top = _dot(kk[0:hk], qt)
        low = _dot(kk[hk:], qt[:, hk:])
        top = jnp.where(lax.broadcasted_iota(jnp.int32, top.shape, 0)
                        <= lax.broadcasted_iota(jnp.int32, top.shape, 1), top, NEG_BIG)
        low = jnp.where(lax.broadcasted_iota(jnp.int32, low.shape, 0)
                        <= lax.broadcasted_iota(jnp.int32, low.shape, 1), low, NEG_BIG)
        s_scr[slot, hi, 0:hk, :] = top
        s_scr[slot, hi, hk:, 0:hk] = jnp.full((hk, hk), NEG_BIG, jnp.float32)
        s_scr[slot, hi, hk:, hk:] = low
        top_max = jnp.max(top, axis=0, keepdims=True)
        mc_scr[slot, hi, :, 0:hk] = top_max[:, 0:hk]
        mc_scr[slot, hi, :, hk:] = jnp.maximum(top_max[:, hk:], jnp.max(low, axis=0, keepdims=True))

    def consume_head(hi, kt, slot):
        st, hh = heads[hi]
        m_prev = m_scr[hi]
        m_new = jnp.maximum(m_prev, mc_scr[slot, hi])
        alpha = jnp.exp2(m_prev - m_new)
        p = jnp.exp2(s_scr[slot, hi] - m_new).astype(jnp.bfloat16)
        acc_scr[hi] = alpha * acc_scr[hi] + _dot(vt_refs[st][0, hh, kt], p)
        m_scr[hi] = m_new

    def produce(q, kt, slot, masked):
        for hi in range(len(heads)):
            produce_head(hi, q, kt, slot, masked)

    def consume(kt, slot):
        for hi in range(len(heads)):
            consume_head(hi, kt, slot)

    def produce_consume(produce_args, consume_args):
        for hi in range(len(heads)):
            produce_head(hi, *produce_args)
            consume_head(hi, *consume_args)

    def reset():
        m_scr[...] = jnp.full_like(m_scr, NEG_BIG)
        acc_scr[...] = jnp.zeros_like(acc_scr)

    def finish(q):
        for st in range(n_streams):
            outs = []
            for hh in range(2):
                acc = acc_scr[2 * st + hh]
                oh = acc[0:vd] / acc[vd:vd + 1]
                ms = jnp.mean(oh * oh, axis=0, keepdims=True)
                outs.append(oh * lax.rsqrt(ms + NORM_EPS))
            o_t = jnp.concatenate(outs, axis=0) * g_refs[st][...]
            o_refs[st][pl.ds(pl.multiple_of(q * ta, ta), ta), :] = o_t.T.astype(o_refs[st].dtype)

    reset()
    produce(0, 0, 0, True)

    def q_tile(n, carry):
        first = (n * (n + 1) // 2) % 2

        def step(j, slot):
            produce_consume((n, j, 1 - slot, False), (jnp.where(j == 0, n, j - 1), slot))

        for par in range(2):
            @pl.when(first == par)
            def _():
                def two_steps(u, c):
                    step(2 * u, par)
                    step(2 * u + 1, 1 - par)
                    return c

                lax.fori_loop(0, n // 2, two_steps, 0)

                @pl.when((n % 2 == 1) & (n < nq - 1))
                def _():
                    step(n - 1, par)
                    produce_consume((n + 1, n + 1, par, True), (n - 1, 1 - par))
                    finish(n)
                    reset()

                @pl.when((n % 2 == 0) & (n < nq - 1))
                def _():
                    produce_consume((n + 1, n + 1, 1 - par, True), (jnp.maximum(n - 1, 0), par))
                    finish(n)
                    reset()

        n_last = nq - 1
        first_last = (n_last * (n_last + 1) // 2) % 2

        @pl.when(n == n_last)
        def _():
            if n_last % 2 == 1:
                step(n - 1, first_last)
                consume(n - 1, 1 - first_last)
            else:
                consume(jnp.maximum(n - 1, 0), first_last)
            finish(n)

        return carry

    lax.fori_loop(0, nq, q_tile, 0)


def _attention(streams, B, S):
    T = streams[0][1].shape[0]
    ta = min(ATTN_TILE, S)
    nq = S // ta
    n_pairs = streams[0][0].shape[1] // 2
    n_streams = len(streams)
    n_heads = 2 * n_streams
    stream_specs = [pl.BlockSpec((1, 2, nq, HEAD_PAD, ta), lambda b, j: (b, j, 0, 0, 0)),
                    pl.BlockSpec((S, 2 * HEAD_PAD), lambda b, j: (b, j)),
                    pl.BlockSpec((1, 2, nq, V_ROWS, ta), lambda b, j: (b, j, 0, 0, 0)),
                    pl.BlockSpec((LANES, ta), lambda b, j: (j, 0))]
    return pl.pallas_call(
        functools.partial(_attn_kernel, ta=ta, nq=nq, n_streams=n_streams),
        grid=(B, n_pairs),
        in_specs=stream_specs * n_streams,
        out_specs=[pl.BlockSpec((S, LANES), lambda b, j: (b, j))] * n_streams,
        out_shape=[jax.ShapeDtypeStruct((T, n_pairs * LANES), jnp.bfloat16)] * n_streams,
        scratch_shapes=[pltpu.VMEM((2, n_heads, ta, ta), jnp.float32),
                        pltpu.VMEM((2, n_heads, 1, ta), jnp.float32),
                        pltpu.VMEM((n_heads, 1, ta), jnp.float32),
                        pltpu.VMEM((n_heads, V_ROWS, ta), jnp.float32)],
        compiler_params=_cparams("parallel", "parallel"),
        name="attention",
    )(*[a for stream in streams for a in stream])


def _post_attn_kernel(fox_ref, mla_ref, h_ref, wo_ref, fnorm_ref, wrh_ref, wrl_ref, tri_ref,
                      h2_ref, xr_ref, route_ref, cnt_ref, carry_ref, xn_scr):
    i = pl.program_id(0)
    D = h_ref.shape[1]

    @pl.when(i == 0)
    def _():
        xn_scr[...] = jnp.zeros_like(xn_scr)

    @pl.when(i <= 1)
    def _():
        carry_ref[...] = jnp.zeros_like(carry_ref)

    xn = xn_scr[...]
    xr_ref[:, 0:D] = xn

    x_hi = xn.astype(jnp.bfloat16)
    x_lo = (xn - x_hi.astype(jnp.float32)).astype(jnp.bfloat16)
    logits = _dot(x_hi, wrh_ref[...]) + _dot(x_lo, wrh_ref[...]) + _dot(x_hi, wrl_ref[...])

    h2 = h_ref[...] + _dot(fox_ref[...], wo_ref[0:FOX_WIDTH, :]) + _dot(mla_ref[...], wo_ref[FOX_WIDTH:, :])
    h2_ref[...] = h2
    xn_scr[...] = _rms(h2, fnorm_ref[...])

    lt = logits.T
    tt = lt.shape[1]
    ng = N_GROUPS
    gl = lt[0:ng]
    sub = lax.broadcasted_iota(jnp.int32, (ng, tt), 0).astype(jnp.float32)

    gmax = jnp.max(gl, axis=0, keepdims=True)
    group_w = 1.0 / jnp.sum(jnp.exp(gl - gmax), axis=0, keepdims=True)
    gid = jnp.min(jnp.where(gl == gmax, sub, float(ng)), axis=0, keepdims=True)
    onehot = sub == gid

    ins = [jnp.sum(jnp.where(onehot, lt[ng * (j + 1):ng * (j + 2)], 0.0), axis=0, keepdims=True)
           for j in range(EXPERTS_PER_GROUP)]

    def first_argmax(vals):
        top = functools.reduce(jnp.maximum, vals)
        idx = jnp.full_like(top, float(len(vals) - 1))
        for j in range(len(vals) - 2, -1, -1):
            idx = jnp.where(vals[j] == top, float(j), idx)
        return top, idx

    e1, i1 = first_argmax(ins)
    e2, i2 = first_argmax([jnp.where(i1 == j, NEG_BIG, v) for j, v in enumerate(ins)])
    t = jnp.exp(e2 - e1)
    g1 = group_w / (1.0 + t)
    g2 = group_w * t / (1.0 + t)
    gates = [jnp.where(i1 == j, g1, jnp.where(i2 == j, g2, 0.0)) for j in range(EXPERTS_PER_GROUP)]

    ones = jnp.where(onehot, 1.0, 0.0)
    prefix = _dot(ones.astype(jnp.bfloat16), tri_ref[...]) + carry_ref[:, 0:1]
    rank = jnp.sum(jnp.where(onehot, prefix, 0.0), axis=0, keepdims=True)
    carry = carry_ref[...] + jnp.sum(ones, axis=1, keepdims=True)
    carry_ref[...] = carry
    cnt_ref[...] = carry

    pad_rows = jnp.zeros((ROUTE_ROWS - EXPERTS_PER_GROUP - 2, tt), jnp.float32)
    info_t = jnp.concatenate(gates + [gid, rank, pad_rows], axis=0)
    route_ref[...] = info_t
    info_pad = jnp.zeros((LANES - ROUTE_ROWS, tt), jnp.float32)
    xr_ref[:, D:D + LANES] = jnp.concatenate([info_t, info_pad], axis=0).T


def _post_attn(fox, mla, h, lw):
    T, D = h.shape
    tt = min(ROW_TILE, T)
    n = T // tt
    proj = lambda i: (jnp.minimum(i, n - 1), 0)
    routed = lambda i: (jnp.maximum(i - 1, 0), 0)
    const = lambda i: (0, 0)
    weights = [lw["w_o"], lw["ffn_norm"], lw["wr_hi"], lw["wr_lo"], lw["tri_before"]]
    return pl.pallas_call(
        _post_attn_kernel,
        grid=(n + 1,),
        in_specs=[pl.BlockSpec((tt, FOX_WIDTH), proj), pl.BlockSpec((tt, MLA_WIDTH), proj),
                  pl.BlockSpec((tt, D), proj)] + [pl.BlockSpec(w.shape, const) for w in weights],
        out_specs=[pl.BlockSpec((tt, D), proj), pl.BlockSpec((tt, D + LANES), routed),
                   pl.BlockSpec((ROUTE_ROWS, tt), lambda i: (0, jnp.maximum(i - 1, 0))),
                   pl.BlockSpec((N_GROUPS, LANES), const)],
        out_shape=[jax.ShapeDtypeStruct((T, D), jnp.float32),
                   jax.ShapeDtypeStruct((T, D + LANES), jnp.float32),
                   jax.ShapeDtypeStruct((ROUTE_ROWS, T), jnp.float32),
                   jax.ShapeDtypeStruct((N_GROUPS, LANES), jnp.float32)],
        scratch_shapes=[pltpu.VMEM((N_GROUPS, LANES), jnp.float32), pltpu.VMEM((tt, D), jnp.float32)],
        compiler_params=_cparams("arbitrary"),
        name="post_attn",
    )(fox, mla, h, *weights)


def _dispatch_kernel(fill_ref, dest_ref, xr_ref, xs_hbm, zero_ref, sem, fill_sem, *, tt, n_sorted):
    first = pl.program_id(0) == 0

    def fill_copies(act):
        for g in range(N_GROUPS):
            start, length = fill_ref[g], fill_ref[N_GROUPS + g]
            head = (-start) & (SUBLANES - 1)
            for r in range(SUBLANES - 1):
                @pl.when(r < head)
                def _(start=start, r=r):
                    act(pltpu.make_async_copy(zero_ref.at[pl.ds(0, 1)], xs_hbm.at[pl.ds(start + r, 1)], fill_sem))
            base, rest = start + head, length - head
            size = MOE_BLOCK // 2
            while size >= SUBLANES:
                @pl.when((rest & size) != 0)
                def _(base=base, rest=rest, size=size):
                    at = pl.multiple_of(base + (rest & ~(2 * size - 1)), SUBLANES)
                    act(pltpu.make_async_copy(zero_ref.at[pl.ds(0, size)], xs_hbm.at[pl.ds(at, size)], fill_sem))
                size //= 2
        used = fill_ref[2 * N_GROUPS]
        for blk in range(N_GROUPS):
            @pl.when(used + (blk + 1) * MOE_BLOCK <= n_sorted)
            def _(blk=blk):
                at = pl.multiple_of(used + blk * MOE_BLOCK, MOE_BLOCK)
                act(pltpu.make_async_copy(zero_ref, xs_hbm.at[pl.ds(at, MOE_BLOCK)], fill_sem))

    @pl.when(first)
    def _():
        zero_ref[...] = jnp.zeros_like(zero_ref)
        fill_copies(lambda cp: cp.start())

    def issue(u, carry):
        for prio in range(DMA_PRIORITIES):
            r = u * DMA_PRIORITIES + prio
            pltpu.make_async_copy(xr_ref.at[pl.ds(r, 1)], xs_hbm.at[pl.ds(dest_ref[0, 0, r], 1)],
                                  sem).start(priority=prio)
        return carry
    lax.fori_loop(0, tt // DMA_PRIORITIES, issue, 0, unroll=ROW_DMA_UNROLL // DMA_PRIORITIES)
    pltpu.make_async_copy(xr_ref, xs_hbm.at[pl.ds(0, tt)], sem).wait()

    @pl.when(first)
    def _():
        fill_copies(lambda cp: cp.wait())


def _dispatch(xr, dest3, fill, n_sorted):
    T, W = xr.shape
    tt = dest3.shape[2]
    grid_spec = pltpu.PrefetchScalarGridSpec(
        num_scalar_prefetch=1,
        grid=(T // tt,),
        in_specs=[pl.BlockSpec((1, 1, tt), lambda i, f: (i, 0, 0), memory_space=pltpu.SMEM),
                  pl.BlockSpec((tt, W), lambda i, f: (i, 0))],
        out_specs=pl.BlockSpec(memory_space=pl.ANY),
        scratch_shapes=[pltpu.VMEM((MOE_BLOCK, W), xr.dtype),
                        pltpu.SemaphoreType.DMA(()), pltpu.SemaphoreType.DMA(())],
    )
    return pl.pallas_call(
        functools.partial(_dispatch_kernel, tt=tt, n_sorted=n_sorted),
        grid_spec=grid_spec,
        out_shape=jax.ShapeDtypeStruct((n_sorted, W), xr.dtype),
        compiler_params=_cparams("arbitrary"),
        name="moe_dispatch",
    )(fill, dest3, xr)


def _moe_kernel(bg_ref, nv_ref, xs_ref, wg32_ref, wu32_ref, wd32_ref, ys_ref, wg_ref, wu_ref, wd_ref):
    D = xs_ref.shape[1] - LANES
    rows = xs_ref.shape[0]
    chunks = D // LANES
    b = pl.program_id(0)
    g = bg_ref[b]
    n_valid = nv_ref[b]

    @pl.when((b == 0) | (g != bg_ref[jnp.maximum(b - 1, 0)]))
    def _():
        wg_ref[...] = wg32_ref[...].astype(jnp.bfloat16)
        wu_ref[...] = wu32_ref[...].astype(jnp.bfloat16)
        wd_ref[...] = wd32_ref[...].astype(jnp.bfloat16)

    @pl.when(n_valid == 0)
    def _():
        ys_ref[...] = jnp.zeros_like(ys_ref)

    @pl.when(n_valid > 0)
    def _():
        x = xs_ref[:, 0:D].astype(jnp.bfloat16)
        info = xs_ref[:, D:D + LANES]
        y = jnp.zeros((rows, D), jnp.float32)
        for j in range(EXPERTS_PER_GROUP):
            a = _dot(x, wg_ref[j])
            u = _dot(x, wu_ref[j])
            hj = (a * jax.nn.sigmoid(a) * u).astype(jnp.bfloat16)
            y = y + info[:, j:j + 1] * _dot(hj, wd_ref[j])
        for c in range(chunks):
            ys_ref[pl.ds(c, rows, stride=chunks), :] = y[:, c * LANES:(c + 1) * LANES]


def _moe(xs, block_group, block_valid, layer, w_gate, w_up, w_down):
    n_sorted, W = xs.shape
    D = W - LANES
    nb = n_sorted // MOE_BLOCK
    epg = EXPERTS_PER_GROUP
    chunks = D // LANES
    grid_spec = pltpu.PrefetchScalarGridSpec(
        num_scalar_prefetch=2,
        grid=(nb,),
        in_specs=[pl.BlockSpec((MOE_BLOCK, W), lambda b, bg, nv: (b, 0)),
                  pl.BlockSpec((None, epg, D, D_EXPERT), lambda b, bg, nv: (layer, bg[b], 0, 0)),
                  pl.BlockSpec((None, epg, D, D_EXPERT), lambda b, bg, nv: (layer, bg[b], 0, 0)),
                  pl.BlockSpec((None, epg, D_EXPERT, D), lambda b, bg, nv: (layer, bg[b], 0, 0))],
        out_specs=pl.BlockSpec((MOE_BLOCK * chunks, LANES), lambda b, bg, nv: (b, 0)),
        scratch_shapes=[pltpu.VMEM((epg, D, D_EXPERT), jnp.bfloat16),
                        pltpu.VMEM((epg, D, D_EXPERT), jnp.bfloat16),
                        pltpu.VMEM((epg, D_EXPERT, D), jnp.bfloat16)],
    )
    return pl.pallas_call(
        _moe_kernel,
        grid_spec=grid_spec,
        out_shape=jax.ShapeDtypeStruct((n_sorted * chunks, LANES), jnp.float32),
        compiler_params=_cparams("arbitrary"),
        name="moe_experts",
    )(block_group, block_valid, xs, w_gate, w_up, w_down)


def _final_kernel(dest_ref, dest_next_ref, h_ref, ys_hbm, gain_ref, o_ref, ybuf_ref, sems, *, tt):
    y = _gather_rows(ys_hbm, dest_ref, dest_next_ref, ybuf_ref, sems,
                     pl.program_id(0), pl.num_programs(0), tt)
    o_ref[...] = _rms(h_ref[...] + y, gain_ref[...])


def _final(h, dest3, ys, gain):
    T, D = h.shape
    tt = dest3.shape[2]
    last = T // tt - 1
    return pl.pallas_call(
        functools.partial(_final_kernel, tt=tt),
        grid=(T // tt,),
        in_specs=[pl.BlockSpec((1, 1, tt), lambda i: (i, 0, 0), memory_space=pltpu.SMEM),
                  pl.BlockSpec((1, 1, tt), lambda i: (jnp.minimum(i + 1, last), 0, 0), memory_space=pltpu.SMEM),
                  pl.BlockSpec((tt, D), lambda i: (i, 0)),
                  pl.BlockSpec(memory_space=pl.ANY),
                  pl.BlockSpec((1, D), lambda i: (0, 0))],
        out_specs=pl.BlockSpec((tt, D), lambda i: (i, 0)),
        out_shape=jax.ShapeDtypeStruct((T, D), jnp.float32),
        scratch_shapes=[pltpu.VMEM((2, tt * (D // LANES), LANES), jnp.float32), pltpu.SemaphoreType.DMA((2,))],
        compiler_params=_cparams("arbitrary"),
        name="final_norm",
    )(dest3, dest3, h, ys, gain)


def _prep_layer(l, tt, w_in, b_f, w_uq, w_ukv, w_o, w_group, w_router,
                attn_norm, q_norm, kv_norm, ffn_norm):
    bf16 = jnp.bfloat16
    D = w_in.shape[1]
    fw = FOX_WIDTH
    o_f = 3 * fw
    o_cq = o_f + FOX_HEADS
    o_ckv = o_cq + Q_LORA_RANK
    o_kr = o_ckv + KV_LORA_RANK
    wi = w_in[l]
    w_f = wi[:, o_f:o_cq]
    w_f3 = jnp.concatenate([w_f] * DECAY_PARTS + [jnp.zeros((D, LANES - DECAY_PARTS * FOX_HEADS), wi.dtype)], axis=1)
    b_f3 = jnp.concatenate([b_f[l]] * DECAY_PARTS + [jnp.zeros((LANES - DECAY_PARTS * FOX_HEADS,), b_f.dtype)])
    kr = wi[:, o_kr:o_kr + MLA_ROPE_DIM]
    kr_swap = jnp.concatenate([-kr[:, ROPE_HALF:], kr[:, :ROPE_HALF]], axis=1)
    zpad = jnp.zeros((D, MLA_ROPE_DIM), wi.dtype)
    w_rows = jnp.concatenate([wi[:, fw:2 * fw], w_f3, wi[:, o_cq:o_kr], kr_swap, zpad, kr, zpad], axis=1)

    src = jnp.arange(LANES)
    dst = (src % FOX_HEADS) * HEAD_PAD + FOX_HEAD_DIM + src // FOX_HEADS
    place = ((jnp.arange(FOX_HEADS * HEAD_PAD)[None, :] == dst[:, None])
             & (src[:, None] < DECAY_PARTS * FOX_HEADS)).astype(bf16)

    qd = MLA_NOPE_DIM + MLA_ROPE_DIM
    wq = w_uq[l].reshape(Q_LORA_RANK, MLA_HEADS, qd)
    zq = jnp.zeros((Q_LORA_RANK, MLA_HEADS, HEAD_PAD - qd), wq.dtype)
    wq_main = jnp.concatenate([wq, zq], axis=2).reshape(Q_LORA_RANK, MLA_HEADS * HEAD_PAD)
    x1 = wq[:, :, MLA_NOPE_DIM:MLA_NOPE_DIM + ROPE_HALF]
    x2 = wq[:, :, MLA_NOPE_DIM + ROPE_HALF:]
    wq_swap = jnp.concatenate([-x2, x1], axis=2).reshape(Q_LORA_RANK, MLA_HEADS * MLA_ROPE_DIM)

    wkv = w_ukv[l].reshape(KV_LORA_RANK, MLA_HEADS, MLA_NOPE_DIM + MLA_V_DIM)
    wk_nope = jnp.concatenate([wkv[:, :, :MLA_NOPE_DIM],
                               jnp.zeros((KV_LORA_RANK, MLA_HEADS, HEAD_PAD - MLA_NOPE_DIM), wkv.dtype)], axis=2)
    wv = wkv[:, :, MLA_NOPE_DIM:].reshape(KV_LORA_RANK, MLA_WIDTH)

    w_exp = w_router[l].reshape(D, N_GROUPS, EXPERTS_PER_GROUP).transpose(0, 2, 1).reshape(D, N_EXPERTS)
    wr = jnp.concatenate([w_group[l], w_exp,
                          jnp.zeros((D, LANES - N_GROUPS - N_EXPERTS), w_group.dtype)], axis=1)
    wr_hi = wr.astype(bf16)
    wr_lo = (wr - wr_hi.astype(jnp.float32)).astype(bf16)

    ar = jnp.arange(tt)
    return dict(
        attn_norm=attn_norm[l].reshape(1, D),
        w_rows=w_rows.astype(bf16),
        w_fq_t=wi[:, 0:fw].T.astype(bf16),
        w_fv_t=wi[:, 2 * fw:3 * fw].T.astype(bf16),
        b_f=b_f3.reshape(1, LANES),
        tri_incl=(ar[:, None] >= ar[None, :]).astype(bf16),
        tri_before=(ar[:, None] < ar[None, :]).astype(bf16),
        place=place,
        q_norm=q_norm[l].reshape(1, Q_LORA_RANK),
        wq_main_t=wq_main.T.astype(bf16),
        wq_swap_t=wq_swap.T.astype(bf16),
        kv_norm=kv_norm[l].reshape(1, KV_LORA_RANK),
        wk_nope=wk_nope.reshape(KV_LORA_RANK, MLA_HEADS * HEAD_PAD).astype(bf16),
        wv_t=wv.T.astype(bf16),
        w_o=w_o[l].astype(bf16),
        ffn_norm=ffn_norm[l].reshape(1, D),
        wr_hi=wr_hi,
        wr_lo=wr_lo,
    )


def _routing_plan(route, counts, n_blocks):
    gid = route[GID_INDEX].astype(jnp.int32)
    rank = route[RANK_INDEX].astype(jnp.int32)
    cnt = counts[:, 0].astype(jnp.int32)
    padded = (cnt + MOE_BLOCK - 1) // MOE_BLOCK * MOE_BLOCK
    pend = jnp.cumsum(padded)
    pstart = pend - padded
    dest = pstart[gid] + rank
    block_start = jnp.arange(n_blocks, dtype=jnp.int32) * MOE_BLOCK
    block_group = jnp.sum((pend[None, :] <= block_start[:, None]).astype(jnp.int32), axis=1)
    block_group = jnp.minimum(block_group, N_GROUPS - 1)
    block_valid = jnp.clip((pstart + cnt)[block_group] - block_start, 0, MOE_BLOCK)
    fill = jnp.concatenate([pstart + cnt, padded - cnt, pend[-1:]])
    return dest, block_group, block_valid, fill


def kernel(x, positions, attn_norm, w_in, b_f, q_norm, w_uq, kv_norm, w_ukv, fox_out_norm, mla_out_norm, w_o,
           ffn_norm, w_group, w_router, w_gate, w_up, w_down, final_norm):
    B, S, D = x.shape
    T = B * S
    depth = w_in.shape[0]
    ta = min(ATTN_TILE, S)
    tt = min(ROW_TILE, S)
    n_blocks = -(-(T + N_GROUPS * (MOE_BLOCK - 1)) // MOE_BLOCK)

    cos_t, sin_t = _rope_tables(positions)
    tabs = (cos_t, sin_t)

    h = x.reshape(T, D)
    moe_in = None
    for l in range(depth):
        lw = _prep_layer(l, tt, w_in, b_f, w_uq, w_ukv, w_o, w_group, w_router,
                         attn_norm, q_norm, kv_norm, ffn_norm)
        h, (fq_t, fk, fv_t, mq_t, mk, mv_t) = _pre_attn(h, moe_in, B, S, lw, tabs)
        fox_gain = jnp.broadcast_to(fox_out_norm[l].reshape(FOX_WIDTH, 1), (FOX_WIDTH, ta))
        mla_gain = jnp.broadcast_to(mla_out_norm[l].reshape(MLA_WIDTH, 1), (MLA_WIDTH, ta))
        fox, mla = _attention([(fq_t, fk, fv_t, fox_gain), (mq_t, mk, mv_t, mla_gain)], B, S)
        h, xr, route, counts = _post_attn(fox, mla, h, lw)
        dest, block_group, block_valid, fill = _routing_plan(route, counts, n_blocks)
        dest3 = dest.reshape(T // tt, 1, tt)
        dt = DISPATCH_TILE if T % DISPATCH_TILE == 0 else tt
        xs = _dispatch(xr, dest.reshape(T // dt, 1, dt), fill, n_blocks * MOE_BLOCK)
        ys = _moe(xs, block_group, block_valid, l, w_gate, w_up, w_down)
        moe_in = (dest3, ys)
    out = _final(h, moe_in[0], moe_in[1], final_norm.reshape(1, D))
    return out.reshape(B, S, D)
```
